```python
import math
import jax
import jax.numpy as jnp
from jax import lax
import numpy as np

D_MODEL = 1024
BATCH = 8
SEQ = 2048
DEPTH = 4

MEM_LEN = 256
HEAD_DIM = 64
ROPE_THETA = 500000.0
ROPE_DIM = HEAD_DIM // 4
NORM_EPS = 1e-5

POOL_WINDOWS = (2, 4, 8, 16)
POOL_GROUPS = 4
POOL_GROUP_DIM = D_MODEL // 16
POOL_DIM = POOL_GROUPS * POOL_GROUP_DIM
MOBA_HEADS = (D_MODEL - POOL_DIM) // HEAD_DIM
MOBA_DIM = MOBA_HEADS * HEAD_DIM
MOBA_BLOCK = 256
MOBA_TOPK = 3
MOBA_QC = 16
AB_IN = POOL_DIM + 3 * MOBA_DIM
NSA_HEADS = D_MODEL // HEAD_DIM
NSA_GROUPS = 4
NSA_HPG = NSA_HEADS // NSA_GROUPS
NSA_KV_DIM = NSA_GROUPS * HEAD_DIM
CMP_LEN = 32
CMP_STRIDE = 16
CMP_HIDDEN = 4 * HEAD_DIM
SLC_BLOCK = 64
SLC_TOPN = 16
WINDOW = 512
NSA_QC = 32
FORCE_SCORE = 1e4
C_IN = NSA_HEADS * HEAD_DIM + 6 * NSA_KV_DIM + 3 * NSA_HEADS
XA_HEADS = 4
XA_DIM = XA_HEADS * HEAD_DIM
D_FF = 2816
N_EXPERTS = 8
TOP_K = 2
D_FF_EXPERT = 3584
N_EVEN = (DEPTH + 1) // 2
N_ODD = DEPTH // 2

kernel_name = 'hybrid_pool_moba_nsa_moe_decoder'


def rmsnorm(x, g):
    xf = x.astype(jnp.float32)
    y = xf * lax.rsqrt(jnp.mean(xf * xf, axis=-1, keepdims=True) + NORM_EPS)
    return y.astype(x.dtype) * g


def rope_tables(positions):
    inv = ROPE_THETA ** (-jnp.arange(0, ROPE_DIM, 2, dtype=jnp.float32) / ROPE_DIM)
    ang = positions.astype(jnp.float32)[:, None] * inv[None, :]
    return jnp.cos(ang), jnp.sin(ang)


def apply_rope(x, cos, sin):
    xr, xp = x[..., :ROPE_DIM], x[..., ROPE_DIM:]
    x1, x2 = xr[..., :ROPE_DIM // 2], xr[..., ROPE_DIM // 2:]
    c = cos[None, :, None, :].astype(x.dtype)
    s = sin[None, :, None, :].astype(x.dtype)
    return jnp.concatenate([x1 * c - x2 * s, x1 * s + x2 * c, xp], axis=-1)


def masked_softmax(s, mask):
    s = jnp.where(mask, s.astype(jnp.float32), -jnp.inf)
    m = jnp.max(s, axis=-1, keepdims=True)
    m = jnp.where(jnp.isfinite(m), m, 0.0)
    e = jnp.where(mask, jnp.exp(s - m), 0.0)
    return e / jnp.maximum(jnp.sum(e, axis=-1, keepdims=True), 1e-30)


def gather_blocks(blocks, idx):
    return jax.vmap(jax.vmap(lambda bl, ix: bl[ix]))(blocks, idx)


def swiglu(h, wg, wu, wd):
    return (jax.nn.silu(h @ wg) * (h @ wu)) @ wd


def pool_mixer(u, pool_w, pool_scale):
    B_, S_, _ = u.shape
    uf = u.astype(jnp.float32)
    cs = jnp.concatenate([jnp.zeros((B_, 1, POOL_DIM), jnp.float32), jnp.cumsum(uf, axis=1)], axis=1)
    t = jnp.arange(S_)
    outs = []
    for g, w in enumerate(POOL_WINDOWS):
        sl = slice(g * POOL_GROUP_DIM, (g + 1) * POOL_GROUP_DIM)
        start = jnp.maximum(t + 1 - w, 0)
        cnt = jnp.minimum(t + 1, w).astype(jnp.float32)
        outs.append((cs[:, t + 1, sl] - cs[:, start, sl]) / cnt[None, :, None])
    pooled = (jnp.concatenate(outs, axis=-1) - uf).astype(u.dtype)
    pg = pooled.reshape(B_, S_, POOL_GROUPS, POOL_GROUP_DIM)
    y = jnp.einsum('bsgc,gcd->bsgd', pg, pool_w).reshape(B_, S_, POOL_DIM)
    return y * pool_scale


def moba_attention(q, k, v):
    B_, S_, H, dh = q.shape
    nb = -(-S_ // MOBA_BLOCK)
    s_pad = nb * MOBA_BLOCK
    pad = ((0, 0), (0, s_pad - S_), (0, 0), (0, 0))
    kb = jnp.pad(k, pad).transpose(0, 2, 1, 3).reshape(B_, H, nb, MOBA_BLOCK, dh)
    vb = jnp.pad(v, pad).transpose(0, 2, 1, 3).reshape(B_, H, nb, MOBA_BLOCK, dh)
    kmean = jnp.mean(kb.astype(jnp.float32), axis=3)
    qh = q.transpose(0, 2, 1, 3)
    scale = dh ** -0.5
    topk = min(MOBA_TOPK, nb - 1)

    def chunk(c):
        s0 = c * MOBA_QC
        tq = s0 + jnp.arange(MOBA_QC)
        qc = lax.dynamic_slice_in_dim(qh, s0, MOBA_QC, axis=2)
        blk = s0 // MOBA_BLOCK
        k_own = lax.dynamic_slice_in_dim(kb, blk, 1, axis=2)[:, :, 0]
        v_own = lax.dynamic_slice_in_dim(vb, blk, 1, axis=2)[:, :, 0]
        pos_own = blk * MOBA_BLOCK + jnp.arange(MOBA_BLOCK)
        m_own = (pos_own[None, :] <= tq[:, None])[None, None]
        s_own = jnp.einsum('bhqd,bhkd->bhqk', qc, k_own) * scale
        if topk == 0:
            p = masked_softmax(s_own, m_own).astype(v.dtype)
            return jnp.einsum('bhqk,bhkd->bhqd', p, v_own)
        gate = jnp.einsum('bhqd,bhnd->bhqn', qc.astype(jnp.float32), kmean)
        gate = jnp.where(jnp.arange(nb) < blk, gate, -jnp.inf)
        _, idx = lax.top_k(gate, topk)
        sel_ok = idx < blk
        k_sel = gather_blocks(kb, idx)
        v_sel = gather_blocks(vb, idx)
        s_sel = jnp.einsum('bhqd,bhqnkd->bhqnk', qc, k_sel).reshape(B_, H, MOBA_QC, topk * MOBA_BLOCK) * scale
        m_sel = jnp.repeat(sel_ok, MOBA_BLOCK, axis=-1)
        m_all = jnp.concatenate([m_sel, jnp.broadcast_to(m_own, (B_, H, MOBA_QC, MOBA_BLOCK))], axis=-1)
        p = masked_softmax(jnp.concatenate([s_sel, s_own], axis=-1), m_all).astype(v.dtype)
        p_sel = p[..., :topk * MOBA_BLOCK].reshape(B_, H, MOBA_QC, topk, MOBA_BLOCK)
        p_own = p[..., topk * MOBA_BLOCK:]
        return (jnp.einsum('bhqnk,bhqnkd->bhqd', p_sel, v_sel)
                + jnp.einsum('bhqk,bhkd->bhqd', p_own, v_own))

    out = lax.map(chunk, jnp.arange(S_ // MOBA_QC))
    return out.transpose(1, 0, 3, 2, 4).reshape(B_, S_, H * dh)


def pool_moba_mixer(h, w_in, pool_w, pool_scale, w_out, cos, sin):
    B_, S_, _ = h.shape
    u = h @ w_in
    u_pool, q, k, v = jnp.split(u, [POOL_DIM, POOL_DIM + MOBA_DIM, POOL_DIM + 2 * MOBA_DIM], axis=-1)
    shp = (B_, S_, MOBA_HEADS, HEAD_DIM)
    q = apply_rope(q.reshape(shp), cos, sin)
    k = apply_rope(k.reshape(shp), cos, sin)
    y_a = pool_mixer(u_pool, pool_w, pool_scale)
    y_b = moba_attention(q, k, v.reshape(shp))
    return jnp.concatenate([y_a, y_b], axis=-1) @ w_out


def compress_blocks(t, pe, w1, w2):
    B_, S_, G, dh = t.shape
    n_cmp = (S_ - CMP_LEN) // CMP_STRIDE + 1
    idx = np.arange(n_cmp)[:, None] * CMP_STRIDE + np.arange(CMP_LEN)[None, :]
    blk = t[:, idx] + pe[None, None, :, None, :]
    flat = blk.transpose(0, 1, 3, 2, 4).reshape(B_, n_cmp, G, CMP_LEN * dh)
    return jax.nn.gelu(flat @ w1) @ w2


def nsa_mixer(h, w_in, pe_k, pe_v, k_w1, k_w2, v_w1, v_w2, w_out, cos, sin):
    B_, S_, _ = h.shape
    G, HPG, dh = NSA_GROUPS, NSA_HPG, HEAD_DIM
    u = h @ w_in
    qd = NSA_HEADS * dh
    offs = [qd + i * NSA_KV_DIM for i in range(1, 7)]
    q, kc, vc, ks, vs, kw, vw, gl = jnp.split(u, [qd] + offs, axis=-1)
    q = q.reshape(B_, S_, NSA_HEADS, dh)
    kvs = (B_, S_, G, dh)
    kc, vc, ks, vs, kw, vw = [a.reshape(kvs) for a in (kc, vc, ks, vs, kw, vw)]
    gates = jax.nn.sigmoid(gl.reshape(B_, S_, NSA_HEADS, 3).astype(jnp.float32)).astype(h.dtype)
    q_rot = apply_rope(q, cos, sin).reshape(B_, S_, G, HPG, dh)
    ks = apply_rope(ks, cos, sin)
    kw = apply_rope(kw, cos, sin)
    q_cmp = q.reshape(B_, S_, G, HPG, dh)
    scale = dh ** -0.5
    t = jnp.arange(S_)

    k_cmp = compress_blocks(kc, pe_k, k_w1, k_w2)
    v_cmp = compress_blocks(vc, pe_v, v_w1, v_w2)
    n_cmp = k_cmp.shape[1]
    cmp_end = jnp.arange(n_cmp) * CMP_STRIDE + CMP_LEN - 1
    m_cmp = cmp_end[None, :] <= t[:, None]
    s_cmp = jnp.einsum('bsghd,bngd->bghsn', q_cmp, k_cmp) * scale
    p_cmp = masked_softmax(s_cmp, m_cmp)
    o_cmp = jnp.einsum('bghsn,bngd->bsghd', p_cmp.astype(v_cmp.dtype), v_cmp).reshape(B_, S_, NSA_HEADS, dh)

    n_slc = -(-S_ // SLC_BLOCK)
    c_s = np.arange(n_cmp) * CMP_STRIDE
    s_s = np.arange(n_slc) * SLC_BLOCK
    ov = np.clip(np.minimum(c_s[:, None] + CMP_LEN, s_s[None, :] + SLC_BLOCK) - np.maximum(c_s[:, None], s_s[None, :]), 0, None) / CMP_LEN
    p_slc = jnp.einsum('bgsn,nj->bgsj', jnp.sum(p_cmp, axis=2), jnp.asarray(ov, jnp.float32))
    t_blk = t // SLC_BLOCK
    j = jnp.arange(n_slc)
    forced = (j[None, :] == 0) | (j[None, :] == t_blk[:, None]) | (j[None, :] == t_blk[:, None] - 1)
    valid = j[None, :] <= t_blk[:, None]
    score = jnp.where(valid, jnp.where(forced, FORCE_SCORE, p_slc), -1.0)
    top_n = min(SLC_TOPN, n_slc)
    _, sel_idx = lax.top_k(score, top_n)

    kbs = ks.transpose(0, 2, 1, 3).reshape(B_, G, n_slc, SLC_BLOCK, dh)
    vbs = vs.transpose(0, 2, 1, 3).reshape(B_, G, n_slc, SLC_BLOCK, dh)
    wpad = ((0, 0), (WINDOW, 0), (0, 0), (0, 0))
    kwp = jnp.pad(kw, wpad)
    vwp = jnp.pad(vw, wpad)

    def chunk(c):
        s0 = c * NSA_QC
        tq = s0 + jnp.arange(NSA_QC)
        qc = lax.dynamic_slice_in_dim(q_rot, s0, NSA_QC, axis=1)
        ic = lax.dynamic_slice_in_dim(sel_idx, s0, NSA_QC, axis=2)
        k_sel = gather_blocks(kbs, ic)
        v_sel = gather_blocks(vbs, ic)
        s_sel = jnp.einsum('bqghd,bgqnkd->bghqnk', qc, k_sel).reshape(B_, G, HPG, NSA_QC, top_n * SLC_BLOCK) * scale
        kpos = ic[..., None] * SLC_BLOCK + jnp.arange(SLC_BLOCK)
        m_sel = (kpos <= tq[None, None, :, None, None]) & (ic <= (tq // SLC_BLOCK)[None, None, :, None])[..., None]
        m_sel = m_sel.reshape(B_, G, NSA_QC, top_n * SLC_BLOCK)[:, :, None]
        p_sel = masked_softmax(s_sel, m_sel).astype(v_sel.dtype).reshape(B_, G, HPG, NSA_QC, top_n, SLC_BLOCK)
        o_sel = jnp.einsum('bghqnk,bgqnkd->bqghd', p_sel, v_sel)
        kwc = lax.dynamic_slice_in_dim(kwp, s0, NSA_QC + WINDOW, axis=1)
        vwc = lax.dynamic_slice_in_dim(vwp, s0, NSA_QC + WINDOW, axis=1)
        kp = s0 - WINDOW + jnp.arange(NSA_QC + WINDOW)
        m_win = (kp[None, :] >= 0) & (kp[None, :] <= tq[:, None]) & (tq[:, None] - kp[None, :] < WINDOW)
        s_win = jnp.einsum('bqghd,bkgd->bghqk', qc, kwc) * scale
        p_win = masked_softmax(s_win, m_win).astype(vwc.dtype)
        o_win = jnp.einsum('bghqk,bkgd->bqghd', p_win, vwc)
        return o_sel, o_win

    o_sel, o_win = lax.map(chunk, jnp.arange(S_ // NSA_QC))
    o_sel = o_sel.transpose(1, 0, 2, 3, 4, 5).reshape(B_, S_, NSA_HEADS, dh)
    o_win = o_win.transpose(1, 0, 2, 3, 4, 5).reshape(B_, S_, NSA_HEADS, dh)
    y = gates[..., 0:1] * o_cmp + gates[..., 1:2] * o_sel + gates[..., 2:3] * o_win
    return y.reshape(B_, S_, NSA_HEADS * dh) @ w_out


def memory_cross_attention(h, mem_n, wq, wkv, wo):
    B_, S_, _ = h.shape
    q = (h @ wq).reshape(B_, S_, XA_HEADS, HEAD_DIM)
    k, v = jnp.split(mem_n @ wkv, 2, axis=-1)
    k = k.reshape(B_, MEM_LEN, XA_HEADS, HEAD_DIM)
    v = v.reshape(B_, MEM_LEN, XA_HEADS, HEAD_DIM)
    s = jnp.einsum('bshd,bmhd->bhsm', q, k) * (HEAD_DIM ** -0.5)
    p = jax.nn.softmax(s.astype(jnp.float32), axis=-1).astype(v.dtype)
    o = jnp.einsum('bhsm,bmhd->bshd', p, v).reshape(B_, S_, XA_DIM)
    return o @ wo


def moe_swiglu(h, router_w, wg, wu, wd):
    logits = (h @ router_w).astype(jnp.float32)
    top_val, top_idx = lax.top_k(logits, TOP_K)
    w = jax.nn.softmax(top_val, axis=-1)
    gate = jnp.sum(jax.nn.one_hot(top_idx, N_EXPERTS, dtype=jnp.float32) * w[..., None], axis=-2).astype(h.dtype)
    out = jnp.zeros_like(h)
    for e in range(N_EXPERTS):
        out = out + gate[..., e:e + 1] * swiglu(h, wg[e], wu[e], wd[e])
    return out


def setup_inputs(seed: int = 0) -> dict:
    key = jax.random.key(seed)
    ks = jax.random.split(key, 28)
    D = D_MODEL
    f32 = jnp.float32

    def nrm(k, shape, fan_in):
        return jax.random.normal(k, shape, f32) * (fan_in ** -0.5)

    return {
        'x': jax.random.normal(ks[0], (BATCH, SEQ, D), f32),
        'mem': jax.random.normal(ks[1], (BATCH, MEM_LEN, D), f32),
        'positions': jnp.arange(SEQ, dtype=jnp.int32),
        'norm_g': 1.0 + 0.02 * jax.random.normal(ks[2], (DEPTH, 3, D), f32),
        'mem_g': 1.0 + 0.02 * jax.random.normal(ks[3], (D,), f32),
        'final_g': 1.0 + 0.02 * jax.random.normal(ks[4], (D,), f32),
        'w_in_ab': nrm(ks[5], (N_EVEN, D, AB_IN), D),
        'pool_w': nrm(ks[6], (N_EVEN, POOL_GROUPS, POOL_GROUP_DIM, POOL_GROUP_DIM), POOL_GROUP_DIM),
        'pool_scale': 1.0 + 0.1 * jax.random.normal(ks[7], (N_EVEN, POOL_DIM), f32),
        'w_out_ab': nrm(ks[8], (N_EVEN, D, D), D),
        'ffn_w_gate': nrm(ks[9], (N_EVEN, D, D_FF), D),
        'ffn_w_up': nrm(ks[10], (N_EVEN, D, D_FF), D),
        'ffn_w_down': nrm(ks[11], (N_EVEN, D_FF, D), D_FF),
        'w_in_c': nrm(ks[12], (N_ODD, D, C_IN), D),
        'cmp_pe_k': 0.1 * jax.random.normal(ks[13], (N_ODD, CMP_LEN, HEAD_DIM), f32),
        'cmp_pe_v': 0.1 * jax.random.normal(ks[14], (N_ODD, CMP_LEN, HEAD_DIM), f32),
        'cmp_k_w1': nrm(ks[15], (N_ODD, CMP_LEN * HEAD_DIM, CMP_HIDDEN), CMP_LEN * HEAD_DIM),
        'cmp_k_w2': nrm(ks[16], (N_ODD, CMP_HIDDEN, HEAD_DIM), CMP_HIDDEN),
        'cmp_v_w1': nrm(ks[17], (N_ODD, CMP_LEN * HEAD_DIM, CMP_HIDDEN), CMP_LEN * HEAD_DIM),
        'cmp_v_w2': nrm(ks[18], (N_ODD, CMP_HIDDEN, HEAD_DIM), CMP_HIDDEN),
        'w_out_c': nrm(ks[19], (N_ODD, D, D), D),
        'router_w': nrm(ks[20], (N_ODD, D, N_EXPERTS), D),
        'moe_w_gate': nrm(ks[21], (N_ODD, N_EXPERTS, D, D_FF_EXPERT), D),
        'moe_w_up': nrm(ks[22], (N_ODD, N_EXPERTS, D, D_FF_EXPERT), D),
        'moe_w_down': nrm(ks[23], (N_ODD, N_EXPERTS, D_FF_EXPERT, D), D_FF_EXPERT),
        'xa_wq': nrm(ks[24], (DEPTH, D, XA_DIM), D),
        'xa_wkv': nrm(ks[25], (DEPTH, D, 2 * XA_DIM), D),
        'xa_wo': nrm(ks[26], (DEPTH, XA_DIM, D), XA_DIM),
    }


def reference(x, mem, positions, norm_g, mem_g, final_g, w_in_ab, pool_w, pool_scale, w_out_ab,
              ffn_w_gate, ffn_w_up, ffn_w_down, w_in_c, cmp_pe_k, cmp_pe_v, cmp_k_w1, cmp_k_w2,
              cmp_v_w1, cmp_v_w2, w_out_c, router_w, moe_w_gate, moe_w_up, moe_w_down,
              xa_wq, xa_wkv, xa_wo):
    cos, sin = rope_tables(positions)
    mem_n = rmsnorm(mem, mem_g)
    for layer in range(DEPTH):
        i = layer // 2
        h = rmsnorm(x, norm_g[layer, 0])
        if layer % 2 == 0:
            x = x + pool_moba_mixer(h, w_in_ab[i], pool_w[i], pool_scale[i], w_out_ab[i], cos, sin)
        else:
            x = x + nsa_mixer(h, w_in_c[i], cmp_pe_k[i], cmp_pe_v[i], cmp_k_w1[i], cmp_k_w2[i],
                              cmp_v_w1[i], cmp_v_w2[i], w_out_c[i], cos, sin)
        h = rmsnorm(x, norm_g[layer, 1])
        x = x + memory_cross_attention(h, mem_n, xa_wq[layer], xa_wkv[layer], xa_wo[layer])
        h = rmsnorm(x, norm_g[layer, 2])
        if layer % 2 == 0:
            x = x + swiglu(h, ffn_w_gate[i], ffn_w_up[i], ffn_w_down[i])
        else:
            x = x + moe_swiglu(h, router_w[i], moe_w_gate[i], moe_w_up[i], moe_w_down[i])
    return rmsnorm(x, final_g)
```

```python
import functools
import math

import numpy as np
import jax
import jax.numpy as jnp
from jax import lax
from jax.experimental import pallas as pl
from jax.experimental.pallas import tpu as pltpu

F32 = jnp.float32
BF16 = jnp.bfloat16

HEAD_DIM = 64
ROPE_DIM = 16
ROPE_THETA = 500000.0
NORM_EPS = 1e-5
QK_SCALE = HEAD_DIM ** -0.5

POOL_WINDOWS = (2, 4, 8, 16)
POOL_DIM = 256
MOBA_HEADS = 12
MOBA_DIM = MOBA_HEADS * HEAD_DIM
MOBA_BLOCK = 256
MOBA_TOPK = 3
NSA_HEADS = 16
NSA_GROUPS = 4
NSA_HPG = 4
NSA_KV_DIM = NSA_GROUPS * HEAD_DIM
CMP_LEN = 32
CMP_STRIDE = 16
SLC_BLOCK = 64
SLC_TOPN = 16
WINDOW = 512
FORCE_SCORE = 1e4
XA_HEADS = 4
XA_DIM = XA_HEADS * HEAD_DIM
N_EXPERTS = 8

LANES = 128
ATT_TILE = 256
VMEM_LIMIT = 56 * 1024 * 1024
NEG_INF = float("-inf")


def _cparams(*sem):
    return pltpu.CompilerParams(dimension_semantics=sem, vmem_limit_bytes=VMEM_LIMIT)


def _dot(a, b):
    return jnp.dot(a, b, preferred_element_type=F32)


def _dot_nt(a, b):
    return lax.dot_general(a, b, (((1,), (1,)), ((), ())), preferred_element_type=F32)


def _split_bf16(a):
    hi = a.astype(BF16)
    lo = (a - hi.astype(F32)).astype(BF16)
    return hi, lo


def _rms_bf16(x, g):
    ms = jnp.mean(x * x, axis=-1, keepdims=True)
    return (x * lax.rsqrt(ms + NORM_EPS) * g).astype(BF16)


def _apply_rope(acc, c, a, b):
    half = ROPE_DIM // 2
    return acc * c + pltpu.roll(acc, LANES - half, axis=1) * a + pltpu.roll(acc, half, axis=1) * b


def _proj_kernel(plan, n_tab, x_ref, g_ref, w_ref, *rest):
    tabs = rest[:3 * n_tab]
    outs = rest[3 * n_tab:]
    h = _rms_bf16(x_ref[...], g_ref[...])
    for wc0, width, sinks in plan:
        acc = _dot(h, w_ref[:, wc0:wc0 + width])
        for oi, oc0, op, tab in sinks:
            for s in range(width // LANES):
                val = acc[:, s * LANES:(s + 1) * LANES]
                if op in ("rope", "rope_scale"):
                    c, a, b = (tabs[3 * tab + i][...] for i in range(3))
                    val = _apply_rope(val, c, a, b)
                if op in ("scale", "rope_scale"):
                    val = val * QK_SCALE
                if op == "sigmoid":
                    val = jax.nn.sigmoid(val)
                o = outs[oi]
                o[:, oc0 + s * LANES:oc0 + (s + 1) * LANES] = val.astype(o.dtype)


def _norm_project(x, g, w, plan, tables, out_defs, seq, tm=512):
    t, d = x.shape
    n = w.shape[1]
    n_seq_tiles = seq // tm
    in_specs = [
        pl.BlockSpec((tm, d), lambda i: (i, 0)),
        pl.BlockSpec((1, d), lambda i: (0, 0)),
        pl.BlockSpec((d, n), lambda i: (0, 0)),
    ]
    flat_tabs = []
    for tset in tables:
        for tb in tset:
            flat_tabs.append(tb)
            in_specs.append(pl.BlockSpec((tm, LANES), lambda i: (i % n_seq_tiles, 0)))
    out_shape = [jax.ShapeDtypeStruct((t, wd), dt) for wd, dt in out_defs]
    out_specs = [pl.BlockSpec((tm, wd), lambda i: (i, 0)) for wd, _ in out_defs]
    return pl.pallas_call(
        functools.partial(_proj_kernel, plan, len(tables)),
        grid=(t // tm,),
        in_specs=in_specs,
        out_specs=out_specs,
        out_shape=out_shape,
        compiler_params=_cparams("parallel"),
    )(x, g.reshape(1, d), w, *flat_tabs)


def _rope_tables(positions):
    half = ROPE_DIM // 2
    inv = ROPE_THETA ** (-jnp.arange(0, ROPE_DIM, 2, dtype=F32) / ROPE_DIM)
    ang = positions.astype(F32)[:, None] * inv[None, :]
    cos, sin = jnp.cos(ang), jnp.sin(ang)
    s = positions.shape[0]
    z8 = jnp.zeros((s, half), F32)
    rest0 = jnp.zeros((s, HEAD_DIM - ROPE_DIM), F32)
    rest1 = jnp.ones((s, HEAD_DIM - ROPE_DIM), F32)
    c64 = jnp.concatenate([cos, cos, rest1], -1)
    a64 = jnp.concatenate([-sin, z8, rest0], -1)
    b64 = jnp.concatenate([z8, sin, rest0], -1)
    one64 = jnp.ones((s, HEAD_DIM), F32)
    zero64 = jnp.zeros((s, HEAD_DIM), F32)
    pair = tuple(jnp.concatenate([m, m], -1) for m in (c64, a64, b64))
    single = (jnp.concatenate([c64, one64], -1), jnp.concatenate([a64, zero64], -1),
              jnp.concatenate([b64, zero64], -1))
    return pair, single


def _norm_matmul_kernel(x_ref, g_ref, w_ref, o_ref):
    h = _rms_bf16(x_ref[...], g_ref[...])
    o_ref[...] = _dot(h, w_ref[...]).astype(o_ref.dtype)


def _norm_matmul(x, g, w, tm=512):
    t, d = x.shape
    n = w.shape[1]
    return pl.pallas_call(
        _norm_matmul_kernel,
        grid=(t // tm,),
        in_specs=[pl.BlockSpec((tm, d), lambda i: (i, 0)),
                  pl.BlockSpec((1, d), lambda i: (0, 0)),
                  pl.BlockSpec((d, n), lambda i: (0, 0))],
        out_specs=pl.BlockSpec((tm, n), lambda i: (i, 0)),
        out_shape=jax.ShapeDtypeStruct((t, n), BF16),
        compiler_params=_cparams("parallel"),
    )(x, g.reshape(1, d), w)


def _final_norm_kernel(x_ref, g_ref, o_ref):
    x = x_ref[...]
    ms = jnp.mean(x * x, axis=-1, keepdims=True)
    o_ref[...] = x * lax.rsqrt(ms + NORM_EPS) * g_ref[...]


def _final_norm(x, g, tm=1024):
    t, d = x.shape
    return pl.pallas_call(
        _final_norm_kernel,
        grid=(t // tm,),
        in_specs=[pl.BlockSpec((tm, d), lambda i: (i, 0)), pl.BlockSpec((1, d), lambda i: (0, 0))],
        out_specs=pl.BlockSpec((tm, d), lambda i: (i, 0)),
        out_shape=jax.ShapeDtypeStruct((t, d), F32),
        compiler_params=_cparams("parallel"),
    )(x, g.reshape(1, d))


def _out_proj_kernel(n_in, x_ref, *rest):
    ys = rest[:n_in]
    ws = rest[n_in:2 * n_in]
    o_ref = rest[2 * n_in]
    acc = x_ref[...]
    for y, w in zip(ys, ws):
        acc = acc + _dot(y[...], w[...])
    o_ref[...] = acc


def _out_proj(x, ys, ws, tm=512):
    t, d = x.shape
    in_specs = [pl.BlockSpec((tm, d), lambda i: (i, 0))]
    in_specs += [pl.BlockSpec((tm, y.shape[1]), lambda i: (i, 0)) for y in ys]
    in_specs += [pl.BlockSpec(w.shape, lambda i: (0, 0)) for w in ws]
    return pl.pallas_call(
        functools.partial(_out_proj_kernel, len(ys)),
        grid=(t // tm,),
        in_specs=in_specs,
        out_specs=pl.BlockSpec((tm, d), lambda i: (i, 0)),
        out_shape=jax.ShapeDtypeStruct((t, d), F32),
        compiler_params=_cparams("parallel"),
    )(x, *ys, *ws)


def _pool_kernel(u_ref, w_ref, sc_ref, o_ref, pad_ref):
    s = u_ref.shape[0]
    maxw = POOL_WINDOWS[-1]
    u = u_ref[...]
    pad_ref[0:maxw, :] = jnp.zeros((maxw, POOL_DIM), F32)
    t1 = (lax.broadcasted_iota(jnp.int32, (s, POOL_DIM), 0) + 1).astype(F32)
    lane = lax.broadcasted_iota(jnp.int32, (s, POOL_DIM), 1)
    gdim = POOL_DIM // len(POOL_WINDOWS)
    cur = u
    pooled = jnp.zeros_like(u)
    shift = 1
    for gi, w in enumerate(POOL_WINDOWS):
        pad_ref[maxw:maxw + s, :] = cur
        cur = cur + pad_ref[maxw - shift:maxw - shift + s, :]
        shift *= 2
        assert shift == w
        mean = cur / jnp.minimum(t1, float(w))
        pooled = jnp.where((lane >= gi * gdim) & (lane < (gi + 1) * gdim), mean, pooled)
    pooled = (pooled - u).astype(BF16)
    o_ref[...] = (_dot(pooled, w_ref[...]) * sc_ref[...]).astype(o_ref.dtype)


def _pool_mixer(u_pool, w_bd, scale, batch, seq):
    return pl.pallas_call(
        _pool_kernel,
        grid=(batch,),
        in_specs=[pl.BlockSpec((seq, POOL_DIM), lambda b: (b, 0)),
                  pl.BlockSpec((POOL_DIM, POOL_DIM), lambda b: (0, 0)),
                  pl.BlockSpec((1, POOL_DIM), lambda b: (0, 0))],
        out_specs=pl.BlockSpec((seq, POOL_DIM), lambda b: (b, 0)),
        out_shape=jax.ShapeDtypeStruct((batch * seq, POOL_DIM), BF16),
        scratch_shapes=[pltpu.VMEM((seq + POOL_WINDOWS[-1], POOL_DIM), F32)],
        compiler_params=_cparams("parallel"),
    )(u_pool, w_bd, scale.reshape(1, POOL_DIM))


def _rank_select(scoreT, valid, n_rows, topn):
    row = lax.broadcasted_iota(jnp.int32, scoreT.shape, 0)
    rank = jnp.zeros(scoreT.shape, F32)
    for j in range(n_rows):
        sj = scoreT[j:j + 1, :]
        beats = (scoreT > sj) | ((scoreT == sj) & (row < j))
        rj = jnp.sum(beats.astype(F32), axis=0, keepdims=True)
        rank = jnp.where(row == j, rj, rank)
    return (rank < float(topn)) & valid


def _to_query_rows(selT):
    n, tq = selT.shape
    padded = jnp.concatenate([selT, jnp.zeros((LANES - n, tq), F32)], axis=0)
    return padded.T


def _flash_init(h, s, v, m_ref, l_ref, acc_ref):
    m = jnp.max(s, axis=1, keepdims=True)
    p = jnp.exp(s - m)
    m_ref[h] = m
    l_ref[h] = jnp.sum(p, axis=1, keepdims=True)
    acc_ref[h] = _dot(p.astype(BF16), v)


def _flash_update(h, s, v, m_ref, l_ref, acc_ref):
    m_prev = m_ref[h]
    m_new = jnp.maximum(m_prev, jnp.max(s, axis=1, keepdims=True))
    alpha = jnp.exp(m_prev - m_new)
    p = jnp.exp(s - m_new)
    l_ref[h] = alpha * l_ref[h] + jnp.sum(p, axis=1, keepdims=True)
    acc_ref[h] = alpha * acc_ref[h] + _dot(p.astype(BF16), v)
    m_ref[h] = m_new


def _moba_kernel(q_ref, k_ref, v_ref, o_ref, kmean_ref, m_ref, l_ref, acc_ref):
    qi = pl.program_id(2)
    tq = ATT_TILE
    nb = k_ref.shape[0] // MOBA_BLOCK

    @pl.when(qi == 0)
    def _():
        kmean_ref[...] = jnp.zeros(kmean_ref.shape, F32)
        for j in range(nb):
            blk = k_ref[j * MOBA_BLOCK:(j + 1) * MOBA_BLOCK, :].astype(F32)
            kmean_ref[j:j + 1, :] = jnp.mean(blk, axis=0, keepdims=True)

    row = lax.broadcasted_iota(jnp.int32, (tq, tq), 0)
    col = lax.broadcasted_iota(jnp.int32, (tq, tq), 1)
    causal = col <= row
    cand = lax.broadcasted_iota(jnp.int32, (nb, tq), 0)
    valid = cand < qi
    own = pl.multiple_of(qi * tq, tq)

    for h in range(2):
        sl = slice(h * HEAD_DIM, (h + 1) * HEAD_DIM)
        q = q_ref[:, sl]
        km_hi, km_lo = _split_bf16(kmean_ref[0:16, sl])
        gate = (_dot_nt(km_hi, q) + _dot_nt(km_lo, q))[0:nb]
        gate = jnp.where(valid, gate, NEG_INF)
        selT = _rank_select(gate, valid, nb, MOBA_TOPK)
        sel = _to_query_rows(selT.astype(F32))

        s = _dot_nt(q, k_ref[pl.ds(own, tq), sl])
        s = jnp.where(causal, s, NEG_INF)
        _flash_init(h, s, v_ref[pl.ds(own, tq), sl], m_ref, l_ref, acc_ref)

        for j in range(nb - 1):
            @pl.when(j < qi)
            def _(j=j, h=h, sl=sl, q=q, sel=sel):
                picked = sel[:, j:j + 1] > 0.5
                sj = _dot_nt(q, k_ref[j * tq:(j + 1) * tq, sl])
                sj = jnp.where(picked, sj, NEG_INF)
                _flash_update(h, sj, v_ref[j * tq:(j + 1) * tq, sl], m_ref, l_ref, acc_ref)

        o_ref[:, sl] = (acc_ref[h] / l_ref[h]).astype(o_ref.dtype)


def _moba_attention(qkv, batch, seq):
    assert MOBA_BLOCK == ATT_TILE
    nq = seq // ATT_TILE
    hp = MOBA_DIM // LANES
    return pl.pallas_call(
        _moba_kernel,
        grid=(batch, hp, nq),
        in_specs=[pl.BlockSpec((ATT_TILE, LANES), lambda b, p, i: (b * nq + i, p)),
                  pl.BlockSpec((seq, LANES), lambda b, p, i: (b, hp + p)),
                  pl.BlockSpec((seq, LANES), lambda b, p, i: (b, 2 * hp + p))],
        out_specs=pl.BlockSpec((ATT_TILE, LANES), lambda b, p, i: (b * nq + i, p)),
        out_shape=jax.ShapeDtypeStruct((batch * seq, MOBA_DIM), BF16),
        scratch_shapes=[pltpu.VMEM((16, LANES), F32),
                        pltpu.VMEM((2, ATT_TILE, 1), F32),
                        pltpu.VMEM((2, ATT_TILE, 1), F32),
                        pltpu.VMEM((2, ATT_TILE, HEAD_DIM), F32)],
        compiler_params=_cparams("parallel", "parallel", "arbitrary"),
    )(qkv, qkv, qkv)


def _xattn_kernel(x_ref, g_ref, wq_ref, kv_ref, wo_ref, o_ref):
    x = x_ref[...]
    h = _rms_bf16(x, g_ref[...])
    q = (_dot(h, wq_ref[...]) * QK_SCALE).astype(BF16)
    out = x
    for hd in range(XA_HEADS):
        sl = slice(hd * HEAD_DIM, (hd + 1) * HEAD_DIM)
        k = kv_ref[:, sl]
        v = kv_ref[:, XA_DIM + hd * HEAD_DIM:XA_DIM + (hd + 1) * HEAD_DIM]
        s = _dot_nt(q[:, sl], k)
        m = jnp.max(s, axis=1, keepdims=True)
        p = jnp.exp(s - m)
        l = jnp.sum(p, axis=1, keepdims=True)
        o = _dot(p.astype(BF16), v) / l
        out = out + _dot(o.astype(BF16), wo_ref[sl, :])
    o_ref[...] = out


def _cross_attention(x, g, wq, kv_all, layer, wo, seq, mem_len, tm=512):
    t, d = x.shape
    tiles_per_seq = seq // tm
    return pl.pallas_call(
        _xattn_kernel,
        grid=(t // tm,),
        in_specs=[pl.BlockSpec((tm, d), lambda i: (i, 0)),
                  pl.BlockSpec((1, d), lambda i: (0, 0)),
                  pl.BlockSpec((d, XA_DIM), lambda i: (0, 0)),
                  pl.BlockSpec((mem_len, 2 * XA_DIM), lambda i: (i // tiles_per_seq, layer)),
                  pl.BlockSpec((XA_DIM, d), lambda i: (0, 0))],
        out_specs=pl.BlockSpec((tm, d), lambda i: (i, 0)),
        out_shape=jax.ShapeDtypeStruct((t, d), F32),
        compiler_params=_cparams("parallel"),
    )(x, g.reshape(1, d), wq, kv_all, wo)


def _ffn_kernel(x_ref, g_ref, wg_ref, wu_ref, wd_ref, o_ref, h_ref):
    f = pl.program_id(1)

    @pl.when(f == 0)
    def _():
        x = x_ref[...]
        h_ref[...] = _rms_bf16(x, g_ref[...])
        o_ref[...] = x

    h = h_ref[...]
    gate = _dot(h, wg_ref[...])
    a = (gate * jax.nn.sigmoid(gate)) * _dot(h, wu_ref[...])
    o_ref[...] += _dot(a.astype(BF16), wd_ref[...])


def _ffn(x, g, wg, wu, wd, tm=1024, tf=256):
    t, d = x.shape
    ff = wg.shape[1]
    return pl.pallas_call(
        _ffn_kernel,
        grid=(t // tm, ff // tf),
        in_specs=[pl.BlockSpec((tm, d), lambda i, f: (i, 0)),
                  pl.BlockSpec((1, d), lambda i, f: (0, 0)),
                  pl.BlockSpec((d, tf), lambda i, f: (0, f)),
                  pl.BlockSpec((d, tf), lambda i, f: (0, f)),
                  pl.BlockSpec((tf, d), lambda i, f: (f, 0))],
        out_specs=pl.BlockSpec((tm, d), lambda i, f: (i, 0)),
        out_shape=jax.ShapeDtypeStruct((t, d), F32),
        scratch_shapes=[pltpu.VMEM((tm, d), BF16)],
        compiler_params=_cparams("parallel", "arbitrary"),
    )(x, g.reshape(1, d), wg, wu, wd)


def _moe_kernel(x_ref, g_ref, rhi_ref, rlo_ref, wg_ref, wu_ref, wd_ref, o_ref, h_ref, gate_ref):
    e = pl.program_id(1)
    f = pl.program_id(2)
    tm = x_ref.shape[0]
    lane = lax.broadcasted_iota(jnp.int32, (tm, LANES), 1)

    @pl.when((e == 0) & (f == 0))
    def _():
        x = x_ref[...]
        h = _rms_bf16(x, g_ref[...])
        h_ref[...] = h
        o_ref[...] = x
        logits = _dot(h, rhi_ref[...]) + _dot(h, rlo_ref[...])
        logits = jnp.where(lane < N_EXPERTS, logits, NEG_INF)
        m1 = jnp.max(logits, axis=1, keepdims=True)
        lane_f = lane.astype(F32)
        i1 = jnp.min(jnp.where(logits == m1, lane_f, float(LANES)), axis=1, keepdims=True)
        rest = jnp.where(lane_f == i1, NEG_INF, logits)
        m2 = jnp.max(rest, axis=1, keepdims=True)
        i2 = jnp.min(jnp.where(rest == m2, lane_f, float(LANES)), axis=1, keepdims=True)
        e2 = jnp.exp(m2 - m1)
        w1 = 1.0 / (1.0 + e2)
        w2 = e2 / (1.0 + e2)
        gate_ref[...] = jnp.where(lane_f == i1, w1, 0.0) + jnp.where(lane_f == i2, w2, 0.0)

    h = h_ref[...]
    gate = _dot(h, wg_ref[...])
    a = (gate * jax.nn.sigmoid(gate)) * _dot(h, wu_ref[...])
    ge = jnp.sum(jnp.where(lane == e, gate_ref[...], 0.0), axis=1, keepdims=True)
    o_ref[...] += _dot((a * ge).astype(BF16), wd_ref[...])


def _moe(x, g, router_w, wg, wu, wd, tm=1024, tf=256):
    t, d = x.shape
    ne, _, ff = wg.shape
    rw = jnp.zeros((d, LANES), F32).at[:, :ne].set(router_w)
    rhi = rw.astype(BF16)
    rlo = (rw - rhi.astype(F32)).astype(BF16)
    return pl.pallas_call(
        _moe_kernel,
        grid=(t // tm, ne, ff // tf),
        in_specs=[pl.BlockSpec((tm, d), lambda i, e, f: (i, 0)),
                  pl.BlockSpec((1, d), lambda i, e, f: (0, 0)),
                  pl.BlockSpec((d, LANES), lambda i, e, f: (0, 0)),
                  pl.BlockSpec((d, LANES), lambda i, e, f: (0, 0)),
                  pl.BlockSpec((None, d, tf), lambda i, e, f: (e, 0, f)),
                  pl.BlockSpec((None, d, tf), lambda i, e, f: (e, 0, f)),
                  pl.BlockSpec((None, tf, d), lambda i, e, f: (e, f, 0))],
        out_specs=pl.BlockSpec((tm, d), lambda i, e, f: (i, 0)),
        out_shape=jax.ShapeDtypeStruct((t, d), F32),
        scratch_shapes=[pltpu.VMEM((tm, d), BF16), pltpu.VMEM((tm, LANES), F32)],
        compiler_params=_cparams("parallel", "arbitrary", "arbitrary"),
    )(x, g.reshape(1, d), rhi, rlo, wg, wu, wd)


def _gelu_tanh(x):
    return 0.5 * x * (1.0 + jnp.tanh(math.sqrt(2.0 / math.pi) * (x + 0.044715 * (x * x * x))))


def _compress_kernel(x_ref, pe_ref, w1_ref, w2_ref, o_ref, sh_ref):
    rows = x_ref.shape[0]
    half = w1_ref.shape[0] // 2
    x = x_ref[...]
    top = _dot(x, w1_ref[0:half, :])
    bot = _dot(x, w1_ref[half:2 * half, :])
    bias = _dot(pe_ref[...], w1_ref[...])[0:1, :]
    sh_ref[0:rows, :] = bot
    sh_ref[rows:rows + 8, :] = jnp.zeros((8, bot.shape[1]), F32)
    hidden = top + sh_ref[1:rows + 1, :] + bias
    o_ref[...] = _dot(_gelu_tanh(hidden).astype(BF16), w2_ref[...]).astype(o_ref.dtype)


def _compress(chunks, pe, w1, w2, rows=1024):
    _, r, cw = chunks.shape
    hid = w1.shape[2]
    return pl.pallas_call(
        _compress_kernel,
        grid=(2, r // rows),
        in_specs=[pl.BlockSpec((None, rows, cw), lambda a, i: (a, i, 0)),
                  pl.BlockSpec((None, 16, 2 * cw), lambda a, i: (a, 0, 0)),
                  pl.BlockSpec((None, 2 * cw, hid), lambda a, i: (a, 0, 0)),
                  pl.BlockSpec((None, hid, HEAD_DIM), lambda a, i: (a, 0, 0))],
        out_specs=pl.BlockSpec((None, rows, HEAD_DIM), lambda a, i: (a, i, 0)),
        out_shape=jax.ShapeDtypeStruct((2, r, HEAD_DIM), BF16),
        scratch_shapes=[pltpu.VMEM((rows + 8, hid), F32)],
        compiler_params=_cparams("parallel", "parallel"),
    )(chunks, pe, w1, w2)


def _nsa_kernel(qraw_ref, qrot_ref, kc_ref, vc_ref, ksel_ref, kwin_ref, gate_ref, ex_ref, ovt_ref, et_ref,
                o_ref, m_ref, l_ref, acc_ref, obr_ref):
    qi = pl.program_id(2)
    tq = ATT_TILE
    n_slc = ksel_ref.shape[0] // SLC_BLOCK
    hsl = [slice(h * HEAD_DIM, (h + 1) * HEAD_DIM) for h in range(NSA_HPG)]
    ksl = slice(0, HEAD_DIM)
    vsl = slice(HEAD_DIM, 2 * HEAD_DIM)
    row = lax.broadcasted_iota(jnp.int32, (tq, tq), 0)
    col = lax.broadcasted_iota(jnp.int32, (tq, tq), 1)
    causal = col <= row
    own = pl.multiple_of(qi * tq, tq)

    n_cmp_pad = kc_ref.shape[0]
    t_q = qi * tq + lax.broadcasted_iota(jnp.int32, (tq, n_cmp_pad), 0)
    n_c = lax.broadcasted_iota(jnp.int32, (tq, n_cmp_pad), 1)
    mask_c = (n_c * CMP_STRIDE + (CMP_LEN - 1)) <= t_q
    kc = kc_ref[...]
    vc = vc_ref[...]
    psum = jnp.zeros((tq, n_cmp_pad), F32)
    for h in range(NSA_HPG):
        s = jnp.where(mask_c, _dot_nt(qraw_ref[:, hsl[h]], kc), NEG_INF)
        m = jnp.max(s, axis=1, keepdims=True)
        m = jnp.where(m == NEG_INF, 0.0, m)
        e = jnp.exp(s - m)
        p = e / jnp.maximum(jnp.sum(e, axis=1, keepdims=True), 1e-30)
        psum = psum + p
        obr_ref[0, :, hsl[h]] = _dot(p.astype(BF16), vc)

    ps_hi, ps_lo = _split_bf16(psum)
    ovt = ovt_ref[...]
    p_slc = (_dot_nt(ovt, ps_hi) + _dot_nt(ovt, ps_lo))[0:n_slc]
    cand = lax.broadcasted_iota(jnp.int32, (n_slc, tq), 0)
    assert SLC_BLOCK == 64
    t_blk = jnp.right_shift(qi * tq + lax.broadcasted_iota(jnp.int32, (n_slc, tq), 1), 6)
    valid = cand <= t_blk
    forced = (cand == 0) | (cand == t_blk) | (cand == t_blk - 1)
    score = jnp.where(valid, jnp.where(forced, FORCE_SCORE, p_slc), -1.0)
    selT = _rank_select(score, valid, n_slc, min(SLC_TOPN, n_slc))
    sel = _to_query_rows(selT.astype(F32)).astype(BF16)

    k_own = ksel_ref[pl.ds(own, tq), ksl]
    v_own = ksel_ref[pl.ds(own, tq), vsl]
    picked = (_dot_nt(sel, et_ref[pl.ds(own, tq), :]) > 0.5) & causal
    for h in range(NSA_HPG):
        s = jnp.where(picked, _dot_nt(qrot_ref[:, hsl[h]], k_own), NEG_INF)
        _flash_init(h, s, v_own, m_ref, l_ref, acc_ref)

    def sel_body(j, carry):
        off = pl.multiple_of(j * tq, tq)
        kj = ksel_ref[pl.ds(off, tq), ksl]
        vj = ksel_ref[pl.ds(off, tq), vsl]
        pk = _dot_nt(sel, et_ref[pl.ds(off, tq), :]) > 0.5
        for h in range(NSA_HPG):
            s = jnp.where(pk, _dot_nt(qrot_ref[:, hsl[h]], kj), NEG_INF)
            _flash_update(h, s, vj, m_ref, l_ref, acc_ref)
        return carry

    lax.fori_loop(0, qi, sel_body, 0)
    for h in range(NSA_HPG):
        obr_ref[1, :, hsl[h]] = acc_ref[h] / l_ref[h]

    kw_own = kwin_ref[pl.ds(own, tq), ksl]
    vw_own = kwin_ref[pl.ds(own, tq), vsl]
    for h in range(NSA_HPG):
        s = jnp.where(causal, _dot_nt(qrot_ref[:, hsl[h]], kw_own), NEG_INF)
        _flash_init(h, s, vw_own, m_ref, l_ref, acc_ref)
    assert WINDOW == 2 * ATT_TILE

    @pl.when(qi >= 1)
    def _():
        off = pl.multiple_of((qi - 1) * tq, tq)
        kj = kwin_ref[pl.ds(off, tq), ksl]
        vj = kwin_ref[pl.ds(off, tq), vsl]
        for h in range(NSA_HPG):
            _flash_update(h, _dot_nt(qrot_ref[:, hsl[h]], kj), vj, m_ref, l_ref, acc_ref)

    @pl.when(qi >= 2)
    def _():
        off = pl.multiple_of((qi - 2) * tq, tq)
        kj = kwin_ref[pl.ds(off, tq), ksl]
        vj = kwin_ref[pl.ds(off, tq), vsl]
        in_win = col > row
        for h in range(NSA_HPG):
            s = jnp.where(in_win, _dot_nt(qrot_ref[:, hsl[h]], kj), NEG_INF)
            _flash_update(h, s, vj, m_ref, l_ref, acc_ref)

    for h in range(NSA_HPG):
        obr_ref[2, :, hsl[h]] = acc_ref[h] / l_ref[h]

    g_hi, g_lo = _split_bf16(gate_ref[...])
    ex = ex_ref[...]
    gexp = _dot(g_hi, ex) + _dot(g_lo, ex)
    gw = NSA_HPG * HEAD_DIM
    y = gexp[:, 0:gw] * obr_ref[0] + gexp[:, gw:2 * gw] * obr_ref[1] + gexp[:, 2 * gw:3 * gw] * obr_ref[2]
    o_ref[...] = y.astype(o_ref.dtype)


def _nsa_attention(q_raw, q_rot, kv, cmp_kv, gates, batch, seq):
    nq = seq // ATT_TILE
    g = NSA_GROUPS
    gw = NSA_HPG * HEAD_DIM
    n_cmp = (seq - CMP_LEN) // CMP_STRIDE + 1
    n_cmp_pad = seq // CMP_STRIDE
    n_slc = seq // SLC_BLOCK
    assert n_cmp_pad == LANES and n_slc <= LANES

    ex = np.zeros((g, LANES, 3 * gw), np.float32)
    for gi in range(g):
        for h in range(NSA_HPG):
            for br in range(3):
                ex[gi, (gi * NSA_HPG + h) * 3 + br, br * gw + h * HEAD_DIM:br * gw + (h + 1) * HEAD_DIM] = 1.0
    c_s = np.arange(n_cmp) * CMP_STRIDE
    s_s = np.arange(n_slc) * SLC_BLOCK
    ov = np.clip(np.minimum(c_s[:, None] + CMP_LEN, s_s[None, :] + SLC_BLOCK)
                 - np.maximum(c_s[:, None], s_s[None, :]), 0, None) / CMP_LEN
    ovt = np.zeros((LANES, n_cmp_pad), np.float32)
    ovt[:n_slc, :n_cmp] = ov.T
    et = np.zeros((seq, LANES), np.float32)
    et[np.arange(seq), np.arange(seq) // SLC_BLOCK] = 1.0

    return pl.pallas_call(
        _nsa_kernel,
        grid=(batch, g, nq),
        in_specs=[pl.BlockSpec((ATT_TILE, gw), lambda b, gi, i: (b * nq + i, gi)),
                  pl.BlockSpec((ATT_TILE, gw), lambda b, gi, i: (b * nq + i, gi)),
                  pl.BlockSpec((None, n_cmp_pad, HEAD_DIM), lambda b, gi, i: (0, b * g + gi, 0)),
                  pl.BlockSpec((None, n_cmp_pad, HEAD_DIM), lambda b, gi, i: (1, b * g + gi, 0)),
                  pl.BlockSpec((seq, LANES), lambda b, gi, i: (b, g + gi)),
                  pl.BlockSpec((seq, LANES), lambda b, gi, i: (b, 2 * g + gi)),
                  pl.BlockSpec((ATT_TILE, LANES), lambda b, gi, i: (b * nq + i, 0)),
                  pl.BlockSpec((None, LANES, 3 * gw), lambda b, gi, i: (gi, 0, 0)),
                  pl.BlockSpec((LANES, n_cmp_pad), lambda b, gi, i: (0, 0)),
                  pl.BlockSpec((seq, LANES), lambda b, gi, i: (0, 0))],
        out_specs=pl.BlockSpec((ATT_TILE, gw), lambda b, gi, i: (b * nq + i, gi)),
        out_shape=jax.ShapeDtypeStruct((batch * seq, NSA_HEADS * HEAD_DIM), BF16),
        scratch_shapes=[pltpu.VMEM((NSA_HPG, ATT_TILE, 1), F32),
                        pltpu.VMEM((NSA_HPG, ATT_TILE, 1), F32),
                        pltpu.VMEM((NSA_HPG, ATT_TILE, HEAD_DIM), F32),
                        pltpu.VMEM((3, ATT_TILE, gw), F32)],
        compiler_params=_cparams("parallel", "parallel", "arbitrary"),
    )(q_raw, q_rot, cmp_kv, cmp_kv, kv, kv, gates,
      jnp.asarray(ex, BF16), jnp.asarray(ovt, BF16), jnp.asarray(et, BF16))


def _even_mixer(x, g, w_in, pool_w, pool_scale, w_out, rope_pair, batch, seq):
    chunk = 2 * LANES
    plan = [(0, POOL_DIM, [(0, 0, "none", 0)])]
    for c in range(3 * MOBA_DIM // chunk):
        op = "rope_scale" if c < MOBA_DIM // chunk else ("rope" if c < 2 * MOBA_DIM // chunk else "none")
        plan.append((POOL_DIM + c * chunk, chunk, [(1, c * chunk, op, 0)]))
    u_pool, qkv = _norm_project(x, g, w_in.astype(BF16), plan, [rope_pair],
                                [(POOL_DIM, F32), (3 * MOBA_DIM, BF16)], seq)
    w_bd = jax.scipy.linalg.block_diag(*[pool_w[i] for i in range(pool_w.shape[0])]).astype(BF16)
    y_a = _pool_mixer(u_pool, w_bd, pool_scale, batch, seq)
    y_b = _moba_attention(qkv, batch, seq)
    w_out = w_out.astype(BF16)
    return _out_proj(x, [y_a, y_b], [w_out[:POOL_DIM], w_out[POOL_DIM:]])


def _odd_mixer(x, g, w_in, pe_k, pe_v, k_w1, k_w2, v_w1, v_w2, w_out, rope_pair, rope_single, batch, seq):
    d = x.shape[1]
    qd = NSA_HEADS * HEAD_DIM
    kvd = NSA_KV_DIM
    wq = w_in[:, :qd]
    parts = [w_in[:, qd + i * kvd:qd + (i + 1) * kvd].reshape(d, NSA_GROUPS, HEAD_DIM) for i in range(6)]
    pairs = [jnp.concatenate([parts[2 * i], parts[2 * i + 1]], axis=-1).reshape(d, 2 * kvd) for i in range(3)]
    n_gate = 3 * NSA_HEADS
    w_gate = jnp.zeros((d, LANES), F32).at[:, :n_gate].set(w_in[:, qd + 6 * kvd:])
    w_all = jnp.concatenate([wq] + pairs + [w_gate], axis=1).astype(BF16)

    chunk = 2 * LANES
    plan = []
    for c in range(qd // chunk):
        plan.append((c * chunk, chunk, [(0, c * chunk, "scale", 0), (1, c * chunk, "rope_scale", 0)]))
    for c in range(6 * kvd // chunk):
        branch = c // (2 * kvd // chunk)
        op = "none" if branch == 0 else "rope"
        plan.append((qd + c * chunk, chunk, [(2, c * chunk, op, 1)]))
    plan.append((qd + 6 * kvd, LANES, [(3, 0, "sigmoid", 0)]))
    q_raw, q_rot, kv, gates = _norm_project(
        x, g, w_all, plan, [rope_pair, rope_single],
        [(qd, BF16), (qd, BF16), (6 * kvd, BF16), (LANES, F32)], seq)

    n_chunks = seq // CMP_STRIDE
    cmp_in = kv[:, :2 * kvd].reshape(batch, n_chunks, CMP_STRIDE, NSA_GROUPS, 2, HEAD_DIM)
    cmp_in = cmp_in.transpose(4, 0, 3, 1, 2, 5).reshape(2, batch * NSA_GROUPS * n_chunks, CMP_STRIDE * HEAD_DIM)
    pe = jnp.stack([pe_k, pe_v]).reshape(2, 1, CMP_LEN * HEAD_DIM)
    pe = jnp.broadcast_to(pe, (2, 16, CMP_LEN * HEAD_DIM)).astype(BF16)
    w1 = jnp.stack([k_w1, v_w1]).astype(BF16)
    w2 = jnp.stack([k_w2, v_w2]).astype(BF16)
    cmp_kv = _compress(cmp_in, pe, w1, w2)

    y = _nsa_attention(q_raw, q_rot, kv, cmp_kv, gates, batch, seq)
    return _out_proj(x, [y], [w_out.astype(BF16)])


def kernel(x, mem, positions, norm_g, mem_g, final_g, w_in_ab, pool_w, pool_scale, w_out_ab, ffn_w_gate, ffn_w_up, ffn_w_down, w_in_c, cmp_pe_k, cmp_pe_v, cmp_k_w1, cmp_k_w2, cmp_v_w1, cmp_v_w2, w_out_c, router_w, moe_w_gate, moe_w_up, moe_w_down, xa_wq, xa_wkv, xa_wo):
    batch, seq, d = x.shape
    mem_len = mem.shape[1]
    depth = norm_g.shape[0]
    rope_pair, rope_single = _rope_tables(positions)

    wkv_all = jnp.concatenate([xa_wkv[l] for l in range(depth)], axis=1).astype(BF16)
    kv_all = _norm_matmul(mem.reshape(batch * mem_len, d), mem_g, wkv_all)

    xf = x.reshape(batch * seq, d)
    for layer in range(depth):
        i = layer // 2
        if layer % 2 == 0:
            xf = _even_mixer(xf, norm_g[layer, 0], w_in_ab[i], pool_w[i], pool_scale[i], w_out_ab[i],
                             rope_pair, batch, seq)
        else:
            xf = _odd_mixer(xf, norm_g[layer, 0], w_in_c[i], cmp_pe_k[i], cmp_pe_v[i], cmp_k_w1[i], cmp_k_w2[i],
                            cmp_v_w1[i], cmp_v_w2[i], w_out_c[i], rope_pair, rope_single, batch, seq)
        xf = _cross_attention(xf, norm_g[layer, 1], xa_wq[layer].astype(BF16), kv_all, layer,
                              xa_wo[layer].astype(BF16), seq, mem_len)
        if layer % 2 == 0:
            xf = _ffn(xf, norm_g[layer, 2], ffn_w_gate[i].astype(BF16), ffn_w_up[i].astype(BF16),
                      ffn_w_down[i].astype(BF16))
        else:
            xf = _moe(xf, norm_g[layer, 2], router_w[i], moe_w_gate[i].astype(BF16), moe_w_up[i].astype(BF16),
                      moe_w_down[i].astype(BF16))
    return _final_norm(xf, final_g).reshape(batch, seq, d)
```

```python
import functools
import math

import numpy as np
import jax
import jax.numpy as jnp
from jax import lax
from jax.experimental import pallas as pl
from jax.experimental.pallas import tpu as pltpu

F32 = jnp.float32
BF16 = jnp.bfloat16

HEAD_DIM = 64
ROPE_DIM = 16
ROPE_THETA = 500000.0
NORM_EPS = 1e-5
QK_SCALE = HEAD_DIM ** -0.5

POOL_WINDOWS = (2, 4, 8, 16)
POOL_DIM = 256
MOBA_HEADS = 12
MOBA_DIM = MOBA_HEADS * HEAD_DIM
MOBA_BLOCK = 256
MOBA_TOPK = 3
NSA_HEADS = 16
NSA_GROUPS = 4
NSA_HPG = 4
NSA_KV_DIM = NSA_GROUPS * HEAD_DIM
CMP_LEN = 32
CMP_STRIDE = 16
SLC_BLOCK = 64
SLC_TOPN = 16
WINDOW = 512
FORCE_SCORE = 1e4
XA_HEADS = 4
XA_DIM = XA_HEADS * HEAD_DIM
N_EXPERTS = 8

LANES = 128
ATT_TILE = 256
VMEM_LIMIT = 56 * 1024 * 1024
NEG_INF = float("-inf")
MASK_NEG = -1e30


def _cparams(*sem):
    return pltpu.CompilerParams(dimension_semantics=sem, vmem_limit_bytes=VMEM_LIMIT)


def _dot(a, b):
    return jnp.dot(a, b, preferred_element_type=F32)


def _dot_nt(a, b):
    return lax.dot_general(a, b, (((1,), (1,)), ((), ())), preferred_element_type=F32)


def _split_bf16(a):
    hi = a.astype(BF16)
    lo = (a - hi.astype(F32)).astype(BF16)
    return hi, lo


def _rms_bf16(x, g):
    ms = jnp.mean(x * x, axis=-1, keepdims=True)
    return (x * lax.rsqrt(ms + NORM_EPS) * g).astype(BF16)


def _apply_rope(acc, c, a, b):
    half = ROPE_DIM // 2
    return acc * c + pltpu.roll(acc, LANES - half, axis=1) * a + pltpu.roll(acc, half, axis=1) * b


def _proj_kernel(plan, n_tab, x_ref, g_ref, w_ref, *rest):
    tabs = rest[:3 * n_tab]
    outs = rest[3 * n_tab:]
    h = _rms_bf16(x_ref[...], g_ref[...])
    for wc0, width, sinks in plan:
        acc = _dot(h, w_ref[:, wc0:wc0 + width])
        for oi, oc0, op, tab in sinks:
            for s in range(width // LANES):
                val = acc[:, s * LANES:(s + 1) * LANES]
                if op in ("rope", "rope_scale"):
                    c, a, b = (tabs[3 * tab + i][...] for i in range(3))
                    val = _apply_rope(val, c, a, b)
                if op in ("scale", "rope_scale"):
                    val = val * QK_SCALE
                if op == "sigmoid":
                    val = jax.nn.sigmoid(val)
                o = outs[oi]
                o[:, oc0 + s * LANES:oc0 + (s + 1) * LANES] = val.astype(o.dtype)


def _norm_project(x, g, w, plan, tables, out_defs, seq, tm=512):
    t, d = x.shape
    n = w.shape[1]
    n_seq_tiles = seq // tm
    in_specs = [
        pl.BlockSpec((tm, d), lambda i: (i, 0)),
        pl.BlockSpec((1, d), lambda i: (0, 0)),
        pl.BlockSpec((d, n), lambda i: (0, 0)),
    ]
    flat_tabs = []
    for tset in tables:
        for tb in tset:
            flat_tabs.append(tb)
            in_specs.append(pl.BlockSpec((tm, LANES), lambda i: (i % n_seq_tiles, 0)))
    out_shape = [jax.ShapeDtypeStruct((t, wd), dt) for wd, dt in out_defs]
    out_specs = [pl.BlockSpec((tm, wd), lambda i: (i, 0)) for wd, _ in out_defs]
    return pl.pallas_call(
        functools.partial(_proj_kernel, plan, len(tables)),
        name="proj",
        grid=(t // tm,),
        in_specs=in_specs,
        out_specs=out_specs,
        out_shape=out_shape,
        compiler_params=_cparams("parallel"),
    )(x, g.reshape(1, d), w, *flat_tabs)


def _rope_tables(positions):
    half = ROPE_DIM // 2
    inv = ROPE_THETA ** (-jnp.arange(0, ROPE_DIM, 2, dtype=F32) / ROPE_DIM)
    ang = positions.astype(F32)[:, None] * inv[None, :]
    cos, sin = jnp.cos(ang), jnp.sin(ang)
    s = positions.shape[0]
    z8 = jnp.zeros((s, half), F32)
    rest0 = jnp.zeros((s, HEAD_DIM - ROPE_DIM), F32)
    rest1 = jnp.ones((s, HEAD_DIM - ROPE_DIM), F32)
    c64 = jnp.concatenate([cos, cos, rest1], -1)
    a64 = jnp.concatenate([-sin, z8, rest0], -1)
    b64 = jnp.concatenate([z8, sin, rest0], -1)
    one64 = jnp.ones((s, HEAD_DIM), F32)
    zero64 = jnp.zeros((s, HEAD_DIM), F32)
    pair = tuple(jnp.concatenate([m, m], -1) for m in (c64, a64, b64))
    single = (jnp.concatenate([c64, one64], -1), jnp.concatenate([a64, zero64], -1),
              jnp.concatenate([b64, zero64], -1))
    return pair, single


def _norm_matmul_kernel(x_ref, g_ref, w_ref, o_ref):
    h = _rms_bf16(x_ref[...], g_ref[...])
    o_ref[...] = _dot(h, w_ref[...]).astype(o_ref.dtype)


def _norm_matmul(x, g, w, tm=512):
    t, d = x.shape
    n = w.shape[1]
    return pl.pallas_call(
        _norm_matmul_kernel,
        name="mem_kv",
        grid=(t // tm,),
        in_specs=[pl.BlockSpec((tm, d), lambda i: (i, 0)),
                  pl.BlockSpec((1, d), lambda i: (0, 0)),
                  pl.BlockSpec((d, n), lambda i: (0, 0))],
        out_specs=pl.BlockSpec((tm, n), lambda i: (i, 0)),
        out_shape=jax.ShapeDtypeStruct((t, n), BF16),
        compiler_params=_cparams("parallel"),
    )(x, g.reshape(1, d), w)


def _final_norm_kernel(x_ref, g_ref, o_ref):
    x = x_ref[...]
    ms = jnp.mean(x * x, axis=-1, keepdims=True)
    o_ref[...] = x * lax.rsqrt(ms + NORM_EPS) * g_ref[...]


def _final_norm(x, g, tm=1024):
    t, d = x.shape
    return pl.pallas_call(
        _final_norm_kernel,
        name="final_norm",
        grid=(t // tm,),
        in_specs=[pl.BlockSpec((tm, d), lambda i: (i, 0)), pl.BlockSpec((1, d), lambda i: (0, 0))],
        out_specs=pl.BlockSpec((tm, d), lambda i: (i, 0)),
        out_shape=jax.ShapeDtypeStruct((t, d), F32),
        compiler_params=_cparams("parallel"),
    )(x, g.reshape(1, d))


def _out_proj_kernel(n_in, x_ref, *rest):
    ys = rest[:n_in]
    ws = rest[n_in:2 * n_in]
    o_ref = rest[2 * n_in]
    acc = x_ref[...]
    for y, w in zip(ys, ws):
        acc = acc + _dot(y[...], w[...])
    o_ref[...] = acc


def _out_proj(x, ys, ws, tm=512):
    t, d = x.shape
    in_specs = [pl.BlockSpec((tm, d), lambda i: (i, 0))]
    in_specs += [pl.BlockSpec((tm, y.shape[1]), lambda i: (i, 0)) for y in ys]
    in_specs += [pl.BlockSpec(w.shape, lambda i: (0, 0)) for w in ws]
    return pl.pallas_call(
        functools.partial(_out_proj_kernel, len(ys)),
        name="out_proj",
        grid=(t // tm,),
        in_specs=in_specs,
        out_specs=pl.BlockSpec((tm, d), lambda i: (i, 0)),
        out_shape=jax.ShapeDtypeStruct((t, d), F32),
        compiler_params=_cparams("parallel"),
    )(x, *ys, *ws)


def _pool_kernel(u_ref, w_ref, sc_ref, o_ref, pad_ref):
    s = u_ref.shape[0]
    maxw = POOL_WINDOWS[-1]
    u = u_ref[...]
    pad_ref[0:maxw, :] = jnp.zeros((maxw, POOL_DIM), F32)
    t1 = (lax.broadcasted_iota(jnp.int32, (s, POOL_DIM), 0) + 1).astype(F32)
    lane = lax.broadcasted_iota(jnp.int32, (s, POOL_DIM), 1)
    gdim = POOL_DIM // len(POOL_WINDOWS)
    cur = u
    pooled = jnp.zeros_like(u)
    shift = 1
    for gi, w in enumerate(POOL_WINDOWS):
        pad_ref[maxw:maxw + s, :] = cur
        cur = cur + pad_ref[maxw - shift:maxw - shift + s, :]
        shift *= 2
        assert shift == w
        mean = cur / jnp.minimum(t1, float(w))
        pooled = jnp.where((lane >= gi * gdim) & (lane < (gi + 1) * gdim), mean, pooled)
    pooled = (pooled - u).astype(BF16)
    o_ref[...] = (_dot(pooled, w_ref[...]) * sc_ref[...]).astype(o_ref.dtype)


def _pool_mixer(u_pool, w_bd, scale, batch, seq):
    return pl.pallas_call(
        _pool_kernel,
        name="pool",
        grid=(batch,),
        in_specs=[pl.BlockSpec((seq, POOL_DIM), lambda b: (b, 0)),
                  pl.BlockSpec((POOL_DIM, POOL_DIM), lambda b: (0, 0)),
                  pl.BlockSpec((1, POOL_DIM), lambda b: (0, 0))],
        out_specs=pl.BlockSpec((seq, POOL_DIM), lambda b: (b, 0)),
        out_shape=jax.ShapeDtypeStruct((batch * seq, POOL_DIM), BF16),
        scratch_shapes=[pltpu.VMEM((seq + POOL_WINDOWS[-1], POOL_DIM), F32)],
        compiler_params=_cparams("parallel"),
    )(u_pool, w_bd, scale.reshape(1, POOL_DIM))


def _rank_select(scoreT, valid, n_rows, topn):
    row = lax.broadcasted_iota(jnp.int32, scoreT.shape, 0)
    rank = jnp.zeros(scoreT.shape, F32)
    for j in range(n_rows):
        sj = scoreT[j:j + 1, :]
        beats = (scoreT > sj) | ((scoreT == sj) & (row < j))
        rj = jnp.sum(beats.astype(F32), axis=0, keepdims=True)
        rank = jnp.where(row == j, rj, rank)
    return (rank < float(topn)) & valid


def _head_halves(rows):
    lane = lax.broadcasted_iota(jnp.int32, (rows, LANES), 1)
    return lane < HEAD_DIM, lane >= HEAD_DIM


def _tile_masks(tq):
    row = lax.broadcasted_iota(jnp.int32, (tq, tq), 0)
    col = lax.broadcasted_iota(jnp.int32, (tq, tq), 1)
    return col <= row, col > row


def _mask_tile(s, t0, mask):
    tq = mask.shape[0]
    parts = []
    if t0 > 0:
        parts.append(s[:, :t0])
    parts.append(jnp.where(mask, s[:, t0:t0 + tq], NEG_INF))
    if t0 + tq < s.shape[1]:
        parts.append(s[:, t0 + tq:])
    return parts[0] if len(parts) == 1 else jnp.concatenate(parts, axis=1)


def _softmax_pv(s, v):
    m = jnp.max(s, axis=1, keepdims=True)
    p = jnp.exp(s - m)
    l = jnp.sum(p, axis=1, keepdims=True)
    return _dot(p.astype(BF16), v) / l


def _block_onehot(seq, shift):
    lane = lax.broadcasted_iota(jnp.int32, (seq, LANES), 1)
    blk = jnp.right_shift(lax.broadcasted_iota(jnp.int32, (seq, LANES), 0), shift)
    return jnp.where((lane & (HEAD_DIM - 1)) == blk, 1.0, 0.0)


def _moba_tile(c, nb, q_ref, v_ref, o_ref, kaug_ref, kmean_ref):
    tq = ATT_TILE
    w = (c + 1) * tq
    qp = q_ref[...]
    qf = qp.astype(F32)
    halves = _head_halves(tq)
    causal, _ = _tile_masks(tq)
    outs = []
    for h in range(2):
        if c > MOBA_TOPK:
            mine_k = _head_halves(kmean_ref.shape[0])[h]
            km_hi, km_lo = _split_bf16(jnp.where(mine_k, kmean_ref[...], 0.0))
            gate = (_dot_nt(km_hi, qp) + _dot_nt(km_lo, qp))[0:nb]
            cand = lax.broadcasted_iota(jnp.int32, (nb, tq), 0)
            valid = cand < c
            gate = jnp.where(valid, gate, NEG_INF)
            keep = _rank_select(gate, valid, c, MOBA_TOPK) | (cand == c)
            neg = jnp.where(keep, 0.0, MASK_NEG)
            lead = HEAD_DIM if h == 0 else 0
            pieces = [neg, jnp.zeros((LANES - lead - nb, tq), F32)]
            if lead:
                pieces.insert(0, jnp.zeros((lead, tq), F32))
            maskcols = jnp.concatenate(pieces, axis=0).T
        else:
            maskcols = jnp.zeros((tq, LANES), F32)
        q_aug = jnp.where(halves[h], qf, maskcols).astype(BF16)
        s = _dot_nt(q_aug, kaug_ref[h, 0:w, :])
        s = _mask_tile(s, c * tq, causal)
        outs.append(_softmax_pv(s, v_ref[0:w, :]))
    o_ref[...] = jnp.where(halves[0], outs[0], outs[1]).astype(o_ref.dtype)


def _moba_kernel(q_ref, k_ref, v_ref, o_ref, kaug_ref, kmean_ref):
    qi = pl.program_id(2)
    seq = k_ref.shape[0]
    nb = seq // MOBA_BLOCK

    @pl.when(qi == 0)
    def _():
        kf = k_ref[...].astype(F32)
        first, second = _head_halves(seq)
        onehot = _block_onehot(seq, int(math.log2(MOBA_BLOCK)))
        kaug_ref[0] = jnp.where(first, kf, onehot).astype(BF16)
        kaug_ref[1] = jnp.where(second, kf, onehot).astype(BF16)
        kmean_ref[...] = jnp.zeros(kmean_ref.shape, F32)
        for j in range(nb):
            blk = k_ref[j * MOBA_BLOCK:(j + 1) * MOBA_BLOCK, :].astype(F32)
            kmean_ref[j:j + 1, :] = jnp.mean(blk, axis=0, keepdims=True)

    for c in range(nb):
        pl.when(qi == c)(functools.partial(_moba_tile, c, nb, q_ref, v_ref, o_ref, kaug_ref, kmean_ref))


def _moba_attention(qkv, batch, seq):
    assert MOBA_BLOCK == ATT_TILE and seq // MOBA_BLOCK <= 16
    nq = seq // ATT_TILE
    hp = MOBA_DIM // LANES
    return pl.pallas_call(
        _moba_kernel,
        name="moba",
        grid=(batch, hp, nq),
        in_specs=[pl.BlockSpec((ATT_TILE, LANES), lambda b, p, i: (b * nq + i, p)),
                  pl.BlockSpec((seq, LANES), lambda b, p, i: (b, hp + p)),
                  pl.BlockSpec((seq, LANES), lambda b, p, i: (b, 2 * hp + p))],
        out_specs=pl.BlockSpec((ATT_TILE, LANES), lambda b, p, i: (b * nq + i, p)),
        out_shape=jax.ShapeDtypeStruct((batch * seq, MOBA_DIM), BF16),
        scratch_shapes=[pltpu.VMEM((2, seq, LANES), BF16), pltpu.VMEM((16, LANES), F32)],
        compiler_params=_cparams("parallel", "parallel", "arbitrary"),
    )(qkv, qkv, qkv)


def _xattn_kernel(x_ref, g_ref, wq_ref, kv_ref, wo_ref, o_ref):
    x = x_ref[...]
    h = _rms_bf16(x, g_ref[...])
    q = (_dot(h, wq_ref[...]) * QK_SCALE).astype(BF16)
    out = x
    for hd in range(XA_HEADS):
        sl = slice(hd * HEAD_DIM, (hd + 1) * HEAD_DIM)
        k = kv_ref[:, sl]
        v = kv_ref[:, XA_DIM + hd * HEAD_DIM:XA_DIM + (hd + 1) * HEAD_DIM]
        s = _dot_nt(q[:, sl], k)
        m = jnp.max(s, axis=1, keepdims=True)
        p = jnp.exp(s - m)
        l = jnp.sum(p, axis=1, keepdims=True)
        o = _dot(p.astype(BF16), v) / l
        out = out + _dot(o.astype(BF16), wo_ref[sl, :])
    o_ref[...] = out


def _cross_attention(x, g, wq, kv_all, layer, wo, seq, mem_len, tm=512):
    t, d = x.shape
    tiles_per_seq = seq // tm
    return pl.pallas_call(
        _xattn_kernel,
        name="xattn",
        grid=(t // tm,),
        in_specs=[pl.BlockSpec((tm, d), lambda i: (i, 0)),
                  pl.BlockSpec((1, d), lambda i: (0, 0)),
                  pl.BlockSpec((d, XA_DIM), lambda i: (0, 0)),
                  pl.BlockSpec((mem_len, 2 * XA_DIM), lambda i: (i // tiles_per_seq, layer)),
                  pl.BlockSpec((XA_DIM, d), lambda i: (0, 0))],
        out_specs=pl.BlockSpec((tm, d), lambda i: (i, 0)),
        out_shape=jax.ShapeDtypeStruct((t, d), F32),
        compiler_params=_cparams("parallel"),
    )(x, g.reshape(1, d), wq, kv_all, wo)


def _ffn_kernel(x_ref, g_ref, wg_ref, wu_ref, wd_ref, o_ref, h_ref):
    f = pl.program_id(1)

    @pl.when(f == 0)
    def _():
        x = x_ref[...]
        h_ref[...] = _rms_bf16(x, g_ref[...])
        o_ref[...] = x

    h = h_ref[...]
    gate = _dot(h, wg_ref[...])
    a = (gate * jax.nn.sigmoid(gate)) * _dot(h, wu_ref[...])
    o_ref[...] += _dot(a.astype(BF16), wd_ref[...])


def _ffn(x, g, wg, wu, wd, tm=1024, tf=256):
    t, d = x.shape
    ff = wg.shape[1]
    return pl.pallas_call(
        _ffn_kernel,
        name="ffn",
        grid=(t // tm, ff // tf),
        in_specs=[pl.BlockSpec((tm, d), lambda i, f: (i, 0)),
                  pl.BlockSpec((1, d), lambda i, f: (0, 0)),
                  pl.BlockSpec((d, tf), lambda i, f: (0, f)),
                  pl.BlockSpec((d, tf), lambda i, f: (0, f)),
                  pl.BlockSpec((tf, d), lambda i, f: (f, 0))],
        out_specs=pl.BlockSpec((tm, d), lambda i, f: (i, 0)),
        out_shape=jax.ShapeDtypeStruct((t, d), F32),
        scratch_shapes=[pltpu.VMEM((tm, d), BF16)],
        compiler_params=_cparams("parallel", "arbitrary"),
    )(x, g.reshape(1, d), wg, wu, wd)


def _moe_kernel(x_ref, g_ref, rhi_ref, rlo_ref, wg_ref, wu_ref, wd_ref, o_ref, h_ref, gate_ref):
    e = pl.program_id(1)
    f = pl.program_id(2)
    tm = x_ref.shape[0]
    lane = lax.broadcasted_iota(jnp.int32, (tm, LANES), 1)

    @pl.when((e == 0) & (f == 0))
    def _():
        x = x_ref[...]
        h = _rms_bf16(x, g_ref[...])
        h_ref[...] = h
        o_ref[...] = x
        logits = _dot(h, rhi_ref[...]) + _dot(h, rlo_ref[...])
        logits = jnp.where(lane < N_EXPERTS, logits, NEG_INF)
        m1 = jnp.max(logits, axis=1, keepdims=True)
        lane_f = lane.astype(F32)
        i1 = jnp.min(jnp.where(logits == m1, lane_f, float(LANES)), axis=1, keepdims=True)
        rest = jnp.where(lane_f == i1, NEG_INF, logits)
        m2 = jnp.max(rest, axis=1, keepdims=True)
        i2 = jnp.min(jnp.where(rest == m2, lane_f, float(LANES)), axis=1, keepdims=True)
        e2 = jnp.exp(m2 - m1)
        w1 = 1.0 / (1.0 + e2)
        w2 = e2 / (1.0 + e2)
        gate_ref[...] = jnp.where(lane_f == i1, w1, 0.0) + jnp.where(lane_f == i2, w2, 0.0)

    h = h_ref[...]
    gate = _dot(h, wg_ref[...])
    a = (gate * jax.nn.sigmoid(gate)) * _dot(h, wu_ref[...])
    ge = jnp.sum(jnp.where(lane == e, gate_ref[...], 0.0), axis=1, keepdims=True)
    o_ref[...] += _dot((a * ge).astype(BF16), wd_ref[...])


def _moe(x, g, router_w, wg, wu, wd, tm=1024, tf=256):
    t, d = x.shape
    ne, _, ff = wg.shape
    rw = jnp.zeros((d, LANES), F32).at[:, :ne].set(router_w)
    rhi = rw.astype(BF16)
    rlo = (rw - rhi.astype(F32)).astype(BF16)
    return pl.pallas_call(
        _moe_kernel,
        name="moe",
        grid=(t // tm, ne, ff // tf),
        in_specs=[pl.BlockSpec((tm, d), lambda i, e, f: (i, 0)),
                  pl.BlockSpec((1, d), lambda i, e, f: (0, 0)),
                  pl.BlockSpec((d, LANES), lambda i, e, f: (0, 0)),
                  pl.BlockSpec((d, LANES), lambda i, e, f: (0, 0)),
                  pl.BlockSpec((None, d, tf), lambda i, e, f: (e, 0, f)),
                  pl.BlockSpec((None, d, tf), lambda i, e, f: (e, 0, f)),
                  pl.BlockSpec((None, tf, d), lambda i, e, f: (e, f, 0))],
        out_specs=pl.BlockSpec((tm, d), lambda i, e, f: (i, 0)),
        out_shape=jax.ShapeDtypeStruct((t, d), F32),
        scratch_shapes=[pltpu.VMEM((tm, d), BF16), pltpu.VMEM((tm, LANES), F32)],
        compiler_params=_cparams("parallel", "arbitrary", "arbitrary"),
    )(x, g.reshape(1, d), rhi, rlo, wg, wu, wd)


def _gelu_tanh(x):
    return 0.5 * x * (1.0 + jnp.tanh(math.sqrt(2.0 / math.pi) * (x + 0.044715 * (x * x * x))))


def _compress_kernel(x_ref, pe_ref, w1_ref, w2_ref, o_ref, sh_ref):
    rows = x_ref.shape[0]
    half = w1_ref.shape[0] // 2
    x = x_ref[...]
    top = _dot(x, w1_ref[0:half, :])
    bot = _dot(x, w1_ref[half:2 * half, :])
    bias = _dot(pe_ref[...], w1_ref[...])[0:1, :]
    sh_ref[0:rows, :] = bot
    sh_ref[rows:rows + 8, :] = jnp.zeros((8, bot.shape[1]), F32)
    hidden = top + sh_ref[1:rows + 1, :] + bias
    o_ref[...] = _dot(_gelu_tanh(hidden).astype(BF16), w2_ref[...]).astype(o_ref.dtype)


def _compress(chunks, pe, w1, w2, rows=1024):
    _, r, cw = chunks.shape
    hid = w1.shape[2]
    return pl.pallas_call(
        _compress_kernel,
        name="compress",
        grid=(2, r // rows),
        in_specs=[pl.BlockSpec((None, rows, cw), lambda a, i: (a, i, 0)),
                  pl.BlockSpec((None, 16, 2 * cw), lambda a, i: (a, 0, 0)),
                  pl.BlockSpec((None, 2 * cw, hid), lambda a, i: (a, 0, 0)),
                  pl.BlockSpec((None, hid, HEAD_DIM), lambda a, i: (a, 0, 0))],
        out_specs=pl.BlockSpec((None, rows, HEAD_DIM), lambda a, i: (a, i, 0)),
        out_shape=jax.ShapeDtypeStruct((2, r, HEAD_DIM), BF16),
        scratch_shapes=[pltpu.VMEM((rows + 8, hid), F32)],
        compiler_params=_cparams("parallel", "parallel"),
    )(chunks, pe, w1, w2)


def _nsa_cmp_kernel(n_slc, q_ref, ka_ref, kb_ref, ovt_ref, ocmp_ref, selm_ref):
    qi = pl.program_id(2)
    tq = ATT_TILE
    n_pad = ka_ref.shape[0]
    t_q = qi * tq + lax.broadcasted_iota(jnp.int32, (tq, n_pad), 0)
    n_c = lax.broadcasted_iota(jnp.int32, (tq, n_pad), 1)
    mask_c = (n_c * CMP_STRIDE + (CMP_LEN - 1)) <= t_q
    halves = _head_halves(tq)
    ka = ka_ref[...]
    kb = kb_ref[...]
    psum = jnp.zeros((tq, n_pad), F32)
    for pr in range(NSA_HPG // 2):
        qf = q_ref[:, pr * LANES:(pr + 1) * LANES].astype(F32)
        outs = []
        for h in range(2):
            qz = jnp.where(halves[h], qf, 0.0).astype(BF16)
            kmat, vmat = (ka, kb) if h == 0 else (kb, ka)
            s = jnp.where(mask_c, _dot_nt(qz, kmat), NEG_INF)
            m = jnp.max(s, axis=1, keepdims=True)
            m = jnp.where(m == NEG_INF, 0.0, m)
            e = jnp.exp(s - m)
            p = e / jnp.maximum(jnp.sum(e, axis=1, keepdims=True), 1e-30)
            psum = psum + p
            outs.append(_dot(p.astype(BF16), vmat))
        ocmp_ref[:, pr * LANES:(pr + 1) * LANES] = jnp.where(halves[0], outs[0], outs[1]).astype(ocmp_ref.dtype)

    ps_hi, ps_lo = _split_bf16(psum)
    ovt = ovt_ref[...]
    p_slc = (_dot_nt(ovt, ps_hi) + _dot_nt(ovt, ps_lo))[0:n_slc]
    cand = lax.broadcasted_iota(jnp.int32, (n_slc, tq), 0)
    t_blk = jnp.right_shift(qi * tq + lax.broadcasted_iota(jnp.int32, (n_slc, tq), 1), int(math.log2(SLC_BLOCK)))
    valid = cand <= t_blk
    forced = (cand == 0) | (cand == t_blk) | (cand == t_blk - 1)
    score = jnp.where(valid, jnp.where(forced, FORCE_SCORE, p_slc), -1.0)
    keep = _rank_select(score, valid, n_slc, min(SLC_TOPN, n_slc))
    neg = jnp.where(keep, 0.0, MASK_NEG)
    zero = jnp.zeros((HEAD_DIM - n_slc, tq), F32)
    selm_ref[...] = jnp.concatenate([neg, zero, neg, zero], axis=0).T.astype(selm_ref.dtype)


def _nsa_sw_tile(c, q_ref, selm_ref, ksel_ref, kwin_ref, gate_ref, ex_ref, ocmp_ref, o_ref,
                 kaug_ref, vb_ref, kwb_ref):
    tq = ATT_TILE
    w = (c + 1) * tq
    lo = max(c - WINDOW // tq, 0) * tq
    qf = q_ref[...].astype(F32)
    selm = selm_ref[...].astype(F32)
    halves = _head_halves(tq)
    causal, beyond = _tile_masks(tq)
    sel_o, win_o = [], []
    for h in range(2):
        q_aug = jnp.where(halves[h], qf, selm).astype(BF16)
        qz = jnp.where(halves[h], qf, 0.0).astype(BF16)
        if h == 0:
            k_sel, v_sel = kaug_ref[0, 0:w, :], vb_ref[0:w, :]
            k_win, v_win = kwin_ref[lo:w, :], kwb_ref[lo:w, :]
        else:
            k_sel, v_sel = kaug_ref[1, 0:w, :], ksel_ref[0:w, :]
            k_win, v_win = kwb_ref[lo:w, :], kwin_ref[lo:w, :]
        s = _mask_tile(_dot_nt(q_aug, k_sel), c * tq, causal)
        sel_o.append(_softmax_pv(s, v_sel))
        s = _mask_tile(_dot_nt(qz, k_win), c * tq - lo, causal)
        if c * tq - lo == WINDOW:
            s = _mask_tile(s, 0, beyond)
        win_o.append(_softmax_pv(s, v_win))
    o_sel = jnp.where(halves[0], sel_o[0], sel_o[1])
    o_win = jnp.where(halves[0], win_o[0], win_o[1])
    g_hi, g_lo = _split_bf16(gate_ref[...])
    ex = ex_ref[...]
    gexp = _dot(g_hi, ex) + _dot(g_lo, ex)
    y = (gexp[:, 0:LANES] * ocmp_ref[...].astype(F32) + gexp[:, LANES:2 * LANES] * o_sel
         + gexp[:, 2 * LANES:3 * LANES] * o_win)
    o_ref[...] = y.astype(o_ref.dtype)


def _nsa_sw_kernel(q_ref, selm_ref, ksel_ref, kwin_ref, gate_ref, ex_ref, ocmp_ref, o_ref,
                   kaug_ref, vb_ref, kwb_ref):
    pr = pl.program_id(2)
    qi = pl.program_id(3)
    seq = ksel_ref.shape[0]

    @pl.when((pr == 0) & (qi == 0))
    def _():
        kv = ksel_ref[...].astype(F32)
        first, second = _head_halves(seq)
        onehot = _block_onehot(seq, int(math.log2(SLC_BLOCK)))
        swapped = pltpu.roll(kv, HEAD_DIM, axis=1)
        kaug_ref[0] = jnp.where(first, kv, onehot).astype(BF16)
        kaug_ref[1] = jnp.where(second, swapped, onehot).astype(BF16)
        vb_ref[...] = swapped.astype(BF16)
        kwb_ref[...] = pltpu.roll(kwin_ref[...].astype(F32), HEAD_DIM, axis=1).astype(BF16)

    for c in range(seq // ATT_TILE):
        pl.when(qi == c)(functools.partial(_nsa_sw_tile, c, q_ref, selm_ref, ksel_ref, kwin_ref, gate_ref, ex_ref,
                                           ocmp_ref, o_ref, kaug_ref, vb_ref, kwb_ref))


def _nsa_attention(q_raw, q_rot, kv, cmp_kv, gates, batch, seq):
    nq = seq // ATT_TILE
    g = NSA_GROUPS
    gw = NSA_HPG * HEAD_DIM
    n_pairs = NSA_HEADS // 2
    n_cmp = (seq - CMP_LEN) // CMP_STRIDE + 1
    n_cmp_pad = seq // CMP_STRIDE
    n_slc = seq // SLC_BLOCK
    assert n_cmp_pad == LANES and n_slc <= HEAD_DIM // 2 and WINDOW % ATT_TILE == 0

    ex = np.zeros((n_pairs, LANES, 3 * LANES), np.float32)
    for hd in range(NSA_HEADS):
        for br in range(3):
            lane0 = br * LANES + (hd % 2) * HEAD_DIM
            ex[hd // 2, hd * 3 + br, lane0:lane0 + HEAD_DIM] = 1.0
    c_s = np.arange(n_cmp) * CMP_STRIDE
    s_s = np.arange(n_slc) * SLC_BLOCK
    ov = np.clip(np.minimum(c_s[:, None] + CMP_LEN, s_s[None, :] + SLC_BLOCK)
                 - np.maximum(c_s[:, None], s_s[None, :]), 0, None) / CMP_LEN
    ovt = np.zeros((LANES, n_cmp_pad), np.float32)
    ovt[:n_slc, :n_cmp] = ov.T

    ka = jnp.concatenate([cmp_kv[0], cmp_kv[1]], axis=-1)
    kb = jnp.concatenate([cmp_kv[1], cmp_kv[0]], axis=-1)
    o_cmp, selm = pl.pallas_call(
        functools.partial(_nsa_cmp_kernel, n_slc),
        name="nsa_cmp",
        grid=(batch, g, nq),
        in_specs=[pl.BlockSpec((ATT_TILE, gw), lambda b, gi, i: (b * nq + i, gi)),
                  pl.BlockSpec((n_cmp_pad, LANES), lambda b, gi, i: (b * g + gi, 0)),
                  pl.BlockSpec((n_cmp_pad, LANES), lambda b, gi, i: (b * g + gi, 0)),
                  pl.BlockSpec((LANES, n_cmp_pad), lambda b, gi, i: (0, 0))],
        out_specs=[pl.BlockSpec((ATT_TILE, gw), lambda b, gi, i: (b * nq + i, gi)),
                   pl.BlockSpec((ATT_TILE, LANES), lambda b, gi, i: (b * nq + i, gi))],
        out_shape=[jax.ShapeDtypeStruct((batch * seq, NSA_HEADS * HEAD_DIM), BF16),
                   jax.ShapeDtypeStruct((batch * seq, g * LANES), BF16)],
        compiler_params=_cparams("parallel", "parallel", "parallel"),
    )(q_raw, ka, kb, jnp.asarray(ovt, BF16))

    ppg = NSA_HPG // 2
    return pl.pallas_call(
        _nsa_sw_kernel,
        name="nsa_sw",
        grid=(batch, g, ppg, nq),
        in_specs=[pl.BlockSpec((ATT_TILE, LANES), lambda b, gi, p, i: (b * nq + i, gi * ppg + p)),
                  pl.BlockSpec((ATT_TILE, LANES), lambda b, gi, p, i: (b * nq + i, gi)),
                  pl.BlockSpec((seq, LANES), lambda b, gi, p, i: (b, g + gi)),
                  pl.BlockSpec((seq, LANES), lambda b, gi, p, i: (b, 2 * g + gi)),
                  pl.BlockSpec((ATT_TILE, LANES), lambda b, gi, p, i: (b * nq + i, 0)),
                  pl.BlockSpec((None, LANES, 3 * LANES), lambda b, gi, p, i: (gi * ppg + p, 0, 0)),
                  pl.BlockSpec((ATT_TILE, LANES), lambda b, gi, p, i: (b * nq + i, gi * ppg + p))],
        out_specs=pl.BlockSpec((ATT_TILE, LANES), lambda b, gi, p, i: (b * nq + i, gi * ppg + p)),
        out_shape=jax.ShapeDtypeStruct((batch * seq, NSA_HEADS * HEAD_DIM), BF16),
        scratch_shapes=[pltpu.VMEM((2, seq, LANES), BF16), pltpu.VMEM((seq, LANES), BF16),
                        pltpu.VMEM((seq, LANES), BF16)],
        compiler_params=_cparams("parallel", "parallel", "arbitrary", "arbitrary"),
    )(q_rot, selm, kv, kv, gates, jnp.asarray(ex, BF16), o_cmp)


def _even_mixer(x, g, w_in, pool_w, pool_scale, w_out, rope_pair, batch, seq):
    chunk = 2 * LANES
    plan = [(0, POOL_DIM, [(0, 0, "none", 0)])]
    for c in range(3 * MOBA_DIM // chunk):
        op = "rope_scale" if c < MOBA_DIM // chunk else ("rope" if c < 2 * MOBA_DIM // chunk else "none")
        plan.append((POOL_DIM + c * chunk, chunk, [(1, c * chunk, op, 0)]))
    u_pool, qkv = _norm_project(x, g, w_in.astype(BF16), plan, [rope_pair],
                                [(POOL_DIM, F32), (3 * MOBA_DIM, BF16)], seq)
    w_bd = jax.scipy.linalg.block_diag(*[pool_w[i] for i in range(pool_w.shape[0])]).astype(BF16)
    y_a = _pool_mixer(u_pool, w_bd, pool_scale, batch, seq)
    y_b = _moba_attention(qkv, batch, seq)
    w_out = w_out.astype(BF16)
    return _out_proj(x, [y_a, y_b], [w_out[:POOL_DIM], w_out[POOL_DIM:]])


def _odd_mixer(x, g, w_in, pe_k, pe_v, k_w1, k_w2, v_w1, v_w2, w_out, rope_pair, rope_single, batch, seq):
    d = x.shape[1]
    qd = NSA_HEADS * HEAD_DIM
    kvd = NSA_KV_DIM
    wq = w_in[:, :qd]
    parts = [w_in[:, qd + i * kvd:qd + (i + 1) * kvd].reshape(d, NSA_GROUPS, HEAD_DIM) for i in range(6)]
    pairs = [jnp.concatenate([parts[2 * i], parts[2 * i + 1]], axis=-1).reshape(d, 2 * kvd) for i in range(3)]
    n_gate = 3 * NSA_HEADS
    w_gate = jnp.zeros((d, LANES), F32).at[:, :n_gate].set(w_in[:, qd + 6 * kvd:])
    w_all = jnp.concatenate([wq] + pairs + [w_gate], axis=1).astype(BF16)

    chunk = 2 * LANES
    plan = []
    for c in range(qd // chunk):
        plan.append((c * chunk, chunk, [(0, c * chunk, "scale", 0), (1, c * chunk, "rope_scale", 0)]))
    for c in range(6 * kvd // chunk):
        branch = c // (2 * kvd // chunk)
        op = "none" if branch == 0 else "rope"
        plan.append((qd + c * chunk, chunk, [(2, c * chunk, op, 1)]))
    plan.append((qd + 6 * kvd, LANES, [(3, 0, "sigmoid", 0)]))
    q_raw, q_rot, kv, gates = _norm_project(
        x, g, w_all, plan, [rope_pair, rope_single],
        [(qd, BF16), (qd, BF16), (6 * kvd, BF16), (LANES, F32)], seq)

    n_chunks = seq // CMP_STRIDE
    cmp_in = kv[:, :2 * kvd].reshape(batch, n_chunks, CMP_STRIDE, NSA_GROUPS, 2, HEAD_DIM)
    cmp_in = cmp_in.transpose(4, 0, 3, 1, 2, 5).reshape(2, batch * NSA_GROUPS * n_chunks, CMP_STRIDE * HEAD_DIM)
    pe = jnp.stack([pe_k, pe_v]).reshape(2, 1, CMP_LEN * HEAD_DIM)
    pe = jnp.broadcast_to(pe, (2, 16, CMP_LEN * HEAD_DIM)).astype(BF16)
    w1 = jnp.stack([k_w1, v_w1]).astype(BF16)
    w2 = jnp.stack([k_w2, v_w2]).astype(BF16)
    cmp_kv = _compress(cmp_in, pe, w1, w2)

    y = _nsa_attention(q_raw, q_rot, kv, cmp_kv, gates, batch, seq)
    return _out_proj(x, [y], [w_out.astype(BF16)])


def kernel(x, mem, positions, norm_g, mem_g, final_g, w_in_ab, pool_w, pool_scale, w_out_ab, ffn_w_gate, ffn_w_up, ffn_w_down, w_in_c, cmp_pe_k, cmp_pe_v, cmp_k_w1, cmp_k_w2, cmp_v_w1, cmp_v_w2, w_out_c, router_w, moe_w_gate, moe_w_up, moe_w_down, xa_wq, xa_wkv, xa_wo):
    batch, seq, d = x.shape
    mem_len = mem.shape[1]
    depth = norm_g.shape[0]
    rope_pair, rope_single = _rope_tables(positions)

    wkv_all = jnp.concatenate([xa_wkv[l] for l in range(depth)], axis=1).astype(BF16)
    kv_all = _norm_matmul(mem.reshape(batch * mem_len, d), mem_g, wkv_all)

    xf = x.reshape(batch * seq, d)
    for layer in range(depth):
        i = layer // 2
        if layer % 2 == 0:
            xf = _even_mixer(xf, norm_g[layer, 0], w_in_ab[i], pool_w[i], pool_scale[i], w_out_ab[i],
                             rope_pair, batch, seq)
        else:
            xf = _odd_mixer(xf, norm_g[layer, 0], w_in_c[i], cmp_pe_k[i], cmp_pe_v[i], cmp_k_w1[i], cmp_k_w2[i],
                            cmp_v_w1[i], cmp_v_w2[i], w_out_c[i], rope_pair, rope_single, batch, seq)
        xf = _cross_attention(xf, norm_g[layer, 1], xa_wq[layer].astype(BF16), kv_all, layer,
                              xa_wo[layer].astype(BF16), seq, mem_len)
        if layer % 2 == 0:
            xf = _ffn(xf, norm_g[layer, 2], ffn_w_gate[i].astype(BF16), ffn_w_up[i].astype(BF16),
                      ffn_w_down[i].astype(BF16))
        else:
            xf = _moe(xf, norm_g[layer, 2], router_w[i], moe_w_gate[i].astype(BF16), moe_w_up[i].astype(BF16),
                      moe_w_down[i].astype(BF16))
    return _final_norm(xf, final_g).reshape(batch, seq, d)
```

```python
import functools
import math

import numpy as np
import jax
import jax.numpy as jnp
from jax import lax
from jax.experimental import pallas as pl
from jax.experimental.pallas import tpu as pltpu

F32 = jnp.float32
BF16 = jnp.bfloat16

HEAD_DIM = 64
ROPE_DIM = 16
ROPE_THETA = 500000.0
NORM_EPS = 1e-5
QK_SCALE = HEAD_DIM ** -0.5

POOL_WINDOWS = (2, 4, 8, 16)
POOL_DIM = 256
MOBA_HEADS = 12
MOBA_DIM = MOBA_HEADS * HEAD_DIM
MOBA_BLOCK = 256
MOBA_TOPK = 3
NSA_HEADS = 16
NSA_GROUPS = 4
NSA_HPG = 4
NSA_KV_DIM = NSA_GROUPS * HEAD_DIM
CMP_LEN = 32
CMP_STRIDE = 16
SLC_BLOCK = 64
SLC_TOPN = 16
WINDOW = 512
FORCE_SCORE = 1e4
XA_HEADS = 4
XA_DIM = XA_HEADS * HEAD_DIM
N_EXPERTS = 8

LANES = 128
ATT_TILE = 256
VMEM_LIMIT = 56 * 1024 * 1024
NEG_INF = float("-inf")
MASK_NEG = -1e30


def _cparams(*sem):
    return pltpu.CompilerParams(dimension_semantics=sem, vmem_limit_bytes=VMEM_LIMIT)


def _dot(a, b):
    return jnp.dot(a, b, preferred_element_type=F32)


def _dot_nt(a, b):
    return lax.dot_general(a, b, (((1,), (1,)), ((), ())), preferred_element_type=F32)


def _split_bf16(a):
    hi = a.astype(BF16)
    lo = (a - hi.astype(F32)).astype(BF16)
    return hi, lo


def _rms_bf16(x, g):
    ms = jnp.mean(x * x, axis=-1, keepdims=True)
    return (x * lax.rsqrt(ms + NORM_EPS) * g).astype(BF16)


def _apply_rope(acc, c, a, b):
    half = ROPE_DIM // 2
    return acc * c + pltpu.roll(acc, LANES - half, axis=1) * a + pltpu.roll(acc, half, axis=1) * b


def _proj_kernel(plan, n_tab, x_ref, g_ref, w_ref, *rest):
    tabs = rest[:3 * n_tab]
    outs = rest[3 * n_tab:]
    h = _rms_bf16(x_ref[...], g_ref[...])
    for wc0, width, sinks in plan:
        acc = _dot(h, w_ref[:, wc0:wc0 + width])
        for oi, oc0, op, tab in sinks:
            for s in range(width // LANES):
                val = acc[:, s * LANES:(s + 1) * LANES]
                if op in ("rope", "rope_scale"):
                    c, a, b = (tabs[3 * tab + i][...] for i in range(3))
                    val = _apply_rope(val, c, a, b)
                if op in ("scale", "rope_scale"):
                    val = val * QK_SCALE
                if op == "sigmoid":
                    val = jax.nn.sigmoid(val)
                o = outs[oi]
                o[:, oc0 + s * LANES:oc0 + (s + 1) * LANES] = val.astype(o.dtype)


def _norm_project(x, g, w, plan, tables, out_defs, seq, tm=512):
    t, d = x.shape
    n = w.shape[1]
    n_seq_tiles = seq // tm
    in_specs = [
        pl.BlockSpec((tm, d), lambda i: (i, 0)),
        pl.BlockSpec((1, d), lambda i: (0, 0)),
        pl.BlockSpec((d, n), lambda i: (0, 0)),
    ]
    flat_tabs = []
    for tset in tables:
        for tb in tset:
            flat_tabs.append(tb)
            in_specs.append(pl.BlockSpec((tm, LANES), lambda i: (i % n_seq_tiles, 0)))
    out_shape = [jax.ShapeDtypeStruct((t, wd), dt) for wd, dt in out_defs]
    out_specs = [pl.BlockSpec((tm, wd), lambda i: (i, 0)) for wd, _ in out_defs]
    return pl.pallas_call(
        functools.partial(_proj_kernel, plan, len(tables)),
        name="proj",
        grid=(t // tm,),
        in_specs=in_specs,
        out_specs=out_specs,
        out_shape=out_shape,
        compiler_params=_cparams("parallel"),
    )(x, g.reshape(1, d), w, *flat_tabs)


def _rope_tables(positions):
    half = ROPE_DIM // 2
    inv = ROPE_THETA ** (-jnp.arange(0, ROPE_DIM, 2, dtype=F32) / ROPE_DIM)
    ang = positions.astype(F32)[:, None] * inv[None, :]
    cos, sin = jnp.cos(ang), jnp.sin(ang)
    s = positions.shape[0]
    z8 = jnp.zeros((s, half), F32)
    rest0 = jnp.zeros((s, HEAD_DIM - ROPE_DIM), F32)
    rest1 = jnp.ones((s, HEAD_DIM - ROPE_DIM), F32)
    c64 = jnp.concatenate([cos, cos, rest1], -1)
    a64 = jnp.concatenate([-sin, z8, rest0], -1)
    b64 = jnp.concatenate([z8, sin, rest0], -1)
    one64 = jnp.ones((s, HEAD_DIM), F32)
    zero64 = jnp.zeros((s, HEAD_DIM), F32)
    pair = tuple(jnp.concatenate([m, m], -1) for m in (c64, a64, b64))
    single = (jnp.concatenate([c64, one64], -1), jnp.concatenate([a64, zero64], -1),
              jnp.concatenate([b64, zero64], -1))
    return pair, single


def _norm_matmul_kernel(x_ref, g_ref, w_ref, o_ref):
    h = _rms_bf16(x_ref[...], g_ref[...])
    o_ref[...] = _dot(h, w_ref[...]).astype(o_ref.dtype)


def _norm_matmul(x, g, w, tm=512):
    t, d = x.shape
    n = w.shape[1]
    return pl.pallas_call(
        _norm_matmul_kernel,
        name="mem_kv",
        grid=(t // tm,),
        in_specs=[pl.BlockSpec((tm, d), lambda i: (i, 0)),
                  pl.BlockSpec((1, d), lambda i: (0, 0)),
                  pl.BlockSpec((d, n), lambda i: (0, 0))],
        out_specs=pl.BlockSpec((tm, n), lambda i: (i, 0)),
        out_shape=jax.ShapeDtypeStruct((t, n), BF16),
        compiler_params=_cparams("parallel"),
    )(x, g.reshape(1, d), w)


def _final_norm_kernel(x_ref, g_ref, o_ref):
    x = x_ref[...]
    ms = jnp.mean(x * x, axis=-1, keepdims=True)
    o_ref[...] = x * lax.rsqrt(ms + NORM_EPS) * g_ref[...]


def _final_norm(x, g, tm=1024):
    t, d = x.shape
    return pl.pallas_call(
        _final_norm_kernel,
        name="final_norm",
        grid=(t // tm,),
        in_specs=[pl.BlockSpec((tm, d), lambda i: (i, 0)), pl.BlockSpec((1, d), lambda i: (0, 0))],
        out_specs=pl.BlockSpec((tm, d), lambda i: (i, 0)),
        out_shape=jax.ShapeDtypeStruct((t, d), F32),
        compiler_params=_cparams("parallel"),
    )(x, g.reshape(1, d))


def _out_proj_kernel(n_in, x_ref, *rest):
    ys = rest[:n_in]
    ws = rest[n_in:2 * n_in]
    o_ref = rest[2 * n_in]
    acc = x_ref[...]
    for y, w in zip(ys, ws):
        acc = acc + _dot(y[...], w[...])
    o_ref[...] = acc


def _out_proj(x, ys, ws, tm=512):
    t, d = x.shape
    in_specs = [pl.BlockSpec((tm, d), lambda i: (i, 0))]
    in_specs += [pl.BlockSpec((tm, y.shape[1]), lambda i: (i, 0)) for y in ys]
    in_specs += [pl.BlockSpec(w.shape, lambda i: (0, 0)) for w in ws]
    return pl.pallas_call(
        functools.partial(_out_proj_kernel, len(ys)),
        name="out_proj",
        grid=(t // tm,),
        in_specs=in_specs,
        out_specs=pl.BlockSpec((tm, d), lambda i: (i, 0)),
        out_shape=jax.ShapeDtypeStruct((t, d), F32),
        compiler_params=_cparams("parallel"),
    )(x, *ys, *ws)


def _pool_kernel(u_ref, w_ref, sc_ref, o_ref, pad_ref):
    s = u_ref.shape[0]
    maxw = POOL_WINDOWS[-1]
    u = u_ref[...]
    pad_ref[0:maxw, :] = jnp.zeros((maxw, POOL_DIM), F32)
    t1 = (lax.broadcasted_iota(jnp.int32, (s, POOL_DIM), 0) + 1).astype(F32)
    lane = lax.broadcasted_iota(jnp.int32, (s, POOL_DIM), 1)
    gdim = POOL_DIM // len(POOL_WINDOWS)
    cur = u
    pooled = jnp.zeros_like(u)
    shift = 1
    for gi, w in enumerate(POOL_WINDOWS):
        pad_ref[maxw:maxw + s, :] = cur
        cur = cur + pad_ref[maxw - shift:maxw - shift + s, :]
        shift *= 2
        assert shift == w
        mean = cur / jnp.minimum(t1, float(w))
        pooled = jnp.where((lane >= gi * gdim) & (lane < (gi + 1) * gdim), mean, pooled)
    pooled = (pooled - u).astype(BF16)
    o_ref[...] = (_dot(pooled, w_ref[...]) * sc_ref[...]).astype(o_ref.dtype)


def _pool_mixer(u_pool, w_bd, scale, batch, seq):
    return pl.pallas_call(
        _pool_kernel,
        name="pool",
        grid=(batch,),
        in_specs=[pl.BlockSpec((seq, POOL_DIM), lambda b: (b, 0)),
                  pl.BlockSpec((POOL_DIM, POOL_DIM), lambda b: (0, 0)),
                  pl.BlockSpec((1, POOL_DIM), lambda b: (0, 0))],
        out_specs=pl.BlockSpec((seq, POOL_DIM), lambda b: (b, 0)),
        out_shape=jax.ShapeDtypeStruct((batch * seq, POOL_DIM), BF16),
        scratch_shapes=[pltpu.VMEM((seq + POOL_WINDOWS[-1], POOL_DIM), F32)],
        compiler_params=_cparams("parallel"),
    )(u_pool, w_bd, scale.reshape(1, POOL_DIM))


def _rank_select(scoreT, valid, n_rows, topn):
    row = lax.broadcasted_iota(jnp.int32, scoreT.shape, 0)
    rank = jnp.zeros(scoreT.shape, F32)
    for j in range(n_rows):
        sj = scoreT[j:j + 1, :]
        beats = (scoreT > sj) | ((scoreT == sj) & (row < j))
        rj = jnp.sum(beats.astype(F32), axis=0, keepdims=True)
        rank = jnp.where(row == j, rj, rank)
    return (rank < float(topn)) & valid


def _head_halves(rows):
    lane = lax.broadcasted_iota(jnp.int32, (rows, LANES), 1)
    return lane < HEAD_DIM, lane >= HEAD_DIM


def _tile_masks(tq):
    row = lax.broadcasted_iota(jnp.int32, (tq, tq), 0)
    col = lax.broadcasted_iota(jnp.int32, (tq, tq), 1)
    return col <= row, col > row


def _mask_tile(s, t0, mask):
    tq = mask.shape[0]
    parts = []
    if t0 > 0:
        parts.append(s[:, :t0])
    parts.append(jnp.where(mask, s[:, t0:t0 + tq], NEG_INF))
    if t0 + tq < s.shape[1]:
        parts.append(s[:, t0 + tq:])
    return parts[0] if len(parts) == 1 else jnp.concatenate(parts, axis=1)


def _softmax_pv(s, v):
    m = jnp.max(s, axis=1, keepdims=True)
    p = jnp.exp(s - m)
    l = jnp.sum(p, axis=1, keepdims=True)
    return _dot(p.astype(BF16), v) / l


def _block_onehot(seq, shift):
    lane = lax.broadcasted_iota(jnp.int32, (seq, LANES), 1)
    blk = jnp.right_shift(lax.broadcasted_iota(jnp.int32, (seq, LANES), 0), shift)
    return jnp.where((lane & (HEAD_DIM - 1)) == blk, 1.0, 0.0)


def _moba_tile(c, nb, q_ref, v_ref, o_ref, kaug_ref, kmean_ref):
    tq = ATT_TILE
    w = (c + 1) * tq
    qp = q_ref[...]
    qf = qp.astype(F32)
    halves = _head_halves(tq)
    causal, _ = _tile_masks(tq)
    outs = []
    for h in range(2):
        if c > MOBA_TOPK:
            mine_k = _head_halves(kmean_ref.shape[0])[h]
            km_hi, km_lo = _split_bf16(jnp.where(mine_k, kmean_ref[...], 0.0))
            gate = (_dot_nt(km_hi, qp) + _dot_nt(km_lo, qp))[0:nb]
            cand = lax.broadcasted_iota(jnp.int32, (nb, tq), 0)
            valid = cand < c
            gate = jnp.where(valid, gate, NEG_INF)
            keep = _rank_select(gate, valid, c, MOBA_TOPK) | (cand == c)
            neg = jnp.where(keep, 0.0, MASK_NEG)
            lead = HEAD_DIM if h == 0 else 0
            pieces = [neg, jnp.zeros((LANES - lead - nb, tq), F32)]
            if lead:
                pieces.insert(0, jnp.zeros((lead, tq), F32))
            maskcols = jnp.concatenate(pieces, axis=0).T
        else:
            maskcols = jnp.zeros((tq, LANES), F32)
        q_aug = jnp.where(halves[h], qf, maskcols).astype(BF16)
        s = _dot_nt(q_aug, kaug_ref[h, 0:w, :])
        s = _mask_tile(s, c * tq, causal)
        outs.append(_softmax_pv(s, v_ref[0:w, :]))
    o_ref[...] = jnp.where(halves[0], outs[0], outs[1]).astype(o_ref.dtype)


def _moba_kernel(q_ref, k_ref, v_ref, o_ref, kaug_ref, kmean_ref):
    qi = pl.program_id(2)
    seq = k_ref.shape[0]
    nb = seq // MOBA_BLOCK

    @pl.when(qi == 0)
    def _():
        kf = k_ref[...].astype(F32)
        first, second = _head_halves(seq)
        onehot = _block_onehot(seq, int(math.log2(MOBA_BLOCK)))
        kaug_ref[0] = jnp.where(first, kf, onehot).astype(BF16)
        kaug_ref[1] = jnp.where(second, kf, onehot).astype(BF16)
        kmean_ref[...] = jnp.zeros(kmean_ref.shape, F32)
        for j in range(nb):
            blk = k_ref[j * MOBA_BLOCK:(j + 1) * MOBA_BLOCK, :].astype(F32)
            kmean_ref[j:j + 1, :] = jnp.mean(blk, axis=0, keepdims=True)

    for c in range(nb):
        pl.when(qi == c)(functools.partial(_moba_tile, c, nb, q_ref, v_ref, o_ref, kaug_ref, kmean_ref))


def _moba_attention(qkv, batch, seq):
    assert MOBA_BLOCK == ATT_TILE and seq // MOBA_BLOCK <= 16
    nq = seq // ATT_TILE
    hp = MOBA_DIM // LANES
    return pl.pallas_call(
        _moba_kernel,
        name="moba",
        grid=(batch, hp, nq),
        in_specs=[pl.BlockSpec((ATT_TILE, LANES), lambda b, p, i: (b * nq + i, p)),
                  pl.BlockSpec((seq, LANES), lambda b, p, i: (b, hp + p)),
                  pl.BlockSpec((seq, LANES), lambda b, p, i: (b, 2 * hp + p))],
        out_specs=pl.BlockSpec((ATT_TILE, LANES), lambda b, p, i: (b * nq + i, p)),
        out_shape=jax.ShapeDtypeStruct((batch * seq, MOBA_DIM), BF16),
        scratch_shapes=[pltpu.VMEM((2, seq, LANES), BF16), pltpu.VMEM((16, LANES), F32)],
        compiler_params=_cparams("parallel", "parallel", "arbitrary"),
    )(qkv, qkv, qkv)


def _xattn_kernel(x_ref, g_ref, wq_ref, kv_ref, wo_ref, o_ref):
    x = x_ref[...]
    h = _rms_bf16(x, g_ref[...])
    q = (_dot(h, wq_ref[...]) * QK_SCALE).astype(BF16)
    out = x
    for hd in range(XA_HEADS):
        sl = slice(hd * HEAD_DIM, (hd + 1) * HEAD_DIM)
        k = kv_ref[:, sl]
        v = kv_ref[:, XA_DIM + hd * HEAD_DIM:XA_DIM + (hd + 1) * HEAD_DIM]
        s = _dot_nt(q[:, sl], k)
        m = jnp.max(s, axis=1, keepdims=True)
        p = jnp.exp(s - m)
        l = jnp.sum(p, axis=1, keepdims=True)
        o = _dot(p.astype(BF16), v) / l
        out = out + _dot(o.astype(BF16), wo_ref[sl, :])
    o_ref[...] = out


def _cross_attention(x, g, wq, kv_all, layer, wo, seq, mem_len, tm=512):
    t, d = x.shape
    tiles_per_seq = seq // tm
    return pl.pallas_call(
        _xattn_kernel,
        name="xattn",
        grid=(t // tm,),
        in_specs=[pl.BlockSpec((tm, d), lambda i: (i, 0)),
                  pl.BlockSpec((1, d), lambda i: (0, 0)),
                  pl.BlockSpec((d, XA_DIM), lambda i: (0, 0)),
                  pl.BlockSpec((mem_len, 2 * XA_DIM), lambda i: (i // tiles_per_seq, layer)),
                  pl.BlockSpec((XA_DIM, d), lambda i: (0, 0))],
        out_specs=pl.BlockSpec((tm, d), lambda i: (i, 0)),
        out_shape=jax.ShapeDtypeStruct((t, d), F32),
        compiler_params=_cparams("parallel"),
    )(x, g.reshape(1, d), wq, kv_all, wo)


def _ffn_kernel(x_ref, g_ref, wg_ref, wu_ref, wd_ref, o_ref, h_ref):
    f = pl.program_id(1)

    @pl.when(f == 0)
    def _():
        x = x_ref[...]
        h_ref[...] = _rms_bf16(x, g_ref[...])
        o_ref[...] = x

    h = h_ref[...]
    gate = _dot(h, wg_ref[...])
    a = (gate * jax.nn.sigmoid(gate)) * _dot(h, wu_ref[...])
    o_ref[...] += _dot(a.astype(BF16), wd_ref[...])


def _ffn(x, g, wg, wu, wd, tm=1024, tf=256):
    t, d = x.shape
    ff = wg.shape[1]
    return pl.pallas_call(
        _ffn_kernel,
        name="ffn",
        grid=(t // tm, ff // tf),
        in_specs=[pl.BlockSpec((tm, d), lambda i, f: (i, 0)),
                  pl.BlockSpec((1, d), lambda i, f: (0, 0)),
                  pl.BlockSpec((d, tf), lambda i, f: (0, f)),
                  pl.BlockSpec((d, tf), lambda i, f: (0, f)),
                  pl.BlockSpec((tf, d), lambda i, f: (f, 0))],
        out_specs=pl.BlockSpec((tm, d), lambda i, f: (i, 0)),
        out_shape=jax.ShapeDtypeStruct((t, d), F32),
        scratch_shapes=[pltpu.VMEM((tm, d), BF16)],
        compiler_params=_cparams("parallel", "arbitrary"),
    )(x, g.reshape(1, d), wg, wu, wd)


ROUTE_I1, ROUTE_I2, ROUTE_W1, ROUTE_W2, ROUTE_R1, ROUTE_R2 = range(6)
MOE_TILE = 512


def _lane_pick(arr, lane, k):
    return jnp.sum(jnp.where(lane == k, arr, 0.0), axis=1, keepdims=True)


def _router_kernel(x_ref, g_ref, rhi_ref, rlo_ref, route_ref, cnt_ref):
    i = pl.program_id(0)
    tm = x_ref.shape[0]
    lane = lax.broadcasted_iota(jnp.int32, (tm, LANES), 1)
    lane_f = lane.astype(F32)

    @pl.when(i == 0)
    def _():
        cnt_ref[...] = jnp.zeros(cnt_ref.shape, F32)

    h = _rms_bf16(x_ref[...], g_ref[...])
    logits = _dot(h, rhi_ref[...]) + _dot(h, rlo_ref[...])
    logits = jnp.where(lane < N_EXPERTS, logits, NEG_INF)
    m1 = jnp.max(logits, axis=1, keepdims=True)
    i1 = jnp.min(jnp.where(logits == m1, lane_f, float(LANES)), axis=1, keepdims=True)
    rest = jnp.where(lane_f == i1, NEG_INF, logits)
    m2 = jnp.max(rest, axis=1, keepdims=True)
    i2 = jnp.min(jnp.where(rest == m2, lane_f, float(LANES)), axis=1, keepdims=True)
    e2 = jnp.exp(m2 - m1)
    w1 = 1.0 / (1.0 + e2)
    w2 = e2 / (1.0 + e2)

    chosen = jnp.where((lane_f == i1) | (lane_f == i2), 1.0, 0.0)
    r = lax.broadcasted_iota(jnp.int32, (tm, tm), 0)
    c = lax.broadcasted_iota(jnp.int32, (tm, tm), 1)
    earlier = jnp.where(c < r, 1.0, 0.0).astype(BF16)
    prefix = _dot(earlier, chosen.astype(BF16)) + cnt_ref[0:1, :]
    rank1 = jnp.sum(jnp.where(lane_f == i1, prefix, 0.0), axis=1, keepdims=True)
    rank2 = jnp.sum(jnp.where(lane_f == i2, prefix, 0.0), axis=1, keepdims=True)
    cnt_ref[0:1, :] = cnt_ref[0:1, :] + jnp.sum(chosen, axis=0, keepdims=True)

    cols = {ROUTE_I1: i1, ROUTE_I2: i2, ROUTE_W1: w1, ROUTE_W2: w2, ROUTE_R1: rank1, ROUTE_R2: rank2}
    route = jnp.zeros((tm, LANES), F32)
    for k, val in cols.items():
        route = jnp.where(lane == k, val, route)
    route_ref[...] = route


def _dispatch_kernel(pos1_ref, pos2_ref, x_hbm, xs_in, xs_hbm, sem):
    del xs_in
    i = pl.program_id(0)
    n = pos1_ref.shape[0]

    def row_copy(t, p):
        return pltpu.make_async_copy(x_hbm.at[pl.ds(t, 1)], xs_hbm.at[pl.ds(p, 1)], sem)

    def issue(r, carry):
        row_copy(i * n + r, pos1_ref[r]).start()
        row_copy(i * n + r, pos2_ref[r]).start()
        return carry

    def drain(r, carry):
        row_copy(0, 0).wait()
        row_copy(0, 0).wait()
        return carry

    lax.fori_loop(0, n, issue, 0)
    lax.fori_loop(0, n, drain, 0)


def _expert_kernel(te_ref, nu_ref, xs_ref, g_ref, wg_ref, wu_ref, wd_ref, y_ref, h_ref):
    del te_ref
    i = pl.program_id(0)
    f = pl.program_id(1)
    used = i < nu_ref[0]

    @pl.when(f == 0)
    def _():
        h_ref[...] = _rms_bf16(xs_ref[...], g_ref[...])
        y_ref[...] = jnp.zeros(y_ref.shape, y_ref.dtype)

    @pl.when(used)
    def _():
        h = h_ref[...]
        gate = _dot(h, wg_ref[...])
        a = (gate * jax.nn.sigmoid(gate)) * _dot(h, wu_ref[...])
        y_ref[...] += _dot(a.astype(BF16), wd_ref[...])


def _combine_kernel(pos1_ref, pos2_ref, x_ref, route_ref, y_hbm, o_ref, buf1, buf2, sem):
    n = pos1_ref.shape[0]

    def row_copy(p, buf, r):
        return pltpu.make_async_copy(y_hbm.at[pl.ds(p, 1)], buf.at[pl.ds(r, 1)], sem)

    def issue(r, carry):
        row_copy(pos1_ref[r], buf1, r).start()
        row_copy(pos2_ref[r], buf2, r).start()
        return carry

    def drain(r, carry):
        row_copy(0, buf1, 0).wait()
        row_copy(0, buf2, 0).wait()
        return carry

    lax.fori_loop(0, n, issue, 0)
    lax.fori_loop(0, n, drain, 0)
    lane = lax.broadcasted_iota(jnp.int32, route_ref.shape, 1)
    route = route_ref[...]
    w1 = _lane_pick(route, lane, ROUTE_W1)
    w2 = _lane_pick(route, lane, ROUTE_W2)
    o_ref[...] = x_ref[...] + w1 * buf1[...] + w2 * buf2[...]


def _moe(x, g, router_w, wg, wu, wd, tf=256, t_route=512, t_disp=1024, t_comb=256):
    t, d = x.shape
    ne, _, ff = wg.shape
    rows = MOE_TILE
    n_rows = 2 * t + ne * rows
    n_tiles = n_rows // rows
    nf = ff // tf
    g2 = g.reshape(1, d)
    rw = jnp.zeros((d, LANES), F32).at[:, :ne].set(router_w)
    rhi = rw.astype(BF16)
    rlo = (rw - rhi.astype(F32)).astype(BF16)

    route, cnt = pl.pallas_call(
        _router_kernel,
        name="moe_route",
        grid=(t // t_route,),
        in_specs=[pl.BlockSpec((t_route, d), lambda i: (i, 0)),
                  pl.BlockSpec((1, d), lambda i: (0, 0)),
                  pl.BlockSpec((d, LANES), lambda i: (0, 0)),
                  pl.BlockSpec((d, LANES), lambda i: (0, 0))],
        out_specs=[pl.BlockSpec((t_route, LANES), lambda i: (i, 0)),
                   pl.BlockSpec((8, LANES), lambda i: (0, 0))],
        out_shape=[jax.ShapeDtypeStruct((t, LANES), F32), jax.ShapeDtypeStruct((8, LANES), F32)],
        compiler_params=_cparams("arbitrary"),
    )(x, g2, rhi, rlo)

    counts = cnt[0, :ne].astype(jnp.int32)
    gsz = (counts + rows - 1) // rows * rows
    ends = jnp.cumsum(gsz)
    offs = ends - gsz
    i1 = route[:, ROUTE_I1].astype(jnp.int32)
    i2 = route[:, ROUTE_I2].astype(jnp.int32)
    pos1 = offs[i1] + route[:, ROUTE_R1].astype(jnp.int32)
    pos2 = offs[i2] + route[:, ROUTE_R2].astype(jnp.int32)
    tile_start = jnp.arange(n_tiles, dtype=jnp.int32) * rows
    tile_expert = jnp.minimum(jnp.sum(tile_start[:, None] >= ends[None, :], axis=1), ne - 1).astype(jnp.int32)
    n_used = (ends[-1] // rows).astype(jnp.int32).reshape(1)

    smem_idx = lambda n: pl.BlockSpec((n,), lambda i: (i,), memory_space=pltpu.SMEM)
    hbm = pl.BlockSpec(memory_space=pl.ANY)
    xs = pl.pallas_call(
        _dispatch_kernel,
        name="moe_dispatch",
        grid=(t // t_disp,),
        in_specs=[smem_idx(t_disp), smem_idx(t_disp), hbm, hbm],
        out_specs=hbm,
        out_shape=jax.ShapeDtypeStruct((n_rows, d), F32),
        scratch_shapes=[pltpu.SemaphoreType.DMA(())],
        input_output_aliases={3: 0},
        compiler_params=_cparams("arbitrary"),
    )(pos1, pos2, x, jnp.zeros((n_rows, d), F32))

    def w_in_map(i, f, te, nu):
        return te[i], 0, jnp.where(i < nu[0], f, nf - 1)

    def w_out_map(i, f, te, nu):
        return te[i], jnp.where(i < nu[0], f, nf - 1), 0

    y = pl.pallas_call(
        _expert_kernel,
        name="moe_experts",
        grid_spec=pltpu.PrefetchScalarGridSpec(
            num_scalar_prefetch=2,
            grid=(n_tiles, nf),
            in_specs=[pl.BlockSpec((rows, d), lambda i, f, te, nu: (i, 0)),
                      pl.BlockSpec((1, d), lambda i, f, te, nu: (0, 0)),
                      pl.BlockSpec((None, d, tf), w_in_map),
                      pl.BlockSpec((None, d, tf), w_in_map),
                      pl.BlockSpec((None, tf, d), w_out_map)],
            out_specs=pl.BlockSpec((rows, d), lambda i, f, te, nu: (i, 0)),
            scratch_shapes=[pltpu.VMEM((rows, d), BF16)]),
        out_shape=jax.ShapeDtypeStruct((n_rows, d), F32),
        compiler_params=_cparams("arbitrary", "arbitrary"),
    )(tile_expert, n_used, xs, g2, wg, wu, wd)

    return pl.pallas_call(
        _combine_kernel,
        name="moe_combine",
        grid=(t // t_comb,),
        in_specs=[smem_idx(t_comb), smem_idx(t_comb),
                  pl.BlockSpec((t_comb, d), lambda i: (i, 0)),
                  pl.BlockSpec((t_comb, LANES), lambda i: (i, 0)),
                  hbm],
        out_specs=pl.BlockSpec((t_comb, d), lambda i: (i, 0)),
        out_shape=jax.ShapeDtypeStruct((t, d), F32),
        scratch_shapes=[pltpu.VMEM((t_comb, d), F32), pltpu.VMEM((t_comb, d), F32),
                        pltpu.SemaphoreType.DMA(())],
        compiler_params=_cparams("arbitrary"),
    )(pos1, pos2, x, route, y)


def _gelu_tanh(x):
    return 0.5 * x * (1.0 + jnp.tanh(math.sqrt(2.0 / math.pi) * (x + 0.044715 * (x * x * x))))


def _compress_kernel(x_ref, pe_ref, w1_ref, w2_ref, o_ref, sh_ref):
    rows = x_ref.shape[0]
    half = w1_ref.shape[0] // 2
    x = x_ref[...]
    top = _dot(x, w1_ref[0:half, :])
    bot = _dot(x, w1_ref[half:2 * half, :])
    bias = _dot(pe_ref[...], w1_ref[...])[0:1, :]
    sh_ref[0:rows, :] = bot
    sh_ref[rows:rows + 8, :] = jnp.zeros((8, bot.shape[1]), F32)
    hidden = top + sh_ref[1:rows + 1, :] + bias
    o_ref[...] = _dot(_gelu_tanh(hidden).astype(BF16), w2_ref[...]).astype(o_ref.dtype)


def _compress(chunks, pe, w1, w2, rows=1024):
    _, r, cw = chunks.shape
    hid = w1.shape[2]
    return pl.pallas_call(
        _compress_kernel,
        name="compress",
        grid=(2, r // rows),
        in_specs=[pl.BlockSpec((None, rows, cw), lambda a, i: (a, i, 0)),
                  pl.BlockSpec((None, 16, 2 * cw), lambda a, i: (a, 0, 0)),
                  pl.BlockSpec((None, 2 * cw, hid), lambda a, i: (a, 0, 0)),
                  pl.BlockSpec((None, hid, HEAD_DIM), lambda a, i: (a, 0, 0))],
        out_specs=pl.BlockSpec((None, rows, HEAD_DIM), lambda a, i: (a, i, 0)),
        out_shape=jax.ShapeDtypeStruct((2, r, HEAD_DIM), BF16),
        scratch_shapes=[pltpu.VMEM((rows + 8, hid), F32)],
        compiler_params=_cparams("parallel", "parallel"),
    )(chunks, pe, w1, w2)


def _nsa_cmp_kernel(n_slc, q_ref, ka_ref, kb_ref, ovt_ref, ocmp_ref, selm_ref):
    qi = pl.program_id(2)
    tq = ATT_TILE
    n_pad = ka_ref.shape[0]
    t_q = qi * tq + lax.broadcasted_iota(jnp.int32, (tq, n_pad), 0)
    n_c = lax.broadcasted_iota(jnp.int32, (tq, n_pad), 1)
    mask_c = (n_c * CMP_STRIDE + (CMP_LEN - 1)) <= t_q
    halves = _head_halves(tq)
    ka = ka_ref[...]
    kb = kb_ref[...]
    psum = jnp.zeros((tq, n_pad), F32)
    for pr in range(NSA_HPG // 2):
        qf = q_ref[:, pr * LANES:(pr + 1) * LANES].astype(F32)
        outs = []
        for h in range(2):
            qz = jnp.where(halves[h], qf, 0.0).astype(BF16)
            kmat, vmat = (ka, kb) if h == 0 else (kb, ka)
            s = jnp.where(mask_c, _dot_nt(qz, kmat), NEG_INF)
            m = jnp.max(s, axis=1, keepdims=True)
            m = jnp.where(m == NEG_INF, 0.0, m)
            e = jnp.exp(s - m)
            p = e / jnp.maximum(jnp.sum(e, axis=1, keepdims=True), 1e-30)
            psum = psum + p
            outs.append(_dot(p.astype(BF16), vmat))
        ocmp_ref[:, pr * LANES:(pr + 1) * LANES] = jnp.where(halves[0], outs[0], outs[1]).astype(ocmp_ref.dtype)

    ps_hi, ps_lo = _split_bf16(psum)
    ovt = ovt_ref[...]
    p_slc = (_dot_nt(ovt, ps_hi) + _dot_nt(ovt, ps_lo))[0:n_slc]
    cand = lax.broadcasted_iota(jnp.int32, (n_slc, tq), 0)
    t_blk = jnp.right_shift(qi * tq + lax.broadcasted_iota(jnp.int32, (n_slc, tq), 1), int(math.log2(SLC_BLOCK)))
    valid = cand <= t_blk
    forced = (cand == 0) | (cand == t_blk) | (cand == t_blk - 1)
    score = jnp.where(valid, jnp.where(forced, FORCE_SCORE, p_slc), -1.0)
    keep = _rank_select(score, valid, n_slc, min(SLC_TOPN, n_slc))
    neg = jnp.where(keep, 0.0, MASK_NEG)
    zero = jnp.zeros((HEAD_DIM - n_slc, tq), F32)
    selm_ref[...] = jnp.concatenate([neg, zero, neg, zero], axis=0).T.astype(selm_ref.dtype)


def _nsa_sw_tile(c, q_ref, selm_ref, ksel_ref, kwin_ref, gate_ref, ex_ref, ocmp_ref, o_ref,
                 kaug_ref, vb_ref, kwb_ref):
    tq = ATT_TILE
    w = (c + 1) * tq
    lo = max(c - WINDOW // tq, 0) * tq
    qf = q_ref[...].astype(F32)
    selm = selm_ref[...].astype(F32)
    halves = _head_halves(tq)
    causal, beyond = _tile_masks(tq)
    sel_o, win_o = [], []
    for h in range(2):
        q_aug = jnp.where(halves[h], qf, selm).astype(BF16)
        qz = jnp.where(halves[h], qf, 0.0).astype(BF16)
        if h == 0:
            k_sel, v_sel = kaug_ref[0, 0:w, :], vb_ref[0:w, :]
            k_win, v_win = kwin_ref[lo:w, :], kwb_ref[lo:w, :]
        else:
            k_sel, v_sel = kaug_ref[1, 0:w, :], ksel_ref[0:w, :]
            k_win, v_win = kwb_ref[lo:w, :], kwin_ref[lo:w, :]
        s = _mask_tile(_dot_nt(q_aug, k_sel), c * tq, causal)
        sel_o.append(_softmax_pv(s, v_sel))
        s = _mask_tile(_dot_nt(qz, k_win), c * tq - lo, causal)
        if c * tq - lo == WINDOW:
            s = _mask_tile(s, 0, beyond)
        win_o.append(_softmax_pv(s, v_win))
    o_sel = jnp.where(halves[0], sel_o[0], sel_o[1])
    o_win = jnp.where(halves[0], win_o[0], win_o[1])
    g_hi, g_lo = _split_bf16(gate_ref[...])
    ex = ex_ref[...]
    gexp = _dot(g_hi, ex) + _dot(g_lo, ex)
    y = (gexp[:, 0:LANES] * ocmp_ref[...].astype(F32) + gexp[:, LANES:2 * LANES] * o_sel
         + gexp[:, 2 * LANES:3 * LANES] * o_win)
    o_ref[...] = y.astype(o_ref.dtype)


def _nsa_sw_kernel(q_ref, selm_ref, ksel_ref, kwin_ref, gate_ref, ex_ref, ocmp_ref, o_ref,
                   kaug_ref, vb_ref, kwb_ref):
    pr = pl.program_id(2)
    qi = pl.program_id(3)
    seq = ksel_ref.shape[0]

    @pl.when((pr == 0) & (qi == 0))
    def _():
        kv = ksel_ref[...].astype(F32)
        first, second = _head_halves(seq)
        onehot = _block_onehot(seq, int(math.log2(SLC_BLOCK)))
        swapped = pltpu.roll(kv, HEAD_DIM, axis=1)
        kaug_ref[0] = jnp.where(first, kv, onehot).astype(BF16)
        kaug_ref[1] = jnp.where(second, swapped, onehot).astype(BF16)
        vb_ref[...] = swapped.astype(BF16)
        kwb_ref[...] = pltpu.roll(kwin_ref[...].astype(F32), HEAD_DIM, axis=1).astype(BF16)

    for c in range(seq // ATT_TILE):
        pl.when(qi == c)(functools.partial(_nsa_sw_tile, c, q_ref, selm_ref, ksel_ref, kwin_ref, gate_ref, ex_ref,
                                           ocmp_ref, o_ref, kaug_ref, vb_ref, kwb_ref))


def _nsa_attention(q_raw, q_rot, kv, cmp_kv, gates, batch, seq):
    nq = seq // ATT_TILE
    g = NSA_GROUPS
    gw = NSA_HPG * HEAD_DIM
    n_pairs = NSA_HEADS // 2
    n_cmp = (seq - CMP_LEN) // CMP_STRIDE + 1
    n_cmp_pad = seq // CMP_STRIDE
    n_slc = seq // SLC_BLOCK
    assert n_cmp_pad == LANES and n_slc <= HEAD_DIM // 2 and WINDOW % ATT_TILE == 0

    ex = np.zeros((n_pairs, LANES, 3 * LANES), np.float32)
    for hd in range(NSA_HEADS):
        for br in range(3):
            lane0 = br * LANES + (hd % 2) * HEAD_DIM
            ex[hd // 2, hd * 3 + br, lane0:lane0 + HEAD_DIM] = 1.0
    c_s = np.arange(n_cmp) * CMP_STRIDE
    s_s = np.arange(n_slc) * SLC_BLOCK
    ov = np.clip(np.minimum(c_s[:, None] + CMP_LEN, s_s[None, :] + SLC_BLOCK)
                 - np.maximum(c_s[:, None], s_s[None, :]), 0, None) / CMP_LEN
    ovt = np.zeros((LANES, n_cmp_pad), np.float32)
    ovt[:n_slc, :n_cmp] = ov.T

    ka = jnp.concatenate([cmp_kv[0], cmp_kv[1]], axis=-1)
    kb = jnp.concatenate([cmp_kv[1], cmp_kv[0]], axis=-1)
    o_cmp, selm = pl.pallas_call(
        functools.partial(_nsa_cmp_kernel, n_slc),
        name="nsa_cmp",
        grid=(batch, g, nq),
        in_specs=[pl.BlockSpec((ATT_TILE, gw), lambda b, gi, i: (b * nq + i, gi)),
                  pl.BlockSpec((n_cmp_pad, LANES), lambda b, gi, i: (b * g + gi, 0)),
                  pl.BlockSpec((n_cmp_pad, LANES), lambda b, gi, i: (b * g + gi, 0)),
                  pl.BlockSpec((LANES, n_cmp_pad), lambda b, gi, i: (0, 0))],
        out_specs=[pl.BlockSpec((ATT_TILE, gw), lambda b, gi, i: (b * nq + i, gi)),
                   pl.BlockSpec((ATT_TILE, LANES), lambda b, gi, i: (b * nq + i, gi))],
        out_shape=[jax.ShapeDtypeStruct((batch * seq, NSA_HEADS * HEAD_DIM), BF16),
                   jax.ShapeDtypeStruct((batch * seq, g * LANES), BF16)],
        compiler_params=_cparams("parallel", "parallel", "parallel"),
    )(q_raw, ka, kb, jnp.asarray(ovt, BF16))

    ppg = NSA_HPG // 2
    return pl.pallas_call(
        _nsa_sw_kernel,
        name="nsa_sw",
        grid=(batch, g, ppg, nq),
        in_specs=[pl.BlockSpec((ATT_TILE, LANES), lambda b, gi, p, i: (b * nq + i, gi * ppg + p)),
                  pl.BlockSpec((ATT_TILE, LANES), lambda b, gi, p, i: (b * nq + i, gi)),
                  pl.BlockSpec((seq, LANES), lambda b, gi, p, i: (b, g + gi)),
                  pl.BlockSpec((seq, LANES), lambda b, gi, p, i: (b, 2 * g + gi)),
                  pl.BlockSpec((ATT_TILE, LANES), lambda b, gi, p, i: (b * nq + i, 0)),
                  pl.BlockSpec((None, LANES, 3 * LANES), lambda b, gi, p, i: (gi * ppg + p, 0, 0)),
                  pl.BlockSpec((ATT_TILE, LANES), lambda b, gi, p, i: (b * nq + i, gi * ppg + p))],
        out_specs=pl.BlockSpec((ATT_TILE, LANES), lambda b, gi, p, i: (b * nq + i, gi * ppg + p)),
        out_shape=jax.ShapeDtypeStruct((batch * seq, NSA_HEADS * HEAD_DIM), BF16),
        scratch_shapes=[pltpu.VMEM((2, seq, LANES), BF16), pltpu.VMEM((seq, LANES), BF16),
                        pltpu.VMEM((seq, LANES), BF16)],
        compiler_params=_cparams("parallel", "parallel", "arbitrary", "arbitrary"),
    )(q_rot, selm, kv, kv, gates, jnp.asarray(ex, BF16), o_cmp)


def _even_mixer(x, g, w_in, pool_w, pool_scale, w_out, rope_pair, batch, seq):
    chunk = 2 * LANES
    plan = [(0, POOL_DIM, [(0, 0, "none", 0)])]
    for c in range(3 * MOBA_DIM // chunk):
        op = "rope_scale" if c < MOBA_DIM // chunk else ("rope" if c < 2 * MOBA_DIM // chunk else "none")
        plan.append((POOL_DIM + c * chunk, chunk, [(1, c * chunk, op, 0)]))
    u_pool, qkv = _norm_project(x, g, w_in.astype(BF16), plan, [rope_pair],
                                [(POOL_DIM, F32), (3 * MOBA_DIM, BF16)], seq)
    w_bd = jax.scipy.linalg.block_diag(*[pool_w[i] for i in range(pool_w.shape[0])]).astype(BF16)
    y_a = _pool_mixer(u_pool, w_bd, pool_scale, batch, seq)
    y_b = _moba_attention(qkv, batch, seq)
    w_out = w_out.astype(BF16)
    return _out_proj(x, [y_a, y_b], [w_out[:POOL_DIM], w_out[POOL_DIM:]])


def _odd_mixer(x, g, w_in, pe_k, pe_v, k_w1, k_w2, v_w1, v_w2, w_out, rope_pair, rope_single, batch, seq):
    d = x.shape[1]
    qd = NSA_HEADS * HEAD_DIM
    kvd = NSA_KV_DIM
    wq = w_in[:, :qd]
    parts = [w_in[:, qd + i * kvd:qd + (i + 1) * kvd].reshape(d, NSA_GROUPS, HEAD_DIM) for i in range(6)]
    pairs = [jnp.concatenate([parts[2 * i], parts[2 * i + 1]], axis=-1).reshape(d, 2 * kvd) for i in range(3)]
    n_gate = 3 * NSA_HEADS
    w_gate = jnp.zeros((d, LANES), F32).at[:, :n_gate].set(w_in[:, qd + 6 * kvd:])
    w_all = jnp.concatenate([wq] + pairs + [w_gate], axis=1).astype(BF16)

    chunk = 2 * LANES
    plan = []
    for c in range(qd // chunk):
        plan.append((c * chunk, chunk, [(0, c * chunk, "scale", 0), (1, c * chunk, "rope_scale", 0)]))
    for c in range(6 * kvd // chunk):
        branch = c // (2 * kvd // chunk)
        op = "none" if branch == 0 else "rope"
        plan.append((qd + c * chunk, chunk, [(2, c * chunk, op, 1)]))
    plan.append((qd + 6 * kvd, LANES, [(3, 0, "sigmoid", 0)]))
    q_raw, q_rot, kv, gates = _norm_project(
        x, g, w_all, plan, [rope_pair, rope_single],
        [(qd, BF16), (qd, BF16), (6 * kvd, BF16), (LANES, F32)], seq)

    n_chunks = seq // CMP_STRIDE
    cmp_in = kv[:, :2 * kvd].reshape(batch, n_chunks, CMP_STRIDE, NSA_GROUPS, 2, HEAD_DIM)
    cmp_in = cmp_in.transpose(4, 0, 3, 1, 2, 5).reshape(2, batch * NSA_GROUPS * n_chunks, CMP_STRIDE * HEAD_DIM)
    pe = jnp.stack([pe_k, pe_v]).reshape(2, 1, CMP_LEN * HEAD_DIM)
    pe = jnp.broadcast_to(pe, (2, 16, CMP_LEN * HEAD_DIM)).astype(BF16)
    w1 = jnp.stack([k_w1, v_w1]).astype(BF16)
    w2 = jnp.stack([k_w2, v_w2]).astype(BF16)
    cmp_kv = _compress(cmp_in, pe, w1, w2)

    y = _nsa_attention(q_raw, q_rot, kv, cmp_kv, gates, batch, seq)
    return _out_proj(x, [y], [w_out.astype(BF16)])


def kernel(x, mem, positions, norm_g, mem_g, final_g, w_in_ab, pool_w, pool_scale, w_out_ab, ffn_w_gate, ffn_w_up, ffn_w_down, w_in_c, cmp_pe_k, cmp_pe_v, cmp_k_w1, cmp_k_w2, cmp_v_w1, cmp_v_w2, w_out_c, router_w, moe_w_gate, moe_w_up, moe_w_down, xa_wq, xa_wkv, xa_wo):
    batch, seq, d = x.shape
    mem_len = mem.shape[1]
    depth = norm_g.shape[0]
    rope_pair, rope_single = _rope_tables(positions)

    wkv_all = jnp.concatenate([xa_wkv[l] for l in range(depth)], axis=1).astype(BF16)
    kv_all = _norm_matmul(mem.reshape(batch * mem_len, d), mem_g, wkv_all)

    xf = x.reshape(batch * seq, d)
    for layer in range(depth):
        i = layer // 2
        if layer % 2 == 0:
            xf = _even_mixer(xf, norm_g[layer, 0], w_in_ab[i], pool_w[i], pool_scale[i], w_out_ab[i],
                             rope_pair, batch, seq)
        else:
            xf = _odd_mixer(xf, norm_g[layer, 0], w_in_c[i], cmp_pe_k[i], cmp_pe_v[i], cmp_k_w1[i], cmp_k_w2[i],
                            cmp_v_w1[i], cmp_v_w2[i], w_out_c[i], rope_pair, rope_single, batch, seq)
        xf = _cross_attention(xf, norm_g[layer, 1], xa_wq[layer].astype(BF16), kv_all, layer,
                              xa_wo[layer].astype(BF16), seq, mem_len)
        if layer % 2 == 0:
            xf = _ffn(xf, norm_g[layer, 2], ffn_w_gate[i].astype(BF16), ffn_w_up[i].astype(BF16),
                      ffn_w_down[i].astype(BF16))
        else:
            xf = _moe(xf, norm_g[layer, 2], router_w[i], moe_w_gate[i].astype(BF16), moe_w_up[i].astype(BF16),
                      moe_w_down[i].astype(BF16))
    return _final_norm(xf, final_g).reshape(batch, seq, d)
```

```python
import functools
import math

import numpy as np
import jax
import jax.numpy as jnp
from jax import lax
from jax.experimental import pallas as pl
from jax.experimental.pallas import tpu as pltpu

F32 = jnp.float32
BF16 = jnp.bfloat16

HEAD_DIM = 64
ROPE_DIM = 16
ROPE_THETA = 500000.0
NORM_EPS = 1e-5
QK_SCALE = HEAD_DIM ** -0.5

POOL_WINDOWS = (2, 4, 8, 16)
POOL_DIM = 256
MOBA_HEADS = 12
MOBA_DIM = MOBA_HEADS * HEAD_DIM
MOBA_BLOCK = 256
MOBA_TOPK = 3
NSA_HEADS = 16
NSA_GROUPS = 4
NSA_HPG = 4
NSA_KV_DIM = NSA_GROUPS * HEAD_DIM
CMP_LEN = 32
CMP_STRIDE = 16
SLC_BLOCK = 64
SLC_TOPN = 16
WINDOW = 512
FORCE_SCORE = 1e4
XA_HEADS = 4
XA_DIM = XA_HEADS * HEAD_DIM
N_EXPERTS = 8

LANES = 128
ATT_TILE = 256
VMEM_LIMIT = 56 * 1024 * 1024
NEG_INF = float("-inf")
MASK_NEG = -1e30


def _cparams(*sem):
    return pltpu.CompilerParams(dimension_semantics=sem, vmem_limit_bytes=VMEM_LIMIT)


def _dot(a, b):
    return jnp.dot(a, b, preferred_element_type=F32)


def _dot_nt(a, b):
    return lax.dot_general(a, b, (((1,), (1,)), ((), ())), preferred_element_type=F32)


def _split_bf16(a):
    hi = a.astype(BF16)
    lo = (a - hi.astype(F32)).astype(BF16)
    return hi, lo


def _rms_bf16(x, g):
    ms = jnp.mean(x * x, axis=-1, keepdims=True)
    return (x * lax.rsqrt(ms + NORM_EPS) * g).astype(BF16)


def _apply_rope(acc, c, a, b):
    half = ROPE_DIM // 2
    return acc * c + pltpu.roll(acc, LANES - half, axis=1) * a + pltpu.roll(acc, half, axis=1) * b


def _proj_kernel(plan, n_tab, x_ref, g_ref, w_ref, *rest):
    tabs = rest[:3 * n_tab]
    outs = rest[3 * n_tab:]
    h = _rms_bf16(x_ref[...], g_ref[...])
    for wc0, width, sinks in plan:
        acc = _dot(h, w_ref[:, wc0:wc0 + width])
        for oi, oc0, op, tab in sinks:
            for s in range(width // LANES):
                val = acc[:, s * LANES:(s + 1) * LANES]
                if op in ("rope", "rope_scale"):
                    c, a, b = (tabs[3 * tab + i][...] for i in range(3))
                    val = _apply_rope(val, c, a, b)
                if op in ("scale", "rope_scale"):
                    val = val * QK_SCALE
                if op == "sigmoid":
                    val = jax.nn.sigmoid(val)
                o = outs[oi]
                o[:, oc0 + s * LANES:oc0 + (s + 1) * LANES] = val.astype(o.dtype)


def _norm_project(x, g, w, plan, tables, out_defs, seq, tm=512):
    t, d = x.shape
    n = w.shape[1]
    n_seq_tiles = seq // tm
    in_specs = [
        pl.BlockSpec((tm, d), lambda i: (i, 0)),
        pl.BlockSpec((1, d), lambda i: (0, 0)),
        pl.BlockSpec((d, n), lambda i: (0, 0)),
    ]
    flat_tabs = []
    for tset in tables:
        for tb in tset:
            flat_tabs.append(tb)
            in_specs.append(pl.BlockSpec((tm, LANES), lambda i: (i % n_seq_tiles, 0)))
    out_shape = [jax.ShapeDtypeStruct((t, wd), dt) for wd, dt in out_defs]
    out_specs = [pl.BlockSpec((tm, wd), lambda i: (i, 0)) for wd, _ in out_defs]
    return pl.pallas_call(
        functools.partial(_proj_kernel, plan, len(tables)),
        name="proj",
        grid=(t // tm,),
        in_specs=in_specs,
        out_specs=out_specs,
        out_shape=out_shape,
        compiler_params=_cparams("parallel"),
    )(x, g.reshape(1, d), w, *flat_tabs)


def _rope_tables(positions):
    half = ROPE_DIM // 2
    inv = ROPE_THETA ** (-jnp.arange(0, ROPE_DIM, 2, dtype=F32) / ROPE_DIM)
    ang = positions.astype(F32)[:, None] * inv[None, :]
    cos, sin = jnp.cos(ang), jnp.sin(ang)
    s = positions.shape[0]
    z8 = jnp.zeros((s, half), F32)
    rest0 = jnp.zeros((s, HEAD_DIM - ROPE_DIM), F32)
    rest1 = jnp.ones((s, HEAD_DIM - ROPE_DIM), F32)
    c64 = jnp.concatenate([cos, cos, rest1], -1)
    a64 = jnp.concatenate([-sin, z8, rest0], -1)
    b64 = jnp.concatenate([z8, sin, rest0], -1)
    one64 = jnp.ones((s, HEAD_DIM), F32)
    zero64 = jnp.zeros((s, HEAD_DIM), F32)
    pair = tuple(jnp.concatenate([m, m], -1) for m in (c64, a64, b64))
    single = (jnp.concatenate([c64, one64], -1), jnp.concatenate([a64, zero64], -1),
              jnp.concatenate([b64, zero64], -1))
    return pair, single


def _norm_matmul_kernel(x_ref, g_ref, w_ref, o_ref):
    h = _rms_bf16(x_ref[...], g_ref[...])
    o_ref[...] = _dot(h, w_ref[...]).astype(o_ref.dtype)


def _norm_matmul(x, g, w, tm=512):
    t, d = x.shape
    n = w.shape[1]
    return pl.pallas_call(
        _norm_matmul_kernel,
        name="mem_kv",
        grid=(t // tm,),
        in_specs=[pl.BlockSpec((tm, d), lambda i: (i, 0)),
                  pl.BlockSpec((1, d), lambda i: (0, 0)),
                  pl.BlockSpec((d, n), lambda i: (0, 0))],
        out_specs=pl.BlockSpec((tm, n), lambda i: (i, 0)),
        out_shape=jax.ShapeDtypeStruct((t, n), BF16),
        compiler_params=_cparams("parallel"),
    )(x, g.reshape(1, d), w)


def _final_norm_kernel(x_ref, g_ref, o_ref):
    x = x_ref[...]
    ms = jnp.mean(x * x, axis=-1, keepdims=True)
    o_ref[...] = x * lax.rsqrt(ms + NORM_EPS) * g_ref[...]


def _final_norm(x, g, tm=1024):
    t, d = x.shape
    return pl.pallas_call(
        _final_norm_kernel,
        name="final_norm",
        grid=(t // tm,),
        in_specs=[pl.BlockSpec((tm, d), lambda i: (i, 0)), pl.BlockSpec((1, d), lambda i: (0, 0))],
        out_specs=pl.BlockSpec((tm, d), lambda i: (i, 0)),
        out_shape=jax.ShapeDtypeStruct((t, d), F32),
        compiler_params=_cparams("parallel"),
    )(x, g.reshape(1, d))


def _out_proj_kernel(n_in, x_ref, *rest):
    ys = rest[:n_in]
    ws = rest[n_in:2 * n_in]
    o_ref = rest[2 * n_in]
    acc = x_ref[...]
    for y, w in zip(ys, ws):
        acc = acc + _dot(y[...], w[...])
    o_ref[...] = acc


def _out_proj(x, ys, ws, tm=512):
    t, d = x.shape
    in_specs = [pl.BlockSpec((tm, d), lambda i: (i, 0))]
    in_specs += [pl.BlockSpec((tm, y.shape[1]), lambda i: (i, 0)) for y in ys]
    in_specs += [pl.BlockSpec(w.shape, lambda i: (0, 0)) for w in ws]
    return pl.pallas_call(
        functools.partial(_out_proj_kernel, len(ys)),
        name="out_proj",
        grid=(t // tm,),
        in_specs=in_specs,
        out_specs=pl.BlockSpec((tm, d), lambda i: (i, 0)),
        out_shape=jax.ShapeDtypeStruct((t, d), F32),
        compiler_params=_cparams("parallel"),
    )(x, *ys, *ws)


def _pool_kernel(u_ref, w_ref, sc_ref, o_ref, pad_ref):
    s = u_ref.shape[0]
    maxw = POOL_WINDOWS[-1]
    u = u_ref[...]
    pad_ref[0:maxw, :] = jnp.zeros((maxw, POOL_DIM), F32)
    t1 = (lax.broadcasted_iota(jnp.int32, (s, POOL_DIM), 0) + 1).astype(F32)
    lane = lax.broadcasted_iota(jnp.int32, (s, POOL_DIM), 1)
    gdim = POOL_DIM // len(POOL_WINDOWS)
    cur = u
    pooled = jnp.zeros_like(u)
    shift = 1
    for gi, w in enumerate(POOL_WINDOWS):
        pad_ref[maxw:maxw + s, :] = cur
        cur = cur + pad_ref[maxw - shift:maxw - shift + s, :]
        shift *= 2
        assert shift == w
        mean = cur / jnp.minimum(t1, float(w))
        pooled = jnp.where((lane >= gi * gdim) & (lane < (gi + 1) * gdim), mean, pooled)
    pooled = (pooled - u).astype(BF16)
    o_ref[...] = (_dot(pooled, w_ref[...]) * sc_ref[...]).astype(o_ref.dtype)


def _pool_mixer(u_pool, w_bd, scale, batch, seq):
    return pl.pallas_call(
        _pool_kernel,
        name="pool",
        grid=(batch,),
        in_specs=[pl.BlockSpec((seq, POOL_DIM), lambda b: (b, 0)),
                  pl.BlockSpec((POOL_DIM, POOL_DIM), lambda b: (0, 0)),
                  pl.BlockSpec((1, POOL_DIM), lambda b: (0, 0))],
        out_specs=pl.BlockSpec((seq, POOL_DIM), lambda b: (b, 0)),
        out_shape=jax.ShapeDtypeStruct((batch * seq, POOL_DIM), BF16),
        scratch_shapes=[pltpu.VMEM((seq + POOL_WINDOWS[-1], POOL_DIM), F32)],
        compiler_params=_cparams("parallel"),
    )(u_pool, w_bd, scale.reshape(1, POOL_DIM))


def _rank_select(scoreT, valid, n_rows, topn):
    row = lax.broadcasted_iota(jnp.int32, scoreT.shape, 0)
    rank = jnp.zeros(scoreT.shape, F32)
    for j in range(n_rows):
        sj = scoreT[j:j + 1, :]
        beats = (scoreT > sj) | ((scoreT == sj) & (row < j))
        rj = jnp.sum(beats.astype(F32), axis=0, keepdims=True)
        rank = jnp.where(row == j, rj, rank)
    return (rank < float(topn)) & valid


def _head_halves(rows):
    lane = lax.broadcasted_iota(jnp.int32, (rows, LANES), 1)
    return lane < HEAD_DIM, lane >= HEAD_DIM


def _tile_masks(tq):
    row = lax.broadcasted_iota(jnp.int32, (tq, tq), 0)
    col = lax.broadcasted_iota(jnp.int32, (tq, tq), 1)
    return col <= row, col > row


def _mask_tile(s, t0, mask):
    tq = mask.shape[0]
    parts = []
    if t0 > 0:
        parts.append(s[:, :t0])
    parts.append(jnp.where(mask, s[:, t0:t0 + tq], NEG_INF))
    if t0 + tq < s.shape[1]:
        parts.append(s[:, t0 + tq:])
    return parts[0] if len(parts) == 1 else jnp.concatenate(parts, axis=1)


def _softmax_pv(s, v):
    m = jnp.max(s, axis=1, keepdims=True)
    p = jnp.exp(s - m)
    l = jnp.sum(p, axis=1, keepdims=True)
    return _dot(p.astype(BF16), v) / l


def _block_onehot(seq, shift):
    lane = lax.broadcasted_iota(jnp.int32, (seq, LANES), 1)
    blk = jnp.right_shift(lax.broadcasted_iota(jnp.int32, (seq, LANES), 0), shift)
    return jnp.where((lane & (HEAD_DIM - 1)) == blk, 1.0, 0.0)


def _moba_tile(c, nb, q_ref, v_ref, o_ref, kaug_ref, kmean_ref):
    tq = ATT_TILE
    w = (c + 1) * tq
    qp = q_ref[...]
    qf = qp.astype(F32)
    halves = _head_halves(tq)
    causal, _ = _tile_masks(tq)
    outs = []
    for h in range(2):
        if c > MOBA_TOPK:
            mine_k = _head_halves(kmean_ref.shape[0])[h]
            km_hi, km_lo = _split_bf16(jnp.where(mine_k, kmean_ref[...], 0.0))
            gate = (_dot_nt(km_hi, qp) + _dot_nt(km_lo, qp))[0:nb]
            cand = lax.broadcasted_iota(jnp.int32, (nb, tq), 0)
            valid = cand < c
            gate = jnp.where(valid, gate, NEG_INF)
            keep = _rank_select(gate, valid, c, MOBA_TOPK) | (cand == c)
            neg = jnp.where(keep, 0.0, MASK_NEG)
            lead = HEAD_DIM if h == 0 else 0
            pieces = [neg, jnp.zeros((LANES - lead - nb, tq), F32)]
            if lead:
                pieces.insert(0, jnp.zeros((lead, tq), F32))
            maskcols = jnp.concatenate(pieces, axis=0).T
        else:
            maskcols = jnp.zeros((tq, LANES), F32)
        q_aug = jnp.where(halves[h], qf, maskcols).astype(BF16)
        s = _dot_nt(q_aug, kaug_ref[h, 0:w, :])
        s = _mask_tile(s, c * tq, causal)
        outs.append(_softmax_pv(s, v_ref[0:w, :]))
    o_ref[...] = jnp.where(halves[0], outs[0], outs[1]).astype(o_ref.dtype)


def _moba_kernel(q_ref, k_ref, v_ref, o_ref, kaug_ref, kmean_ref):
    qi = pl.program_id(2)
    seq = k_ref.shape[0]
    nb = seq // MOBA_BLOCK

    @pl.when(qi == 0)
    def _():
        kf = k_ref[...].astype(F32)
        first, second = _head_halves(seq)
        onehot = _block_onehot(seq, int(math.log2(MOBA_BLOCK)))
        kaug_ref[0] = jnp.where(first, kf, onehot).astype(BF16)
        kaug_ref[1] = jnp.where(second, kf, onehot).astype(BF16)
        kmean_ref[...] = jnp.zeros(kmean_ref.shape, F32)
        for j in range(nb):
            blk = k_ref[j * MOBA_BLOCK:(j + 1) * MOBA_BLOCK, :].astype(F32)
            kmean_ref[j:j + 1, :] = jnp.mean(blk, axis=0, keepdims=True)

    for c in range(nb):
        pl.when(qi == c)(functools.partial(_moba_tile, c, nb, q_ref, v_ref, o_ref, kaug_ref, kmean_ref))


def _moba_attention(qkv, batch, seq):
    assert MOBA_BLOCK == ATT_TILE and seq // MOBA_BLOCK <= 16
    nq = seq // ATT_TILE
    hp = MOBA_DIM // LANES
    return pl.pallas_call(
        _moba_kernel,
        name="moba",
        grid=(batch, hp, nq),
        in_specs=[pl.BlockSpec((ATT_TILE, LANES), lambda b, p, i: (b * nq + i, p)),
                  pl.BlockSpec((seq, LANES), lambda b, p, i: (b, hp + p)),
                  pl.BlockSpec((seq, LANES), lambda b, p, i: (b, 2 * hp + p))],
        out_specs=pl.BlockSpec((ATT_TILE, LANES), lambda b, p, i: (b * nq + i, p)),
        out_shape=jax.ShapeDtypeStruct((batch * seq, MOBA_DIM), BF16),
        scratch_shapes=[pltpu.VMEM((2, seq, LANES), BF16), pltpu.VMEM((16, LANES), F32)],
        compiler_params=_cparams("parallel", "parallel", "arbitrary"),
    )(qkv, qkv, qkv)


def _xattn_kernel(x_ref, g_ref, wq_ref, kv_ref, wo_ref, o_ref):
    x = x_ref[...]
    h = _rms_bf16(x, g_ref[...])
    q = (_dot(h, wq_ref[...]) * QK_SCALE).astype(BF16)
    out = x
    for hd in range(XA_HEADS):
        sl = slice(hd * HEAD_DIM, (hd + 1) * HEAD_DIM)
        k = kv_ref[:, sl]
        v = kv_ref[:, XA_DIM + hd * HEAD_DIM:XA_DIM + (hd + 1) * HEAD_DIM]
        s = _dot_nt(q[:, sl], k)
        m = jnp.max(s, axis=1, keepdims=True)
        p = jnp.exp(s - m)
        l = jnp.sum(p, axis=1, keepdims=True)
        o = _dot(p.astype(BF16), v) / l
        out = out + _dot(o.astype(BF16), wo_ref[sl, :])
    o_ref[...] = out


def _cross_attention(x, g, wq, kv_all, layer, wo, seq, mem_len, tm=512):
    t, d = x.shape
    tiles_per_seq = seq // tm
    return pl.pallas_call(
        _xattn_kernel,
        name="xattn",
        grid=(t // tm,),
        in_specs=[pl.BlockSpec((tm, d), lambda i: (i, 0)),
                  pl.BlockSpec((1, d), lambda i: (0, 0)),
                  pl.BlockSpec((d, XA_DIM), lambda i: (0, 0)),
                  pl.BlockSpec((mem_len, 2 * XA_DIM), lambda i: (i // tiles_per_seq, layer)),
                  pl.BlockSpec((XA_DIM, d), lambda i: (0, 0))],
        out_specs=pl.BlockSpec((tm, d), lambda i: (i, 0)),
        out_shape=jax.ShapeDtypeStruct((t, d), F32),
        compiler_params=_cparams("parallel"),
    )(x, g.reshape(1, d), wq, kv_all, wo)


def _ffn_kernel(x_ref, g_ref, wg_ref, wu_ref, wd_ref, o_ref, h_ref):
    f = pl.program_id(1)

    @pl.when(f == 0)
    def _():
        x = x_ref[...]
        h_ref[...] = _rms_bf16(x, g_ref[...])
        o_ref[...] = x

    h = h_ref[...]
    gate = _dot(h, wg_ref[...])
    a = (gate * jax.nn.sigmoid(gate)) * _dot(h, wu_ref[...])
    o_ref[...] += _dot(a.astype(BF16), wd_ref[...])


def _ffn(x, g, wg, wu, wd, tm=1024, tf=256):
    t, d = x.shape
    ff = wg.shape[1]
    return pl.pallas_call(
        _ffn_kernel,
        name="ffn",
        grid=(t // tm, ff // tf),
        in_specs=[pl.BlockSpec((tm, d), lambda i, f: (i, 0)),
                  pl.BlockSpec((1, d), lambda i, f: (0, 0)),
                  pl.BlockSpec((d, tf), lambda i, f: (0, f)),
                  pl.BlockSpec((d, tf), lambda i, f: (0, f)),
                  pl.BlockSpec((tf, d), lambda i, f: (f, 0))],
        out_specs=pl.BlockSpec((tm, d), lambda i, f: (i, 0)),
        out_shape=jax.ShapeDtypeStruct((t, d), F32),
        scratch_shapes=[pltpu.VMEM((tm, d), BF16)],
        compiler_params=_cparams("parallel", "arbitrary"),
    )(x, g.reshape(1, d), wg, wu, wd)


ROUTE_I1, ROUTE_I2, ROUTE_W1, ROUTE_W2, ROUTE_R1, ROUTE_R2 = range(6)
MOE_TILE = 1024


def _lane_pick(arr, lane, k):
    return jnp.sum(jnp.where(lane == k, arr, 0.0), axis=1, keepdims=True)


def _router_kernel(x_ref, g_ref, rhi_ref, rlo_ref, route_ref, cnt_ref):
    i = pl.program_id(0)
    tm = x_ref.shape[0]
    lane = lax.broadcasted_iota(jnp.int32, (tm, LANES), 1)
    lane_f = lane.astype(F32)

    @pl.when(i == 0)
    def _():
        cnt_ref[...] = jnp.zeros(cnt_ref.shape, F32)

    h = _rms_bf16(x_ref[...], g_ref[...])
    logits = _dot(h, rhi_ref[...]) + _dot(h, rlo_ref[...])
    logits = jnp.where(lane < N_EXPERTS, logits, NEG_INF)
    m1 = jnp.max(logits, axis=1, keepdims=True)
    i1 = jnp.min(jnp.where(logits == m1, lane_f, float(LANES)), axis=1, keepdims=True)
    rest = jnp.where(lane_f == i1, NEG_INF, logits)
    m2 = jnp.max(rest, axis=1, keepdims=True)
    i2 = jnp.min(jnp.where(rest == m2, lane_f, float(LANES)), axis=1, keepdims=True)
    e2 = jnp.exp(m2 - m1)
    w1 = 1.0 / (1.0 + e2)
    w2 = e2 / (1.0 + e2)

    chosen = jnp.where((lane_f == i1) | (lane_f == i2), 1.0, 0.0)
    r = lax.broadcasted_iota(jnp.int32, (tm, tm), 0)
    c = lax.broadcasted_iota(jnp.int32, (tm, tm), 1)
    earlier = jnp.where(c < r, 1.0, 0.0).astype(BF16)
    prefix = _dot(earlier, chosen.astype(BF16)) + cnt_ref[0:1, :]
    rank1 = jnp.sum(jnp.where(lane_f == i1, prefix, 0.0), axis=1, keepdims=True)
    rank2 = jnp.sum(jnp.where(lane_f == i2, prefix, 0.0), axis=1, keepdims=True)
    cnt_ref[0:1, :] = cnt_ref[0:1, :] + jnp.sum(chosen, axis=0, keepdims=True)

    cols = {ROUTE_I1: i1, ROUTE_I2: i2, ROUTE_W1: w1, ROUTE_W2: w2, ROUTE_R1: rank1, ROUTE_R2: rank2}
    route = jnp.zeros((tm, LANES), F32)
    for k, val in cols.items():
        route = jnp.where(lane == k, val, route)
    route_ref[...] = route


def _dispatch_kernel(pos1_ref, pos2_ref, x_ref, xs_in, xs_hbm, sem):
    del xs_in
    n = pos1_ref.shape[0]

    def row_copy(r, p):
        return pltpu.make_async_copy(x_ref.at[pl.ds(r, 1)], xs_hbm.at[pl.ds(p, 1)], sem)

    def issue(r, carry):
        row_copy(r, pos1_ref[r]).start()
        row_copy(r, pos2_ref[r]).start()
        return carry

    def drain(r, carry):
        row_copy(0, 0).wait()
        row_copy(0, 0).wait()
        return carry

    lax.fori_loop(0, n, issue, 0)
    lax.fori_loop(0, n, drain, 0)


def _expert_kernel(te_ref, nu_ref, xs_ref, g_ref, wg_ref, wu_ref, wd_ref, y_ref, h_ref):
    del te_ref
    i = pl.program_id(0)
    f = pl.program_id(1)
    used = i < nu_ref[0]

    @pl.when(f == 0)
    def _():
        h_ref[...] = _rms_bf16(xs_ref[...], g_ref[...])
        y_ref[...] = jnp.zeros(y_ref.shape, y_ref.dtype)

    @pl.when(used)
    def _():
        h = h_ref[...]
        gate = _dot(h, wg_ref[...].astype(BF16))
        a = (gate * jax.nn.sigmoid(gate)) * _dot(h, wu_ref[...].astype(BF16))
        y_ref[...] += _dot(a.astype(BF16), wd_ref[...].astype(BF16))


def _combine_kernel(pos1_ref, pos2_ref, x_ref, route_ref, y_hbm, o_ref, buf1, buf2, sem):
    n = pos1_ref.shape[0]

    def row_copy(p, buf, r):
        return pltpu.make_async_copy(y_hbm.at[pl.ds(p, 1)], buf.at[pl.ds(r, 1)], sem)

    def issue(r, carry):
        row_copy(pos1_ref[r], buf1, r).start()
        row_copy(pos2_ref[r], buf2, r).start()
        return carry

    def drain(r, carry):
        row_copy(0, buf1, 0).wait()
        row_copy(0, buf2, 0).wait()
        return carry

    lax.fori_loop(0, n, issue, 0)
    lax.fori_loop(0, n, drain, 0)
    lane = lax.broadcasted_iota(jnp.int32, route_ref.shape, 1)
    route = route_ref[...]
    w1 = _lane_pick(route, lane, ROUTE_W1)
    w2 = _lane_pick(route, lane, ROUTE_W2)
    o_ref[...] = x_ref[...] + w1 * buf1[...] + w2 * buf2[...]


def _moe(x, g, router_w, wg, wu, wd, li, tf=256, t_route=512, t_disp=512, t_comb=256):
    t, d = x.shape
    _, ne, _, ff = wg.shape
    rows = MOE_TILE
    n_rows = 2 * t + ne * rows
    n_tiles = n_rows // rows
    nf = ff // tf
    g2 = g.reshape(1, d)
    rw = jnp.zeros((d, LANES), F32).at[:, :ne].set(router_w)
    rhi = rw.astype(BF16)
    rlo = (rw - rhi.astype(F32)).astype(BF16)

    route, cnt = pl.pallas_call(
        _router_kernel,
        name="moe_route",
        grid=(t // t_route,),
        in_specs=[pl.BlockSpec((t_route, d), lambda i: (i, 0)),
                  pl.BlockSpec((1, d), lambda i: (0, 0)),
                  pl.BlockSpec((d, LANES), lambda i: (0, 0)),
                  pl.BlockSpec((d, LANES), lambda i: (0, 0))],
        out_specs=[pl.BlockSpec((t_route, LANES), lambda i: (i, 0)),
                   pl.BlockSpec((8, LANES), lambda i: (0, 0))],
        out_shape=[jax.ShapeDtypeStruct((t, LANES), F32), jax.ShapeDtypeStruct((8, LANES), F32)],
        compiler_params=_cparams("arbitrary"),
    )(x, g2, rhi, rlo)

    counts = cnt[0, :ne].astype(jnp.int32)
    gsz = (counts + rows - 1) // rows * rows
    ends = jnp.cumsum(gsz)
    offs = ends - gsz
    i1 = route[:, ROUTE_I1].astype(jnp.int32)
    i2 = route[:, ROUTE_I2].astype(jnp.int32)
    pos1 = offs[i1] + route[:, ROUTE_R1].astype(jnp.int32)
    pos2 = offs[i2] + route[:, ROUTE_R2].astype(jnp.int32)
    tile_start = jnp.arange(n_tiles, dtype=jnp.int32) * rows
    tile_expert = jnp.minimum(jnp.sum(tile_start[:, None] >= ends[None, :], axis=1), ne - 1).astype(jnp.int32)
    n_used = (ends[-1] // rows).astype(jnp.int32).reshape(1)

    smem_idx = lambda n: pl.BlockSpec((n,), lambda i: (i,), memory_space=pltpu.SMEM)
    hbm = pl.BlockSpec(memory_space=pl.ANY)
    xs = pl.pallas_call(
        _dispatch_kernel,
        name="moe_dispatch",
        grid=(t // t_disp,),
        in_specs=[smem_idx(t_disp), smem_idx(t_disp), pl.BlockSpec((t_disp, d), lambda i: (i, 0)), hbm],
        out_specs=hbm,
        out_shape=jax.ShapeDtypeStruct((n_rows, d), F32),
        scratch_shapes=[pltpu.SemaphoreType.DMA(())],
        input_output_aliases={3: 0},
        compiler_params=_cparams("arbitrary"),
    )(pos1, pos2, x, jnp.zeros((n_rows, d), F32))

    def w_in_map(i, f, te, nu):
        return li, te[i], 0, jnp.where(i < nu[0], f, nf - 1)

    def w_out_map(i, f, te, nu):
        return li, te[i], jnp.where(i < nu[0], f, nf - 1), 0

    y = pl.pallas_call(
        _expert_kernel,
        name="moe_experts",
        grid_spec=pltpu.PrefetchScalarGridSpec(
            num_scalar_prefetch=2,
            grid=(n_tiles, nf),
            in_specs=[pl.BlockSpec((rows, d), lambda i, f, te, nu: (i, 0)),
                      pl.BlockSpec((1, d), lambda i, f, te, nu: (0, 0)),
                      pl.BlockSpec((None, None, d, tf), w_in_map),
                      pl.BlockSpec((None, None, d, tf), w_in_map),
                      pl.BlockSpec((None, None, tf, d), w_out_map)],
            out_specs=pl.BlockSpec((rows, d), lambda i, f, te, nu: (i, 0)),
            scratch_shapes=[pltpu.VMEM((rows, d), BF16)]),
        out_shape=jax.ShapeDtypeStruct((n_rows, d), F32),
        compiler_params=_cparams("arbitrary", "arbitrary"),
    )(tile_expert, n_used, xs, g2, wg, wu, wd)

    return pl.pallas_call(
        _combine_kernel,
        name="moe_combine",
        grid=(t // t_comb,),
        in_specs=[smem_idx(t_comb), smem_idx(t_comb),
                  pl.BlockSpec((t_comb, d), lambda i: (i, 0)),
                  pl.BlockSpec((t_comb, LANES), lambda i: (i, 0)),
                  hbm],
        out_specs=pl.BlockSpec((t_comb, d), lambda i: (i, 0)),
        out_shape=jax.ShapeDtypeStruct((t, d), F32),
        scratch_shapes=[pltpu.VMEM((t_comb, d), F32), pltpu.VMEM((t_comb, d), F32),
                        pltpu.SemaphoreType.DMA(())],
        compiler_params=_cparams("arbitrary"),
    )(pos1, pos2, x, route, y)


def _gelu_tanh(x):
    return 0.5 * x * (1.0 + jnp.tanh(math.sqrt(2.0 / math.pi) * (x + 0.044715 * (x * x * x))))


def _compress_kernel(x_ref, pe_ref, w1_ref, w2_ref, o_ref, sh_ref):
    rows = x_ref.shape[0]
    half = w1_ref.shape[0] // 2
    x = x_ref[...]
    top = _dot(x, w1_ref[0:half, :])
    bot = _dot(x, w1_ref[half:2 * half, :])
    bias = _dot(pe_ref[...], w1_ref[...])[0:1, :]
    sh_ref[0:rows, :] = bot
    sh_ref[rows:rows + 8, :] = jnp.zeros((8, bot.shape[1]), F32)
    hidden = top + sh_ref[1:rows + 1, :] + bias
    o_ref[...] = _dot(_gelu_tanh(hidden).astype(BF16), w2_ref[...]).astype(o_ref.dtype)


def _compress(chunks, pe, w1, w2, rows=1024):
    _, r, cw = chunks.shape
    hid = w1.shape[2]
    return pl.pallas_call(
        _compress_kernel,
        name="compress",
        grid=(2, r // rows),
        in_specs=[pl.BlockSpec((None, rows, cw), lambda a, i: (a, i, 0)),
                  pl.BlockSpec((None, 16, 2 * cw), lambda a, i: (a, 0, 0)),
                  pl.BlockSpec((None, 2 * cw, hid), lambda a, i: (a, 0, 0)),
                  pl.BlockSpec((None, hid, HEAD_DIM), lambda a, i: (a, 0, 0))],
        out_specs=pl.BlockSpec((None, rows, HEAD_DIM), lambda a, i: (a, i, 0)),
        out_shape=jax.ShapeDtypeStruct((2, r, HEAD_DIM), BF16),
        scratch_shapes=[pltpu.VMEM((rows + 8, hid), F32)],
        compiler_params=_cparams("parallel", "parallel"),
    )(chunks, pe, w1, w2)


def _nsa_cmp_kernel(n_slc, q_ref, ka_ref, kb_ref, ovt_ref, ocmp_ref, selm_ref):
    qi = pl.program_id(2)
    tq = ATT_TILE
    n_pad = ka_ref.shape[0]
    t_q = qi * tq + lax.broadcasted_iota(jnp.int32, (tq, n_pad), 0)
    n_c = lax.broadcasted_iota(jnp.int32, (tq, n_pad), 1)
    mask_c = (n_c * CMP_STRIDE + (CMP_LEN - 1)) <= t_q
    halves = _head_halves(tq)
    ka = ka_ref[...]
    kb = kb_ref[...]
    psum = jnp.zeros((tq, n_pad), F32)
    for pr in range(NSA_HPG // 2):
        qf = q_ref[:, pr * LANES:(pr + 1) * LANES].astype(F32)
        outs = []
        for h in range(2):
            qz = jnp.where(halves[h], qf, 0.0).astype(BF16)
            kmat, vmat = (ka, kb) if h == 0 else (kb, ka)
            s = jnp.where(mask_c, _dot_nt(qz, kmat), NEG_INF)
            m = jnp.max(s, axis=1, keepdims=True)
            m = jnp.where(m == NEG_INF, 0.0, m)
            e = jnp.exp(s - m)
            p = e / jnp.maximum(jnp.sum(e, axis=1, keepdims=True), 1e-30)
            psum = psum + p
            outs.append(_dot(p.astype(BF16), vmat))
        ocmp_ref[:, pr * LANES:(pr + 1) * LANES] = jnp.where(halves[0], outs[0], outs[1]).astype(ocmp_ref.dtype)

    ps_hi, ps_lo = _split_bf16(psum)
    ovt = ovt_ref[...]
    p_slc = (_dot_nt(ovt, ps_hi) + _dot_nt(ovt, ps_lo))[0:n_slc]
    cand = lax.broadcasted_iota(jnp.int32, (n_slc, tq), 0)
    t_blk = jnp.right_shift(qi * tq + lax.broadcasted_iota(jnp.int32, (n_slc, tq), 1), int(math.log2(SLC_BLOCK)))
    valid = cand <= t_blk
    forced = (cand == 0) | (cand == t_blk) | (cand == t_blk - 1)
    score = jnp.where(valid, jnp.where(forced, FORCE_SCORE, p_slc), -1.0)
    keep = _rank_select(score, valid, n_slc, min(SLC_TOPN, n_slc))
    neg = jnp.where(keep, 0.0, MASK_NEG)
    zero = jnp.zeros((HEAD_DIM - n_slc, tq), F32)
    selm_ref[...] = jnp.concatenate([neg, zero, neg, zero], axis=0).T.astype(selm_ref.dtype)


def _nsa_sw_tile(c, q_ref, selm_ref, ksel_ref, kwin_ref, gate_ref, ex_ref, ocmp_ref, o_ref,
                 kaug_ref, vb_ref, kwb_ref):
    tq = ATT_TILE
    w = (c + 1) * tq
    lo = max(c - WINDOW // tq, 0) * tq
    qf = q_ref[...].astype(F32)
    selm = selm_ref[...].astype(F32)
    halves = _head_halves(tq)
    causal, beyond = _tile_masks(tq)
    sel_o, win_o = [], []
    for h in range(2):
        q_aug = jnp.where(halves[h], qf, selm).astype(BF16)
        qz = jnp.where(halves[h], qf, 0.0).astype(BF16)
        if h == 0:
            k_sel, v_sel = kaug_ref[0, 0:w, :], vb_ref[0:w, :]
            k_win, v_win = kwin_ref[lo:w, :], kwb_ref[lo:w, :]
        else:
            k_sel, v_sel = kaug_ref[1, 0:w, :], ksel_ref[0:w, :]
            k_win, v_win = kwb_ref[lo:w, :], kwin_ref[lo:w, :]
        s = _mask_tile(_dot_nt(q_aug, k_sel), c * tq, causal)
        sel_o.append(_softmax_pv(s, v_sel))
        s = _mask_tile(_dot_nt(qz, k_win), c * tq - lo, causal)
        if c * tq - lo == WINDOW:
            s = _mask_tile(s, 0, beyond)
        win_o.append(_softmax_pv(s, v_win))
    o_sel = jnp.where(halves[0], sel_o[0], sel_o[1])
    o_win = jnp.where(halves[0], win_o[0], win_o[1])
    g_hi, g_lo = _split_bf16(gate_ref[...])
    ex = ex_ref[...]
    gexp = _dot(g_hi, ex) + _dot(g_lo, ex)
    y = (gexp[:, 0:LANES] * ocmp_ref[...].astype(F32) + gexp[:, LANES:2 * LANES] * o_sel
         + gexp[:, 2 * LANES:3 * LANES] * o_win)
    o_ref[...] = y.astype(o_ref.dtype)


def _nsa_sw_kernel(q_ref, selm_ref, ksel_ref, kwin_ref, gate_ref, ex_ref, ocmp_ref, o_ref,
                   kaug_ref, vb_ref, kwb_ref):
    pr = pl.program_id(2)
    qi = pl.program_id(3)
    seq = ksel_ref.shape[0]

    @pl.when((pr == 0) & (qi == 0))
    def _():
        kv = ksel_ref[...].astype(F32)
        first, second = _head_halves(seq)
        onehot = _block_onehot(seq, int(math.log2(SLC_BLOCK)))
        swapped = pltpu.roll(kv, HEAD_DIM, axis=1)
        kaug_ref[0] = jnp.where(first, kv, onehot).astype(BF16)
        kaug_ref[1] = jnp.where(second, swapped, onehot).astype(BF16)
        vb_ref[...] = swapped.astype(BF16)
        kwb_ref[...] = pltpu.roll(kwin_ref[...].astype(F32), HEAD_DIM, axis=1).astype(BF16)

    for c in range(seq // ATT_TILE):
        pl.when(qi == c)(functools.partial(_nsa_sw_tile, c, q_ref, selm_ref, ksel_ref, kwin_ref, gate_ref, ex_ref,
                                           ocmp_ref, o_ref, kaug_ref, vb_ref, kwb_ref))


def _nsa_attention(q_raw, q_rot, kv, cmp_kv, gates, batch, seq):
    nq = seq // ATT_TILE
    g = NSA_GROUPS
    gw = NSA_HPG * HEAD_DIM
    n_pairs = NSA_HEADS // 2
    n_cmp = (seq - CMP_LEN) // CMP_STRIDE + 1
    n_cmp_pad = seq // CMP_STRIDE
    n_slc = seq // SLC_BLOCK
    assert n_cmp_pad == LANES and n_slc <= HEAD_DIM // 2 and WINDOW % ATT_TILE == 0

    ex = np.zeros((n_pairs, LANES, 3 * LANES), np.float32)
    for hd in range(NSA_HEADS):
        for br in range(3):
            lane0 = br * LANES + (hd % 2) * HEAD_DIM
            ex[hd // 2, hd * 3 + br, lane0:lane0 + HEAD_DIM] = 1.0
    c_s = np.arange(n_cmp) * CMP_STRIDE
    s_s = np.arange(n_slc) * SLC_BLOCK
    ov = np.clip(np.minimum(c_s[:, None] + CMP_LEN, s_s[None, :] + SLC_BLOCK)
                 - np.maximum(c_s[:, None], s_s[None, :]), 0, None) / CMP_LEN
    ovt = np.zeros((LANES, n_cmp_pad), np.float32)
    ovt[:n_slc, :n_cmp] = ov.T

    ka = jnp.concatenate([cmp_kv[0], cmp_kv[1]], axis=-1)
    kb = jnp.concatenate([cmp_kv[1], cmp_kv[0]], axis=-1)
    o_cmp, selm = pl.pallas_call(
        functools.partial(_nsa_cmp_kernel, n_slc),
        name="nsa_cmp",
        grid=(batch, g, nq),
        in_specs=[pl.BlockSpec((ATT_TILE, gw), lambda b, gi, i: (b * nq + i, gi)),
                  pl.BlockSpec((n_cmp_pad, LANES), lambda b, gi, i: (b * g + gi, 0)),
                  pl.BlockSpec((n_cmp_pad, LANES), lambda b, gi, i: (b * g + gi, 0)),
                  pl.BlockSpec((LANES, n_cmp_pad), lambda b, gi, i: (0, 0))],
        out_specs=[pl.BlockSpec((ATT_TILE, gw), lambda b, gi, i: (b * nq + i, gi)),
                   pl.BlockSpec((ATT_TILE, LANES), lambda b, gi, i: (b * nq + i, gi))],
        out_shape=[jax.ShapeDtypeStruct((batch * seq, NSA_HEADS * HEAD_DIM), BF16),
                   jax.ShapeDtypeStruct((batch * seq, g * LANES), BF16)],
        compiler_params=_cparams("parallel", "parallel", "parallel"),
    )(q_raw, ka, kb, jnp.asarray(ovt, BF16))

    ppg = NSA_HPG // 2
    return pl.pallas_call(
        _nsa_sw_kernel,
        name="nsa_sw",
        grid=(batch, g, ppg, nq),
        in_specs=[pl.BlockSpec((ATT_TILE, LANES), lambda b, gi, p, i: (b * nq + i, gi * ppg + p)),
                  pl.BlockSpec((ATT_TILE, LANES), lambda b, gi, p, i: (b * nq + i, gi)),
                  pl.BlockSpec((seq, LANES), lambda b, gi, p, i: (b, g + gi)),
                  pl.BlockSpec((seq, LANES), lambda b, gi, p, i: (b, 2 * g + gi)),
                  pl.BlockSpec((ATT_TILE, LANES), lambda b, gi, p, i: (b * nq + i, 0)),
                  pl.BlockSpec((None, LANES, 3 * LANES), lambda b, gi, p, i: (gi * ppg + p, 0, 0)),
                  pl.BlockSpec((ATT_TILE, LANES), lambda b, gi, p, i: (b * nq + i, gi * ppg + p))],
        out_specs=pl.BlockSpec((ATT_TILE, LANES), lambda b, gi, p, i: (b * nq + i, gi * ppg + p)),
        out_shape=jax.ShapeDtypeStruct((batch * seq, NSA_HEADS * HEAD_DIM), BF16),
        scratch_shapes=[pltpu.VMEM((2, seq, LANES), BF16), pltpu.VMEM((seq, LANES), BF16),
                        pltpu.VMEM((seq, LANES), BF16)],
        compiler_params=_cparams("parallel", "parallel", "arbitrary", "arbitrary"),
    )(q_rot, selm, kv, kv, gates, jnp.asarray(ex, BF16), o_cmp)


def _even_mixer(x, g, w_in, pool_w, pool_scale, w_out, rope_pair, batch, seq):
    chunk = 2 * LANES
    plan = [(0, POOL_DIM, [(0, 0, "none", 0)])]
    for c in range(3 * MOBA_DIM // chunk):
        op = "rope_scale" if c < MOBA_DIM // chunk else ("rope" if c < 2 * MOBA_DIM // chunk else "none")
        plan.append((POOL_DIM + c * chunk, chunk, [(1, c * chunk, op, 0)]))
    u_pool, qkv = _norm_project(x, g, w_in.astype(BF16), plan, [rope_pair],
                                [(POOL_DIM, F32), (3 * MOBA_DIM, BF16)], seq)
    w_bd = jax.scipy.linalg.block_diag(*[pool_w[i] for i in range(pool_w.shape[0])]).astype(BF16)
    y_a = _pool_mixer(u_pool, w_bd, pool_scale, batch, seq)
    y_b = _moba_attention(qkv, batch, seq)
    w_out = w_out.astype(BF16)
    return _out_proj(x, [y_a, y_b], [w_out[:POOL_DIM], w_out[POOL_DIM:]])


def _odd_mixer(x, g, w_in, pe_k, pe_v, k_w1, k_w2, v_w1, v_w2, w_out, rope_pair, rope_single, batch, seq):
    d = x.shape[1]
    qd = NSA_HEADS * HEAD_DIM
    kvd = NSA_KV_DIM
    wq = w_in[:, :qd]
    parts = [w_in[:, qd + i * kvd:qd + (i + 1) * kvd].reshape(d, NSA_GROUPS, HEAD_DIM) for i in range(6)]
    pairs = [jnp.concatenate([parts[2 * i], parts[2 * i + 1]], axis=-1).reshape(d, 2 * kvd) for i in range(3)]
    n_gate = 3 * NSA_HEADS
    w_gate = jnp.zeros((d, LANES), F32).at[:, :n_gate].set(w_in[:, qd + 6 * kvd:])
    w_all = jnp.concatenate([wq] + pairs + [w_gate], axis=1).astype(BF16)

    chunk = 2 * LANES
    plan = []
    for c in range(qd // chunk):
        plan.append((c * chunk, chunk, [(0, c * chunk, "scale", 0), (1, c * chunk, "rope_scale", 0)]))
    for c in range(6 * kvd // chunk):
        branch = c // (2 * kvd // chunk)
        op = "none" if branch == 0 else "rope"
        plan.append((qd + c * chunk, chunk, [(2, c * chunk, op, 1)]))
    plan.append((qd + 6 * kvd, LANES, [(3, 0, "sigmoid", 0)]))
    q_raw, q_rot, kv, gates = _norm_project(
        x, g, w_all, plan, [rope_pair, rope_single],
        [(qd, BF16), (qd, BF16), (6 * kvd, BF16), (LANES, F32)], seq)

    n_chunks = seq // CMP_STRIDE
    cmp_in = kv[:, :2 * kvd].reshape(batch, n_chunks, CMP_STRIDE, NSA_GROUPS, 2, HEAD_DIM)
    cmp_in = cmp_in.transpose(4, 0, 3, 1, 2, 5).reshape(2, batch * NSA_GROUPS * n_chunks, CMP_STRIDE * HEAD_DIM)
    pe = jnp.stack([pe_k, pe_v]).reshape(2, 1, CMP_LEN * HEAD_DIM)
    pe = jnp.broadcast_to(pe, (2, 16, CMP_LEN * HEAD_DIM)).astype(BF16)
    w1 = jnp.stack([k_w1, v_w1]).astype(BF16)
    w2 = jnp.stack([k_w2, v_w2]).astype(BF16)
    cmp_kv = _compress(cmp_in, pe, w1, w2)

    y = _nsa_attention(q_raw, q_rot, kv, cmp_kv, gates, batch, seq)
    return _out_proj(x, [y], [w_out.astype(BF16)])


def kernel(x, mem, positions, norm_g, mem_g, final_g, w_in_ab, pool_w, pool_scale, w_out_ab, ffn_w_gate, ffn_w_up, ffn_w_down, w_in_c, cmp_pe_k, cmp_pe_v, cmp_k_w1, cmp_k_w2, cmp_v_w1, cmp_v_w2, w_out_c, router_w, moe_w_gate, moe_w_up, moe_w_down, xa_wq, xa_wkv, xa_wo):
    batch, seq, d = x.shape
    mem_len = mem.shape[1]
    depth = norm_g.shape[0]
    rope_pair, rope_single = _rope_tables(positions)

    wkv_all = jnp.concatenate([xa_wkv[l] for l in range(depth)], axis=1).astype(BF16)
    kv_all = _norm_matmul(mem.reshape(batch * mem_len, d), mem_g, wkv_all)

    xf = x.reshape(batch * seq, d)
    for layer in range(depth):
        i = layer // 2
        if layer % 2 == 0:
            xf = _even_mixer(xf, norm_g[layer, 0], w_in_ab[i], pool_w[i], pool_scale[i], w_out_ab[i],
                             rope_pair, batch, seq)
        else:
            xf = _odd_mixer(xf, norm_g[layer, 0], w_in_c[i], cmp_pe_k[i], cmp_pe_v[i], cmp_k_w1[i], cmp_k_w2[i],
                            cmp_v_w1[i], cmp_v_w2[i], w_out_c[i], rope_pair, rope_single, batch, seq)
        xf = _cross_attention(xf, norm_g[layer, 1], xa_wq[layer].astype(BF16), kv_all, layer,
                              xa_wo[layer].astype(BF16), seq, mem_len)
        if layer % 2 == 0:
            xf = _ffn(xf, norm_g[layer, 2], ffn_w_gate[i].astype(BF16), ffn_w_up[i].astype(BF16),
                      ffn_w_down[i].astype(BF16))
        else:
            xf = _moe(xf, norm_g[layer, 2], router_w[i], moe_w_gate, moe_w_up, moe_w_down, i)
    return _final_norm(xf, final_g).reshape(batch, seq, d)
```

```python
import functools
import math

import numpy as np
import jax
import jax.numpy as jnp
from jax import lax
from jax.experimental import pallas as pl
from jax.experimental.pallas import tpu as pltpu

F32 = jnp.float32
BF16 = jnp.bfloat16

HEAD_DIM = 64
ROPE_DIM = 16
ROPE_THETA = 500000.0
NORM_EPS = 1e-5
QK_SCALE = HEAD_DIM ** -0.5

POOL_WINDOWS = (2, 4, 8, 16)
POOL_DIM = 256
MOBA_HEADS = 12
MOBA_DIM = MOBA_HEADS * HEAD_DIM
MOBA_BLOCK = 256
MOBA_TOPK = 3
NSA_HEADS = 16
NSA_GROUPS = 4
NSA_HPG = 4
NSA_KV_DIM = NSA_GROUPS * HEAD_DIM
CMP_LEN = 32
CMP_STRIDE = 16
SLC_BLOCK = 64
SLC_TOPN = 16
WINDOW = 512
FORCE_SCORE = 1e4
XA_HEADS = 4
XA_DIM = XA_HEADS * HEAD_DIM
N_EXPERTS = 8

LANES = 128
ATT_TILE = 256
VMEM_LIMIT = 56 * 1024 * 1024
NEG_INF = float("-inf")
MASK_NEG = -1e30


def _cparams(*sem):
    return pltpu.CompilerParams(dimension_semantics=sem, vmem_limit_bytes=VMEM_LIMIT)


def _dot(a, b):
    return jnp.dot(a, b, preferred_element_type=F32)


def _dot_nt(a, b):
    return lax.dot_general(a, b, (((1,), (1,)), ((), ())), preferred_element_type=F32)


def _split_bf16(a):
    hi = a.astype(BF16)
    lo = (a - hi.astype(F32)).astype(BF16)
    return hi, lo


def _rms_bf16(x, g):
    ms = jnp.mean(x * x, axis=-1, keepdims=True)
    return (x * lax.rsqrt(ms + NORM_EPS) * g).astype(BF16)


def _apply_rope(acc, c, a, b):
    half = ROPE_DIM // 2
    return acc * c + pltpu.roll(acc, LANES - half, axis=1) * a + pltpu.roll(acc, half, axis=1) * b


def _proj_kernel(plan, n_tab, x_ref, g_ref, w_ref, *rest):
    tabs = rest[:3 * n_tab]
    outs = rest[3 * n_tab:]
    h = _rms_bf16(x_ref[...], g_ref[...])
    for wc0, width, sinks in plan:
        acc = _dot(h, w_ref[:, wc0:wc0 + width])
        for oi, oc0, op, tab in sinks:
            for s in range(width // LANES):
                val = acc[:, s * LANES:(s + 1) * LANES]
                if op in ("rope", "rope_scale"):
                    c, a, b = (tabs[3 * tab + i][...] for i in range(3))
                    val = _apply_rope(val, c, a, b)
                if op in ("scale", "rope_scale"):
                    val = val * QK_SCALE
                if op == "sigmoid":
                    val = jax.nn.sigmoid(val)
                o = outs[oi]
                o[:, oc0 + s * LANES:oc0 + (s + 1) * LANES] = val.astype(o.dtype)


def _norm_project(x, g, w, plan, tables, out_defs, seq, tm=512):
    t, d = x.shape
    n = w.shape[1]
    n_seq_tiles = seq // tm
    in_specs = [
        pl.BlockSpec((tm, d), lambda i: (i, 0)),
        pl.BlockSpec((1, d), lambda i: (0, 0)),
        pl.BlockSpec((d, n), lambda i: (0, 0)),
    ]
    flat_tabs = []
    for tset in tables:
        for tb in tset:
            flat_tabs.append(tb)
            in_specs.append(pl.BlockSpec((tm, LANES), lambda i: (i % n_seq_tiles, 0)))
    out_shape = [jax.ShapeDtypeStruct((t, wd), dt) for wd, dt in out_defs]
    out_specs = [pl.BlockSpec((tm, wd), lambda i: (i, 0)) for wd, _ in out_defs]
    return pl.pallas_call(
        functools.partial(_proj_kernel, plan, len(tables)),
        name="proj",
        grid=(t // tm,),
        in_specs=in_specs,
        out_specs=out_specs,
        out_shape=out_shape,
        compiler_params=_cparams("parallel"),
    )(x, g.reshape(1, d), w, *flat_tabs)


def _rope_tables(positions):
    half = ROPE_DIM // 2
    inv = ROPE_THETA ** (-jnp.arange(0, ROPE_DIM, 2, dtype=F32) / ROPE_DIM)
    ang = positions.astype(F32)[:, None] * inv[None, :]
    cos, sin = jnp.cos(ang), jnp.sin(ang)
    s = positions.shape[0]
    z8 = jnp.zeros((s, half), F32)
    rest0 = jnp.zeros((s, HEAD_DIM - ROPE_DIM), F32)
    rest1 = jnp.ones((s, HEAD_DIM - ROPE_DIM), F32)
    c64 = jnp.concatenate([cos, cos, rest1], -1)
    a64 = jnp.concatenate([-sin, z8, rest0], -1)
    b64 = jnp.concatenate([z8, sin, rest0], -1)
    one64 = jnp.ones((s, HEAD_DIM), F32)
    zero64 = jnp.zeros((s, HEAD_DIM), F32)
    pair = tuple(jnp.concatenate([m, m], -1) for m in (c64, a64, b64))
    single = (jnp.concatenate([c64, one64], -1), jnp.concatenate([a64, zero64], -1),
              jnp.concatenate([b64, zero64], -1))
    return pair, single


def _norm_matmul_kernel(x_ref, g_ref, w_ref, o_ref):
    h = _rms_bf16(x_ref[...], g_ref[...])
    o_ref[...] = _dot(h, w_ref[...]).astype(o_ref.dtype)


def _norm_matmul(x, g, w, tm=512):
    t, d = x.shape
    n = w.shape[1]
    return pl.pallas_call(
        _norm_matmul_kernel,
        name="mem_kv",
        grid=(t // tm,),
        in_specs=[pl.BlockSpec((tm, d), lambda i: (i, 0)),
                  pl.BlockSpec((1, d), lambda i: (0, 0)),
                  pl.BlockSpec((d, n), lambda i: (0, 0))],
        out_specs=pl.BlockSpec((tm, n), lambda i: (i, 0)),
        out_shape=jax.ShapeDtypeStruct((t, n), BF16),
        compiler_params=_cparams("parallel"),
    )(x, g.reshape(1, d), w)


def _final_norm_kernel(x_ref, g_ref, o_ref):
    x = x_ref[...]
    ms = jnp.mean(x * x, axis=-1, keepdims=True)
    o_ref[...] = x * lax.rsqrt(ms + NORM_EPS) * g_ref[...]


def _final_norm(x, g, tm=1024):
    t, d = x.shape
    return pl.pallas_call(
        _final_norm_kernel,
        name="final_norm",
        grid=(t // tm,),
        in_specs=[pl.BlockSpec((tm, d), lambda i: (i, 0)), pl.BlockSpec((1, d), lambda i: (0, 0))],
        out_specs=pl.BlockSpec((tm, d), lambda i: (i, 0)),
        out_shape=jax.ShapeDtypeStruct((t, d), F32),
        compiler_params=_cparams("parallel"),
    )(x, g.reshape(1, d))


def _out_proj_kernel(n_in, x_ref, *rest):
    ys = rest[:n_in]
    ws = rest[n_in:2 * n_in]
    o_ref = rest[2 * n_in]
    acc = x_ref[...]
    for y, w in zip(ys, ws):
        acc = acc + _dot(y[...], w[...])
    o_ref[...] = acc


def _out_proj(x, ys, ws, tm=512):
    t, d = x.shape
    in_specs = [pl.BlockSpec((tm, d), lambda i: (i, 0))]
    in_specs += [pl.BlockSpec((tm, y.shape[1]), lambda i: (i, 0)) for y in ys]
    in_specs += [pl.BlockSpec(w.shape, lambda i: (0, 0)) for w in ws]
    return pl.pallas_call(
        functools.partial(_out_proj_kernel, len(ys)),
        name="out_proj",
        grid=(t // tm,),
        in_specs=in_specs,
        out_specs=pl.BlockSpec((tm, d), lambda i: (i, 0)),
        out_shape=jax.ShapeDtypeStruct((t, d), F32),
        compiler_params=_cparams("parallel"),
    )(x, *ys, *ws)


def _pool_kernel(u_ref, w_ref, sc_ref, o_ref, pad_ref):
    s = u_ref.shape[0]
    maxw = POOL_WINDOWS[-1]
    u = u_ref[...]
    pad_ref[0:maxw, :] = jnp.zeros((maxw, POOL_DIM), F32)
    t1 = (lax.broadcasted_iota(jnp.int32, (s, POOL_DIM), 0) + 1).astype(F32)
    lane = lax.broadcasted_iota(jnp.int32, (s, POOL_DIM), 1)
    gdim = POOL_DIM // len(POOL_WINDOWS)
    cur = u
    pooled = jnp.zeros_like(u)
    shift = 1
    for gi, w in enumerate(POOL_WINDOWS):
        pad_ref[maxw:maxw + s, :] = cur
        cur = cur + pad_ref[maxw - shift:maxw - shift + s, :]
        shift *= 2
        assert shift == w
        mean = cur / jnp.minimum(t1, float(w))
        pooled = jnp.where((lane >= gi * gdim) & (lane < (gi + 1) * gdim), mean, pooled)
    pooled = (pooled - u).astype(BF16)
    o_ref[...] = (_dot(pooled, w_ref[...]) * sc_ref[...]).astype(o_ref.dtype)


def _pool_mixer(u_pool, w_bd, scale, batch, seq):
    return pl.pallas_call(
        _pool_kernel,
        name="pool",
        grid=(batch,),
        in_specs=[pl.BlockSpec((seq, POOL_DIM), lambda b: (b, 0)),
                  pl.BlockSpec((POOL_DIM, POOL_DIM), lambda b: (0, 0)),
                  pl.BlockSpec((1, POOL_DIM), lambda b: (0, 0))],
        out_specs=pl.BlockSpec((seq, POOL_DIM), lambda b: (b, 0)),
        out_shape=jax.ShapeDtypeStruct((batch * seq, POOL_DIM), BF16),
        scratch_shapes=[pltpu.VMEM((seq + POOL_WINDOWS[-1], POOL_DIM), F32)],
        compiler_params=_cparams("parallel"),
    )(u_pool, w_bd, scale.reshape(1, POOL_DIM))


def _rank_select(scoreT, valid, n_rows, topn):
    row = lax.broadcasted_iota(jnp.int32, scoreT.shape, 0)
    rank = jnp.zeros(scoreT.shape, F32)
    for j in range(n_rows):
        sj = scoreT[j:j + 1, :]
        beats = (scoreT > sj) | ((scoreT == sj) & (row < j))
        rj = jnp.sum(beats.astype(F32), axis=0, keepdims=True)
        rank = jnp.where(row == j, rj, rank)
    return (rank < float(topn)) & valid


def _head_halves(rows):
    lane = lax.broadcasted_iota(jnp.int32, (rows, LANES), 1)
    return lane < HEAD_DIM, lane >= HEAD_DIM


def _tile_masks(tq):
    row = lax.broadcasted_iota(jnp.int32, (tq, tq), 0)
    col = lax.broadcasted_iota(jnp.int32, (tq, tq), 1)
    return col <= row, col > row


def _mask_tile(s, t0, mask):
    tq = mask.shape[0]
    parts = []
    if t0 > 0:
        parts.append(s[:, :t0])
    parts.append(jnp.where(mask, s[:, t0:t0 + tq], NEG_INF))
    if t0 + tq < s.shape[1]:
        parts.append(s[:, t0 + tq:])
    return parts[0] if len(parts) == 1 else jnp.concatenate(parts, axis=1)


def _softmax_pv(s, v):
    m = jnp.max(s, axis=1, keepdims=True)
    p = jnp.exp(s - m)
    l = jnp.sum(p, axis=1, keepdims=True)
    return _dot(p.astype(BF16), v) / l


def _block_onehot(seq, shift):
    lane = lax.broadcasted_iota(jnp.int32, (seq, LANES), 1)
    blk = jnp.right_shift(lax.broadcasted_iota(jnp.int32, (seq, LANES), 0), shift)
    return jnp.where((lane & (HEAD_DIM - 1)) == blk, 1.0, 0.0)


def _moba_tile(c, nb, q_ref, v_ref, o_ref, kaug_ref, kmean_ref):
    tq = ATT_TILE
    w = (c + 1) * tq
    qp = q_ref[...]
    qf = qp.astype(F32)
    halves = _head_halves(tq)
    causal, _ = _tile_masks(tq)
    outs = []
    for h in range(2):
        if c > MOBA_TOPK:
            mine_k = _head_halves(kmean_ref.shape[0])[h]
            km_hi, km_lo = _split_bf16(jnp.where(mine_k, kmean_ref[...], 0.0))
            gate = (_dot_nt(km_hi, qp) + _dot_nt(km_lo, qp))[0:nb]
            cand = lax.broadcasted_iota(jnp.int32, (nb, tq), 0)
            valid = cand < c
            gate = jnp.where(valid, gate, NEG_INF)
            keep = _rank_select(gate, valid, c, MOBA_TOPK) | (cand == c)
            neg = jnp.where(keep, 0.0, MASK_NEG)
            lead = HEAD_DIM if h == 0 else 0
            pieces = [neg, jnp.zeros((LANES - lead - nb, tq), F32)]
            if lead:
                pieces.insert(0, jnp.zeros((lead, tq), F32))
            maskcols = jnp.concatenate(pieces, axis=0).T
        else:
            maskcols = jnp.zeros((tq, LANES), F32)
        q_aug = jnp.where(halves[h], qf, maskcols).astype(BF16)
        s = _dot_nt(q_aug, kaug_ref[h, 0:w, :])
        s = _mask_tile(s, c * tq, causal)
        outs.append(_softmax_pv(s, v_ref[0:w, :]))
    o_ref[...] = jnp.where(halves[0], outs[0], outs[1]).astype(o_ref.dtype)


def _moba_kernel(q_ref, k_ref, v_ref, o_ref, kaug_ref, kmean_ref):
    qi = pl.program_id(2)
    seq = k_ref.shape[0]
    nb = seq // MOBA_BLOCK

    @pl.when(qi == 0)
    def _():
        kf = k_ref[...].astype(F32)
        first, second = _head_halves(seq)
        onehot = _block_onehot(seq, int(math.log2(MOBA_BLOCK)))
        kaug_ref[0] = jnp.where(first, kf, onehot).astype(BF16)
        kaug_ref[1] = jnp.where(second, kf, onehot).astype(BF16)
        kmean_ref[...] = jnp.zeros(kmean_ref.shape, F32)
        for j in range(nb):
            blk = k_ref[j * MOBA_BLOCK:(j + 1) * MOBA_BLOCK, :].astype(F32)
            kmean_ref[j:j + 1, :] = jnp.mean(blk, axis=0, keepdims=True)

    for c in range(nb):
        pl.when(qi == c)(functools.partial(_moba_tile, c, nb, q_ref, v_ref, o_ref, kaug_ref, kmean_ref))


def _moba_attention(qkv, batch, seq):
    assert MOBA_BLOCK == ATT_TILE and seq // MOBA_BLOCK <= 16
    nq = seq // ATT_TILE
    hp = MOBA_DIM // LANES
    return pl.pallas_call(
        _moba_kernel,
        name="moba",
        grid=(batch, hp, nq),
        in_specs=[pl.BlockSpec((ATT_TILE, LANES), lambda b, p, i: (b * nq + i, p)),
                  pl.BlockSpec((seq, LANES), lambda b, p, i: (b, hp + p)),
                  pl.BlockSpec((seq, LANES), lambda b, p, i: (b, 2 * hp + p))],
        out_specs=pl.BlockSpec((ATT_TILE, LANES), lambda b, p, i: (b * nq + i, p)),
        out_shape=jax.ShapeDtypeStruct((batch * seq, MOBA_DIM), BF16),
        scratch_shapes=[pltpu.VMEM((2, seq, LANES), BF16), pltpu.VMEM((16, LANES), F32)],
        compiler_params=_cparams("parallel", "parallel", "arbitrary"),
    )(qkv, qkv, qkv)


def _xattn_kernel(x_ref, g_ref, wq_ref, kv_ref, wo_ref, o_ref):
    x = x_ref[...]
    h = _rms_bf16(x, g_ref[...])
    q = (_dot(h, wq_ref[...]) * QK_SCALE).astype(BF16)
    out = x
    for hd in range(XA_HEADS):
        sl = slice(hd * HEAD_DIM, (hd + 1) * HEAD_DIM)
        k = kv_ref[:, sl]
        v = kv_ref[:, XA_DIM + hd * HEAD_DIM:XA_DIM + (hd + 1) * HEAD_DIM]
        s = _dot_nt(q[:, sl], k)
        m = jnp.max(s, axis=1, keepdims=True)
        p = jnp.exp(s - m)
        l = jnp.sum(p, axis=1, keepdims=True)
        o = _dot(p.astype(BF16), v) / l
        out = out + _dot(o.astype(BF16), wo_ref[sl, :])
    o_ref[...] = out


def _cross_attention(x, g, wq, kv_all, layer, wo, seq, mem_len, tm=512):
    t, d = x.shape
    tiles_per_seq = seq // tm
    return pl.pallas_call(
        _xattn_kernel,
        name="xattn",
        grid=(t // tm,),
        in_specs=[pl.BlockSpec((tm, d), lambda i: (i, 0)),
                  pl.BlockSpec((1, d), lambda i: (0, 0)),
                  pl.BlockSpec((d, XA_DIM), lambda i: (0, 0)),
                  pl.BlockSpec((mem_len, 2 * XA_DIM), lambda i: (i // tiles_per_seq, layer)),
                  pl.BlockSpec((XA_DIM, d), lambda i: (0, 0))],
        out_specs=pl.BlockSpec((tm, d), lambda i: (i, 0)),
        out_shape=jax.ShapeDtypeStruct((t, d), F32),
        compiler_params=_cparams("parallel"),
    )(x, g.reshape(1, d), wq, kv_all, wo)


def _ffn_kernel(x_ref, g_ref, wg_ref, wu_ref, wd_ref, o_ref, h_ref):
    f = pl.program_id(1)

    @pl.when(f == 0)
    def _():
        x = x_ref[...]
        h_ref[...] = _rms_bf16(x, g_ref[...])
        o_ref[...] = x

    h = h_ref[...]
    gate = _dot(h, wg_ref[...])
    a = (gate * jax.nn.sigmoid(gate)) * _dot(h, wu_ref[...])
    o_ref[...] += _dot(a.astype(BF16), wd_ref[...])


def _ffn(x, g, wg, wu, wd, tm=1024, tf=256):
    t, d = x.shape
    ff = wg.shape[1]
    return pl.pallas_call(
        _ffn_kernel,
        name="ffn",
        grid=(t // tm, ff // tf),
        in_specs=[pl.BlockSpec((tm, d), lambda i, f: (i, 0)),
                  pl.BlockSpec((1, d), lambda i, f: (0, 0)),
                  pl.BlockSpec((d, tf), lambda i, f: (0, f)),
                  pl.BlockSpec((d, tf), lambda i, f: (0, f)),
                  pl.BlockSpec((tf, d), lambda i, f: (f, 0))],
        out_specs=pl.BlockSpec((tm, d), lambda i, f: (i, 0)),
        out_shape=jax.ShapeDtypeStruct((t, d), F32),
        scratch_shapes=[pltpu.VMEM((tm, d), BF16)],
        compiler_params=_cparams("parallel", "arbitrary"),
    )(x, g.reshape(1, d), wg, wu, wd)


ROUTE_I1, ROUTE_I2, ROUTE_W1, ROUTE_W2, ROUTE_R1, ROUTE_R2 = range(6)
MOE_TILE = 1024


def _lane_pick(arr, lane, k):
    return jnp.sum(jnp.where(lane == k, arr, 0.0), axis=1, keepdims=True)


def _router_kernel(x_ref, g_ref, rhi_ref, rlo_ref, route_ref, cnt_ref):
    i = pl.program_id(0)
    tm = x_ref.shape[0]
    lane = lax.broadcasted_iota(jnp.int32, (tm, LANES), 1)
    lane_f = lane.astype(F32)

    @pl.when(i == 0)
    def _():
        cnt_ref[...] = jnp.zeros(cnt_ref.shape, F32)

    h = _rms_bf16(x_ref[...], g_ref[...])
    logits = _dot(h, rhi_ref[...]) + _dot(h, rlo_ref[...])
    logits = jnp.where(lane < N_EXPERTS, logits, NEG_INF)
    m1 = jnp.max(logits, axis=1, keepdims=True)
    i1 = jnp.min(jnp.where(logits == m1, lane_f, float(LANES)), axis=1, keepdims=True)
    rest = jnp.where(lane_f == i1, NEG_INF, logits)
    m2 = jnp.max(rest, axis=1, keepdims=True)
    i2 = jnp.min(jnp.where(rest == m2, lane_f, float(LANES)), axis=1, keepdims=True)
    e2 = jnp.exp(m2 - m1)
    w1 = 1.0 / (1.0 + e2)
    w2 = e2 / (1.0 + e2)

    chosen = jnp.where((lane_f == i1) | (lane_f == i2), 1.0, 0.0)
    r = lax.broadcasted_iota(jnp.int32, (tm, tm), 0)
    c = lax.broadcasted_iota(jnp.int32, (tm, tm), 1)
    earlier = jnp.where(c < r, 1.0, 0.0).astype(BF16)
    prefix = _dot(earlier, chosen.astype(BF16)) + cnt_ref[0:1, :]
    rank1 = jnp.sum(jnp.where(lane_f == i1, prefix, 0.0), axis=1, keepdims=True)
    rank2 = jnp.sum(jnp.where(lane_f == i2, prefix, 0.0), axis=1, keepdims=True)
    cnt_ref[0:1, :] = cnt_ref[0:1, :] + jnp.sum(chosen, axis=0, keepdims=True)

    cols = {ROUTE_I1: i1, ROUTE_I2: i2, ROUTE_W1: w1, ROUTE_W2: w2, ROUTE_R1: rank1, ROUTE_R2: rank2}
    route = jnp.zeros((tm, LANES), F32)
    for k, val in cols.items():
        route = jnp.where(lane == k, val, route)
    route_ref[...] = route


def _dispatch_kernel(pos1_ref, pos2_ref, x_ref, xs_in, xs_hbm, sem):
    del xs_in
    n = pos1_ref.shape[0]

    def row_copy(r, p):
        return pltpu.make_async_copy(x_ref.at[pl.ds(r, 1)], xs_hbm.at[pl.ds(p, 1)], sem)

    def issue(r, carry):
        row_copy(r, pos1_ref[r]).start()
        row_copy(r, pos2_ref[r]).start()
        return carry

    lax.fori_loop(0, n, issue, 0, unroll=8)
    for _ in range(2):
        pltpu.make_async_copy(x_ref, xs_hbm.at[pl.ds(0, n)], sem).wait()


def _expert_kernel(te_ref, nu_ref, xs_ref, g_ref, wg_ref, wu_ref, wd_ref, y_ref, h_ref):
    del te_ref
    i = pl.program_id(0)
    f = pl.program_id(1)
    used = i < nu_ref[0]

    @pl.when(f == 0)
    def _():
        h_ref[...] = _rms_bf16(xs_ref[...], g_ref[...])
        y_ref[...] = jnp.zeros(y_ref.shape, y_ref.dtype)

    @pl.when(used)
    def _():
        h = h_ref[...]
        gate = _dot(h, wg_ref[...].astype(BF16))
        a = (gate * jax.nn.sigmoid(gate)) * _dot(h, wu_ref[...].astype(BF16))
        y_ref[...] += _dot(a.astype(BF16), wd_ref[...].astype(BF16))


def _combine_kernel(pos1_ref, pos2_ref, x_ref, route_ref, y_hbm, o_ref, buf1, buf2, sem):
    n = pos1_ref.shape[0]

    def row_copy(p, buf, r):
        return pltpu.make_async_copy(y_hbm.at[pl.ds(p, 1)], buf.at[pl.ds(r, 1)], sem)

    def issue(r, carry):
        row_copy(pos1_ref[r], buf1, r).start()
        row_copy(pos2_ref[r], buf2, r).start()
        return carry

    lax.fori_loop(0, n, issue, 0, unroll=8)
    for buf in (buf1, buf2):
        pltpu.make_async_copy(y_hbm.at[pl.ds(0, n)], buf, sem).wait()
    lane = lax.broadcasted_iota(jnp.int32, route_ref.shape, 1)
    route = route_ref[...]
    w1 = _lane_pick(route, lane, ROUTE_W1)
    w2 = _lane_pick(route, lane, ROUTE_W2)
    o_ref[...] = x_ref[...] + w1 * buf1[...] + w2 * buf2[...]


def _moe(x, g, router_w, wg, wu, wd, li, tf=512, t_route=512, t_disp=512, t_comb=256):
    t, d = x.shape
    _, ne, _, ff = wg.shape
    rows = MOE_TILE
    n_rows = 2 * t + ne * rows
    n_tiles = n_rows // rows
    nf = ff // tf
    g2 = g.reshape(1, d)
    rw = jnp.zeros((d, LANES), F32).at[:, :ne].set(router_w)
    rhi = rw.astype(BF16)
    rlo = (rw - rhi.astype(F32)).astype(BF16)

    route, cnt = pl.pallas_call(
        _router_kernel,
        name="moe_route",
        grid=(t // t_route,),
        in_specs=[pl.BlockSpec((t_route, d), lambda i: (i, 0)),
                  pl.BlockSpec((1, d), lambda i: (0, 0)),
                  pl.BlockSpec((d, LANES), lambda i: (0, 0)),
                  pl.BlockSpec((d, LANES), lambda i: (0, 0))],
        out_specs=[pl.BlockSpec((t_route, LANES), lambda i: (i, 0)),
                   pl.BlockSpec((8, LANES), lambda i: (0, 0))],
        out_shape=[jax.ShapeDtypeStruct((t, LANES), F32), jax.ShapeDtypeStruct((8, LANES), F32)],
        compiler_params=_cparams("arbitrary"),
    )(x, g2, rhi, rlo)

    counts = cnt[0, :ne].astype(jnp.int32)
    gsz = (counts + rows - 1) // rows * rows
    ends = jnp.cumsum(gsz)
    offs = ends - gsz
    i1 = route[:, ROUTE_I1].astype(jnp.int32)
    i2 = route[:, ROUTE_I2].astype(jnp.int32)
    pos1 = offs[i1] + route[:, ROUTE_R1].astype(jnp.int32)
    pos2 = offs[i2] + route[:, ROUTE_R2].astype(jnp.int32)
    tile_start = jnp.arange(n_tiles, dtype=jnp.int32) * rows
    tile_expert = jnp.minimum(jnp.sum(tile_start[:, None] >= ends[None, :], axis=1), ne - 1).astype(jnp.int32)
    n_used = (ends[-1] // rows).astype(jnp.int32).reshape(1)

    smem_idx = lambda n: pl.BlockSpec((n,), lambda i: (i,), memory_space=pltpu.SMEM)
    hbm = pl.BlockSpec(memory_space=pl.ANY)
    xs = pl.pallas_call(
        _dispatch_kernel,
        name="moe_dispatch",
        grid=(t // t_disp,),
        in_specs=[smem_idx(t_disp), smem_idx(t_disp), pl.BlockSpec((t_disp, d), lambda i: (i, 0)), hbm],
        out_specs=hbm,
        out_shape=jax.ShapeDtypeStruct((n_rows, d), F32),
        scratch_shapes=[pltpu.SemaphoreType.DMA(())],
        input_output_aliases={3: 0},
        compiler_params=_cparams("arbitrary"),
    )(pos1, pos2, x, jnp.zeros((n_rows, d), F32))

    def w_in_map(i, f, te, nu):
        return li, te[i], 0, jnp.where(i < nu[0], f, nf - 1)

    def w_out_map(i, f, te, nu):
        return li, te[i], jnp.where(i < nu[0], f, nf - 1), 0

    y = pl.pallas_call(
        _expert_kernel,
        name="moe_experts",
        grid_spec=pltpu.PrefetchScalarGridSpec(
            num_scalar_prefetch=2,
            grid=(n_tiles, nf),
            in_specs=[pl.BlockSpec((rows, d), lambda i, f, te, nu: (i, 0)),
                      pl.BlockSpec((1, d), lambda i, f, te, nu: (0, 0)),
                      pl.BlockSpec((None, None, d, tf), w_in_map),
                      pl.BlockSpec((None, None, d, tf), w_in_map),
                      pl.BlockSpec((None, None, tf, d), w_out_map)],
            out_specs=pl.BlockSpec((rows, d), lambda i, f, te, nu: (i, 0)),
            scratch_shapes=[pltpu.VMEM((rows, d), BF16)]),
        out_shape=jax.ShapeDtypeStruct((n_rows, d), F32),
        compiler_params=_cparams("arbitrary", "arbitrary"),
    )(tile_expert, n_used, xs, g2, wg, wu, wd)

    return pl.pallas_call(
        _combine_kernel,
        name="moe_combine",
        grid=(t // t_comb,),
        in_specs=[smem_idx(t_comb), smem_idx(t_comb),
                  pl.BlockSpec((t_comb, d), lambda i: (i, 0)),
                  pl.BlockSpec((t_comb, LANES), lambda i: (i, 0)),
                  hbm],
        out_specs=pl.BlockSpec((t_comb, d), lambda i: (i, 0)),
        out_shape=jax.ShapeDtypeStruct((t, d), F32),
        scratch_shapes=[pltpu.VMEM((t_comb, d), F32), pltpu.VMEM((t_comb, d), F32),
                        pltpu.SemaphoreType.DMA(())],
        compiler_params=_cparams("arbitrary"),
    )(pos1, pos2, x, route, y)


def _gelu_tanh(x):
    return 0.5 * x * (1.0 + jnp.tanh(math.sqrt(2.0 / math.pi) * (x + 0.044715 * (x * x * x))))


def _compress_kernel(kv_ref, pe_ref, w1_ref, w2a_ref, w2b_ref, ka_ref, kb_ref, xf_ref, sh_ref):
    seq = kv_ref.shape[0]
    n = seq // CMP_STRIDE
    hid2 = w1_ref.shape[2]
    xf_ref[...] = kv_ref[...].astype(F32)
    top = jnp.zeros((n, hid2), F32)
    bot = jnp.zeros((n, hid2), F32)
    for i in range(CMP_STRIDE):
        xi = xf_ref[pl.ds(i, n, stride=CMP_STRIDE), :]
        top = top + _dot((xi + pe_ref[i:i + 1, :]).astype(BF16), w1_ref[i])
        bot = bot + _dot((xi + pe_ref[CMP_STRIDE + i:CMP_STRIDE + i + 1, :]).astype(BF16), w1_ref[CMP_STRIDE + i])
    sh_ref[0:n, :] = bot
    sh_ref[n:n + 8, :] = jnp.zeros((8, hid2), F32)
    act = _gelu_tanh(top + sh_ref[1:n + 1, :]).astype(BF16)
    ka_ref[...] = _dot(act, w2a_ref[...]).astype(ka_ref.dtype)
    kb_ref[...] = _dot(act, w2b_ref[...]).astype(kb_ref.dtype)


def _compress(kv, pe_k, pe_v, k_w1, k_w2, v_w1, v_w2, batch, seq):
    assert CMP_LEN == 2 * CMP_STRIDE
    hid = k_w1.shape[1]
    n = seq // CMP_STRIDE
    w1 = jnp.zeros((CMP_LEN, LANES, 2 * hid), F32)
    w1 = w1.at[:, :HEAD_DIM, :hid].set(k_w1.reshape(CMP_LEN, HEAD_DIM, hid))
    w1 = w1.at[:, HEAD_DIM:, hid:].set(v_w1.reshape(CMP_LEN, HEAD_DIM, hid)).astype(BF16)
    zero = jnp.zeros((hid, HEAD_DIM), F32)
    w2a = jnp.block([[k_w2, zero], [zero, v_w2]]).astype(BF16)
    w2b = jnp.block([[zero, k_w2], [v_w2, zero]]).astype(BF16)
    pe = jnp.concatenate([pe_k, pe_v], axis=-1)
    g = NSA_GROUPS
    out = jax.ShapeDtypeStruct((batch * g * n, LANES), BF16)
    return pl.pallas_call(
        _compress_kernel,
        name="compress",
        grid=(batch, g),
        in_specs=[pl.BlockSpec((seq, LANES), lambda b, gi: (b, gi)),
                  pl.BlockSpec((CMP_LEN, LANES), lambda b, gi: (0, 0)),
                  pl.BlockSpec((CMP_LEN, LANES, 2 * hid), lambda b, gi: (0, 0, 0)),
                  pl.BlockSpec((2 * hid, LANES), lambda b, gi: (0, 0)),
                  pl.BlockSpec((2 * hid, LANES), lambda b, gi: (0, 0))],
        out_specs=[pl.BlockSpec((n, LANES), lambda b, gi: (b * g + gi, 0)),
                   pl.BlockSpec((n, LANES), lambda b, gi: (b * g + gi, 0))],
        out_shape=[out, out],
        scratch_shapes=[pltpu.VMEM((seq, LANES), F32), pltpu.VMEM((n + 8, 2 * hid), F32)],
        compiler_params=_cparams("parallel", "parallel"),
    )(kv, pe, w1, w2a, w2b)


def _nsa_cmp_kernel(n_slc, q_ref, ka_ref, kb_ref, ovt_ref, ocmp_ref, selm_ref):
    qi = pl.program_id(2)
    tq = ATT_TILE
    n_pad = ka_ref.shape[0]
    t_q = qi * tq + lax.broadcasted_iota(jnp.int32, (tq, n_pad), 0)
    n_c = lax.broadcasted_iota(jnp.int32, (tq, n_pad), 1)
    mask_c = (n_c * CMP_STRIDE + (CMP_LEN - 1)) <= t_q
    halves = _head_halves(tq)
    ka = ka_ref[...]
    kb = kb_ref[...]
    psum = jnp.zeros((tq, n_pad), F32)
    for pr in range(NSA_HPG // 2):
        qf = q_ref[:, pr * LANES:(pr + 1) * LANES].astype(F32)
        outs = []
        for h in range(2):
            qz = jnp.where(halves[h], qf, 0.0).astype(BF16)
            kmat, vmat = (ka, kb) if h == 0 else (kb, ka)
            s = jnp.where(mask_c, _dot_nt(qz, kmat), NEG_INF)
            m = jnp.max(s, axis=1, keepdims=True)
            m = jnp.where(m == NEG_INF, 0.0, m)
            e = jnp.exp(s - m)
            p = e / jnp.maximum(jnp.sum(e, axis=1, keepdims=True), 1e-30)
            psum = psum + p
            outs.append(_dot(p.astype(BF16), vmat))
        ocmp_ref[:, pr * LANES:(pr + 1) * LANES] = jnp.where(halves[0], outs[0], outs[1]).astype(ocmp_ref.dtype)

    ps_hi, ps_lo = _split_bf16(psum)
    ovt = ovt_ref[...]
    p_slc = (_dot_nt(ovt, ps_hi) + _dot_nt(ovt, ps_lo))[0:n_slc]
    cand = lax.broadcasted_iota(jnp.int32, (n_slc, tq), 0)
    t_blk = jnp.right_shift(qi * tq + lax.broadcasted_iota(jnp.int32, (n_slc, tq), 1), int(math.log2(SLC_BLOCK)))
    valid = cand <= t_blk
    forced = (cand == 0) | (cand == t_blk) | (cand == t_blk - 1)
    score = jnp.where(valid, jnp.where(forced, FORCE_SCORE, p_slc), -1.0)
    keep = _rank_select(score, valid, n_slc, min(SLC_TOPN, n_slc))
    neg = jnp.where(keep, 0.0, MASK_NEG)
    zero = jnp.zeros((HEAD_DIM - n_slc, tq), F32)
    selm_ref[...] = jnp.concatenate([neg, zero, neg, zero], axis=0).T.astype(selm_ref.dtype)


def _nsa_sw_tile(c, q_ref, selm_ref, ksel_ref, kwin_ref, gate_ref, ex_ref, ocmp_ref, o_ref,
                 kaug_ref, vb_ref, kwb_ref):
    tq = ATT_TILE
    w = (c + 1) * tq
    lo = max(c - WINDOW // tq, 0) * tq
    qf = q_ref[...].astype(F32)
    selm = selm_ref[...].astype(F32)
    halves = _head_halves(tq)
    causal, beyond = _tile_masks(tq)
    sel_o, win_o = [], []
    for h in range(2):
        q_aug = jnp.where(halves[h], qf, selm).astype(BF16)
        qz = jnp.where(halves[h], qf, 0.0).astype(BF16)
        if h == 0:
            k_sel, v_sel = kaug_ref[0, 0:w, :], vb_ref[0:w, :]
            k_win, v_win = kwin_ref[lo:w, :], kwb_ref[lo:w, :]
        else:
            k_sel, v_sel = kaug_ref[1, 0:w, :], ksel_ref[0:w, :]
            k_win, v_win = kwb_ref[lo:w, :], kwin_ref[lo:w, :]
        s = _mask_tile(_dot_nt(q_aug, k_sel), c * tq, causal)
        sel_o.append(_softmax_pv(s, v_sel))
        s = _mask_tile(_dot_nt(qz, k_win), c * tq - lo, causal)
        if c * tq - lo == WINDOW:
            s = _mask_tile(s, 0, beyond)
        win_o.append(_softmax_pv(s, v_win))
    o_sel = jnp.where(halves[0], sel_o[0], sel_o[1])
    o_win = jnp.where(halves[0], win_o[0], win_o[1])
    g_hi, g_lo = _split_bf16(gate_ref[...])
    ex = ex_ref[...]
    gexp = _dot(g_hi, ex) + _dot(g_lo, ex)
    y = (gexp[:, 0:LANES] * ocmp_ref[...].astype(F32) + gexp[:, LANES:2 * LANES] * o_sel
         + gexp[:, 2 * LANES:3 * LANES] * o_win)
    o_ref[...] = y.astype(o_ref.dtype)


def _nsa_sw_kernel(q_ref, selm_ref, ksel_ref, kwin_ref, gate_ref, ex_ref, ocmp_ref, o_ref,
                   kaug_ref, vb_ref, kwb_ref):
    pr = pl.program_id(2)
    qi = pl.program_id(3)
    seq = ksel_ref.shape[0]

    @pl.when((pr == 0) & (qi == 0))
    def _():
        kv = ksel_ref[...].astype(F32)
        first, second = _head_halves(seq)
        onehot = _block_onehot(seq, int(math.log2(SLC_BLOCK)))
        swapped = pltpu.roll(kv, HEAD_DIM, axis=1)
        kaug_ref[0] = jnp.where(first, kv, onehot).astype(BF16)
        kaug_ref[1] = jnp.where(second, swapped, onehot).astype(BF16)
        vb_ref[...] = swapped.astype(BF16)
        kwb_ref[...] = pltpu.roll(kwin_ref[...].astype(F32), HEAD_DIM, axis=1).astype(BF16)

    for c in range(seq // ATT_TILE):
        pl.when(qi == c)(functools.partial(_nsa_sw_tile, c, q_ref, selm_ref, ksel_ref, kwin_ref, gate_ref, ex_ref,
                                           ocmp_ref, o_ref, kaug_ref, vb_ref, kwb_ref))


def _nsa_attention(q_raw, q_rot, kv, ka, kb, gates, batch, seq):
    nq = seq // ATT_TILE
    g = NSA_GROUPS
    gw = NSA_HPG * HEAD_DIM
    n_pairs = NSA_HEADS // 2
    n_cmp = (seq - CMP_LEN) // CMP_STRIDE + 1
    n_cmp_pad = seq // CMP_STRIDE
    n_slc = seq // SLC_BLOCK
    assert n_cmp_pad == LANES and n_slc <= HEAD_DIM // 2 and WINDOW % ATT_TILE == 0

    ex = np.zeros((n_pairs, LANES, 3 * LANES), np.float32)
    for hd in range(NSA_HEADS):
        for br in range(3):
            lane0 = br * LANES + (hd % 2) * HEAD_DIM
            ex[hd // 2, hd * 3 + br, lane0:lane0 + HEAD_DIM] = 1.0
    c_s = np.arange(n_cmp) * CMP_STRIDE
    s_s = np.arange(n_slc) * SLC_BLOCK
    ov = np.clip(np.minimum(c_s[:, None] + CMP_LEN, s_s[None, :] + SLC_BLOCK)
                 - np.maximum(c_s[:, None], s_s[None, :]), 0, None) / CMP_LEN
    ovt = np.zeros((LANES, n_cmp_pad), np.float32)
    ovt[:n_slc, :n_cmp] = ov.T

    o_cmp, selm = pl.pallas_call(
        functools.partial(_nsa_cmp_kernel, n_slc),
        name="nsa_cmp",
        grid=(batch, g, nq),
        in_specs=[pl.BlockSpec((ATT_TILE, gw), lambda b, gi, i: (b * nq + i, gi)),
                  pl.BlockSpec((n_cmp_pad, LANES), lambda b, gi, i: (b * g + gi, 0)),
                  pl.BlockSpec((n_cmp_pad, LANES), lambda b, gi, i: (b * g + gi, 0)),
                  pl.BlockSpec((LANES, n_cmp_pad), lambda b, gi, i: (0, 0))],
        out_specs=[pl.BlockSpec((ATT_TILE, gw), lambda b, gi, i: (b * nq + i, gi)),
                   pl.BlockSpec((ATT_TILE, LANES), lambda b, gi, i: (b * nq + i, gi))],
        out_shape=[jax.ShapeDtypeStruct((batch * seq, NSA_HEADS * HEAD_DIM), BF16),
                   jax.ShapeDtypeStruct((batch * seq, g * LANES), BF16)],
        compiler_params=_cparams("parallel", "parallel", "parallel"),
    )(q_raw, ka, kb, jnp.asarray(ovt, BF16))

    ppg = NSA_HPG // 2
    return pl.pallas_call(
        _nsa_sw_kernel,
        name="nsa_sw",
        grid=(batch, g, ppg, nq),
        in_specs=[pl.BlockSpec((ATT_TILE, LANES), lambda b, gi, p, i: (b * nq + i, gi * ppg + p)),
                  pl.BlockSpec((ATT_TILE, LANES), lambda b, gi, p, i: (b * nq + i, gi)),
                  pl.BlockSpec((seq, LANES), lambda b, gi, p, i: (b, g + gi)),
                  pl.BlockSpec((seq, LANES), lambda b, gi, p, i: (b, 2 * g + gi)),
                  pl.BlockSpec((ATT_TILE, LANES), lambda b, gi, p, i: (b * nq + i, 0)),
                  pl.BlockSpec((None, LANES, 3 * LANES), lambda b, gi, p, i: (gi * ppg + p, 0, 0)),
                  pl.BlockSpec((ATT_TILE, LANES), lambda b, gi, p, i: (b * nq + i, gi * ppg + p))],
        out_specs=pl.BlockSpec((ATT_TILE, LANES), lambda b, gi, p, i: (b * nq + i, gi * ppg + p)),
        out_shape=jax.ShapeDtypeStruct((batch * seq, NSA_HEADS * HEAD_DIM), BF16),
        scratch_shapes=[pltpu.VMEM((2, seq, LANES), BF16), pltpu.VMEM((seq, LANES), BF16),
                        pltpu.VMEM((seq, LANES), BF16)],
        compiler_params=_cparams("parallel", "parallel", "arbitrary", "arbitrary"),
    )(q_rot, selm, kv, kv, gates, jnp.asarray(ex, BF16), o_cmp)


def _even_mixer(x, g, w_in, pool_w, pool_scale, w_out, rope_pair, batch, seq):
    chunk = 2 * LANES
    plan = [(0, POOL_DIM, [(0, 0, "none", 0)])]
    for c in range(3 * MOBA_DIM // chunk):
        op = "rope_scale" if c < MOBA_DIM // chunk else ("rope" if c < 2 * MOBA_DIM // chunk else "none")
        plan.append((POOL_DIM + c * chunk, chunk, [(1, c * chunk, op, 0)]))
    u_pool, qkv = _norm_project(x, g, w_in.astype(BF16), plan, [rope_pair],
                                [(POOL_DIM, F32), (3 * MOBA_DIM, BF16)], seq)
    w_bd = jax.scipy.linalg.block_diag(*[pool_w[i] for i in range(pool_w.shape[0])]).astype(BF16)
    y_a = _pool_mixer(u_pool, w_bd, pool_scale, batch, seq)
    y_b = _moba_attention(qkv, batch, seq)
    w_out = w_out.astype(BF16)
    return _out_proj(x, [y_a, y_b], [w_out[:POOL_DIM], w_out[POOL_DIM:]])


def _odd_mixer(x, g, w_in, pe_k, pe_v, k_w1, k_w2, v_w1, v_w2, w_out, rope_pair, rope_single, batch, seq):
    d = x.shape[1]
    qd = NSA_HEADS * HEAD_DIM
    kvd = NSA_KV_DIM
    wq = w_in[:, :qd]
    parts = [w_in[:, qd + i * kvd:qd + (i + 1) * kvd].reshape(d, NSA_GROUPS, HEAD_DIM) for i in range(6)]
    pairs = [jnp.concatenate([parts[2 * i], parts[2 * i + 1]], axis=-1).reshape(d, 2 * kvd) for i in range(3)]
    n_gate = 3 * NSA_HEADS
    w_gate = jnp.zeros((d, LANES), F32).at[:, :n_gate].set(w_in[:, qd + 6 * kvd:])
    w_all = jnp.concatenate([wq] + pairs + [w_gate], axis=1).astype(BF16)

    chunk = 2 * LANES
    plan = []
    for c in range(qd // chunk):
        plan.append((c * chunk, chunk, [(0, c * chunk, "scale", 0), (1, c * chunk, "rope_scale", 0)]))
    for c in range(6 * kvd // chunk):
        branch = c // (2 * kvd // chunk)
        op = "none" if branch == 0 else "rope"
        plan.append((qd + c * chunk, chunk, [(2, c * chunk, op, 1)]))
    plan.append((qd + 6 * kvd, LANES, [(3, 0, "sigmoid", 0)]))
    q_raw, q_rot, kv, gates = _norm_project(
        x, g, w_all, plan, [rope_pair, rope_single],
        [(qd, BF16), (qd, BF16), (6 * kvd, BF16), (LANES, F32)], seq)

    ka, kb = _compress(kv, pe_k, pe_v, k_w1, k_w2, v_w1, v_w2, batch, seq)
    y = _nsa_attention(q_raw, q_rot, kv, ka, kb, gates, batch, seq)
    return _out_proj(x, [y], [w_out.astype(BF16)])


def kernel(x, mem, positions, norm_g, mem_g, final_g, w_in_ab, pool_w, pool_scale, w_out_ab, ffn_w_gate, ffn_w_up, ffn_w_down, w_in_c, cmp_pe_k, cmp_pe_v, cmp_k_w1, cmp_k_w2, cmp_v_w1, cmp_v_w2, w_out_c, router_w, moe_w_gate, moe_w_up, moe_w_down, xa_wq, xa_wkv, xa_wo):
    batch, seq, d = x.shape
    mem_len = mem.shape[1]
    depth = norm_g.shape[0]
    rope_pair, rope_single = _rope_tables(positions)

    wkv_all = jnp.concatenate([xa_wkv[l] for l in range(depth)], axis=1).astype(BF16)
    kv_all = _norm_matmul(mem.reshape(batch * mem_len, d), mem_g, wkv_all)

    xf = x.reshape(batch * seq, d)
    for layer in range(depth):
        i = layer // 2
        if layer % 2 == 0:
            xf = _even_mixer(xf, norm_g[layer, 0], w_in_ab[i], pool_w[i], pool_scale[i], w_out_ab[i],
                             rope_pair, batch, seq)
        else:
            xf = _odd_mixer(xf, norm_g[layer, 0], w_in_c[i], cmp_pe_k[i], cmp_pe_v[i], cmp_k_w1[i], cmp_k_w2[i],
                            cmp_v_w1[i], cmp_v_w2[i], w_out_c[i], rope_pair, rope_single, batch, seq)
        xf = _cross_attention(xf, norm_g[layer, 1], xa_wq[layer].astype(BF16), kv_all, layer,
                              xa_wo[layer].astype(BF16), seq, mem_len)
        if layer % 2 == 0:
            xf = _ffn(xf, norm_g[layer, 2], ffn_w_gate[i].astype(BF16), ffn_w_up[i].astype(BF16),
                      ffn_w_down[i].astype(BF16))
        else:
            xf = _moe(xf, norm_g[layer, 2], router_w[i], moe_w_gate, moe_w_up, moe_w_down, i)
    return _final_norm(xf, final_g).reshape(batch, seq, d)
```

```python
import functools
import math

import numpy as np
import jax
import jax.numpy as jnp
from jax import lax
from jax.experimental import pallas as pl
from jax.experimental.pallas import tpu as pltpu

F32 = jnp.float32
BF16 = jnp.bfloat16

HEAD_DIM = 64
ROPE_DIM = 16
ROPE_THETA = 500000.0
NORM_EPS = 1e-5
QK_SCALE = HEAD_DIM ** -0.5
QK_SCALE_LOG2E = QK_SCALE * math.log2(math.e)

POOL_WINDOWS = (2, 4, 8, 16)
POOL_DIM = 256
MOBA_HEADS = 12
MOBA_DIM = MOBA_HEADS * HEAD_DIM
MOBA_BLOCK = 256
MOBA_TOPK = 3
NSA_HEADS = 16
NSA_GROUPS = 4
NSA_HPG = 4
NSA_KV_DIM = NSA_GROUPS * HEAD_DIM
CMP_LEN = 32
CMP_STRIDE = 16
SLC_BLOCK = 64
SLC_TOPN = 16
WINDOW = 512
FORCE_SCORE = 1e4
XA_HEADS = 4
XA_DIM = XA_HEADS * HEAD_DIM
N_EXPERTS = 8

LANES = 128
ATT_TILE = 256
CMP_TILE = 1024
VMEM_LIMIT = 56 * 1024 * 1024
NEG_INF = float("-inf")
MASK_NEG = -1e30


def _cparams(*sem):
    return pltpu.CompilerParams(dimension_semantics=sem, vmem_limit_bytes=VMEM_LIMIT)


def _dot(a, b):
    return jnp.dot(a, b, preferred_element_type=F32)


def _dot_nt(a, b):
    return lax.dot_general(a, b, (((1,), (1,)), ((), ())), preferred_element_type=F32)


def _split_bf16(a):
    hi = a.astype(BF16)
    lo = (a - hi.astype(F32)).astype(BF16)
    return hi, lo


def _rms_bf16(x, g):
    ms = jnp.mean(x * x, axis=-1, keepdims=True)
    return (x * lax.rsqrt(ms + NORM_EPS) * g).astype(BF16)


def _apply_rope(acc, c, a, b):
    half = ROPE_DIM // 2
    return acc * c + pltpu.roll(acc, LANES - half, axis=1) * a + pltpu.roll(acc, half, axis=1) * b


def _proj_kernel(plan, n_tab, x_ref, g_ref, w_ref, *rest):
    tabs = rest[:3 * n_tab]
    outs = rest[3 * n_tab:]
    h = _rms_bf16(x_ref[...], g_ref[...])
    for wc0, width, sinks in plan:
        acc = _dot(h, w_ref[:, wc0:wc0 + width])
        for oi, oc0, op, tab in sinks:
            for s in range(width // LANES):
                val = acc[:, s * LANES:(s + 1) * LANES]
                if op in ("rope", "rope_scale"):
                    c, a, b = (tabs[3 * tab + i][...] for i in range(3))
                    val = _apply_rope(val, c, a, b)
                if op == "scale":
                    val = val * QK_SCALE
                if op == "rope_scale":
                    val = val * QK_SCALE_LOG2E
                if op == "sigmoid":
                    val = jax.nn.sigmoid(val)
                o = outs[oi]
                o[:, oc0 + s * LANES:oc0 + (s + 1) * LANES] = val.astype(o.dtype)


def _norm_project(x, g, w, plan, tables, out_defs, seq, tm=512):
    t, d = x.shape
    n = w.shape[1]
    n_seq_tiles = seq // tm
    in_specs = [
        pl.BlockSpec((tm, d), lambda i: (i, 0)),
        pl.BlockSpec((1, d), lambda i: (0, 0)),
        pl.BlockSpec((d, n), lambda i: (0, 0)),
    ]
    flat_tabs = []
    for tset in tables:
        for tb in tset:
            flat_tabs.append(tb)
            in_specs.append(pl.BlockSpec((tm, LANES), lambda i: (i % n_seq_tiles, 0)))
    out_shape = [jax.ShapeDtypeStruct((t, wd), dt) for wd, dt in out_defs]
    out_specs = [pl.BlockSpec((tm, wd), lambda i: (i, 0)) for wd, _ in out_defs]
    return pl.pallas_call(
        functools.partial(_proj_kernel, plan, len(tables)),
        name="proj",
        grid=(t // tm,),
        in_specs=in_specs,
        out_specs=out_specs,
        out_shape=out_shape,
        compiler_params=_cparams("parallel"),
    )(x, g.reshape(1, d), w, *flat_tabs)


def _rope_tables(positions):
    half = ROPE_DIM // 2
    inv = ROPE_THETA ** (-jnp.arange(0, ROPE_DIM, 2, dtype=F32) / ROPE_DIM)
    ang = positions.astype(F32)[:, None] * inv[None, :]
    cos, sin = jnp.cos(ang), jnp.sin(ang)
    s = positions.shape[0]
    z8 = jnp.zeros((s, half), F32)
    rest0 = jnp.zeros((s, HEAD_DIM - ROPE_DIM), F32)
    rest1 = jnp.ones((s, HEAD_DIM - ROPE_DIM), F32)
    c64 = jnp.concatenate([cos, cos, rest1], -1)
    a64 = jnp.concatenate([-sin, z8, rest0], -1)
    b64 = jnp.concatenate([z8, sin, rest0], -1)
    one64 = jnp.ones((s, HEAD_DIM), F32)
    zero64 = jnp.zeros((s, HEAD_DIM), F32)
    pair = tuple(jnp.concatenate([m, m], -1) for m in (c64, a64, b64))
    single = (jnp.concatenate([c64, one64], -1), jnp.concatenate([a64, zero64], -1),
              jnp.concatenate([b64, zero64], -1))
    return pair, single


def _norm_matmul_kernel(x_ref, g_ref, w_ref, o_ref):
    h = _rms_bf16(x_ref[...], g_ref[...])
    o_ref[...] = _dot(h, w_ref[...]).astype(o_ref.dtype)


def _norm_matmul(x, g, w, tm=512):
    t, d = x.shape
    n = w.shape[1]
    return pl.pallas_call(
        _norm_matmul_kernel,
        name="mem_kv",
        grid=(t // tm,),
        in_specs=[pl.BlockSpec((tm, d), lambda i: (i, 0)),
                  pl.BlockSpec((1, d), lambda i: (0, 0)),
                  pl.BlockSpec((d, n), lambda i: (0, 0))],
        out_specs=pl.BlockSpec((tm, n), lambda i: (i, 0)),
        out_shape=jax.ShapeDtypeStruct((t, n), BF16),
        compiler_params=_cparams("parallel"),
    )(x, g.reshape(1, d), w)


def _final_norm_kernel(x_ref, g_ref, o_ref):
    x = x_ref[...]
    ms = jnp.mean(x * x, axis=-1, keepdims=True)
    o_ref[...] = x * lax.rsqrt(ms + NORM_EPS) * g_ref[...]


def _final_norm(x, g, tm=1024):
    t, d = x.shape
    return pl.pallas_call(
        _final_norm_kernel,
        name="final_norm",
        grid=(t // tm,),
        in_specs=[pl.BlockSpec((tm, d), lambda i: (i, 0)), pl.BlockSpec((1, d), lambda i: (0, 0))],
        out_specs=pl.BlockSpec((tm, d), lambda i: (i, 0)),
        out_shape=jax.ShapeDtypeStruct((t, d), F32),
        compiler_params=_cparams("parallel"),
    )(x, g.reshape(1, d))


def _out_proj_kernel(n_in, x_ref, *rest):
    ys = rest[:n_in]
    ws = rest[n_in:2 * n_in]
    o_ref = rest[2 * n_in]
    acc = x_ref[...]
    for y, w in zip(ys, ws):
        acc = acc + _dot(y[...], w[...])
    o_ref[...] = acc


def _out_proj(x, ys, ws, tm=512):
    t, d = x.shape
    in_specs = [pl.BlockSpec((tm, d), lambda i: (i, 0))]
    in_specs += [pl.BlockSpec((tm, y.shape[1]), lambda i: (i, 0)) for y in ys]
    in_specs += [pl.BlockSpec(w.shape, lambda i: (0, 0)) for w in ws]
    return pl.pallas_call(
        functools.partial(_out_proj_kernel, len(ys)),
        name="out_proj",
        grid=(t // tm,),
        in_specs=in_specs,
        out_specs=pl.BlockSpec((tm, d), lambda i: (i, 0)),
        out_shape=jax.ShapeDtypeStruct((t, d), F32),
        compiler_params=_cparams("parallel"),
    )(x, *ys, *ws)


def _pool_kernel(u_ref, w_ref, sc_ref, o_ref, pad_ref):
    s = u_ref.shape[0]
    maxw = POOL_WINDOWS[-1]
    u = u_ref[...]
    pad_ref[0:maxw, :] = jnp.zeros((maxw, POOL_DIM), F32)
    t1 = (lax.broadcasted_iota(jnp.int32, (s, POOL_DIM), 0) + 1).astype(F32)
    lane = lax.broadcasted_iota(jnp.int32, (s, POOL_DIM), 1)
    gdim = POOL_DIM // len(POOL_WINDOWS)
    cur = u
    pooled = jnp.zeros_like(u)
    shift = 1
    for gi, w in enumerate(POOL_WINDOWS):
        pad_ref[maxw:maxw + s, :] = cur
        cur = cur + pad_ref[maxw - shift:maxw - shift + s, :]
        shift *= 2
        assert shift == w
        mean = cur / jnp.minimum(t1, float(w))
        pooled = jnp.where((lane >= gi * gdim) & (lane < (gi + 1) * gdim), mean, pooled)
    pooled = (pooled - u).astype(BF16)
    o_ref[...] = (_dot(pooled, w_ref[...]) * sc_ref[...]).astype(o_ref.dtype)


def _pool_mixer(u_pool, w_bd, scale, batch, seq):
    return pl.pallas_call(
        _pool_kernel,
        name="pool",
        grid=(batch,),
        in_specs=[pl.BlockSpec((seq, POOL_DIM), lambda b: (b, 0)),
                  pl.BlockSpec((POOL_DIM, POOL_DIM), lambda b: (0, 0)),
                  pl.BlockSpec((1, POOL_DIM), lambda b: (0, 0))],
        out_specs=pl.BlockSpec((seq, POOL_DIM), lambda b: (b, 0)),
        out_shape=jax.ShapeDtypeStruct((batch * seq, POOL_DIM), BF16),
        scratch_shapes=[pltpu.VMEM((seq + POOL_WINDOWS[-1], POOL_DIM), F32)],
        compiler_params=_cparams("parallel"),
    )(u_pool, w_bd, scale.reshape(1, POOL_DIM))


def _rank_select(scoreT, valid, n_rows, topn):
    row = lax.broadcasted_iota(jnp.int32, scoreT.shape, 0)
    rank = jnp.zeros(scoreT.shape, F32)
    for j in range(n_rows):
        sj = scoreT[j:j + 1, :]
        beats = (scoreT > sj) | ((scoreT == sj) & (row < j))
        rj = jnp.sum(beats.astype(F32), axis=0, keepdims=True)
        rank = jnp.where(row == j, rj, rank)
    return (rank < float(topn)) & valid


def _head_halves(rows):
    lane = lax.broadcasted_iota(jnp.int32, (rows, LANES), 1)
    return lane < HEAD_DIM, lane >= HEAD_DIM


def _tile_masks(tq):
    qry = lax.broadcasted_iota(jnp.int32, (tq, tq), 0)
    key = lax.broadcasted_iota(jnp.int32, (tq, tq), 1)
    return key <= qry, key > qry


def _mask_tile(s, t0, mask):
    tq = mask.shape[0]
    parts = []
    if t0 > 0:
        parts.append(s[:, :t0])
    parts.append(jnp.where(mask, s[:, t0:t0 + tq], NEG_INF))
    if t0 + tq < s.shape[1]:
        parts.append(s[:, t0 + tq:])
    return parts[0] if len(parts) == 1 else jnp.concatenate(parts, axis=1)


def _softmax_pv(s, v, h):
    m = jnp.max(s, axis=1, keepdims=True)
    p = jnp.exp2((s - m).astype(BF16))
    o = _dot(p, v)
    denom_lane = (1 - h) * HEAD_DIM
    return o / o[:, denom_lane:denom_lane + 1]


def _with_ones(v, own):
    return jnp.where(own, v, 1.0).astype(BF16)


def _block_onehot(seq, shift):
    lane = lax.broadcasted_iota(jnp.int32, (seq, LANES), 1)
    blk = jnp.right_shift(lax.broadcasted_iota(jnp.int32, (seq, LANES), 0), shift)
    return jnp.where((lane & (HEAD_DIM - 1)) == blk, 1.0, 0.0)


def _moba_tile(c, o_ref, qaug_ref, kaug_ref, vaug_ref):
    tq = ATT_TILE
    w = (c + 1) * tq
    halves = _head_halves(tq)
    causal, _ = _tile_masks(tq)
    outs = []
    for h in range(2):
        s = _dot_nt(qaug_ref[h, c * tq:(c + 1) * tq, :], kaug_ref[h, 0:w, :])
        s = _mask_tile(s, c * tq, causal)
        outs.append(_softmax_pv(s, vaug_ref[h, 0:w, :], h))
    o_ref[...] = jnp.where(halves[0], outs[0], outs[1]).astype(o_ref.dtype)


def _moba_kernel(q_ref, k_ref, v_ref, o_ref, qaug_ref, kaug_ref, vaug_ref, kmean_ref):
    qi = pl.program_id(2)
    seq = k_ref.shape[0]
    nb = seq // MOBA_BLOCK
    blk_shift = int(math.log2(MOBA_BLOCK))

    @pl.when(qi == 0)
    def _():
        halves = _head_halves(seq)
        kf = k_ref[...].astype(F32)
        vf = v_ref[...].astype(F32)
        qp = q_ref[...]
        qf = qp.astype(F32)
        onehot = _block_onehot(seq, blk_shift)
        kmean_ref[...] = jnp.zeros(kmean_ref.shape, F32)
        for j in range(nb):
            kmean_ref[j:j + 1, :] = jnp.mean(kf[j * MOBA_BLOCK:(j + 1) * MOBA_BLOCK], axis=0, keepdims=True)
        cand = lax.broadcasted_iota(jnp.int32, (nb, seq), 0)
        own = jnp.right_shift(lax.broadcasted_iota(jnp.int32, (nb, seq), 1), blk_shift)
        valid = cand < own
        for h in range(2):
            kaug_ref[h] = jnp.where(halves[h], kf, onehot).astype(BF16)
            vaug_ref[h] = _with_ones(vf, halves[h])
            mine_k = _head_halves(kmean_ref.shape[0])[h]
            km_hi, km_lo = _split_bf16(jnp.where(mine_k, kmean_ref[...], 0.0))
            gate = (_dot_nt(km_hi, qp) + _dot_nt(km_lo, qp))[0:nb]
            gate = jnp.where(valid, gate, NEG_INF)
            keep = _rank_select(gate, valid, nb, MOBA_TOPK) | (cand == own)
            neg = jnp.where(keep, 0.0, MASK_NEG)
            lead = HEAD_DIM if h == 0 else 0
            pieces = [neg, jnp.zeros((LANES - lead - nb, seq), F32)]
            if lead:
                pieces.insert(0, jnp.zeros((lead, seq), F32))
            maskcols = jnp.concatenate(pieces, axis=0).T
            qaug_ref[h] = jnp.where(halves[h], qf, maskcols).astype(BF16)

    for c in range(nb):
        pl.when(qi == c)(functools.partial(_moba_tile, c, o_ref, qaug_ref, kaug_ref, vaug_ref))


def _moba_attention(qkv, batch, seq):
    assert MOBA_BLOCK == ATT_TILE and seq // MOBA_BLOCK <= 16
    nq = seq // ATT_TILE
    hp = MOBA_DIM // LANES
    return pl.pallas_call(
        _moba_kernel,
        name="moba",
        grid=(batch, hp, nq),
        in_specs=[pl.BlockSpec((seq, LANES), lambda b, p, i: (b, p)),
                  pl.BlockSpec((seq, LANES), lambda b, p, i: (b, hp + p)),
                  pl.BlockSpec((seq, LANES), lambda b, p, i: (b, 2 * hp + p))],
        out_specs=pl.BlockSpec((ATT_TILE, LANES), lambda b, p, i: (b * nq + i, p)),
        out_shape=jax.ShapeDtypeStruct((batch * seq, MOBA_DIM), BF16),
        scratch_shapes=[pltpu.VMEM((2, seq, LANES), BF16), pltpu.VMEM((2, seq, LANES), BF16),
                        pltpu.VMEM((2, seq, LANES), BF16), pltpu.VMEM((16, LANES), F32)],
        compiler_params=_cparams("parallel", "parallel", "arbitrary"),
    )(qkv, qkv, qkv)


def _xattn_kernel(x_ref, g_ref, wq_ref, kv_ref, wo_ref, o_ref):
    x = x_ref[...]
    h = _rms_bf16(x, g_ref[...])
    q = (_dot(h, wq_ref[...]) * QK_SCALE).astype(BF16)
    out = x
    for hd in range(XA_HEADS):
        sl = slice(hd * HEAD_DIM, (hd + 1) * HEAD_DIM)
        k = kv_ref[:, sl]
        v = kv_ref[:, XA_DIM + hd * HEAD_DIM:XA_DIM + (hd + 1) * HEAD_DIM]
        s = _dot_nt(q[:, sl], k)
        m = jnp.max(s, axis=1, keepdims=True)
        p = jnp.exp(s - m)
        l = jnp.sum(p, axis=1, keepdims=True)
        o = _dot(p.astype(BF16), v) / l
        out = out + _dot(o.astype(BF16), wo_ref[sl, :])
    o_ref[...] = out


def _cross_attention(x, g, wq, kv_all, layer, wo, seq, mem_len, tm=512):
    t, d = x.shape
    tiles_per_seq = seq // tm
    return pl.pallas_call(
        _xattn_kernel,
        name="xattn",
        grid=(t // tm,),
        in_specs=[pl.BlockSpec((tm, d), lambda i: (i, 0)),
                  pl.BlockSpec((1, d), lambda i: (0, 0)),
                  pl.BlockSpec((d, XA_DIM), lambda i: (0, 0)),
                  pl.BlockSpec((mem_len, 2 * XA_DIM), lambda i: (i // tiles_per_seq, layer)),
                  pl.BlockSpec((XA_DIM, d), lambda i: (0, 0))],
        out_specs=pl.BlockSpec((tm, d), lambda i: (i, 0)),
        out_shape=jax.ShapeDtypeStruct((t, d), F32),
        compiler_params=_cparams("parallel"),
    )(x, g.reshape(1, d), wq, kv_all, wo)


def _ffn_kernel(x_ref, g_ref, wg_ref, wu_ref, wd_ref, o_ref, h_ref):
    f = pl.program_id(1)

    @pl.when(f == 0)
    def _():
        x = x_ref[...]
        h_ref[...] = _rms_bf16(x, g_ref[...])
        o_ref[...] = x

    h = h_ref[...]
    gate = _dot(h, wg_ref[...])
    a = (gate * jax.nn.sigmoid(gate)) * _dot(h, wu_ref[...])
    o_ref[...] += _dot(a.astype(BF16), wd_ref[...])


def _ffn(x, g, wg, wu, wd, tm=1024, tf=256):
    t, d = x.shape
    ff = wg.shape[1]
    return pl.pallas_call(
        _ffn_kernel,
        name="ffn",
        grid=(t // tm, ff // tf),
        in_specs=[pl.BlockSpec((tm, d), lambda i, f: (i, 0)),
                  pl.BlockSpec((1, d), lambda i, f: (0, 0)),
                  pl.BlockSpec((d, tf), lambda i, f: (0, f)),
                  pl.BlockSpec((d, tf), lambda i, f: (0, f)),
                  pl.BlockSpec((tf, d), lambda i, f: (f, 0))],
        out_specs=pl.BlockSpec((tm, d), lambda i, f: (i, 0)),
        out_shape=jax.ShapeDtypeStruct((t, d), F32),
        scratch_shapes=[pltpu.VMEM((tm, d), BF16)],
        compiler_params=_cparams("parallel", "arbitrary"),
    )(x, g.reshape(1, d), wg, wu, wd)


ROUTE_I1, ROUTE_I2, ROUTE_W1, ROUTE_W2, ROUTE_R1, ROUTE_R2 = range(6)
MOE_TILE = 1024


def _lane_pick(arr, lane, k):
    return jnp.sum(jnp.where(lane == k, arr, 0.0), axis=1, keepdims=True)


def _router_kernel(x_ref, g_ref, rhi_ref, rlo_ref, route_ref, cnt_ref):
    i = pl.program_id(0)
    tm = x_ref.shape[0]
    lane = lax.broadcasted_iota(jnp.int32, (tm, LANES), 1)
    lane_f = lane.astype(F32)

    @pl.when(i == 0)
    def _():
        cnt_ref[...] = jnp.zeros(cnt_ref.shape, F32)

    h = _rms_bf16(x_ref[...], g_ref[...])
    logits = _dot(h, rhi_ref[...]) + _dot(h, rlo_ref[...])
    logits = jnp.where(lane < N_EXPERTS, logits, NEG_INF)
    m1 = jnp.max(logits, axis=1, keepdims=True)
    i1 = jnp.min(jnp.where(logits == m1, lane_f, float(LANES)), axis=1, keepdims=True)
    rest = jnp.where(lane_f == i1, NEG_INF, logits)
    m2 = jnp.max(rest, axis=1, keepdims=True)
    i2 = jnp.min(jnp.where(rest == m2, lane_f, float(LANES)), axis=1, keepdims=True)
    e2 = jnp.exp(m2 - m1)
    w1 = 1.0 / (1.0 + e2)
    w2 = e2 / (1.0 + e2)

    chosen = jnp.where((lane_f == i1) | (lane_f == i2), 1.0, 0.0)
    r = lax.broadcasted_iota(jnp.int32, (tm, tm), 0)
    c = lax.broadcasted_iota(jnp.int32, (tm, tm), 1)
    earlier = jnp.where(c < r, 1.0, 0.0).astype(BF16)
    prefix = _dot(earlier, chosen.astype(BF16)) + cnt_ref[0:1, :]
    rank1 = jnp.sum(jnp.where(lane_f == i1, prefix, 0.0), axis=1, keepdims=True)
    rank2 = jnp.sum(jnp.where(lane_f == i2, prefix, 0.0), axis=1, keepdims=True)
    cnt_ref[0:1, :] = cnt_ref[0:1, :] + jnp.sum(chosen, axis=0, keepdims=True)

    cols = {ROUTE_I1: i1, ROUTE_I2: i2, ROUTE_W1: w1, ROUTE_W2: w2, ROUTE_R1: rank1, ROUTE_R2: rank2}
    route = jnp.zeros((tm, LANES), F32)
    for k, val in cols.items():
        route = jnp.where(lane == k, val, route)
    route_ref[...] = route


def _dispatch_kernel(pos1_ref, pos2_ref, x_ref, xs_in, xs_hbm, sem):
    del xs_in
    n = pos1_ref.shape[0]

    def row_copy(r, p):
        return pltpu.make_async_copy(x_ref.at[pl.ds(r, 1)], xs_hbm.at[pl.ds(p, 1)], sem)

    def issue(r, carry):
        row_copy(r, pos1_ref[r]).start()
        row_copy(r, pos2_ref[r]).start()
        return carry

    lax.fori_loop(0, n, issue, 0, unroll=8)
    for _ in range(2):
        pltpu.make_async_copy(x_ref, xs_hbm.at[pl.ds(0, n)], sem).wait()


def _expert_kernel(te_ref, nu_ref, xs_ref, g_ref, wg_ref, wu_ref, wd_ref, y_ref, h_ref):
    del te_ref
    i = pl.program_id(0)
    f = pl.program_id(1)
    used = i < nu_ref[0]

    @pl.when(f == 0)
    def _():
        h_ref[...] = _rms_bf16(xs_ref[...], g_ref[...])
        y_ref[...] = jnp.zeros(y_ref.shape, y_ref.dtype)

    @pl.when(used)
    def _():
        h = h_ref[...]
        gate = _dot(h, wg_ref[...].astype(BF16))
        a = (gate * jax.nn.sigmoid(gate)) * _dot(h, wu_ref[...].astype(BF16))
        y_ref[...] += _dot(a.astype(BF16), wd_ref[...].astype(BF16))


def _combine_kernel(pos1_ref, pos2_ref, x_ref, route_ref, y_hbm, o_ref, buf1, buf2, sem):
    n = pos1_ref.shape[0]

    def row_copy(p, buf, r):
        return pltpu.make_async_copy(y_hbm.at[pl.ds(p, 1)], buf.at[pl.ds(r, 1)], sem)

    def issue(r, carry):
        row_copy(pos1_ref[r], buf1, r).start()
        row_copy(pos2_ref[r], buf2, r).start()
        return carry

    lax.fori_loop(0, n, issue, 0, unroll=8)
    for buf in (buf1, buf2):
        pltpu.make_async_copy(y_hbm.at[pl.ds(0, n)], buf, sem).wait()
    lane = lax.broadcasted_iota(jnp.int32, route_ref.shape, 1)
    route = route_ref[...]
    w1 = _lane_pick(route, lane, ROUTE_W1)
    w2 = _lane_pick(route, lane, ROUTE_W2)
    o_ref[...] = x_ref[...] + w1 * buf1[...] + w2 * buf2[...]


def _moe(x, g, router_w, wg, wu, wd, li, tf=512, t_route=512, t_disp=512, t_comb=256):
    t, d = x.shape
    _, ne, _, ff = wg.shape
    rows = MOE_TILE
    n_rows = 2 * t + ne * rows
    n_tiles = n_rows // rows
    nf = ff // tf
    g2 = g.reshape(1, d)
    rw = jnp.zeros((d, LANES), F32).at[:, :ne].set(router_w)
    rhi = rw.astype(BF16)
    rlo = (rw - rhi.astype(F32)).astype(BF16)

    route, cnt = pl.pallas_call(
        _router_kernel,
        name="moe_route",
        grid=(t // t_route,),
        in_specs=[pl.BlockSpec((t_route, d), lambda i: (i, 0)),
                  pl.BlockSpec((1, d), lambda i: (0, 0)),
                  pl.BlockSpec((d, LANES), lambda i: (0, 0)),
                  pl.BlockSpec((d, LANES), lambda i: (0, 0))],
        out_specs=[pl.BlockSpec((t_route, LANES), lambda i: (i, 0)),
                   pl.BlockSpec((8, LANES), lambda i: (0, 0))],
        out_shape=[jax.ShapeDtypeStruct((t, LANES), F32), jax.ShapeDtypeStruct((8, LANES), F32)],
        compiler_params=_cparams("arbitrary"),
    )(x, g2, rhi, rlo)

    counts = cnt[0, :ne].astype(jnp.int32)
    gsz = (counts + rows - 1) // rows * rows
    ends = jnp.cumsum(gsz)
    offs = ends - gsz
    i1 = route[:, ROUTE_I1].astype(jnp.int32)
    i2 = route[:, ROUTE_I2].astype(jnp.int32)
    pos1 = offs[i1] + route[:, ROUTE_R1].astype(jnp.int32)
    pos2 = offs[i2] + route[:, ROUTE_R2].astype(jnp.int32)
    tile_start = jnp.arange(n_tiles, dtype=jnp.int32) * rows
    tile_expert = jnp.minimum(jnp.sum(tile_start[:, None] >= ends[None, :], axis=1), ne - 1).astype(jnp.int32)
    n_used = (ends[-1] // rows).astype(jnp.int32).reshape(1)

    smem_idx = lambda n: pl.BlockSpec((n,), lambda i: (i,), memory_space=pltpu.SMEM)
    hbm = pl.BlockSpec(memory_space=pl.ANY)
    xs = pl.pallas_call(
        _dispatch_kernel,
        name="moe_dispatch",
        grid=(t // t_disp,),
        in_specs=[smem_idx(t_disp), smem_idx(t_disp), pl.BlockSpec((t_disp, d), lambda i: (i, 0)), hbm],
        out_specs=hbm,
        out_shape=jax.ShapeDtypeStruct((n_rows, d), F32),
        scratch_shapes=[pltpu.SemaphoreType.DMA(())],
        input_output_aliases={3: 0},
        compiler_params=_cparams("arbitrary"),
    )(pos1, pos2, x, jnp.zeros((n_rows, d), F32))

    def w_in_map(i, f, te, nu):
        return li, te[i], 0, jnp.where(i < nu[0], f, nf - 1)

    def w_out_map(i, f, te, nu):
        return li, te[i], jnp.where(i < nu[0], f, nf - 1), 0

    y = pl.pallas_call(
        _expert_kernel,
        name="moe_experts",
        grid_spec=pltpu.PrefetchScalarGridSpec(
            num_scalar_prefetch=2,
            grid=(n_tiles, nf),
            in_specs=[pl.BlockSpec((rows, d), lambda i, f, te, nu: (i, 0)),
                      pl.BlockSpec((1, d), lambda i, f, te, nu: (0, 0)),
                      pl.BlockSpec((None, None, d, tf), w_in_map),
                      pl.BlockSpec((None, None, d, tf), w_in_map),
                      pl.BlockSpec((None, None, tf, d), w_out_map)],
            out_specs=pl.BlockSpec((rows, d), lambda i, f, te, nu: (i, 0)),
            scratch_shapes=[pltpu.VMEM((rows, d), BF16)]),
        out_shape=jax.ShapeDtypeStruct((n_rows, d), F32),
        compiler_params=_cparams("arbitrary", "arbitrary"),
    )(tile_expert, n_used, xs, g2, wg, wu, wd)

    return pl.pallas_call(
        _combine_kernel,
        name="moe_combine",
        grid=(t // t_comb,),
        in_specs=[smem_idx(t_comb), smem_idx(t_comb),
                  pl.BlockSpec((t_comb, d), lambda i: (i, 0)),
                  pl.BlockSpec((t_comb, LANES), lambda i: (i, 0)),
                  hbm],
        out_specs=pl.BlockSpec((t_comb, d), lambda i: (i, 0)),
        out_shape=jax.ShapeDtypeStruct((t, d), F32),
        scratch_shapes=[pltpu.VMEM((t_comb, d), F32), pltpu.VMEM((t_comb, d), F32),
                        pltpu.SemaphoreType.DMA(())],
        compiler_params=_cparams("arbitrary"),
    )(pos1, pos2, x, route, y)


def _gelu_tanh(x):
    return 0.5 * x * (1.0 + jnp.tanh(math.sqrt(2.0 / math.pi) * (x + 0.044715 * (x * x * x))))


def _compress_kernel(kv_ref, pe_ref, w1_ref, w2a_ref, w2b_ref, ka_ref, kb_ref, xf_ref, sh_ref):
    seq = kv_ref.shape[0]
    n = seq // CMP_STRIDE
    hid2 = w1_ref.shape[2]
    xf_ref[...] = kv_ref[...].astype(F32)
    top = jnp.zeros((n, hid2), F32)
    bot = jnp.zeros((n, hid2), F32)
    for i in range(CMP_STRIDE):
        xi = xf_ref[pl.ds(i, n, stride=CMP_STRIDE), :]
        top = top + _dot((xi + pe_ref[i:i + 1, :]).astype(BF16), w1_ref[i])
        bot = bot + _dot((xi + pe_ref[CMP_STRIDE + i:CMP_STRIDE + i + 1, :]).astype(BF16), w1_ref[CMP_STRIDE + i])
    sh_ref[0:n, :] = bot
    sh_ref[n:n + 8, :] = jnp.zeros((8, hid2), F32)
    act = _gelu_tanh(top + sh_ref[1:n + 1, :]).astype(BF16)
    ka_ref[...] = _dot(act, w2a_ref[...]).astype(ka_ref.dtype)
    kb_ref[...] = _dot(act, w2b_ref[...]).astype(kb_ref.dtype)


def _compress(kv, pe_k, pe_v, k_w1, k_w2, v_w1, v_w2, batch, seq):
    assert CMP_LEN == 2 * CMP_STRIDE
    hid = k_w1.shape[1]
    n = seq // CMP_STRIDE
    w1 = jnp.zeros((CMP_LEN, LANES, 2 * hid), F32)
    w1 = w1.at[:, :HEAD_DIM, :hid].set(k_w1.reshape(CMP_LEN, HEAD_DIM, hid))
    w1 = w1.at[:, HEAD_DIM:, hid:].set(v_w1.reshape(CMP_LEN, HEAD_DIM, hid)).astype(BF16)
    zero = jnp.zeros((hid, HEAD_DIM), F32)
    w2a = jnp.block([[k_w2, zero], [zero, v_w2]]).astype(BF16)
    w2b = jnp.block([[zero, k_w2], [v_w2, zero]]).astype(BF16)
    pe = jnp.concatenate([pe_k, pe_v], axis=-1)
    g = NSA_GROUPS
    out = jax.ShapeDtypeStruct((batch * g * n, LANES), BF16)
    return pl.pallas_call(
        _compress_kernel,
        name="compress",
        grid=(batch, g),
        in_specs=[pl.BlockSpec((seq, LANES), lambda b, gi: (b, gi)),
                  pl.BlockSpec((CMP_LEN, LANES), lambda b, gi: (0, 0)),
                  pl.BlockSpec((CMP_LEN, LANES, 2 * hid), lambda b, gi: (0, 0, 0)),
                  pl.BlockSpec((2 * hid, LANES), lambda b, gi: (0, 0)),
                  pl.BlockSpec((2 * hid, LANES), lambda b, gi: (0, 0))],
        out_specs=[pl.BlockSpec((n, LANES), lambda b, gi: (b * g + gi, 0)),
                   pl.BlockSpec((n, LANES), lambda b, gi: (b * g + gi, 0))],
        out_shape=[out, out],
        scratch_shapes=[pltpu.VMEM((seq, LANES), F32), pltpu.VMEM((n + 8, 2 * hid), F32)],
        compiler_params=_cparams("parallel", "parallel"),
    )(kv, pe, w1, w2a, w2b)


def _nsa_cmp_kernel(n_slc, q_ref, ka_ref, kb_ref, ovt_ref, ocmp_ref, selm_ref):
    qi = pl.program_id(2)
    tq = q_ref.shape[0]
    n_pad = ka_ref.shape[0]
    t_q = qi * tq + lax.broadcasted_iota(jnp.int32, (tq, n_pad), 0)
    n_c = lax.broadcasted_iota(jnp.int32, (tq, n_pad), 1)
    mask_c = (n_c * CMP_STRIDE + (CMP_LEN - 1)) <= t_q
    halves = _head_halves(tq)
    ka = ka_ref[...]
    kb = kb_ref[...]
    psum = jnp.zeros((tq, n_pad), F32)
    for pr in range(NSA_HPG // 2):
        qf = q_ref[:, pr * LANES:(pr + 1) * LANES].astype(F32)
        outs = []
        for h in range(2):
            qz = jnp.where(halves[h], qf, 0.0).astype(BF16)
            kmat, vmat = (ka, kb) if h == 0 else (kb, ka)
            s = jnp.where(mask_c, _dot_nt(qz, kmat), NEG_INF)
            m = jnp.max(s, axis=1, keepdims=True)
            m = jnp.where(m == NEG_INF, 0.0, m)
            e = jnp.exp(s - m)
            p = e / jnp.maximum(jnp.sum(e, axis=1, keepdims=True), 1e-30)
            psum = psum + p
            outs.append(_dot(p.astype(BF16), vmat))
        ocmp_ref[:, pr * LANES:(pr + 1) * LANES] = jnp.where(halves[0], outs[0], outs[1]).astype(ocmp_ref.dtype)

    ps_hi, ps_lo = _split_bf16(psum)
    ovt = ovt_ref[...]
    p_slc = (_dot_nt(ovt, ps_hi) + _dot_nt(ovt, ps_lo))[0:n_slc]
    cand = lax.broadcasted_iota(jnp.int32, (n_slc, tq), 0)
    t_blk = jnp.right_shift(qi * tq + lax.broadcasted_iota(jnp.int32, (n_slc, tq), 1), int(math.log2(SLC_BLOCK)))
    valid = cand <= t_blk
    forced = (cand == 0) | (cand == t_blk) | (cand == t_blk - 1)
    score = jnp.where(valid, jnp.where(forced, FORCE_SCORE, p_slc), -1.0)
    keep = _rank_select(score, valid, n_slc, min(SLC_TOPN, n_slc))
    neg = jnp.where(keep, 0.0, MASK_NEG)
    zero = jnp.zeros((HEAD_DIM - n_slc, tq), F32)
    selm_ref[...] = jnp.concatenate([neg, zero, neg, zero], axis=0).T.astype(selm_ref.dtype)


def _nsa_sw_tile(c, q_ref, selm_ref, gate_ref, ex_ref, ocmp_ref, o_ref, ksel_ref, vsel_ref, kwin_ref, vwin_ref):
    tq = ATT_TILE
    w = (c + 1) * tq
    lo = max(c - WINDOW // tq, 0) * tq
    qf = q_ref[...].astype(F32)
    selm = selm_ref[...].astype(F32)
    halves = _head_halves(tq)
    causal, beyond = _tile_masks(tq)
    sel_o, win_o = [], []
    for h in range(2):
        q_aug = jnp.where(halves[h], qf, selm).astype(BF16)
        qz = jnp.where(halves[h], qf, 0.0).astype(BF16)
        s = _mask_tile(_dot_nt(q_aug, ksel_ref[h, 0:w, :]), c * tq, causal)
        sel_o.append(_softmax_pv(s, vsel_ref[h, 0:w, :], h))
        s = _mask_tile(_dot_nt(qz, kwin_ref[h, lo:w, :]), c * tq - lo, causal)
        if c * tq - lo == WINDOW:
            s = _mask_tile(s, 0, beyond)
        win_o.append(_softmax_pv(s, vwin_ref[h, lo:w, :], h))
    o_sel = jnp.where(halves[0], sel_o[0], sel_o[1])
    o_win = jnp.where(halves[0], win_o[0], win_o[1])
    g_hi, g_lo = _split_bf16(gate_ref[...])
    ex = ex_ref[...]
    gexp = _dot(g_hi, ex) + _dot(g_lo, ex)
    y = (gexp[:, 0:LANES] * ocmp_ref[...].astype(F32) + gexp[:, LANES:2 * LANES] * o_sel
         + gexp[:, 2 * LANES:3 * LANES] * o_win)
    o_ref[...] = y.astype(o_ref.dtype)


def _nsa_sw_kernel(q_ref, selm_ref, kvs_ref, kvw_ref, gate_ref, ex_ref, ocmp_ref, o_ref,
                   ksel_ref, vsel_ref, kwin_ref, vwin_ref):
    pr = pl.program_id(2)
    qi = pl.program_id(3)
    seq = kvs_ref.shape[0]

    @pl.when((pr == 0) & (qi == 0))
    def _():
        first, second = _head_halves(seq)
        onehot = _block_onehot(seq, int(math.log2(SLC_BLOCK)))
        kv = kvs_ref[...].astype(F32)
        swapped = pltpu.roll(kv, HEAD_DIM, axis=1)
        ksel_ref[0] = jnp.where(first, kv, onehot).astype(BF16)
        ksel_ref[1] = jnp.where(second, swapped, onehot).astype(BF16)
        vsel_ref[0] = _with_ones(swapped, first)
        vsel_ref[1] = _with_ones(kv, second)
        kv = kvw_ref[...].astype(F32)
        swapped = pltpu.roll(kv, HEAD_DIM, axis=1)
        kwin_ref[0] = kvw_ref[...]
        kwin_ref[1] = swapped.astype(BF16)
        vwin_ref[0] = _with_ones(swapped, first)
        vwin_ref[1] = _with_ones(kv, second)

    for c in range(seq // ATT_TILE):
        pl.when(qi == c)(functools.partial(_nsa_sw_tile, c, q_ref, selm_ref, gate_ref, ex_ref, ocmp_ref, o_ref,
                                           ksel_ref, vsel_ref, kwin_ref, vwin_ref))


def _nsa_attention(q_raw, q_rot, kv, ka, kb, gates, batch, seq):
    nq = seq // ATT_TILE
    g = NSA_GROUPS
    gw = NSA_HPG * HEAD_DIM
    n_pairs = NSA_HEADS // 2
    n_cmp = (seq - CMP_LEN) // CMP_STRIDE + 1
    n_cmp_pad = seq // CMP_STRIDE
    n_slc = seq // SLC_BLOCK
    assert n_cmp_pad == LANES and n_slc <= HEAD_DIM // 2 and WINDOW % ATT_TILE == 0

    ex = np.zeros((n_pairs, LANES, 3 * LANES), np.float32)
    for hd in range(NSA_HEADS):
        for br in range(3):
            lane0 = br * LANES + (hd % 2) * HEAD_DIM
            ex[hd // 2, hd * 3 + br, lane0:lane0 + HEAD_DIM] = 1.0
    c_s = np.arange(n_cmp) * CMP_STRIDE
    s_s = np.arange(n_slc) * SLC_BLOCK
    ov = np.clip(np.minimum(c_s[:, None] + CMP_LEN, s_s[None, :] + SLC_BLOCK)
                 - np.maximum(c_s[:, None], s_s[None, :]), 0, None) / CMP_LEN
    ovt = np.zeros((LANES, n_cmp_pad), np.float32)
    ovt[:n_slc, :n_cmp] = ov.T

    n_ct = seq // CMP_TILE
    o_cmp, selm = pl.pallas_call(
        functools.partial(_nsa_cmp_kernel, n_slc),
        name="nsa_cmp",
        grid=(batch, g, n_ct),
        in_specs=[pl.BlockSpec((CMP_TILE, gw), lambda b, gi, i: (b * n_ct + i, gi)),
                  pl.BlockSpec((n_cmp_pad, LANES), lambda b, gi, i: (b * g + gi, 0)),
                  pl.BlockSpec((n_cmp_pad, LANES), lambda b, gi, i: (b * g + gi, 0)),
                  pl.BlockSpec((LANES, n_cmp_pad), lambda b, gi, i: (0, 0))],
        out_specs=[pl.BlockSpec((CMP_TILE, gw), lambda b, gi, i: (b * n_ct + i, gi)),
                   pl.BlockSpec((CMP_TILE, LANES), lambda b, gi, i: (b * n_ct + i, gi))],
        out_shape=[jax.ShapeDtypeStruct((batch * seq, NSA_HEADS * HEAD_DIM), BF16),
                   jax.ShapeDtypeStruct((batch * seq, g * LANES), BF16)],
        compiler_params=_cparams("parallel", "parallel", "parallel"),
    )(q_raw, ka, kb, jnp.asarray(ovt, BF16))

    ppg = NSA_HPG // 2
    return pl.pallas_call(
        _nsa_sw_kernel,
        name="nsa_sw",
        grid=(batch, g, ppg, nq),
        in_specs=[pl.BlockSpec((ATT_TILE, LANES), lambda b, gi, p, i: (b * nq + i, gi * ppg + p)),
                  pl.BlockSpec((ATT_TILE, LANES), lambda b, gi, p, i: (b * nq + i, gi)),
                  pl.BlockSpec((seq, LANES), lambda b, gi, p, i: (b, g + gi)),
                  pl.BlockSpec((seq, LANES), lambda b, gi, p, i: (b, 2 * g + gi)),
                  pl.BlockSpec((ATT_TILE, LANES), lambda b, gi, p, i: (b * nq + i, 0)),
                  pl.BlockSpec((None, LANES, 3 * LANES), lambda b, gi, p, i: (gi * ppg + p, 0, 0)),
                  pl.BlockSpec((ATT_TILE, LANES), lambda b, gi, p, i: (b * nq + i, gi * ppg + p))],
        out_specs=pl.BlockSpec((ATT_TILE, LANES), lambda b, gi, p, i: (b * nq + i, gi * ppg + p)),
        out_shape=jax.ShapeDtypeStruct((batch * seq, NSA_HEADS * HEAD_DIM), BF16),
        scratch_shapes=[pltpu.VMEM((2, seq, LANES), BF16) for _ in range(4)],
        compiler_params=_cparams("parallel", "parallel", "arbitrary", "arbitrary"),
    )(q_rot, selm, kv, kv, gates, jnp.asarray(ex, BF16), o_cmp)


def _even_mixer(x, g, w_in, pool_w, pool_scale, w_out, rope_pair, batch, seq):
    chunk = 2 * LANES
    plan = [(0, POOL_DIM, [(0, 0, "none", 0)])]
    for c in range(3 * MOBA_DIM // chunk):
        op = "rope_scale" if c < MOBA_DIM // chunk else ("rope" if c < 2 * MOBA_DIM // chunk else "none")
        plan.append((POOL_DIM + c * chunk, chunk, [(1, c * chunk, op, 0)]))
    u_pool, qkv = _norm_project(x, g, w_in.astype(BF16), plan, [rope_pair],
                                [(POOL_DIM, F32), (3 * MOBA_DIM, BF16)], seq)
    w_bd = jax.scipy.linalg.block_diag(*[pool_w[i] for i in range(pool_w.shape[0])]).astype(BF16)
    y_a = _pool_mixer(u_pool, w_bd, pool_scale, batch, seq)
    y_b = _moba_attention(qkv, batch, seq)
    w_out = w_out.astype(BF16)
    return _out_proj(x, [y_a, y_b], [w_out[:POOL_DIM], w_out[POOL_DIM:]])


def _odd_mixer(x, g, w_in, pe_k, pe_v, k_w1, k_w2, v_w1, v_w2, w_out, rope_pair, rope_single, batch, seq):
    d = x.shape[1]
    qd = NSA_HEADS * HEAD_DIM
    kvd = NSA_KV_DIM
    wq = w_in[:, :qd]
    parts = [w_in[:, qd + i * kvd:qd + (i + 1) * kvd].reshape(d, NSA_GROUPS, HEAD_DIM) for i in range(6)]
    pairs = [jnp.concatenate([parts[2 * i], parts[2 * i + 1]], axis=-1).reshape(d, 2 * kvd) for i in range(3)]
    n_gate = 3 * NSA_HEADS
    w_gate = jnp.zeros((d, LANES), F32).at[:, :n_gate].set(w_in[:, qd + 6 * kvd:])
    w_all = jnp.concatenate([wq] + pairs + [w_gate], axis=1).astype(BF16)

    chunk = 2 * LANES
    plan = []
    for c in range(qd // chunk):
        plan.append((c * chunk, chunk, [(0, c * chunk, "scale", 0), (1, c * chunk, "rope_scale", 0)]))
    for c in range(6 * kvd // chunk):
        branch = c // (2 * kvd // chunk)
        op = "none" if branch == 0 else "rope"
        plan.append((qd + c * chunk, chunk, [(2, c * chunk, op, 1)]))
    plan.append((qd + 6 * kvd, LANES, [(3, 0, "sigmoid", 0)]))
    q_raw, q_rot, kv, gates = _norm_project(
        x, g, w_all, plan, [rope_pair, rope_single],
        [(qd, BF16), (qd, BF16), (6 * kvd, BF16), (LANES, F32)], seq)

    ka, kb = _compress(kv, pe_k, pe_v, k_w1, k_w2, v_w1, v_w2, batch, seq)
    y = _nsa_attention(q_raw, q_rot, kv, ka, kb, gates, batch, seq)
    return _out_proj(x, [y], [w_out.astype(BF16)])


def kernel(x, mem, positions, norm_g, mem_g, final_g, w_in_ab, pool_w, pool_scale, w_out_ab, ffn_w_gate, ffn_w_up, ffn_w_down, w_in_c, cmp_pe_k, cmp_pe_v, cmp_k_w1, cmp_k_w2, cmp_v_w1, cmp_v_w2, w_out_c, router_w, moe_w_gate, moe_w_up, moe_w_down, xa_wq, xa_wkv, xa_wo):
    batch, seq, d = x.shape
    mem_len = mem.shape[1]
    depth = norm_g.shape[0]
    rope_pair, rope_single = _rope_tables(positions)

    wkv_all = jnp.concatenate([xa_wkv[l] for l in range(depth)], axis=1).astype(BF16)
    kv_all = _norm_matmul(mem.reshape(batch * mem_len, d), mem_g, wkv_all)

    xf = x.reshape(batch * seq, d)
    for layer in range(depth):
        i = layer // 2
        if layer % 2 == 0:
            xf = _even_mixer(xf, norm_g[layer, 0], w_in_ab[i], pool_w[i], pool_scale[i], w_out_ab[i],
                             rope_pair, batch, seq)
        else:
            xf = _odd_mixer(xf, norm_g[layer, 0], w_in_c[i], cmp_pe_k[i], cmp_pe_v[i], cmp_k_w1[i], cmp_k_w2[i],
                            cmp_v_w1[i], cmp_v_w2[i], w_out_c[i], rope_pair, rope_single, batch, seq)
        xf = _cross_attention(xf, norm_g[layer, 1], xa_wq[layer].astype(BF16), kv_all, layer,
                              xa_wo[layer].astype(BF16), seq, mem_len)
        if layer % 2 == 0:
            xf = _ffn(xf, norm_g[layer, 2], ffn_w_gate[i].astype(BF16), ffn_w_up[i].astype(BF16),
                      ffn_w_down[i].astype(BF16))
        else:
            xf = _moe(xf, norm_g[layer, 2], router_w[i], moe_w_gate, moe_w_up, moe_w_down, i)
    return _final_norm(xf, final_g).reshape(batch, seq, d)
```

```python
import functools
import math

import numpy as np
import jax
import jax.numpy as jnp
from jax import lax
from jax.experimental import pallas as pl
from jax.experimental.pallas import tpu as pltpu

F32 = jnp.float32
BF16 = jnp.bfloat16

HEAD_DIM = 64
ROPE_DIM = 16
ROPE_THETA = 500000.0
NORM_EPS = 1e-5
QK_SCALE = HEAD_DIM ** -0.5
QK_SCALE_LOG2E = QK_SCALE * math.log2(math.e)

POOL_WINDOWS = (2, 4, 8, 16)
POOL_DIM = 256
MOBA_HEADS = 12
MOBA_DIM = MOBA_HEADS * HEAD_DIM
MOBA_BLOCK = 256
MOBA_TOPK = 3
NSA_HEADS = 16
NSA_GROUPS = 4
NSA_HPG = 4
NSA_KV_DIM = NSA_GROUPS * HEAD_DIM
CMP_LEN = 32
CMP_STRIDE = 16
SLC_BLOCK = 64
SLC_TOPN = 16
WINDOW = 512
FORCE_SCORE = 1e4
XA_HEADS = 4
XA_DIM = XA_HEADS * HEAD_DIM
N_EXPERTS = 8

LANES = 128
ATT_TILE = 256
CMP_TILE = 1024
VMEM_LIMIT = 56 * 1024 * 1024
NEG_INF = float("-inf")
MASK_NEG = -1e30


def _cparams(*sem):
    return pltpu.CompilerParams(dimension_semantics=sem, vmem_limit_bytes=VMEM_LIMIT)


def _dot(a, b):
    return jnp.dot(a, b, preferred_element_type=F32)


def _dot_nt(a, b):
    return lax.dot_general(a, b, (((1,), (1,)), ((), ())), preferred_element_type=F32)


def _split_bf16(a):
    hi = a.astype(BF16)
    lo = (a - hi.astype(F32)).astype(BF16)
    return hi, lo


def _rms_bf16(x, g):
    ms = jnp.mean(x * x, axis=-1, keepdims=True)
    return (x * lax.rsqrt(ms + NORM_EPS) * g).astype(BF16)


def _apply_rope(acc, c, a, b):
    half = ROPE_DIM // 2
    return acc * c + pltpu.roll(acc, LANES - half, axis=1) * a + pltpu.roll(acc, half, axis=1) * b


def _proj_kernel(plan, n_tab, x_ref, g_ref, w_ref, *rest):
    tabs = rest[:3 * n_tab]
    outs = rest[3 * n_tab:]
    h = _rms_bf16(x_ref[...], g_ref[...])
    for wc0, width, sinks in plan:
        acc = _dot(h, w_ref[:, wc0:wc0 + width])
        for oi, oc0, op, tab in sinks:
            for s in range(width // LANES):
                val = acc[:, s * LANES:(s + 1) * LANES]
                if op in ("rope", "rope_scale"):
                    c, a, b = (tabs[3 * tab + i][...] for i in range(3))
                    val = _apply_rope(val, c, a, b)
                if op == "scale":
                    val = val * QK_SCALE
                if op == "rope_scale":
                    val = val * QK_SCALE_LOG2E
                if op == "sigmoid":
                    val = jax.nn.sigmoid(val)
                o = outs[oi]
                o[:, oc0 + s * LANES:oc0 + (s + 1) * LANES] = val.astype(o.dtype)


def _norm_project(x, g, w, plan, tables, out_defs, seq, tm=512):
    t, d = x.shape
    n = w.shape[1]
    n_seq_tiles = seq // tm
    in_specs = [
        pl.BlockSpec((tm, d), lambda i: (i, 0)),
        pl.BlockSpec((1, d), lambda i: (0, 0)),
        pl.BlockSpec((d, n), lambda i: (0, 0)),
    ]
    flat_tabs = []
    for tset in tables:
        for tb in tset:
            flat_tabs.append(tb)
            in_specs.append(pl.BlockSpec((tm, LANES), lambda i: (i % n_seq_tiles, 0)))
    out_shape = [jax.ShapeDtypeStruct((t, wd), dt) for wd, dt in out_defs]
    out_specs = [pl.BlockSpec((tm, wd), lambda i: (i, 0)) for wd, _ in out_defs]
    return pl.pallas_call(
        functools.partial(_proj_kernel, plan, len(tables)),
        name="proj",
        grid=(t // tm,),
        in_specs=in_specs,
        out_specs=out_specs,
        out_shape=out_shape,
        compiler_params=_cparams("parallel"),
    )(x, g.reshape(1, d), w, *flat_tabs)


def _rope_tables(positions):
    half = ROPE_DIM // 2
    inv = ROPE_THETA ** (-jnp.arange(0, ROPE_DIM, 2, dtype=F32) / ROPE_DIM)
    ang = positions.astype(F32)[:, None] * inv[None, :]
    cos, sin = jnp.cos(ang), jnp.sin(ang)
    s = positions.shape[0]
    z8 = jnp.zeros((s, half), F32)
    rest0 = jnp.zeros((s, HEAD_DIM - ROPE_DIM), F32)
    rest1 = jnp.ones((s, HEAD_DIM - ROPE_DIM), F32)
    c64 = jnp.concatenate([cos, cos, rest1], -1)
    a64 = jnp.concatenate([-sin, z8, rest0], -1)
    b64 = jnp.concatenate([z8, sin, rest0], -1)
    one64 = jnp.ones((s, HEAD_DIM), F32)
    zero64 = jnp.zeros((s, HEAD_DIM), F32)
    pair = tuple(jnp.concatenate([m, m], -1) for m in (c64, a64, b64))
    single = (jnp.concatenate([c64, one64], -1), jnp.concatenate([a64, zero64], -1),
              jnp.concatenate([b64, zero64], -1))
    return pair, single


def _norm_matmul_kernel(x_ref, g_ref, w_ref, o_ref):
    h = _rms_bf16(x_ref[...], g_ref[...])
    o_ref[...] = _dot(h, w_ref[...]).astype(o_ref.dtype)


def _norm_matmul(x, g, w, tm=512):
    t, d = x.shape
    n = w.shape[1]
    return pl.pallas_call(
        _norm_matmul_kernel,
        name="mem_kv",
        grid=(t // tm,),
        in_specs=[pl.BlockSpec((tm, d), lambda i: (i, 0)),
                  pl.BlockSpec((1, d), lambda i: (0, 0)),
                  pl.BlockSpec((d, n), lambda i: (0, 0))],
        out_specs=pl.BlockSpec((tm, n), lambda i: (i, 0)),
        out_shape=jax.ShapeDtypeStruct((t, n), BF16),
        compiler_params=_cparams("parallel"),
    )(x, g.reshape(1, d), w)


def _final_norm_kernel(x_ref, g_ref, o_ref):
    x = x_ref[...]
    ms = jnp.mean(x * x, axis=-1, keepdims=True)
    o_ref[...] = x * lax.rsqrt(ms + NORM_EPS) * g_ref[...]


def _final_norm(x, g, tm=1024):
    t, d = x.shape
    return pl.pallas_call(
        _final_norm_kernel,
        name="final_norm",
        grid=(t // tm,),
        in_specs=[pl.BlockSpec((tm, d), lambda i: (i, 0)), pl.BlockSpec((1, d), lambda i: (0, 0))],
        out_specs=pl.BlockSpec((tm, d), lambda i: (i, 0)),
        out_shape=jax.ShapeDtypeStruct((t, d), F32),
        compiler_params=_cparams("parallel"),
    )(x, g.reshape(1, d))


def _out_proj_kernel(n_in, x_ref, *rest):
    ys = rest[:n_in]
    ws = rest[n_in:2 * n_in]
    o_ref = rest[2 * n_in]
    acc = x_ref[...]
    for y, w in zip(ys, ws):
        acc = acc + _dot(y[...], w[...])
    o_ref[...] = acc


def _out_proj(x, ys, ws, tm=512):
    t, d = x.shape
    in_specs = [pl.BlockSpec((tm, d), lambda i: (i, 0))]
    in_specs += [pl.BlockSpec((tm, y.shape[1]), lambda i: (i, 0)) for y in ys]
    in_specs += [pl.BlockSpec(w.shape, lambda i: (0, 0)) for w in ws]
    return pl.pallas_call(
        functools.partial(_out_proj_kernel, len(ys)),
        name="out_proj",
        grid=(t // tm,),
        in_specs=in_specs,
        out_specs=pl.BlockSpec((tm, d), lambda i: (i, 0)),
        out_shape=jax.ShapeDtypeStruct((t, d), F32),
        compiler_params=_cparams("parallel"),
    )(x, *ys, *ws)


def _pool_kernel(u_ref, w_ref, sc_ref, o_ref, pad_ref):
    s = u_ref.shape[0]
    maxw = POOL_WINDOWS[-1]
    u = u_ref[...]
    pad_ref[0:maxw, :] = jnp.zeros((maxw, POOL_DIM), F32)
    t1 = (lax.broadcasted_iota(jnp.int32, (s, POOL_DIM), 0) + 1).astype(F32)
    lane = lax.broadcasted_iota(jnp.int32, (s, POOL_DIM), 1)
    gdim = POOL_DIM // len(POOL_WINDOWS)
    cur = u
    pooled = jnp.zeros_like(u)
    shift = 1
    for gi, w in enumerate(POOL_WINDOWS):
        pad_ref[maxw:maxw + s, :] = cur
        cur = cur + pad_ref[maxw - shift:maxw - shift + s, :]
        shift *= 2
        assert shift == w
        mean = cur / jnp.minimum(t1, float(w))
        pooled = jnp.where((lane >= gi * gdim) & (lane < (gi + 1) * gdim), mean, pooled)
    pooled = (pooled - u).astype(BF16)
    o_ref[...] = (_dot(pooled, w_ref[...]) * sc_ref[...]).astype(o_ref.dtype)


def _pool_mixer(u_pool, w_bd, scale, batch, seq):
    return pl.pallas_call(
        _pool_kernel,
        name="pool",
        grid=(batch,),
        in_specs=[pl.BlockSpec((seq, POOL_DIM), lambda b: (b, 0)),
                  pl.BlockSpec((POOL_DIM, POOL_DIM), lambda b: (0, 0)),
                  pl.BlockSpec((1, POOL_DIM), lambda b: (0, 0))],
        out_specs=pl.BlockSpec((seq, POOL_DIM), lambda b: (b, 0)),
        out_shape=jax.ShapeDtypeStruct((batch * seq, POOL_DIM), BF16),
        scratch_shapes=[pltpu.VMEM((seq + POOL_WINDOWS[-1], POOL_DIM), F32)],
        compiler_params=_cparams("parallel"),
    )(u_pool, w_bd, scale.reshape(1, POOL_DIM))


def _rank_select(scoreT, valid, n_rows, topn):
    row = lax.broadcasted_iota(jnp.int32, scoreT.shape, 0)
    rank = jnp.zeros(scoreT.shape, F32)
    for j in range(n_rows):
        sj = scoreT[j:j + 1, :]
        beats = (scoreT > sj) | ((scoreT == sj) & (row < j))
        rj = jnp.sum(beats.astype(F32), axis=0, keepdims=True)
        rank = jnp.where(row == j, rj, rank)
    return (rank < float(topn)) & valid


def _head_halves(rows):
    lane = lax.broadcasted_iota(jnp.int32, (rows, LANES), 1)
    return lane < HEAD_DIM, lane >= HEAD_DIM


def _tile_masks(tq):
    qry = lax.broadcasted_iota(jnp.int32, (tq, tq), 0)
    key = lax.broadcasted_iota(jnp.int32, (tq, tq), 1)
    return key <= qry, key > qry


def _mask_tile(s, t0, mask):
    tq = mask.shape[0]
    parts = []
    if t0 > 0:
        parts.append(s[:, :t0])
    parts.append(jnp.where(mask, s[:, t0:t0 + tq], NEG_INF))
    if t0 + tq < s.shape[1]:
        parts.append(s[:, t0 + tq:])
    return parts[0] if len(parts) == 1 else jnp.concatenate(parts, axis=1)


def _softmax_pv(s, v, h):
    m = jnp.max(s, axis=1, keepdims=True)
    p = jnp.exp2((s - m).astype(BF16))
    o = _dot(p, v)
    denom_lane = (1 - h) * HEAD_DIM
    return o / o[:, denom_lane:denom_lane + 1]


def _with_ones(v, own):
    return jnp.where(own, v, 1.0).astype(BF16)


def _block_onehot(seq, shift):
    lane = lax.broadcasted_iota(jnp.int32, (seq, LANES), 1)
    blk = jnp.right_shift(lax.broadcasted_iota(jnp.int32, (seq, LANES), 0), shift)
    return jnp.where((lane & (HEAD_DIM - 1)) == blk, 1.0, 0.0)


def _moba_tiles(tiles, o_ref, qaug_ref, kaug_ref, vaug_ref):
    tq = ATT_TILE
    halves = _head_halves(tq)
    causal, _ = _tile_masks(tq)
    scores = {}
    for c in tiles:
        for h in range(2):
            s = _dot_nt(qaug_ref[h, c * tq:(c + 1) * tq, :], kaug_ref[h, 0:(c + 1) * tq, :])
            scores[c, h] = _mask_tile(s, c * tq, causal)
    for c in tiles:
        outs = [_softmax_pv(scores[c, h], vaug_ref[h, 0:(c + 1) * tq, :], h) for h in range(2)]
        o_ref[c * tq:(c + 1) * tq, :] = jnp.where(halves[0], outs[0], outs[1]).astype(o_ref.dtype)


def _moba_kernel(q_ref, k_ref, v_ref, o_ref, qaug_ref, kaug_ref, vaug_ref, kmean_ref):
    qi = pl.program_id(2)
    seq = k_ref.shape[0]
    nb = seq // MOBA_BLOCK
    blk_shift = int(math.log2(MOBA_BLOCK))

    @pl.when(qi == 0)
    def _():
        halves = _head_halves(seq)
        kf = k_ref[...].astype(F32)
        vf = v_ref[...].astype(F32)
        qp = q_ref[...]
        qf = qp.astype(F32)
        onehot = _block_onehot(seq, blk_shift)
        kmean_ref[...] = jnp.zeros(kmean_ref.shape, F32)
        for j in range(nb):
            kmean_ref[j:j + 1, :] = jnp.mean(kf[j * MOBA_BLOCK:(j + 1) * MOBA_BLOCK], axis=0, keepdims=True)
        cand = lax.broadcasted_iota(jnp.int32, (nb, seq), 0)
        own = jnp.right_shift(lax.broadcasted_iota(jnp.int32, (nb, seq), 1), blk_shift)
        valid = cand < own
        for h in range(2):
            kaug_ref[h] = jnp.where(halves[h], kf, onehot).astype(BF16)
            vaug_ref[h] = _with_ones(vf, halves[h])
            mine_k = _head_halves(kmean_ref.shape[0])[h]
            km_hi, km_lo = _split_bf16(jnp.where(mine_k, kmean_ref[...], 0.0))
            gate = (_dot_nt(km_hi, qp) + _dot_nt(km_lo, qp))[0:nb]
            gate = jnp.where(valid, gate, NEG_INF)
            keep = _rank_select(gate, valid, nb, MOBA_TOPK) | (cand == own)
            neg = jnp.where(keep, 0.0, MASK_NEG)
            lead = HEAD_DIM if h == 0 else 0
            pieces = [neg, jnp.zeros((LANES - lead - nb, seq), F32)]
            if lead:
                pieces.insert(0, jnp.zeros((lead, seq), F32))
            maskcols = jnp.concatenate(pieces, axis=0).T
            qaug_ref[h] = jnp.where(halves[h], qf, maskcols).astype(BF16)

    for j in range(nb // 2):
        pl.when(qi == j)(functools.partial(_moba_tiles, (nb - 1 - j, j), o_ref, qaug_ref, kaug_ref, vaug_ref))


def _moba_attention(qkv, batch, seq):
    nq = seq // ATT_TILE
    assert MOBA_BLOCK == ATT_TILE and nq <= 16 and nq % 2 == 0
    hp = MOBA_DIM // LANES
    return pl.pallas_call(
        _moba_kernel,
        name="moba",
        grid=(batch, hp, nq // 2),
        in_specs=[pl.BlockSpec((seq, LANES), lambda b, p, i: (b, p)),
                  pl.BlockSpec((seq, LANES), lambda b, p, i: (b, hp + p)),
                  pl.BlockSpec((seq, LANES), lambda b, p, i: (b, 2 * hp + p))],
        out_specs=pl.BlockSpec((seq, LANES), lambda b, p, i: (b, p)),
        out_shape=jax.ShapeDtypeStruct((batch * seq, MOBA_DIM), BF16),
        scratch_shapes=[pltpu.VMEM((2, seq, LANES), BF16), pltpu.VMEM((2, seq, LANES), BF16),
                        pltpu.VMEM((2, seq, LANES), BF16), pltpu.VMEM((16, LANES), F32)],
        compiler_params=_cparams("parallel", "parallel", "arbitrary"),
    )(qkv, qkv, qkv)


def _xattn_kernel(x_ref, g_ref, wq_ref, kv_ref, wo_ref, o_ref):
    x = x_ref[...]
    h = _rms_bf16(x, g_ref[...])
    q = (_dot(h, wq_ref[...]) * QK_SCALE).astype(BF16)
    out = x
    for hd in range(XA_HEADS):
        sl = slice(hd * HEAD_DIM, (hd + 1) * HEAD_DIM)
        k = kv_ref[:, sl]
        v = kv_ref[:, XA_DIM + hd * HEAD_DIM:XA_DIM + (hd + 1) * HEAD_DIM]
        s = _dot_nt(q[:, sl], k)
        m = jnp.max(s, axis=1, keepdims=True)
        p = jnp.exp(s - m)
        l = jnp.sum(p, axis=1, keepdims=True)
        o = _dot(p.astype(BF16), v) / l
        out = out + _dot(o.astype(BF16), wo_ref[sl, :])
    o_ref[...] = out


def _cross_attention(x, g, wq, kv_all, layer, wo, seq, mem_len, tm=512):
    t, d = x.shape
    tiles_per_seq = seq // tm
    return pl.pallas_call(
        _xattn_kernel,
        name="xattn",
        grid=(t // tm,),
        in_specs=[pl.BlockSpec((tm, d), lambda i: (i, 0)),
                  pl.BlockSpec((1, d), lambda i: (0, 0)),
                  pl.BlockSpec((d, XA_DIM), lambda i: (0, 0)),
                  pl.BlockSpec((mem_len, 2 * XA_DIM), lambda i: (i // tiles_per_seq, layer)),
                  pl.BlockSpec((XA_DIM, d), lambda i: (0, 0))],
        out_specs=pl.BlockSpec((tm, d), lambda i: (i, 0)),
        out_shape=jax.ShapeDtypeStruct((t, d), F32),
        compiler_params=_cparams("parallel"),
    )(x, g.reshape(1, d), wq, kv_all, wo)


def _ffn_kernel(x_ref, g_ref, wg_ref, wu_ref, wd_ref, o_ref, h_ref):
    f = pl.program_id(1)

    @pl.when(f == 0)
    def _():
        x = x_ref[...]
        h_ref[...] = _rms_bf16(x, g_ref[...])
        o_ref[...] = x

    h = h_ref[...]
    gate = _dot(h, wg_ref[...])
    a = (gate * jax.nn.sigmoid(gate)) * _dot(h, wu_ref[...])
    o_ref[...] += _dot(a.astype(BF16), wd_ref[...])


def _ffn(x, g, wg, wu, wd, tm=1024, tf=256):
    t, d = x.shape
    ff = wg.shape[1]
    return pl.pallas_call(
        _ffn_kernel,
        name="ffn",
        grid=(t // tm, ff // tf),
        in_specs=[pl.BlockSpec((tm, d), lambda i, f: (i, 0)),
                  pl.BlockSpec((1, d), lambda i, f: (0, 0)),
                  pl.BlockSpec((d, tf), lambda i, f: (0, f)),
                  pl.BlockSpec((d, tf), lambda i, f: (0, f)),
                  pl.BlockSpec((tf, d), lambda i, f: (f, 0))],
        out_specs=pl.BlockSpec((tm, d), lambda i, f: (i, 0)),
        out_shape=jax.ShapeDtypeStruct((t, d), F32),
        scratch_shapes=[pltpu.VMEM((tm, d), BF16)],
        compiler_params=_cparams("parallel", "arbitrary"),
    )(x, g.reshape(1, d), wg, wu, wd)


ROUTE_I1, ROUTE_I2, ROUTE_W1, ROUTE_W2, ROUTE_R1, ROUTE_R2 = range(6)
MOE_TILE = 1024


def _lane_pick(arr, lane, k):
    return jnp.sum(jnp.where(lane == k, arr, 0.0), axis=1, keepdims=True)


def _router_kernel(x_ref, g_ref, rhi_ref, rlo_ref, route_ref, cnt_ref):
    i = pl.program_id(0)
    tm = x_ref.shape[0]
    lane = lax.broadcasted_iota(jnp.int32, (tm, LANES), 1)
    lane_f = lane.astype(F32)

    @pl.when(i == 0)
    def _():
        cnt_ref[...] = jnp.zeros(cnt_ref.shape, F32)

    h = _rms_bf16(x_ref[...], g_ref[...])
    logits = _dot(h, rhi_ref[...]) + _dot(h, rlo_ref[...])
    logits = jnp.where(lane < N_EXPERTS, logits, NEG_INF)
    m1 = jnp.max(logits, axis=1, keepdims=True)
    i1 = jnp.min(jnp.where(logits == m1, lane_f, float(LANES)), axis=1, keepdims=True)
    rest = jnp.where(lane_f == i1, NEG_INF, logits)
    m2 = jnp.max(rest, axis=1, keepdims=True)
    i2 = jnp.min(jnp.where(rest == m2, lane_f, float(LANES)), axis=1, keepdims=True)
    e2 = jnp.exp(m2 - m1)
    w1 = 1.0 / (1.0 + e2)
    w2 = e2 / (1.0 + e2)

    chosen = jnp.where((lane_f == i1) | (lane_f == i2), 1.0, 0.0)
    r = lax.broadcasted_iota(jnp.int32, (tm, tm), 0)
    c = lax.broadcasted_iota(jnp.int32, (tm, tm), 1)
    earlier = jnp.where(c < r, 1.0, 0.0).astype(BF16)
    prefix = _dot(earlier, chosen.astype(BF16)) + cnt_ref[0:1, :]
    rank1 = jnp.sum(jnp.where(lane_f == i1, prefix, 0.0), axis=1, keepdims=True)
    rank2 = jnp.sum(jnp.where(lane_f == i2, prefix, 0.0), axis=1, keepdims=True)
    cnt_ref[0:1, :] = cnt_ref[0:1, :] + jnp.sum(chosen, axis=0, keepdims=True)

    cols = {ROUTE_I1: i1, ROUTE_I2: i2, ROUTE_W1: w1, ROUTE_W2: w2, ROUTE_R1: rank1, ROUTE_R2: rank2}
    route = jnp.zeros((tm, LANES), F32)
    for k, val in cols.items():
        route = jnp.where(lane == k, val, route)
    route_ref[...] = route


def _dispatch_kernel(pos1_ref, pos2_ref, x_ref, xs_in, xs_hbm, sem):
    del xs_in
    n = pos1_ref.shape[0]

    def row_copy(r, p):
        return pltpu.make_async_copy(x_ref.at[pl.ds(r, 1)], xs_hbm.at[pl.ds(p, 1)], sem)

    def issue(r, carry):
        row_copy(r, pos1_ref[r]).start()
        row_copy(r, pos2_ref[r]).start()
        return carry

    lax.fori_loop(0, n, issue, 0, unroll=8)
    for _ in range(2):
        pltpu.make_async_copy(x_ref, xs_hbm.at[pl.ds(0, n)], sem).wait()


def _expert_kernel(te_ref, nu_ref, xs_ref, g_ref, wg_ref, wu_ref, wd_ref, y_ref, h_ref):
    del te_ref
    i = pl.program_id(0)
    f = pl.program_id(1)
    used = i < nu_ref[0]

    @pl.when(f == 0)
    def _():
        h_ref[...] = _rms_bf16(xs_ref[...], g_ref[...])
        y_ref[...] = jnp.zeros(y_ref.shape, y_ref.dtype)

    @pl.when(used)
    def _():
        h = h_ref[...]
        gate = _dot(h, wg_ref[...].astype(BF16))
        a = (gate * jax.nn.sigmoid(gate)) * _dot(h, wu_ref[...].astype(BF16))
        y_ref[...] += _dot(a.astype(BF16), wd_ref[...].astype(BF16))


def _combine_kernel(pos1_ref, pos2_ref, x_ref, route_ref, y_hbm, o_ref, buf1, buf2, sem):
    n = pos1_ref.shape[0]

    def row_copy(p, buf, r):
        return pltpu.make_async_copy(y_hbm.at[pl.ds(p, 1)], buf.at[pl.ds(r, 1)], sem)

    def issue(r, carry):
        row_copy(pos1_ref[r], buf1, r).start()
        row_copy(pos2_ref[r], buf2, r).start()
        return carry

    lax.fori_loop(0, n, issue, 0, unroll=8)
    for buf in (buf1, buf2):
        pltpu.make_async_copy(y_hbm.at[pl.ds(0, n)], buf, sem).wait()
    lane = lax.broadcasted_iota(jnp.int32, route_ref.shape, 1)
    route = route_ref[...]
    w1 = _lane_pick(route, lane, ROUTE_W1)
    w2 = _lane_pick(route, lane, ROUTE_W2)
    o_ref[...] = x_ref[...] + w1 * buf1[...] + w2 * buf2[...]


def _moe(x, g, router_w, wg, wu, wd, li, tf=512, t_route=512, t_disp=512, t_comb=256):
    t, d = x.shape
    _, ne, _, ff = wg.shape
    rows = MOE_TILE
    n_rows = 2 * t + ne * rows
    n_tiles = n_rows // rows
    nf = ff // tf
    g2 = g.reshape(1, d)
    rw = jnp.zeros((d, LANES), F32).at[:, :ne].set(router_w)
    rhi = rw.astype(BF16)
    rlo = (rw - rhi.astype(F32)).astype(BF16)

    route, cnt = pl.pallas_call(
        _router_kernel,
        name="moe_route",
        grid=(t // t_route,),
        in_specs=[pl.BlockSpec((t_route, d), lambda i: (i, 0)),
                  pl.BlockSpec((1, d), lambda i: (0, 0)),
                  pl.BlockSpec((d, LANES), lambda i: (0, 0)),
                  pl.BlockSpec((d, LANES), lambda i: (0, 0))],
        out_specs=[pl.BlockSpec((t_route, LANES), lambda i: (i, 0)),
                   pl.BlockSpec((8, LANES), lambda i: (0, 0))],
        out_shape=[jax.ShapeDtypeStruct((t, LANES), F32), jax.ShapeDtypeStruct((8, LANES), F32)],
        compiler_params=_cparams("arbitrary"),
    )(x, g2, rhi, rlo)

    counts = cnt[0, :ne].astype(jnp.int32)
    gsz = (counts + rows - 1) // rows * rows
    ends = jnp.cumsum(gsz)
    offs = ends - gsz
    i1 = route[:, ROUTE_I1].astype(jnp.int32)
    i2 = route[:, ROUTE_I2].astype(jnp.int32)
    pos1 = offs[i1] + route[:, ROUTE_R1].astype(jnp.int32)
    pos2 = offs[i2] + route[:, ROUTE_R2].astype(jnp.int32)
    tile_start = jnp.arange(n_tiles, dtype=jnp.int32) * rows
    tile_expert = jnp.minimum(jnp.sum(tile_start[:, None] >= ends[None, :], axis=1), ne - 1).astype(jnp.int32)
    n_used = (ends[-1] // rows).astype(jnp.int32).reshape(1)

    smem_idx = lambda n: pl.BlockSpec((n,), lambda i: (i,), memory_space=pltpu.SMEM)
    hbm = pl.BlockSpec(memory_space=pl.ANY)
    xs = pl.pallas_call(
        _dispatch_kernel,
        name="moe_dispatch",
        grid=(t // t_disp,),
        in_specs=[smem_idx(t_disp), smem_idx(t_disp), pl.BlockSpec((t_disp, d), lambda i: (i, 0)), hbm],
        out_specs=hbm,
        out_shape=jax.ShapeDtypeStruct((n_rows, d), F32),
        scratch_shapes=[pltpu.SemaphoreType.DMA(())],
        input_output_aliases={3: 0},
        compiler_params=_cparams("arbitrary"),
    )(pos1, pos2, x, jnp.zeros((n_rows, d), F32))

    def w_in_map(i, f, te, nu):
        return li, te[i], 0, jnp.where(i < nu[0], f, nf - 1)

    def w_out_map(i, f, te, nu):
        return li, te[i], jnp.where(i < nu[0], f, nf - 1), 0

    y = pl.pallas_call(
        _expert_kernel,
        name="moe_experts",
        grid_spec=pltpu.PrefetchScalarGridSpec(
            num_scalar_prefetch=2,
            grid=(n_tiles, nf),
            in_specs=[pl.BlockSpec((rows, d), lambda i, f, te, nu: (i, 0)),
                      pl.BlockSpec((1, d), lambda i, f, te, nu: (0, 0)),
                      pl.BlockSpec((None, None, d, tf), w_in_map),
                      pl.BlockSpec((None, None, d, tf), w_in_map),
                      pl.BlockSpec((None, None, tf, d), w_out_map)],
            out_specs=pl.BlockSpec((rows, d), lambda i, f, te, nu: (i, 0)),
            scratch_shapes=[pltpu.VMEM((rows, d), BF16)]),
        out_shape=jax.ShapeDtypeStruct((n_rows, d), F32),
        compiler_params=_cparams("arbitrary", "arbitrary"),
    )(tile_expert, n_used, xs, g2, wg, wu, wd)

    return pl.pallas_call(
        _combine_kernel,
        name="moe_combine",
        grid=(t // t_comb,),
        in_specs=[smem_idx(t_comb), smem_idx(t_comb),
                  pl.BlockSpec((t_comb, d), lambda i: (i, 0)),
                  pl.BlockSpec((t_comb, LANES), lambda i: (i, 0)),
                  hbm],
        out_specs=pl.BlockSpec((t_comb, d), lambda i: (i, 0)),
        out_shape=jax.ShapeDtypeStruct((t, d), F32),
        scratch_shapes=[pltpu.VMEM((t_comb, d), F32), pltpu.VMEM((t_comb, d), F32),
                        pltpu.SemaphoreType.DMA(())],
        compiler_params=_cparams("arbitrary"),
    )(pos1, pos2, x, route, y)


def _gelu_tanh(x):
    return 0.5 * x * (1.0 + jnp.tanh(math.sqrt(2.0 / math.pi) * (x + 0.044715 * (x * x * x))))


def _compress_kernel(kv_ref, pe_ref, w1_ref, w2a_ref, w2b_ref, ka_ref, kb_ref, xf_ref, sh_ref):
    seq = kv_ref.shape[0]
    n = seq // CMP_STRIDE
    hid2 = w1_ref.shape[2]
    xf_ref[...] = kv_ref[...].astype(F32)
    top = jnp.zeros((n, hid2), F32)
    bot = jnp.zeros((n, hid2), F32)
    for i in range(CMP_STRIDE):
        xi = xf_ref[pl.ds(i, n, stride=CMP_STRIDE), :]
        top = top + _dot((xi + pe_ref[i:i + 1, :]).astype(BF16), w1_ref[i])
        bot = bot + _dot((xi + pe_ref[CMP_STRIDE + i:CMP_STRIDE + i + 1, :]).astype(BF16), w1_ref[CMP_STRIDE + i])
    sh_ref[0:n, :] = bot
    sh_ref[n:n + 8, :] = jnp.zeros((8, hid2), F32)
    act = _gelu_tanh(top + sh_ref[1:n + 1, :]).astype(BF16)
    ka_ref[...] = _dot(act, w2a_ref[...]).astype(ka_ref.dtype)
    kb_ref[...] = _dot(act, w2b_ref[...]).astype(kb_ref.dtype)


def _compress(kv, pe_k, pe_v, k_w1, k_w2, v_w1, v_w2, batch, seq):
    assert CMP_LEN == 2 * CMP_STRIDE
    hid = k_w1.shape[1]
    n = seq // CMP_STRIDE
    w1 = jnp.zeros((CMP_LEN, LANES, 2 * hid), F32)
    w1 = w1.at[:, :HEAD_DIM, :hid].set(k_w1.reshape(CMP_LEN, HEAD_DIM, hid))
    w1 = w1.at[:, HEAD_DIM:, hid:].set(v_w1.reshape(CMP_LEN, HEAD_DIM, hid)).astype(BF16)
    zero = jnp.zeros((hid, HEAD_DIM), F32)
    w2a = jnp.block([[k_w2, zero], [zero, v_w2]]).astype(BF16)
    w2b = jnp.block([[zero, k_w2], [v_w2, zero]]).astype(BF16)
    pe = jnp.concatenate([pe_k, pe_v], axis=-1)
    g = NSA_GROUPS
    out = jax.ShapeDtypeStruct((batch * g * n, LANES), BF16)
    return pl.pallas_call(
        _compress_kernel,
        name="compress",
        grid=(batch, g),
        in_specs=[pl.BlockSpec((seq, LANES), lambda b, gi: (b, gi)),
                  pl.BlockSpec((CMP_LEN, LANES), lambda b, gi: (0, 0)),
                  pl.BlockSpec((CMP_LEN, LANES, 2 * hid), lambda b, gi: (0, 0, 0)),
                  pl.BlockSpec((2 * hid, LANES), lambda b, gi: (0, 0)),
                  pl.BlockSpec((2 * hid, LANES), lambda b, gi: (0, 0))],
        out_specs=[pl.BlockSpec((n, LANES), lambda b, gi: (b * g + gi, 0)),
                   pl.BlockSpec((n, LANES), lambda b, gi: (b * g + gi, 0))],
        out_shape=[out, out],
        scratch_shapes=[pltpu.VMEM((seq, LANES), F32), pltpu.VMEM((n + 8, 2 * hid), F32)],
        compiler_params=_cparams("parallel", "parallel"),
    )(kv, pe, w1, w2a, w2b)


def _nsa_cmp_kernel(n_slc, q_ref, ka_ref, kb_ref, ovt_ref, ocmp_ref, selm_ref):
    qi = pl.program_id(2)
    tq = q_ref.shape[0]
    n_pad = ka_ref.shape[0]
    t_q = qi * tq + lax.broadcasted_iota(jnp.int32, (tq, n_pad), 0)
    n_c = lax.broadcasted_iota(jnp.int32, (tq, n_pad), 1)
    mask_c = (n_c * CMP_STRIDE + (CMP_LEN - 1)) <= t_q
    halves = _head_halves(tq)
    ka = ka_ref[...]
    kb = kb_ref[...]
    psum = jnp.zeros((tq, n_pad), F32)
    for pr in range(NSA_HPG // 2):
        qf = q_ref[:, pr * LANES:(pr + 1) * LANES].astype(F32)
        outs = []
        for h in range(2):
            qz = jnp.where(halves[h], qf, 0.0).astype(BF16)
            kmat, vmat = (ka, kb) if h == 0 else (kb, ka)
            s = jnp.where(mask_c, _dot_nt(qz, kmat), NEG_INF)
            m = jnp.max(s, axis=1, keepdims=True)
            m = jnp.where(m == NEG_INF, 0.0, m)
            e = jnp.exp(s - m)
            p = e / jnp.maximum(jnp.sum(e, axis=1, keepdims=True), 1e-30)
            psum = psum + p
            outs.append(_dot(p.astype(BF16), vmat))
        ocmp_ref[:, pr * LANES:(pr + 1) * LANES] = jnp.where(halves[0], outs[0], outs[1]).astype(ocmp_ref.dtype)

    ps_hi, ps_lo = _split_bf16(psum)
    ovt = ovt_ref[...]
    p_slc = (_dot_nt(ovt, ps_hi) + _dot_nt(ovt, ps_lo))[0:n_slc]
    cand = lax.broadcasted_iota(jnp.int32, (n_slc, tq), 0)
    t_blk = jnp.right_shift(qi * tq + lax.broadcasted_iota(jnp.int32, (n_slc, tq), 1), int(math.log2(SLC_BLOCK)))
    valid = cand <= t_blk
    forced = (cand == 0) | (cand == t_blk) | (cand == t_blk - 1)
    score = jnp.where(valid, jnp.where(forced, FORCE_SCORE, p_slc), -1.0)
    keep = _rank_select(score, valid, n_slc, min(SLC_TOPN, n_slc))
    neg = jnp.where(keep, 0.0, MASK_NEG)
    zero = jnp.zeros((HEAD_DIM - n_slc, tq), F32)
    selm_ref[...] = jnp.concatenate([neg, zero, neg, zero], axis=0).T.astype(selm_ref.dtype)


def _nsa_sw_tiles(tiles, q_ref, selm_ref, gate_ref, ex_ref, ocmp_ref, o_ref, ksel_ref, vsel_ref, kwin_ref, vwin_ref):
    tq = ATT_TILE
    halves = _head_halves(tq)
    causal, beyond = _tile_masks(tq)
    ex = ex_ref[...]
    sel_s, win_s = {}, {}
    for c in tiles:
        rows = slice(c * tq, (c + 1) * tq)
        w = (c + 1) * tq
        lo = max(c - WINDOW // tq, 0) * tq
        qf = q_ref[rows, :].astype(F32)
        selm = selm_ref[rows, :].astype(F32)
        for h in range(2):
            q_aug = jnp.where(halves[h], qf, selm).astype(BF16)
            qz = jnp.where(halves[h], qf, 0.0).astype(BF16)
            sel_s[c, h] = _mask_tile(_dot_nt(q_aug, ksel_ref[h, 0:w, :]), c * tq, causal)
            s = _mask_tile(_dot_nt(qz, kwin_ref[h, lo:w, :]), c * tq - lo, causal)
            if c * tq - lo == WINDOW:
                s = _mask_tile(s, 0, beyond)
            win_s[c, h] = s
    for c in tiles:
        rows = slice(c * tq, (c + 1) * tq)
        w = (c + 1) * tq
        lo = max(c - WINDOW // tq, 0) * tq
        win_o = [_softmax_pv(win_s[c, h], vwin_ref[h, lo:w, :], h) for h in range(2)]
        sel_o = [_softmax_pv(sel_s[c, h], vsel_ref[h, 0:w, :], h) for h in range(2)]
        o_sel = jnp.where(halves[0], sel_o[0], sel_o[1])
        o_win = jnp.where(halves[0], win_o[0], win_o[1])
        g_hi, g_lo = _split_bf16(gate_ref[rows, :])
        gexp = _dot(g_hi, ex) + _dot(g_lo, ex)
        y = (gexp[:, 0:LANES] * ocmp_ref[rows, :].astype(F32) + gexp[:, LANES:2 * LANES] * o_sel
             + gexp[:, 2 * LANES:3 * LANES] * o_win)
        o_ref[rows, :] = y.astype(o_ref.dtype)


def _nsa_sw_kernel(q_ref, selm_ref, kvs_ref, kvw_ref, gate_ref, ex_ref, ocmp_ref, o_ref,
                   ksel_ref, vsel_ref, kwin_ref, vwin_ref):
    pr = pl.program_id(2)
    qi = pl.program_id(3)
    seq = kvs_ref.shape[0]

    @pl.when((pr == 0) & (qi == 0))
    def _():
        first, second = _head_halves(seq)
        onehot = _block_onehot(seq, int(math.log2(SLC_BLOCK)))
        kv = kvs_ref[...].astype(F32)
        swapped = pltpu.roll(kv, HEAD_DIM, axis=1)
        ksel_ref[0] = jnp.where(first, kv, onehot).astype(BF16)
        ksel_ref[1] = jnp.where(second, swapped, onehot).astype(BF16)
        vsel_ref[0] = _with_ones(swapped, first)
        vsel_ref[1] = _with_ones(kv, second)
        kv = kvw_ref[...].astype(F32)
        swapped = pltpu.roll(kv, HEAD_DIM, axis=1)
        kwin_ref[0] = kvw_ref[...]
        kwin_ref[1] = swapped.astype(BF16)
        vwin_ref[0] = _with_ones(swapped, first)
        vwin_ref[1] = _with_ones(kv, second)

    nq = seq // ATT_TILE
    for j in range(nq // 2):
        pl.when(qi == j)(functools.partial(_nsa_sw_tiles, (nq - 1 - j, j), q_ref, selm_ref, gate_ref, ex_ref,
                                           ocmp_ref, o_ref, ksel_ref, vsel_ref, kwin_ref, vwin_ref))


def _nsa_attention(q_raw, q_rot, kv, ka, kb, gates, batch, seq):
    nq = seq // ATT_TILE
    g = NSA_GROUPS
    gw = NSA_HPG * HEAD_DIM
    n_pairs = NSA_HEADS // 2
    n_cmp = (seq - CMP_LEN) // CMP_STRIDE + 1
    n_cmp_pad = seq // CMP_STRIDE
    n_slc = seq // SLC_BLOCK
    assert n_cmp_pad == LANES and n_slc <= HEAD_DIM // 2 and WINDOW % ATT_TILE == 0

    ex = np.zeros((n_pairs, LANES, 3 * LANES), np.float32)
    for hd in range(NSA_HEADS):
        for br in range(3):
            lane0 = br * LANES + (hd % 2) * HEAD_DIM
            ex[hd // 2, hd * 3 + br, lane0:lane0 + HEAD_DIM] = 1.0
    c_s = np.arange(n_cmp) * CMP_STRIDE
    s_s = np.arange(n_slc) * SLC_BLOCK
    ov = np.clip(np.minimum(c_s[:, None] + CMP_LEN, s_s[None, :] + SLC_BLOCK)
                 - np.maximum(c_s[:, None], s_s[None, :]), 0, None) / CMP_LEN
    ovt = np.zeros((LANES, n_cmp_pad), np.float32)
    ovt[:n_slc, :n_cmp] = ov.T

    n_ct = seq // CMP_TILE
    o_cmp, selm = pl.pallas_call(
        functools.partial(_nsa_cmp_kernel, n_slc),
        name="nsa_cmp",
        grid=(batch, g, n_ct),
        in_specs=[pl.BlockSpec((CMP_TILE, gw), lambda b, gi, i: (b * n_ct + i, gi)),
                  pl.BlockSpec((n_cmp_pad, LANES), lambda b, gi, i: (b * g + gi, 0)),
                  pl.BlockSpec((n_cmp_pad, LANES), lambda b, gi, i: (b * g + gi, 0)),
                  pl.BlockSpec((LANES, n_cmp_pad), lambda b, gi, i: (0, 0))],
        out_specs=[pl.BlockSpec((CMP_TILE, gw), lambda b, gi, i: (b * n_ct + i, gi)),
                   pl.BlockSpec((CMP_TILE, LANES), lambda b, gi, i: (b * n_ct + i, gi))],
        out_shape=[jax.ShapeDtypeStruct((batch * seq, NSA_HEADS * HEAD_DIM), BF16),
                   jax.ShapeDtypeStruct((batch * seq, g * LANES), BF16)],
        compiler_params=_cparams("parallel", "parallel", "parallel"),
    )(q_raw, ka, kb, jnp.asarray(ovt, BF16))

    ppg = NSA_HPG // 2
    return pl.pallas_call(
        _nsa_sw_kernel,
        name="nsa_sw",
        grid=(batch, g, ppg, nq // 2),
        in_specs=[pl.BlockSpec((seq, LANES), lambda b, gi, p, i: (b, gi * ppg + p)),
                  pl.BlockSpec((seq, LANES), lambda b, gi, p, i: (b, gi)),
                  pl.BlockSpec((seq, LANES), lambda b, gi, p, i: (b, g + gi)),
                  pl.BlockSpec((seq, LANES), lambda b, gi, p, i: (b, 2 * g + gi)),
                  pl.BlockSpec((seq, LANES), lambda b, gi, p, i: (b, 0)),
                  pl.BlockSpec((None, LANES, 3 * LANES), lambda b, gi, p, i: (gi * ppg + p, 0, 0)),
                  pl.BlockSpec((seq, LANES), lambda b, gi, p, i: (b, gi * ppg + p))],
        out_specs=pl.BlockSpec((seq, LANES), lambda b, gi, p, i: (b, gi * ppg + p)),
        out_shape=jax.ShapeDtypeStruct((batch * seq, NSA_HEADS * HEAD_DIM), BF16),
        scratch_shapes=[pltpu.VMEM((2, seq, LANES), BF16) for _ in range(4)],
        compiler_params=_cparams("parallel", "parallel", "arbitrary", "arbitrary"),
    )(q_rot, selm, kv, kv, gates, jnp.asarray(ex, BF16), o_cmp)


def _even_mixer(x, g, w_in, pool_w, pool_scale, w_out, rope_pair, batch, seq):
    chunk = 2 * LANES
    plan = [(0, POOL_DIM, [(0, 0, "none", 0)])]
    for c in range(3 * MOBA_DIM // chunk):
        op = "rope_scale" if c < MOBA_DIM // chunk else ("rope" if c < 2 * MOBA_DIM // chunk else "none")
        plan.append((POOL_DIM + c * chunk, chunk, [(1, c * chunk, op, 0)]))
    u_pool, qkv = _norm_project(x, g, w_in.astype(BF16), plan, [rope_pair],
                                [(POOL_DIM, F32), (3 * MOBA_DIM, BF16)], seq)
    w_bd = jax.scipy.linalg.block_diag(*[pool_w[i] for i in range(pool_w.shape[0])]).astype(BF16)
    y_a = _pool_mixer(u_pool, w_bd, pool_scale, batch, seq)
    y_b = _moba_attention(qkv, batch, seq)
    w_out = w_out.astype(BF16)
    return _out_proj(x, [y_a, y_b], [w_out[:POOL_DIM], w_out[POOL_DIM:]])


def _odd_mixer(x, g, w_in, pe_k, pe_v, k_w1, k_w2, v_w1, v_w2, w_out, rope_pair, rope_single, batch, seq):
    d = x.shape[1]
    qd = NSA_HEADS * HEAD_DIM
    kvd = NSA_KV_DIM
    wq = w_in[:, :qd]
    parts = [w_in[:, qd + i * kvd:qd + (i + 1) * kvd].reshape(d, NSA_GROUPS, HEAD_DIM) for i in range(6)]
    pairs = [jnp.concatenate([parts[2 * i], parts[2 * i + 1]], axis=-1).reshape(d, 2 * kvd) for i in range(3)]
    n_gate = 3 * NSA_HEADS
    w_gate = jnp.zeros((d, LANES), F32).at[:, :n_gate].set(w_in[:, qd + 6 * kvd:])
    w_all = jnp.concatenate([wq] + pairs + [w_gate], axis=1).astype(BF16)

    chunk = 2 * LANES
    plan = []
    for c in range(qd // chunk):
        plan.append((c * chunk, chunk, [(0, c * chunk, "scale", 0), (1, c * chunk, "rope_scale", 0)]))
    for c in range(6 * kvd // chunk):
        branch = c // (2 * kvd // chunk)
        op = "none" if branch == 0 else "rope"
        plan.append((qd + c * chunk, chunk, [(2, c * chunk, op, 1)]))
    plan.append((qd + 6 * kvd, LANES, [(3, 0, "sigmoid", 0)]))
    q_raw, q_rot, kv, gates = _norm_project(
        x, g, w_all, plan, [rope_pair, rope_single],
        [(qd, BF16), (qd, BF16), (6 * kvd, BF16), (LANES, F32)], seq)

    ka, kb = _compress(kv, pe_k, pe_v, k_w1, k_w2, v_w1, v_w2, batch, seq)
    y = _nsa_attention(q_raw, q_rot, kv, ka, kb, gates, batch, seq)
    return _out_proj(x, [y], [w_out.astype(BF16)])


def kernel(x, mem, positions, norm_g, mem_g, final_g, w_in_ab, pool_w, pool_scale, w_out_ab, ffn_w_gate, ffn_w_up, ffn_w_down, w_in_c, cmp_pe_k, cmp_pe_v, cmp_k_w1, cmp_k_w2, cmp_v_w1, cmp_v_w2, w_out_c, router_w, moe_w_gate, moe_w_up, moe_w_down, xa_wq, xa_wkv, xa_wo):
    batch, seq, d = x.shape
    mem_len = mem.shape[1]
    depth = norm_g.shape[0]
    rope_pair, rope_single = _rope_tables(positions)

    wkv_all = jnp.concatenate([xa_wkv[l] for l in range(depth)], axis=1).astype(BF16)
    kv_all = _norm_matmul(mem.reshape(batch * mem_len, d), mem_g, wkv_all)

    xf = x.reshape(batch * seq, d)
    for layer in range(depth):
        i = layer // 2
        if layer % 2 == 0:
            xf = _even_mixer(xf, norm_g[layer, 0], w_in_ab[i], pool_w[i], pool_scale[i], w_out_ab[i],
                             rope_pair, batch, seq)
        else:
            xf = _odd_mixer(xf, norm_g[layer, 0], w_in_c[i], cmp_pe_k[i], cmp_pe_v[i], cmp_k_w1[i], cmp_k_w2[i],
                            cmp_v_w1[i], cmp_v_w2[i], w_out_c[i], rope_pair, rope_single, batch, seq)
        xf = _cross_attention(xf, norm_g[layer, 1], xa_wq[layer].astype(BF16), kv_all, layer,
                              xa_wo[layer].astype(BF16), seq, mem_len)
        if layer % 2 == 0:
            xf = _ffn(xf, norm_g[layer, 2], ffn_w_gate[i].astype(BF16), ffn_w_up[i].astype(BF16),
                      ffn_w_down[i].astype(BF16))
        else:
            xf = _moe(xf, norm_g[layer, 2], router_w[i], moe_w_gate, moe_w_up, moe_w_down, i)
    return _final_norm(xf, final_g).reshape(batch, seq, d)
```

```python
import functools
import math

import numpy as np
import jax
import jax.numpy as jnp
from jax import lax
from jax.experimental import pallas as pl
from jax.experimental.pallas import tpu as pltpu

F32 = jnp.float32
BF16 = jnp.bfloat16

HEAD_DIM = 64
ROPE_DIM = 16
ROPE_THETA = 500000.0
NORM_EPS = 1e-5
QK_SCALE = HEAD_DIM ** -0.5
QK_SCALE_LOG2E = QK_SCALE * math.log2(math.e)

POOL_WINDOWS = (2, 4, 8, 16)
POOL_DIM = 256
MOBA_HEADS = 12
MOBA_DIM = MOBA_HEADS * HEAD_DIM
MOBA_BLOCK = 256
MOBA_TOPK = 3
NSA_HEADS = 16
NSA_GROUPS = 4
NSA_HPG = 4
NSA_KV_DIM = NSA_GROUPS * HEAD_DIM
CMP_LEN = 32
CMP_STRIDE = 16
SLC_BLOCK = 64
SLC_TOPN = 16
WINDOW = 512
FORCE_SCORE = 1e4
XA_HEADS = 4
XA_DIM = XA_HEADS * HEAD_DIM
N_EXPERTS = 8

LANES = 128
ATT_TILE = 256
CMP_TILE = 1024
VMEM_LIMIT = 56 * 1024 * 1024
NEG_INF = float("-inf")
MASK_NEG = -1e30


def _cparams(*sem):
    return pltpu.CompilerParams(dimension_semantics=sem, vmem_limit_bytes=VMEM_LIMIT)


def _dot(a, b):
    return jnp.dot(a, b, preferred_element_type=F32)


def _dot_nt(a, b):
    return lax.dot_general(a, b, (((1,), (1,)), ((), ())), preferred_element_type=F32)


def _split_bf16(a):
    hi = a.astype(BF16)
    lo = (a - hi.astype(F32)).astype(BF16)
    return hi, lo


def _rms_bf16(x, g):
    ms = jnp.mean(x * x, axis=-1, keepdims=True)
    return (x * lax.rsqrt(ms + NORM_EPS) * g).astype(BF16)


def _apply_rope(acc, c, a, b):
    half = ROPE_DIM // 2
    return acc * c + pltpu.roll(acc, LANES - half, axis=1) * a + pltpu.roll(acc, half, axis=1) * b


def _proj_kernel(plan, n_tab, x_ref, g_ref, w_ref, *rest):
    tabs = rest[:3 * n_tab]
    outs = rest[3 * n_tab:]
    h = _rms_bf16(x_ref[...], g_ref[...])
    for wc0, width, sinks in plan:
        acc = _dot(h, w_ref[:, wc0:wc0 + width])
        for oi, oc0, op, tab in sinks:
            for s in range(width // LANES):
                val = acc[:, s * LANES:(s + 1) * LANES]
                if op in ("rope", "rope_scale"):
                    c, a, b = (tabs[3 * tab + i][...] for i in range(3))
                    val = _apply_rope(val, c, a, b)
                if op == "scale":
                    val = val * QK_SCALE
                if op == "rope_scale":
                    val = val * QK_SCALE_LOG2E
                if op == "sigmoid":
                    val = jax.nn.sigmoid(val)
                o = outs[oi]
                o[:, oc0 + s * LANES:oc0 + (s + 1) * LANES] = val.astype(o.dtype)


def _norm_project(x, g, w, plan, tables, out_defs, seq, tm=512):
    t, d = x.shape
    n = w.shape[1]
    n_seq_tiles = seq // tm
    in_specs = [
        pl.BlockSpec((tm, d), lambda i: (i, 0)),
        pl.BlockSpec((1, d), lambda i: (0, 0)),
        pl.BlockSpec((d, n), lambda i: (0, 0)),
    ]
    flat_tabs = []
    for tset in tables:
        for tb in tset:
            flat_tabs.append(tb)
            in_specs.append(pl.BlockSpec((tm, LANES), lambda i: (i % n_seq_tiles, 0)))
    out_shape = [jax.ShapeDtypeStruct((t, wd), dt) for wd, dt in out_defs]
    out_specs = [pl.BlockSpec((tm, wd), lambda i: (i, 0)) for wd, _ in out_defs]
    return pl.pallas_call(
        functools.partial(_proj_kernel, plan, len(tables)),
        name="proj",
        grid=(t // tm,),
        in_specs=in_specs,
        out_specs=out_specs,
        out_shape=out_shape,
        compiler_params=_cparams("parallel"),
    )(x, g.reshape(1, d), w, *flat_tabs)


def _rope_tables(positions):
    half = ROPE_DIM // 2
    inv = ROPE_THETA ** (-jnp.arange(0, ROPE_DIM, 2, dtype=F32) / ROPE_DIM)
    ang = positions.astype(F32)[:, None] * inv[None, :]
    cos, sin = jnp.cos(ang), jnp.sin(ang)
    s = positions.shape[0]
    z8 = jnp.zeros((s, half), F32)
    rest0 = jnp.zeros((s, HEAD_DIM - ROPE_DIM), F32)
    rest1 = jnp.ones((s, HEAD_DIM - ROPE_DIM), F32)
    c64 = jnp.concatenate([cos, cos, rest1], -1)
    a64 = jnp.concatenate([-sin, z8, rest0], -1)
    b64 = jnp.concatenate([z8, sin, rest0], -1)
    one64 = jnp.ones((s, HEAD_DIM), F32)
    zero64 = jnp.zeros((s, HEAD_DIM), F32)
    pair = tuple(jnp.concatenate([m, m], -1) for m in (c64, a64, b64))
    single = (jnp.concatenate([c64, one64], -1), jnp.concatenate([a64, zero64], -1),
              jnp.concatenate([b64, zero64], -1))
    return pair, single


def _norm_matmul_kernel(x_ref, g_ref, w_ref, o_ref):
    h = _rms_bf16(x_ref[...], g_ref[...])
    o_ref[...] = _dot(h, w_ref[...]).astype(o_ref.dtype)


def _norm_matmul(x, g, w, tm=512):
    t, d = x.shape
    n = w.shape[1]
    return pl.pallas_call(
        _norm_matmul_kernel,
        name="mem_kv",
        grid=(t // tm,),
        in_specs=[pl.BlockSpec((tm, d), lambda i: (i, 0)),
                  pl.BlockSpec((1, d), lambda i: (0, 0)),
                  pl.BlockSpec((d, n), lambda i: (0, 0))],
        out_specs=pl.BlockSpec((tm, n), lambda i: (i, 0)),
        out_shape=jax.ShapeDtypeStruct((t, n), BF16),
        compiler_params=_cparams("parallel"),
    )(x, g.reshape(1, d), w)


def _final_norm_kernel(x_ref, g_ref, o_ref):
    x = x_ref[...]
    ms = jnp.mean(x * x, axis=-1, keepdims=True)
    o_ref[...] = x * lax.rsqrt(ms + NORM_EPS) * g_ref[...]


def _final_norm(x, g, tm=1024):
    t, d = x.shape
    return pl.pallas_call(
        _final_norm_kernel,
        name="final_norm",
        grid=(t // tm,),
        in_specs=[pl.BlockSpec((tm, d), lambda i: (i, 0)), pl.BlockSpec((1, d), lambda i: (0, 0))],
        out_specs=pl.BlockSpec((tm, d), lambda i: (i, 0)),
        out_shape=jax.ShapeDtypeStruct((t, d), F32),
        compiler_params=_cparams("parallel"),
    )(x, g.reshape(1, d))


def _out_proj_kernel(n_in, x_ref, *rest):
    ys = rest[:n_in]
    ws = rest[n_in:2 * n_in]
    o_ref = rest[2 * n_in]
    acc = x_ref[...]
    for y, w in zip(ys, ws):
        acc = acc + _dot(y[...], w[...])
    o_ref[...] = acc


def _out_proj(x, ys, ws, tm=512):
    t, d = x.shape
    in_specs = [pl.BlockSpec((tm, d), lambda i: (i, 0))]
    in_specs += [pl.BlockSpec((tm, y.shape[1]), lambda i: (i, 0)) for y in ys]
    in_specs += [pl.BlockSpec(w.shape, lambda i: (0, 0)) for w in ws]
    return pl.pallas_call(
        functools.partial(_out_proj_kernel, len(ys)),
        name="out_proj",
        grid=(t // tm,),
        in_specs=in_specs,
        out_specs=pl.BlockSpec((tm, d), lambda i: (i, 0)),
        out_shape=jax.ShapeDtypeStruct((t, d), F32),
        compiler_params=_cparams("parallel"),
    )(x, *ys, *ws)


def _pool_kernel(u_ref, w_ref, sc_ref, o_ref, pad_ref):
    s = u_ref.shape[0]
    maxw = POOL_WINDOWS[-1]
    u = u_ref[...]
    pad_ref[0:maxw, :] = jnp.zeros((maxw, POOL_DIM), F32)
    t1 = (lax.broadcasted_iota(jnp.int32, (s, POOL_DIM), 0) + 1).astype(F32)
    lane = lax.broadcasted_iota(jnp.int32, (s, POOL_DIM), 1)
    gdim = POOL_DIM // len(POOL_WINDOWS)
    cur = u
    pooled = jnp.zeros_like(u)
    shift = 1
    for gi, w in enumerate(POOL_WINDOWS):
        pad_ref[maxw:maxw + s, :] = cur
        cur = cur + pad_ref[maxw - shift:maxw - shift + s, :]
        shift *= 2
        assert shift == w
        mean = cur / jnp.minimum(t1, float(w))
        pooled = jnp.where((lane >= gi * gdim) & (lane < (gi + 1) * gdim), mean, pooled)
    pooled = (pooled - u).astype(BF16)
    o_ref[...] = (_dot(pooled, w_ref[...]) * sc_ref[...]).astype(o_ref.dtype)


def _pool_mixer(u_pool, w_bd, scale, batch, seq):
    return pl.pallas_call(
        _pool_kernel,
        name="pool",
        grid=(batch,),
        in_specs=[pl.BlockSpec((seq, POOL_DIM), lambda b: (b, 0)),
                  pl.BlockSpec((POOL_DIM, POOL_DIM), lambda b: (0, 0)),
                  pl.BlockSpec((1, POOL_DIM), lambda b: (0, 0))],
        out_specs=pl.BlockSpec((seq, POOL_DIM), lambda b: (b, 0)),
        out_shape=jax.ShapeDtypeStruct((batch * seq, POOL_DIM), BF16),
        scratch_shapes=[pltpu.VMEM((seq + POOL_WINDOWS[-1], POOL_DIM), F32)],
        compiler_params=_cparams("parallel"),
    )(u_pool, w_bd, scale.reshape(1, POOL_DIM))


def _rank_select(scoreT, valid, n_rows, topn):
    row = lax.broadcasted_iota(jnp.int32, scoreT.shape, 0)
    rank = jnp.zeros(scoreT.shape, F32)
    for j in range(n_rows):
        sj = scoreT[j:j + 1, :]
        beats = (scoreT > sj) | ((scoreT == sj) & (row < j))
        rj = jnp.sum(beats.astype(F32), axis=0, keepdims=True)
        rank = jnp.where(row == j, rj, rank)
    return (rank < float(topn)) & valid


def _head_halves(rows):
    lane = lax.broadcasted_iota(jnp.int32, (rows, LANES), 1)
    return lane < HEAD_DIM, lane >= HEAD_DIM


def _tile_masks(tq):
    qry = lax.broadcasted_iota(jnp.int32, (tq, tq), 0)
    key = lax.broadcasted_iota(jnp.int32, (tq, tq), 1)
    return key <= qry, key > qry


def _mask_tile(s, t0, mask):
    tq = mask.shape[0]
    parts = []
    if t0 > 0:
        parts.append(s[:, :t0])
    parts.append(jnp.where(mask, s[:, t0:t0 + tq], NEG_INF))
    if t0 + tq < s.shape[1]:
        parts.append(s[:, t0 + tq:])
    return parts[0] if len(parts) == 1 else jnp.concatenate(parts, axis=1)


def _softmax_pv(s, v, h):
    m = jnp.max(s, axis=1, keepdims=True)
    p = jnp.exp2((s - m).astype(BF16))
    o = _dot(p, v)
    denom_lane = (1 - h) * HEAD_DIM
    return o / o[:, denom_lane:denom_lane + 1]


def _with_ones(v, own):
    return jnp.where(own, v, 1.0).astype(BF16)


def _block_onehot(seq, shift):
    lane = lax.broadcasted_iota(jnp.int32, (seq, LANES), 1)
    blk = jnp.right_shift(lax.broadcasted_iota(jnp.int32, (seq, LANES), 0), shift)
    return jnp.where((lane & (HEAD_DIM - 1)) == blk, 1.0, 0.0)


def _moba_tiles(tiles, o_ref, qaug_ref, kaug_ref, vaug_ref):
    tq = ATT_TILE
    halves = _head_halves(tq)
    causal, _ = _tile_masks(tq)
    scores = {}
    for c in tiles:
        for h in range(2):
            s = _dot_nt(qaug_ref[h, c * tq:(c + 1) * tq, :], kaug_ref[h, 0:(c + 1) * tq, :])
            scores[c, h] = _mask_tile(s, c * tq, causal)
    for c in tiles:
        outs = [_softmax_pv(scores[c, h], vaug_ref[h, 0:(c + 1) * tq, :], h) for h in range(2)]
        o_ref[c * tq:(c + 1) * tq, :] = jnp.where(halves[0], outs[0], outs[1]).astype(o_ref.dtype)


def _moba_kernel(q_ref, k_ref, v_ref, o_ref, qaug_ref, kaug_ref, vaug_ref, kmean_ref):
    qi = pl.program_id(2)
    seq = k_ref.shape[0]
    nb = seq // MOBA_BLOCK
    blk_shift = int(math.log2(MOBA_BLOCK))

    @pl.when(qi == 0)
    def _():
        halves = _head_halves(seq)
        kf = k_ref[...].astype(F32)
        vf = v_ref[...].astype(F32)
        qp = q_ref[...]
        qf = qp.astype(F32)
        onehot = _block_onehot(seq, blk_shift)
        kmean_ref[...] = jnp.zeros(kmean_ref.shape, F32)
        for j in range(nb):
            kmean_ref[j:j + 1, :] = jnp.mean(kf[j * MOBA_BLOCK:(j + 1) * MOBA_BLOCK], axis=0, keepdims=True)
        cand = lax.broadcasted_iota(jnp.int32, (nb, seq), 0)
        own = jnp.right_shift(lax.broadcasted_iota(jnp.int32, (nb, seq), 1), blk_shift)
        valid = cand < own
        for h in range(2):
            kaug_ref[h] = jnp.where(halves[h], kf, onehot).astype(BF16)
            vaug_ref[h] = _with_ones(vf, halves[h])
            mine_k = _head_halves(kmean_ref.shape[0])[h]
            km_hi, km_lo = _split_bf16(jnp.where(mine_k, kmean_ref[...], 0.0))
            gate = (_dot_nt(km_hi, qp) + _dot_nt(km_lo, qp))[0:nb]
            gate = jnp.where(valid, gate, NEG_INF)
            keep = _rank_select(gate, valid, nb, MOBA_TOPK) | (cand == own)
            neg = jnp.where(keep, 0.0, MASK_NEG)
            lead = HEAD_DIM if h == 0 else 0
            pieces = [neg, jnp.zeros((LANES - lead - nb, seq), F32)]
            if lead:
                pieces.insert(0, jnp.zeros((lead, seq), F32))
            maskcols = jnp.concatenate(pieces, axis=0).T
            qaug_ref[h] = jnp.where(halves[h], qf, maskcols).astype(BF16)

    for j in range(nb // 2):
        pl.when(qi == j)(functools.partial(_moba_tiles, (nb - 1 - j, j), o_ref, qaug_ref, kaug_ref, vaug_ref))


def _moba_attention(qkv, batch, seq):
    nq = seq // ATT_TILE
    assert MOBA_BLOCK == ATT_TILE and nq <= 16 and nq % 2 == 0
    hp = MOBA_DIM // LANES
    return pl.pallas_call(
        _moba_kernel,
        name="moba",
        grid=(batch, hp, nq // 2),
        in_specs=[pl.BlockSpec((seq, LANES), lambda b, p, i: (b, p)),
                  pl.BlockSpec((seq, LANES), lambda b, p, i: (b, hp + p)),
                  pl.BlockSpec((seq, LANES), lambda b, p, i: (b, 2 * hp + p))],
        out_specs=pl.BlockSpec((seq, LANES), lambda b, p, i: (b, p)),
        out_shape=jax.ShapeDtypeStruct((batch * seq, MOBA_DIM), BF16),
        scratch_shapes=[pltpu.VMEM((2, seq, LANES), BF16), pltpu.VMEM((2, seq, LANES), BF16),
                        pltpu.VMEM((2, seq, LANES), BF16), pltpu.VMEM((16, LANES), F32)],
        compiler_params=_cparams("parallel", "parallel", "arbitrary"),
    )(qkv, qkv, qkv)


def _xattn_kernel(x_ref, g_ref, wq_ref, kv_ref, wo_ref, o_ref):
    x = x_ref[...]
    h = _rms_bf16(x, g_ref[...])
    q = (_dot(h, wq_ref[...]) * QK_SCALE).astype(BF16)
    out = x
    for hd in range(XA_HEADS):
        sl = slice(hd * HEAD_DIM, (hd + 1) * HEAD_DIM)
        k = kv_ref[:, sl]
        v = kv_ref[:, XA_DIM + hd * HEAD_DIM:XA_DIM + (hd + 1) * HEAD_DIM]
        s = _dot_nt(q[:, sl], k)
        m = jnp.max(s, axis=1, keepdims=True)
        p = jnp.exp(s - m)
        l = jnp.sum(p, axis=1, keepdims=True)
        o = _dot(p.astype(BF16), v) / l
        out = out + _dot(o.astype(BF16), wo_ref[sl, :])
    o_ref[...] = out


def _cross_attention(x, g, wq, kv_all, layer, wo, seq, mem_len, tm=512):
    t, d = x.shape
    tiles_per_seq = seq // tm
    return pl.pallas_call(
        _xattn_kernel,
        name="xattn",
        grid=(t // tm,),
        in_specs=[pl.BlockSpec((tm, d), lambda i: (i, 0)),
                  pl.BlockSpec((1, d), lambda i: (0, 0)),
                  pl.BlockSpec((d, XA_DIM), lambda i: (0, 0)),
                  pl.BlockSpec((mem_len, 2 * XA_DIM), lambda i: (i // tiles_per_seq, layer)),
                  pl.BlockSpec((XA_DIM, d), lambda i: (0, 0))],
        out_specs=pl.BlockSpec((tm, d), lambda i: (i, 0)),
        out_shape=jax.ShapeDtypeStruct((t, d), F32),
        compiler_params=_cparams("parallel"),
    )(x, g.reshape(1, d), wq, kv_all, wo)


def _ffn_kernel(x_ref, g_ref, wg_ref, wu_ref, wd_ref, o_ref, h_ref):
    f = pl.program_id(1)

    @pl.when(f == 0)
    def _():
        x = x_ref[...]
        h_ref[...] = _rms_bf16(x, g_ref[...])
        o_ref[...] = x

    h = h_ref[...]
    gate = _dot(h, wg_ref[...])
    a = (gate * jax.nn.sigmoid(gate)) * _dot(h, wu_ref[...])
    o_ref[...] += _dot(a.astype(BF16), wd_ref[...])


def _ffn(x, g, wg, wu, wd, tm=1024, tf=256):
    t, d = x.shape
    ff = wg.shape[1]
    return pl.pallas_call(
        _ffn_kernel,
        name="ffn",
        grid=(t // tm, ff // tf),
        in_specs=[pl.BlockSpec((tm, d), lambda i, f: (i, 0)),
                  pl.BlockSpec((1, d), lambda i, f: (0, 0)),
                  pl.BlockSpec((d, tf), lambda i, f: (0, f)),
                  pl.BlockSpec((d, tf), lambda i, f: (0, f)),
                  pl.BlockSpec((tf, d), lambda i, f: (f, 0))],
        out_specs=pl.BlockSpec((tm, d), lambda i, f: (i, 0)),
        out_shape=jax.ShapeDtypeStruct((t, d), F32),
        scratch_shapes=[pltpu.VMEM((tm, d), BF16)],
        compiler_params=_cparams("parallel", "arbitrary"),
    )(x, g.reshape(1, d), wg, wu, wd)


ROUTE_I1, ROUTE_I2, ROUTE_W1, ROUTE_W2, ROUTE_R1, ROUTE_R2 = range(6)
MOE_TILE = 512


def _lane_pick(arr, lane, k):
    return jnp.sum(jnp.where(lane == k, arr, 0.0), axis=1, keepdims=True)


def _router_kernel(x_ref, g_ref, rhi_ref, rlo_ref, route_ref, cnt_ref):
    i = pl.program_id(0)
    tm = x_ref.shape[0]
    lane = lax.broadcasted_iota(jnp.int32, (tm, LANES), 1)
    lane_f = lane.astype(F32)

    @pl.when(i == 0)
    def _():
        cnt_ref[...] = jnp.zeros(cnt_ref.shape, F32)

    h = _rms_bf16(x_ref[...], g_ref[...])
    logits = _dot(h, rhi_ref[...]) + _dot(h, rlo_ref[...])
    logits = jnp.where(lane < N_EXPERTS, logits, NEG_INF)
    m1 = jnp.max(logits, axis=1, keepdims=True)
    i1 = jnp.min(jnp.where(logits == m1, lane_f, float(LANES)), axis=1, keepdims=True)
    rest = jnp.where(lane_f == i1, NEG_INF, logits)
    m2 = jnp.max(rest, axis=1, keepdims=True)
    i2 = jnp.min(jnp.where(rest == m2, lane_f, float(LANES)), axis=1, keepdims=True)
    e2 = jnp.exp(m2 - m1)
    w1 = 1.0 / (1.0 + e2)
    w2 = e2 / (1.0 + e2)

    chosen = jnp.where((lane_f == i1) | (lane_f == i2), 1.0, 0.0)
    r = lax.broadcasted_iota(jnp.int32, (tm, tm), 0)
    c = lax.broadcasted_iota(jnp.int32, (tm, tm), 1)
    earlier = jnp.where(c < r, 1.0, 0.0).astype(BF16)
    prefix = _dot(earlier, chosen.astype(BF16)) + cnt_ref[0:1, :]
    rank1 = jnp.sum(jnp.where(lane_f == i1, prefix, 0.0), axis=1, keepdims=True)
    rank2 = jnp.sum(jnp.where(lane_f == i2, prefix, 0.0), axis=1, keepdims=True)
    cnt_ref[0:1, :] = cnt_ref[0:1, :] + jnp.sum(chosen, axis=0, keepdims=True)

    cols = {ROUTE_I1: i1, ROUTE_I2: i2, ROUTE_W1: w1, ROUTE_W2: w2, ROUTE_R1: rank1, ROUTE_R2: rank2}
    route = jnp.zeros((tm, LANES), F32)
    for k, val in cols.items():
        route = jnp.where(lane == k, val, route)
    route_ref[...] = route


def _dispatch_kernel(pos1_ref, pos2_ref, x_ref, xs_in, xs_hbm, sem):
    del xs_in
    n = pos1_ref.shape[0]

    def row_copy(r, p):
        return pltpu.make_async_copy(x_ref.at[pl.ds(r, 1)], xs_hbm.at[pl.ds(p, 1)], sem)

    def issue(r, carry):
        row_copy(r, pos1_ref[r]).start()
        row_copy(r, pos2_ref[r]).start()
        return carry

    lax.fori_loop(0, n, issue, 0, unroll=8)
    for _ in range(2):
        pltpu.make_async_copy(x_ref, xs_hbm.at[pl.ds(0, n)], sem).wait()


def _expert_kernel(te_ref, nu_ref, first_ref, xs_ref, g_ref, wg_ref, wu_ref, wd_ref, y_ref,
                   h_ref, wg_res, wu_res, wd_res):
    del te_ref
    i = pl.program_id(0)
    f = pl.program_id(1)
    used = i < nu_ref[0]

    @pl.when(f == 0)
    def _():
        h_ref[...] = _rms_bf16(xs_ref[...], g_ref[...])
        y_ref[...] = jnp.zeros(y_ref.shape, y_ref.dtype)

    @pl.when(first_ref[i] == 1)
    def _():
        wg_res[f] = wg_ref[...].astype(BF16)
        wu_res[f] = wu_ref[...].astype(BF16)
        wd_res[f] = wd_ref[...].astype(BF16)

    @pl.when(used)
    def _():
        h = h_ref[...]
        gate = _dot(h, wg_res[f])
        a = (gate * jax.nn.sigmoid(gate)) * _dot(h, wu_res[f])
        y_ref[...] += _dot(a.astype(BF16), wd_res[f])


def _combine_kernel(pos1_ref, pos2_ref, x_ref, route_ref, y_hbm, o_ref, buf1, buf2, sem):
    n = pos1_ref.shape[0]

    def row_copy(p, buf, r):
        return pltpu.make_async_copy(y_hbm.at[pl.ds(p, 1)], buf.at[pl.ds(r, 1)], sem)

    def issue(r, carry):
        row_copy(pos1_ref[r], buf1, r).start()
        row_copy(pos2_ref[r], buf2, r).start()
        return carry

    lax.fori_loop(0, n, issue, 0, unroll=8)
    for buf in (buf1, buf2):
        pltpu.make_async_copy(y_hbm.at[pl.ds(0, n)], buf, sem).wait()
    lane = lax.broadcasted_iota(jnp.int32, route_ref.shape, 1)
    route = route_ref[...]
    w1 = _lane_pick(route, lane, ROUTE_W1)
    w2 = _lane_pick(route, lane, ROUTE_W2)
    o_ref[...] = x_ref[...] + w1 * buf1[...] + w2 * buf2[...]


def _moe(x, g, router_w, wg, wu, wd, li, tf=512, t_route=512, t_disp=512, t_comb=256):
    t, d = x.shape
    _, ne, _, ff = wg.shape
    rows = MOE_TILE
    n_rows = 2 * t + ne * rows
    n_tiles = n_rows // rows
    nf = ff // tf
    g2 = g.reshape(1, d)
    rw = jnp.zeros((d, LANES), F32).at[:, :ne].set(router_w)
    rhi = rw.astype(BF16)
    rlo = (rw - rhi.astype(F32)).astype(BF16)

    route, cnt = pl.pallas_call(
        _router_kernel,
        name="moe_route",
        grid=(t // t_route,),
        in_specs=[pl.BlockSpec((t_route, d), lambda i: (i, 0)),
                  pl.BlockSpec((1, d), lambda i: (0, 0)),
                  pl.BlockSpec((d, LANES), lambda i: (0, 0)),
                  pl.BlockSpec((d, LANES), lambda i: (0, 0))],
        out_specs=[pl.BlockSpec((t_route, LANES), lambda i: (i, 0)),
                   pl.BlockSpec((8, LANES), lambda i: (0, 0))],
        out_shape=[jax.ShapeDtypeStruct((t, LANES), F32), jax.ShapeDtypeStruct((8, LANES), F32)],
        compiler_params=_cparams("arbitrary"),
    )(x, g2, rhi, rlo)

    counts = cnt[0, :ne].astype(jnp.int32)
    gsz = (counts + rows - 1) // rows * rows
    ends = jnp.cumsum(gsz)
    offs = ends - gsz
    i1 = route[:, ROUTE_I1].astype(jnp.int32)
    i2 = route[:, ROUTE_I2].astype(jnp.int32)
    pos1 = offs[i1] + route[:, ROUTE_R1].astype(jnp.int32)
    pos2 = offs[i2] + route[:, ROUTE_R2].astype(jnp.int32)
    tile_start = jnp.arange(n_tiles, dtype=jnp.int32) * rows
    tile_expert = jnp.minimum(jnp.sum(tile_start[:, None] >= ends[None, :], axis=1), ne - 1).astype(jnp.int32)
    n_used = (ends[-1] // rows).astype(jnp.int32).reshape(1)
    prev_expert = jnp.concatenate([jnp.full((1,), -1, jnp.int32), tile_expert[:-1]])
    tile_first = ((tile_expert != prev_expert) & (tile_start < ends[-1])).astype(jnp.int32)

    smem_idx = lambda n: pl.BlockSpec((n,), lambda i: (i,), memory_space=pltpu.SMEM)
    hbm = pl.BlockSpec(memory_space=pl.ANY)
    xs = pl.pallas_call(
        _dispatch_kernel,
        name="moe_dispatch",
        grid=(t // t_disp,),
        in_specs=[smem_idx(t_disp), smem_idx(t_disp), pl.BlockSpec((t_disp, d), lambda i: (i, 0)), hbm],
        out_specs=hbm,
        out_shape=jax.ShapeDtypeStruct((n_rows, d), F32),
        scratch_shapes=[pltpu.SemaphoreType.DMA(())],
        input_output_aliases={3: 0},
        compiler_params=_cparams("arbitrary"),
    )(pos1, pos2, x, jnp.zeros((n_rows, d), F32))

    def w_in_map(i, f, te, nu, first):
        return li, te[i], 0, jnp.where(first[i] == 1, f, nf - 1)

    def w_out_map(i, f, te, nu, first):
        return li, te[i], jnp.where(first[i] == 1, f, nf - 1), 0

    row_map = lambda i, f, te, nu, first: (i, 0)
    y = pl.pallas_call(
        _expert_kernel,
        name="moe_experts",
        grid_spec=pltpu.PrefetchScalarGridSpec(
            num_scalar_prefetch=3,
            grid=(n_tiles, nf),
            in_specs=[pl.BlockSpec((rows, d), row_map),
                      pl.BlockSpec((1, d), lambda i, f, te, nu, first: (0, 0)),
                      pl.BlockSpec((None, None, d, tf), w_in_map),
                      pl.BlockSpec((None, None, d, tf), w_in_map),
                      pl.BlockSpec((None, None, tf, d), w_out_map)],
            out_specs=pl.BlockSpec((rows, d), row_map),
            scratch_shapes=[pltpu.VMEM((rows, d), BF16), pltpu.VMEM((nf, d, tf), BF16),
                            pltpu.VMEM((nf, d, tf), BF16), pltpu.VMEM((nf, tf, d), BF16)]),
        out_shape=jax.ShapeDtypeStruct((n_rows, d), F32),
        compiler_params=_cparams("arbitrary", "arbitrary"),
    )(tile_expert, n_used, tile_first, xs, g2, wg, wu, wd)

    return pl.pallas_call(
        _combine_kernel,
        name="moe_combine",
        grid=(t // t_comb,),
        in_specs=[smem_idx(t_comb), smem_idx(t_comb),
                  pl.BlockSpec((t_comb, d), lambda i: (i, 0)),
                  pl.BlockSpec((t_comb, LANES), lambda i: (i, 0)),
                  hbm],
        out_specs=pl.BlockSpec((t_comb, d), lambda i: (i, 0)),
        out_shape=jax.ShapeDtypeStruct((t, d), F32),
        scratch_shapes=[pltpu.VMEM((t_comb, d), F32), pltpu.VMEM((t_comb, d), F32),
                        pltpu.SemaphoreType.DMA(())],
        compiler_params=_cparams("arbitrary"),
    )(pos1, pos2, x, route, y)


def _gelu_tanh(x):
    return 0.5 * x * (1.0 + jnp.tanh(math.sqrt(2.0 / math.pi) * (x + 0.044715 * (x * x * x))))


def _compress_kernel(kv_ref, pe_ref, w1_ref, w2a_ref, w2b_ref, ka_ref, kb_ref, xf_ref, sh_ref):
    seq = kv_ref.shape[0]
    n = seq // CMP_STRIDE
    hid2 = w1_ref.shape[2]
    xf_ref[...] = kv_ref[...].astype(F32)
    top = jnp.zeros((n, hid2), F32)
    bot = jnp.zeros((n, hid2), F32)
    for i in range(CMP_STRIDE):
        xi = xf_ref[pl.ds(i, n, stride=CMP_STRIDE), :]
        top = top + _dot((xi + pe_ref[i:i + 1, :]).astype(BF16), w1_ref[i])
        bot = bot + _dot((xi + pe_ref[CMP_STRIDE + i:CMP_STRIDE + i + 1, :]).astype(BF16), w1_ref[CMP_STRIDE + i])
    sh_ref[0:n, :] = bot
    sh_ref[n:n + 8, :] = jnp.zeros((8, hid2), F32)
    act = _gelu_tanh(top + sh_ref[1:n + 1, :]).astype(BF16)
    ka_ref[...] = _dot(act, w2a_ref[...]).astype(ka_ref.dtype)
    kb_ref[...] = _dot(act, w2b_ref[...]).astype(kb_ref.dtype)


def _compress(kv, pe_k, pe_v, k_w1, k_w2, v_w1, v_w2, batch, seq):
    assert CMP_LEN == 2 * CMP_STRIDE
    hid = k_w1.shape[1]
    n = seq // CMP_STRIDE
    w1 = jnp.zeros((CMP_LEN, LANES, 2 * hid), F32)
    w1 = w1.at[:, :HEAD_DIM, :hid].set(k_w1.reshape(CMP_LEN, HEAD_DIM, hid))
    w1 = w1.at[:, HEAD_DIM:, hid:].set(v_w1.reshape(CMP_LEN, HEAD_DIM, hid)).astype(BF16)
    zero = jnp.zeros((hid, HEAD_DIM), F32)
    w2a = jnp.block([[k_w2, zero], [zero, v_w2]]).astype(BF16)
    w2b = jnp.block([[zero, k_w2], [v_w2, zero]]).astype(BF16)
    pe = jnp.concatenate([pe_k, pe_v], axis=-1)
    g = NSA_GROUPS
    out = jax.ShapeDtypeStruct((batch * g * n, LANES), BF16)
    return pl.pallas_call(
        _compress_kernel,
        name="compress",
        grid=(batch, g),
        in_specs=[pl.BlockSpec((seq, LANES), lambda b, gi: (b, gi)),
                  pl.BlockSpec((CMP_LEN, LANES), lambda b, gi: (0, 0)),
                  pl.BlockSpec((CMP_LEN, LANES, 2 * hid), lambda b, gi: (0, 0, 0)),
                  pl.BlockSpec((2 * hid, LANES), lambda b, gi: (0, 0)),
                  pl.BlockSpec((2 * hid, LANES), lambda b, gi: (0, 0))],
        out_specs=[pl.BlockSpec((n, LANES), lambda b, gi: (b * g + gi, 0)),
                   pl.BlockSpec((n, LANES), lambda b, gi: (b * g + gi, 0))],
        out_shape=[out, out],
        scratch_shapes=[pltpu.VMEM((seq, LANES), F32), pltpu.VMEM((n + 8, 2 * hid), F32)],
        compiler_params=_cparams("parallel", "parallel"),
    )(kv, pe, w1, w2a, w2b)


def _nsa_cmp_kernel(n_slc, q_ref, ka_ref, kb_ref, ovt_ref, ocmp_ref, selm_ref):
    qi = pl.program_id(2)
    tq = q_ref.shape[0]
    n_pad = ka_ref.shape[0]
    t_q = qi * tq + lax.broadcasted_iota(jnp.int32, (tq, n_pad), 0)
    n_c = lax.broadcasted_iota(jnp.int32, (tq, n_pad), 1)
    mask_c = (n_c * CMP_STRIDE + (CMP_LEN - 1)) <= t_q
    halves = _head_halves(tq)
    ka = ka_ref[...]
    kb = kb_ref[...]
    psum = jnp.zeros((tq, n_pad), F32)
    for pr in range(NSA_HPG // 2):
        qf = q_ref[:, pr * LANES:(pr + 1) * LANES].astype(F32)
        outs = []
        for h in range(2):
            qz = jnp.where(halves[h], qf, 0.0).astype(BF16)
            kmat, vmat = (ka, kb) if h == 0 else (kb, ka)
            s = jnp.where(mask_c, _dot_nt(qz, kmat), NEG_INF)
            m = jnp.max(s, axis=1, keepdims=True)
            m = jnp.where(m == NEG_INF, 0.0, m)
            e = jnp.exp(s - m)
            p = e / jnp.maximum(jnp.sum(e, axis=1, keepdims=True), 1e-30)
            psum = psum + p
            outs.append(_dot(p.astype(BF16), vmat))
        ocmp_ref[:, pr * LANES:(pr + 1) * LANES] = jnp.where(halves[0], outs[0], outs[1]).astype(ocmp_ref.dtype)

    ps_hi, ps_lo = _split_bf16(psum)
    ovt = ovt_ref[...]
    p_slc = (_dot_nt(ovt, ps_hi) + _dot_nt(ovt, ps_lo))[0:n_slc]
    cand = lax.broadcasted_iota(jnp.int32, (n_slc, tq), 0)
    t_blk = jnp.right_shift(qi * tq + lax.broadcasted_iota(jnp.int32, (n_slc, tq), 1), int(math.log2(SLC_BLOCK)))
    valid = cand <= t_blk
    forced = (cand == 0) | (cand == t_blk) | (cand == t_blk - 1)
    score = jnp.where(valid, jnp.where(forced, FORCE_SCORE, p_slc), -1.0)
    keep = _rank_select(score, valid, n_slc, min(SLC_TOPN, n_slc))
    neg = jnp.where(keep, 0.0, MASK_NEG)
    zero = jnp.zeros((HEAD_DIM - n_slc, tq), F32)
    selm_ref[...] = jnp.concatenate([neg, zero, neg, zero], axis=0).T.astype(selm_ref.dtype)


def _nsa_sw_tiles(tiles, q_ref, selm_ref, gate_ref, ex_ref, ocmp_ref, o_ref, ksel_ref, vsel_ref, kwin_ref, vwin_ref):
    tq = ATT_TILE
    halves = _head_halves(tq)
    causal, beyond = _tile_masks(tq)
    ex = ex_ref[...]
    sel_s, win_s = {}, {}
    for c in tiles:
        rows = slice(c * tq, (c + 1) * tq)
        w = (c + 1) * tq
        lo = max(c - WINDOW // tq, 0) * tq
        qf = q_ref[rows, :].astype(F32)
        selm = selm_ref[rows, :].astype(F32)
        for h in range(2):
            q_aug = jnp.where(halves[h], qf, selm).astype(BF16)
            qz = jnp.where(halves[h], qf, 0.0).astype(BF16)
            sel_s[c, h] = _mask_tile(_dot_nt(q_aug, ksel_ref[h, 0:w, :]), c * tq, causal)
            s = _mask_tile(_dot_nt(qz, kwin_ref[h, lo:w, :]), c * tq - lo, causal)
            if c * tq - lo == WINDOW:
                s = _mask_tile(s, 0, beyond)
            win_s[c, h] = s
    for c in tiles:
        rows = slice(c * tq, (c + 1) * tq)
        w = (c + 1) * tq
        lo = max(c - WINDOW // tq, 0) * tq
        win_o = [_softmax_pv(win_s[c, h], vwin_ref[h, lo:w, :], h) for h in range(2)]
        sel_o = [_softmax_pv(sel_s[c, h], vsel_ref[h, 0:w, :], h) for h in range(2)]
        o_sel = jnp.where(halves[0], sel_o[0], sel_o[1])
        o_win = jnp.where(halves[0], win_o[0], win_o[1])
        g_hi, g_lo = _split_bf16(gate_ref[rows, :])
        gexp = _dot(g_hi, ex) + _dot(g_lo, ex)
        y = (gexp[:, 0:LANES] * ocmp_ref[rows, :].astype(F32) + gexp[:, LANES:2 * LANES] * o_sel
             + gexp[:, 2 * LANES:3 * LANES] * o_win)
        o_ref[rows, :] = y.astype(o_ref.dtype)


def _nsa_sw_kernel(q_ref, selm_ref, kvs_ref, kvw_ref, gate_ref, ex_ref, ocmp_ref, o_ref,
                   ksel_ref, vsel_ref, kwin_ref, vwin_ref):
    pr = pl.program_id(2)
    qi = pl.program_id(3)
    seq = kvs_ref.shape[0]

    @pl.when((pr == 0) & (qi == 0))
    def _():
        first, second = _head_halves(seq)
        onehot = _block_onehot(seq, int(math.log2(SLC_BLOCK)))
        kv = kvs_ref[...].astype(F32)
        swapped = pltpu.roll(kv, HEAD_DIM, axis=1)
        ksel_ref[0] = jnp.where(first, kv, onehot).astype(BF16)
        ksel_ref[1] = jnp.where(second, swapped, onehot).astype(BF16)
        vsel_ref[0] = _with_ones(swapped, first)
        vsel_ref[1] = _with_ones(kv, second)
        kv = kvw_ref[...].astype(F32)
        swapped = pltpu.roll(kv, HEAD_DIM, axis=1)
        kwin_ref[0] = kvw_ref[...]
        kwin_ref[1] = swapped.astype(BF16)
        vwin_ref[0] = _with_ones(swapped, first)
        vwin_ref[1] = _with_ones(kv, second)

    nq = seq // ATT_TILE
    for j in range(nq // 2):
        pl.when(qi == j)(functools.partial(_nsa_sw_tiles, (nq - 1 - j, j), q_ref, selm_ref, gate_ref, ex_ref,
                                           ocmp_ref, o_ref, ksel_ref, vsel_ref, kwin_ref, vwin_ref))


def _nsa_attention(q_raw, q_rot, kv, ka, kb, gates, batch, seq):
    nq = seq // ATT_TILE
    g = NSA_GROUPS
    gw = NSA_HPG * HEAD_DIM
    n_pairs = NSA_HEADS // 2
    n_cmp = (seq - CMP_LEN) // CMP_STRIDE + 1
    n_cmp_pad = seq // CMP_STRIDE
    n_slc = seq // SLC_BLOCK
    assert n_cmp_pad == LANES and n_slc <= HEAD_DIM // 2 and WINDOW % ATT_TILE == 0

    ex = np.zeros((n_pairs, LANES, 3 * LANES), np.float32)
    for hd in range(NSA_HEADS):
        for br in range(3):
            lane0 = br * LANES + (hd % 2) * HEAD_DIM
            ex[hd // 2, hd * 3 + br, lane0:lane0 + HEAD_DIM] = 1.0
    c_s = np.arange(n_cmp) * CMP_STRIDE
    s_s = np.arange(n_slc) * SLC_BLOCK
    ov = np.clip(np.minimum(c_s[:, None] + CMP_LEN, s_s[None, :] + SLC_BLOCK)
                 - np.maximum(c_s[:, None], s_s[None, :]), 0, None) / CMP_LEN
    ovt = np.zeros((LANES, n_cmp_pad), np.float32)
    ovt[:n_slc, :n_cmp] = ov.T

    n_ct = seq // CMP_TILE
    o_cmp, selm = pl.pallas_call(
        functools.partial(_nsa_cmp_kernel, n_slc),
        name="nsa_cmp",
        grid=(batch, g, n_ct),
        in_specs=[pl.BlockSpec((CMP_TILE, gw), lambda b, gi, i: (b * n_ct + i, gi)),
                  pl.BlockSpec((n_cmp_pad, LANES), lambda b, gi, i: (b * g + gi, 0)),
                  pl.BlockSpec((n_cmp_pad, LANES), lambda b, gi, i: (b * g + gi, 0)),
                  pl.BlockSpec((LANES, n_cmp_pad), lambda b, gi, i: (0, 0))],
        out_specs=[pl.BlockSpec((CMP_TILE, gw), lambda b, gi, i: (b * n_ct + i, gi)),
                   pl.BlockSpec((CMP_TILE, LANES), lambda b, gi, i: (b * n_ct + i, gi))],
        out_shape=[jax.ShapeDtypeStruct((batch * seq, NSA_HEADS * HEAD_DIM), BF16),
                   jax.ShapeDtypeStruct((batch * seq, g * LANES), BF16)],
        compiler_params=_cparams("parallel", "parallel", "parallel"),
    )(q_raw, ka, kb, jnp.asarray(ovt, BF16))

    ppg = NSA_HPG // 2
    return pl.pallas_call(
        _nsa_sw_kernel,
        name="nsa_sw",
        grid=(batch, g, ppg, nq // 2),
        in_specs=[pl.BlockSpec((seq, LANES), lambda b, gi, p, i: (b, gi * ppg + p)),
                  pl.BlockSpec((seq, LANES), lambda b, gi, p, i: (b, gi)),
                  pl.BlockSpec((seq, LANES), lambda b, gi, p, i: (b, g + gi)),
                  pl.BlockSpec((seq, LANES), lambda b, gi, p, i: (b, 2 * g + gi)),
                  pl.BlockSpec((seq, LANES), lambda b, gi, p, i: (b, 0)),
                  pl.BlockSpec((None, LANES, 3 * LANES), lambda b, gi, p, i: (gi * ppg + p, 0, 0)),
                  pl.BlockSpec((seq, LANES), lambda b, gi, p, i: (b, gi * ppg + p))],
        out_specs=pl.BlockSpec((seq, LANES), lambda b, gi, p, i: (b, gi * ppg + p)),
        out_shape=jax.ShapeDtypeStruct((batch * seq, NSA_HEADS * HEAD_DIM), BF16),
        scratch_shapes=[pltpu.VMEM((2, seq, LANES), BF16) for _ in range(4)],
        compiler_params=_cparams("parallel", "parallel", "arbitrary", "arbitrary"),
    )(q_rot, selm, kv, kv, gates, jnp.asarray(ex, BF16), o_cmp)


def _even_mixer(x, g, w_in, pool_w, pool_scale, w_out, rope_pair, batch, seq):
    chunk = 2 * LANES
    plan = [(0, POOL_DIM, [(0, 0, "none", 0)])]
    for c in range(3 * MOBA_DIM // chunk):
        op = "rope_scale" if c < MOBA_DIM // chunk else ("rope" if c < 2 * MOBA_DIM // chunk else "none")
        plan.append((POOL_DIM + c * chunk, chunk, [(1, c * chunk, op, 0)]))
    u_pool, qkv = _norm_project(x, g, w_in.astype(BF16), plan, [rope_pair],
                                [(POOL_DIM, F32), (3 * MOBA_DIM, BF16)], seq)
    w_bd = jax.scipy.linalg.block_diag(*[pool_w[i] for i in range(pool_w.shape[0])]).astype(BF16)
    y_a = _pool_mixer(u_pool, w_bd, pool_scale, batch, seq)
    y_b = _moba_attention(qkv, batch, seq)
    w_out = w_out.astype(BF16)
    return _out_proj(x, [y_a, y_b], [w_out[:POOL_DIM], w_out[POOL_DIM:]])


def _odd_mixer(x, g, w_in, pe_k, pe_v, k_w1, k_w2, v_w1, v_w2, w_out, rope_pair, rope_single, batch, seq):
    d = x.shape[1]
    qd = NSA_HEADS * HEAD_DIM
    kvd = NSA_KV_DIM
    wq = w_in[:, :qd]
    parts = [w_in[:, qd + i * kvd:qd + (i + 1) * kvd].reshape(d, NSA_GROUPS, HEAD_DIM) for i in range(6)]
    pairs = [jnp.concatenate([parts[2 * i], parts[2 * i + 1]], axis=-1).reshape(d, 2 * kvd) for i in range(3)]
    n_gate = 3 * NSA_HEADS
    w_gate = jnp.zeros((d, LANES), F32).at[:, :n_gate].set(w_in[:, qd + 6 * kvd:])
    w_all = jnp.concatenate([wq] + pairs + [w_gate], axis=1).astype(BF16)

    chunk = 2 * LANES
    plan = []
    for c in range(qd // chunk):
        plan.append((c * chunk, chunk, [(0, c * chunk, "scale", 0), (1, c * chunk, "rope_scale", 0)]))
    for c in range(6 * kvd // chunk):
        branch = c // (2 * kvd // chunk)
        op = "none" if branch == 0 else "rope"
        plan.append((qd + c * chunk, chunk, [(2, c * chunk, op, 1)]))
    plan.append((qd + 6 * kvd, LANES, [(3, 0, "sigmoid", 0)]))
    q_raw, q_rot, kv, gates = _norm_project(
        x, g, w_all, plan, [rope_pair, rope_single],
        [(qd, BF16), (qd, BF16), (6 * kvd, BF16), (LANES, F32)], seq)

    ka, kb = _compress(kv, pe_k, pe_v, k_w1, k_w2, v_w1, v_w2, batch, seq)
    y = _nsa_attention(q_raw, q_rot, kv, ka, kb, gates, batch, seq)
    return _out_proj(x, [y], [w_out.astype(BF16)])


def kernel(x, mem, positions, norm_g, mem_g, final_g, w_in_ab, pool_w, pool_scale, w_out_ab, ffn_w_gate, ffn_w_up, ffn_w_down, w_in_c, cmp_pe_k, cmp_pe_v, cmp_k_w1, cmp_k_w2, cmp_v_w1, cmp_v_w2, w_out_c, router_w, moe_w_gate, moe_w_up, moe_w_down, xa_wq, xa_wkv, xa_wo):
    batch, seq, d = x.shape
    mem_len = mem.shape[1]
    depth = norm_g.shape[0]
    rope_pair, rope_single = _rope_tables(positions)

    wkv_all = jnp.concatenate([xa_wkv[l] for l in range(depth)], axis=1).astype(BF16)
    kv_all = _norm_matmul(mem.reshape(batch * mem_len, d), mem_g, wkv_all)

    xf = x.reshape(batch * seq, d)
    for layer in range(depth):
        i = layer // 2
        if layer % 2 == 0:
            xf = _even_mixer(xf, norm_g[layer, 0], w_in_ab[i], pool_w[i], pool_scale[i], w_out_ab[i],
                             rope_pair, batch, seq)
        else:
            xf = _odd_mixer(xf, norm_g[layer, 0], w_in_c[i], cmp_pe_k[i], cmp_pe_v[i], cmp_k_w1[i], cmp_k_w2[i],
                            cmp_v_w1[i], cmp_v_w2[i], w_out_c[i], rope_pair, rope_single, batch, seq)
        xf = _cross_attention(xf, norm_g[layer, 1], xa_wq[layer].astype(BF16), kv_all, layer,
                              xa_wo[layer].astype(BF16), seq, mem_len)
        if layer % 2 == 0:
            xf = _ffn(xf, norm_g[layer, 2], ffn_w_gate[i].astype(BF16), ffn_w_up[i].astype(BF16),
                      ffn_w_down[i].astype(BF16))
        else:
            xf = _moe(xf, norm_g[layer, 2], router_w[i], moe_w_gate, moe_w_up, moe_w_down, i)
    return _final_norm(xf, final_g).reshape(batch, seq, d)
```

```python
import functools
import math

import numpy as np
import jax
import jax.numpy as jnp
from jax import lax
from jax.experimental import pallas as pl
from jax.experimental.pallas import tpu as pltpu

F32 = jnp.float32
BF16 = jnp.bfloat16

HEAD_DIM = 64
ROPE_DIM = 16
ROPE_THETA = 500000.0
NORM_EPS = 1e-5
QK_SCALE = HEAD_DIM ** -0.5
QK_SCALE_LOG2E = QK_SCALE * math.log2(math.e)

POOL_WINDOWS = (2, 4, 8, 16)
POOL_DIM = 256
MOBA_HEADS = 12
MOBA_DIM = MOBA_HEADS * HEAD_DIM
MOBA_BLOCK = 256
MOBA_TOPK = 3
NSA_HEADS = 16
NSA_GROUPS = 4
NSA_HPG = 4
NSA_KV_DIM = NSA_GROUPS * HEAD_DIM
CMP_LEN = 32
CMP_STRIDE = 16
SLC_BLOCK = 64
SLC_TOPN = 16
WINDOW = 512
FORCE_SCORE = 1e4
XA_HEADS = 4
XA_DIM = XA_HEADS * HEAD_DIM
N_EXPERTS = 8

LANES = 128
ATT_TILE = 256
CMP_TILE = 1024
TILES_PER_STEP = 4
VMEM_LIMIT = 56 * 1024 * 1024
NEG_INF = float("-inf")
MASK_NEG = -1e30


def _cparams(*sem):
    return pltpu.CompilerParams(dimension_semantics=sem, vmem_limit_bytes=VMEM_LIMIT)


def _dot(a, b):
    return jnp.dot(a, b, preferred_element_type=F32)


def _dot_nt(a, b):
    return lax.dot_general(a, b, (((1,), (1,)), ((), ())), preferred_element_type=F32)


def _split_bf16(a):
    hi = a.astype(BF16)
    lo = (a - hi.astype(F32)).astype(BF16)
    return hi, lo


def _rms_bf16(x, g):
    ms = jnp.mean(x * x, axis=-1, keepdims=True)
    return (x * lax.rsqrt(ms + NORM_EPS) * g).astype(BF16)


def _apply_rope(acc, c, a, b):
    half = ROPE_DIM // 2
    return acc * c + pltpu.roll(acc, LANES - half, axis=1) * a + pltpu.roll(acc, half, axis=1) * b


def _proj_kernel(plan, n_tab, x_ref, g_ref, w_ref, *rest):
    tabs = rest[:3 * n_tab]
    outs = rest[3 * n_tab:]
    h = _rms_bf16(x_ref[...], g_ref[...])
    for wc0, width, sinks in plan:
        acc = _dot(h, w_ref[:, wc0:wc0 + width])
        for oi, oc0, op, tab in sinks:
            for s in range(width // LANES):
                val = acc[:, s * LANES:(s + 1) * LANES]
                if op in ("rope", "rope_scale"):
                    c, a, b = (tabs[3 * tab + i][...] for i in range(3))
                    val = _apply_rope(val, c, a, b)
                if op == "scale":
                    val = val * QK_SCALE
                if op == "rope_scale":
                    val = val * QK_SCALE_LOG2E
                if op == "sigmoid":
                    val = jax.nn.sigmoid(val)
                o = outs[oi]
                o[:, oc0 + s * LANES:oc0 + (s + 1) * LANES] = val.astype(o.dtype)


def _norm_project(x, g, w, plan, tables, out_defs, seq, tm=512):
    t, d = x.shape
    n = w.shape[1]
    n_seq_tiles = seq // tm
    in_specs = [
        pl.BlockSpec((tm, d), lambda i: (i, 0)),
        pl.BlockSpec((1, d), lambda i: (0, 0)),
        pl.BlockSpec((d, n), lambda i: (0, 0)),
    ]
    flat_tabs = []
    for tset in tables:
        for tb in tset:
            flat_tabs.append(tb)
            in_specs.append(pl.BlockSpec((tm, LANES), lambda i: (i % n_seq_tiles, 0)))
    out_shape = [jax.ShapeDtypeStruct((t, wd), dt) for wd, dt in out_defs]
    out_specs = [pl.BlockSpec((tm, wd), lambda i: (i, 0)) for wd, _ in out_defs]
    return pl.pallas_call(
        functools.partial(_proj_kernel, plan, len(tables)),
        name="proj",
        grid=(t // tm,),
        in_specs=in_specs,
        out_specs=out_specs,
        out_shape=out_shape,
        compiler_params=_cparams("parallel"),
    )(x, g.reshape(1, d), w, *flat_tabs)


def _rope_tables(positions):
    half = ROPE_DIM // 2
    inv = ROPE_THETA ** (-jnp.arange(0, ROPE_DIM, 2, dtype=F32) / ROPE_DIM)
    ang = positions.astype(F32)[:, None] * inv[None, :]
    cos, sin = jnp.cos(ang), jnp.sin(ang)
    s = positions.shape[0]
    z8 = jnp.zeros((s, half), F32)
    rest0 = jnp.zeros((s, HEAD_DIM - ROPE_DIM), F32)
    rest1 = jnp.ones((s, HEAD_DIM - ROPE_DIM), F32)
    c64 = jnp.concatenate([cos, cos, rest1], -1)
    a64 = jnp.concatenate([-sin, z8, rest0], -1)
    b64 = jnp.concatenate([z8, sin, rest0], -1)
    one64 = jnp.ones((s, HEAD_DIM), F32)
    zero64 = jnp.zeros((s, HEAD_DIM), F32)
    pair = tuple(jnp.concatenate([m, m], -1) for m in (c64, a64, b64))
    single = (jnp.concatenate([c64, one64], -1), jnp.concatenate([a64, zero64], -1),
              jnp.concatenate([b64, zero64], -1))
    return pair, single


def _norm_matmul_kernel(x_ref, g_ref, w_ref, o_ref):
    h = _rms_bf16(x_ref[...], g_ref[...])
    o_ref[...] = _dot(h, w_ref[...]).astype(o_ref.dtype)


def _norm_matmul(x, g, w, tm=512):
    t, d = x.shape
    n = w.shape[1]
    return pl.pallas_call(
        _norm_matmul_kernel,
        name="mem_kv",
        grid=(t // tm,),
        in_specs=[pl.BlockSpec((tm, d), lambda i: (i, 0)),
                  pl.BlockSpec((1, d), lambda i: (0, 0)),
                  pl.BlockSpec((d, n), lambda i: (0, 0))],
        out_specs=pl.BlockSpec((tm, n), lambda i: (i, 0)),
        out_shape=jax.ShapeDtypeStruct((t, n), BF16),
        compiler_params=_cparams("parallel"),
    )(x, g.reshape(1, d), w)


def _final_norm_kernel(x_ref, g_ref, o_ref):
    x = x_ref[...]
    ms = jnp.mean(x * x, axis=-1, keepdims=True)
    o_ref[...] = x * lax.rsqrt(ms + NORM_EPS) * g_ref[...]


def _final_norm(x, g, tm=1024):
    t, d = x.shape
    return pl.pallas_call(
        _final_norm_kernel,
        name="final_norm",
        grid=(t // tm,),
        in_specs=[pl.BlockSpec((tm, d), lambda i: (i, 0)), pl.BlockSpec((1, d), lambda i: (0, 0))],
        out_specs=pl.BlockSpec((tm, d), lambda i: (i, 0)),
        out_shape=jax.ShapeDtypeStruct((t, d), F32),
        compiler_params=_cparams("parallel"),
    )(x, g.reshape(1, d))


def _pool_kernel(u_ref, w_ref, sc_ref, o_ref, pad_ref):
    s = u_ref.shape[0]
    maxw = POOL_WINDOWS[-1]
    u = u_ref[...]
    pad_ref[0:maxw, :] = jnp.zeros((maxw, POOL_DIM), F32)
    t1 = (lax.broadcasted_iota(jnp.int32, (s, POOL_DIM), 0) + 1).astype(F32)
    lane = lax.broadcasted_iota(jnp.int32, (s, POOL_DIM), 1)
    gdim = POOL_DIM // len(POOL_WINDOWS)
    cur = u
    pooled = jnp.zeros_like(u)
    shift = 1
    for gi, w in enumerate(POOL_WINDOWS):
        pad_ref[maxw:maxw + s, :] = cur
        cur = cur + pad_ref[maxw - shift:maxw - shift + s, :]
        shift *= 2
        assert shift == w
        mean = cur / jnp.minimum(t1, float(w))
        pooled = jnp.where((lane >= gi * gdim) & (lane < (gi + 1) * gdim), mean, pooled)
    pooled = (pooled - u).astype(BF16)
    o_ref[...] = (_dot(pooled, w_ref[...]) * sc_ref[...]).astype(o_ref.dtype)


def _pool_mixer(u_pool, w_bd, scale, batch, seq):
    return pl.pallas_call(
        _pool_kernel,
        name="pool",
        grid=(batch,),
        in_specs=[pl.BlockSpec((seq, POOL_DIM), lambda b: (b, 0)),
                  pl.BlockSpec((POOL_DIM, POOL_DIM), lambda b: (0, 0)),
                  pl.BlockSpec((1, POOL_DIM), lambda b: (0, 0))],
        out_specs=pl.BlockSpec((seq, POOL_DIM), lambda b: (b, 0)),
        out_shape=jax.ShapeDtypeStruct((batch * seq, POOL_DIM), BF16),
        scratch_shapes=[pltpu.VMEM((seq + POOL_WINDOWS[-1], POOL_DIM), F32)],
        compiler_params=_cparams("parallel"),
    )(u_pool, w_bd, scale.reshape(1, POOL_DIM))


def _rank_select(scoreT, valid, n_rows, topn):
    row = lax.broadcasted_iota(jnp.int32, scoreT.shape, 0)
    rank = jnp.zeros(scoreT.shape, F32)
    for j in range(n_rows):
        sj = scoreT[j:j + 1, :]
        beats = (scoreT > sj) | ((scoreT == sj) & (row < j))
        rj = jnp.sum(beats.astype(F32), axis=0, keepdims=True)
        rank = jnp.where(row == j, rj, rank)
    return (rank < float(topn)) & valid


def _head_halves(rows):
    lane = lax.broadcasted_iota(jnp.int32, (rows, LANES), 1)
    return lane < HEAD_DIM, lane >= HEAD_DIM


def _tile_masks(tq):
    qry = lax.broadcasted_iota(jnp.int32, (tq, tq), 0)
    key = lax.broadcasted_iota(jnp.int32, (tq, tq), 1)
    return key <= qry, key > qry


def _mask_tile(s, t0, mask):
    tq = mask.shape[0]
    parts = []
    if t0 > 0:
        parts.append(s[:, :t0])
    parts.append(jnp.where(mask, s[:, t0:t0 + tq], NEG_INF))
    if t0 + tq < s.shape[1]:
        parts.append(s[:, t0 + tq:])
    return parts[0] if len(parts) == 1 else jnp.concatenate(parts, axis=1)


def _softmax_pv(s, v, h):
    m = jnp.max(s, axis=1, keepdims=True)
    p = jnp.exp2((s - m).astype(BF16))
    o = _dot(p, v)
    denom_lane = (1 - h) * HEAD_DIM
    return o / o[:, denom_lane:denom_lane + 1]


def _with_ones(v, own):
    return jnp.where(own, v, 1.0).astype(BF16)


def _step_tiles(j, nq):
    half = TILES_PER_STEP // 2
    return tuple(t for k in range(half) for t in (nq - 1 - (j * half + k), j * half + k))


def _block_onehot(seq, shift):
    lane = lax.broadcasted_iota(jnp.int32, (seq, LANES), 1)
    blk = jnp.right_shift(lax.broadcasted_iota(jnp.int32, (seq, LANES), 0), shift)
    return jnp.where((lane & (HEAD_DIM - 1)) == blk, 1.0, 0.0)


def _moba_tiles(tiles, o_ref, qaug_ref, kaug_ref, vaug_ref):
    tq = ATT_TILE
    halves = _head_halves(tq)
    causal, _ = _tile_masks(tq)
    scores = {}
    for c in tiles:
        for h in range(2):
            s = _dot_nt(qaug_ref[h, c * tq:(c + 1) * tq, :], kaug_ref[h, 0:(c + 1) * tq, :])
            scores[c, h] = _mask_tile(s, c * tq, causal)
    for c in tiles:
        outs = [_softmax_pv(scores[c, h], vaug_ref[h, 0:(c + 1) * tq, :], h) for h in range(2)]
        o_ref[c * tq:(c + 1) * tq, :] = jnp.where(halves[0], outs[0], outs[1]).astype(o_ref.dtype)


def _moba_kernel(q_ref, k_ref, v_ref, o_ref, qaug_ref, kaug_ref, vaug_ref, kmean_ref):
    qi = pl.program_id(2)
    seq = k_ref.shape[0]
    nb = seq // MOBA_BLOCK
    blk_shift = int(math.log2(MOBA_BLOCK))

    @pl.when(qi == 0)
    def _():
        halves = _head_halves(seq)
        kf = k_ref[...].astype(F32)
        vf = v_ref[...].astype(F32)
        qp = q_ref[...]
        qf = qp.astype(F32)
        onehot = _block_onehot(seq, blk_shift)
        kmean_ref[...] = jnp.zeros(kmean_ref.shape, F32)
        for j in range(nb):
            kmean_ref[j:j + 1, :] = jnp.mean(kf[j * MOBA_BLOCK:(j + 1) * MOBA_BLOCK], axis=0, keepdims=True)
        cand = lax.broadcasted_iota(jnp.int32, (nb, seq), 0)
        own = jnp.right_shift(lax.broadcasted_iota(jnp.int32, (nb, seq), 1), blk_shift)
        valid = cand < own
        for h in range(2):
            kaug_ref[h] = jnp.where(halves[h], kf, onehot).astype(BF16)
            vaug_ref[h] = _with_ones(vf, halves[h])
            mine_k = _head_halves(kmean_ref.shape[0])[h]
            km_hi, km_lo = _split_bf16(jnp.where(mine_k, kmean_ref[...], 0.0))
            gate = (_dot_nt(km_hi, qp) + _dot_nt(km_lo, qp))[0:nb]
            gate = jnp.where(valid, gate, NEG_INF)
            keep = _rank_select(gate, valid, nb, MOBA_TOPK) | (cand == own)
            neg = jnp.where(keep, 0.0, MASK_NEG)
            lead = HEAD_DIM if h == 0 else 0
            pieces = [neg, jnp.zeros((LANES - lead - nb, seq), F32)]
            if lead:
                pieces.insert(0, jnp.zeros((lead, seq), F32))
            maskcols = jnp.concatenate(pieces, axis=0).T
            qaug_ref[h] = jnp.where(halves[h], qf, maskcols).astype(BF16)

    for j in range(nb // TILES_PER_STEP):
        pl.when(qi == j)(functools.partial(_moba_tiles, _step_tiles(j, nb), o_ref, qaug_ref, kaug_ref, vaug_ref))


def _moba_attention(qkv, batch, seq):
    nq = seq // ATT_TILE
    assert MOBA_BLOCK == ATT_TILE and nq <= 16 and nq % TILES_PER_STEP == 0
    hp = MOBA_DIM // LANES
    return pl.pallas_call(
        _moba_kernel,
        name="moba",
        grid=(batch, hp, nq // TILES_PER_STEP),
        in_specs=[pl.BlockSpec((seq, LANES), lambda b, p, i: (b, p)),
                  pl.BlockSpec((seq, LANES), lambda b, p, i: (b, hp + p)),
                  pl.BlockSpec((seq, LANES), lambda b, p, i: (b, 2 * hp + p))],
        out_specs=pl.BlockSpec((seq, LANES), lambda b, p, i: (b, p)),
        out_shape=jax.ShapeDtypeStruct((batch * seq, MOBA_DIM), BF16),
        scratch_shapes=[pltpu.VMEM((2, seq, LANES), BF16), pltpu.VMEM((2, seq, LANES), BF16),
                        pltpu.VMEM((2, seq, LANES), BF16), pltpu.VMEM((16, LANES), F32)],
        compiler_params=_cparams("parallel", "parallel", "arbitrary"),
    )(qkv, qkv, qkv)


def _out_proj_kernel(n_in, x_ref, *rest):
    ys = rest[:n_in]
    ws = rest[n_in:2 * n_in]
    o_ref = rest[2 * n_in]
    acc = x_ref[...]
    for y, w in zip(ys, ws):
        acc = acc + _dot(y[...], w[...])
    o_ref[...] = acc


def _out_proj(x, ys, ws, tm=512):
    t, d = x.shape
    in_specs = [pl.BlockSpec((tm, d), lambda i: (i, 0))]
    in_specs += [pl.BlockSpec((tm, y.shape[1]), lambda i: (i, 0)) for y in ys]
    in_specs += [pl.BlockSpec(w.shape, lambda i: (0, 0)) for w in ws]
    return pl.pallas_call(
        functools.partial(_out_proj_kernel, len(ys)),
        name="out_proj",
        grid=(t // tm,),
        in_specs=in_specs,
        out_specs=pl.BlockSpec((tm, d), lambda i: (i, 0)),
        out_shape=jax.ShapeDtypeStruct((t, d), F32),
        compiler_params=_cparams("parallel"),
    )(x, *ys, *ws)


def _mid_kernel(with_route, x_ref, g_ref, wq_ref, kv_ref, wo_ref, *rest):
    o_ref = rest[3] if with_route else rest[0]
    x = x_ref[...]
    h = _rms_bf16(x, g_ref[...])
    q = _dot(h, wq_ref[...]) * QK_SCALE_LOG2E
    halves = _head_halves(x.shape[0])
    mem_halves = _head_halves(kv_ref.shape[0])
    out = x
    for pr in range(XA_HEADS // 2):
        lanes = slice(pr * LANES, (pr + 1) * LANES)
        qf = q[:, lanes]
        kp = kv_ref[:, lanes]
        vf = kv_ref[:, XA_DIM + pr * LANES:XA_DIM + (pr + 1) * LANES].astype(F32)
        scores = [_dot_nt(jnp.where(halves[hh], qf, 0.0).astype(BF16), kp) for hh in range(2)]
        outs = [_softmax_pv(scores[hh], _with_ones(vf, mem_halves[hh]), hh) for hh in range(2)]
        o_pair = jnp.where(halves[0], outs[0], outs[1]).astype(BF16)
        out = out + _dot(o_pair, wo_ref[lanes, :])
    o_ref[...] = out

    if with_route:
        g3_ref, rhi_ref, rlo_ref, _, route_ref, cnt_ref = rest

        @pl.when(pl.program_id(0) == 0)
        def _():
            cnt_ref[...] = jnp.zeros(cnt_ref.shape, F32)

        route_ref[...] = _route_rows(_rms_bf16(out, g3_ref[...]), rhi_ref[...], rlo_ref[...], cnt_ref)


def _mid_layer(x, g, wq, kv_all, layer, wo, seq, mem_len, router=None, tm=512):
    t, d = x.shape
    tiles_per_seq = seq // tm
    const = lambda i: (0, 0)
    rows = lambda i: (i, 0)
    in_specs = [pl.BlockSpec((tm, d), rows),
                pl.BlockSpec((1, d), const),
                pl.BlockSpec((d, XA_DIM), const),
                pl.BlockSpec((mem_len, 2 * XA_DIM), lambda i: (i // tiles_per_seq, layer)),
                pl.BlockSpec((XA_DIM, d), const)]
    args = [x, g.reshape(1, d), wq, kv_all, wo]
    out_specs = [pl.BlockSpec((tm, d), rows)]
    out_shape = [jax.ShapeDtypeStruct((t, d), F32)]
    if router is not None:
        g3, router_w = router
        rw = jnp.zeros((d, LANES), F32).at[:, :router_w.shape[1]].set(router_w)
        rhi = rw.astype(BF16)
        rlo = (rw - rhi.astype(F32)).astype(BF16)
        in_specs += [pl.BlockSpec((1, d), const), pl.BlockSpec((d, LANES), const), pl.BlockSpec((d, LANES), const)]
        args += [g3.reshape(1, d), rhi, rlo]
        out_specs += [pl.BlockSpec((tm, LANES), rows), pl.BlockSpec((8, LANES), const)]
        out_shape += [jax.ShapeDtypeStruct((t, LANES), F32), jax.ShapeDtypeStruct((8, LANES), F32)]
    res = pl.pallas_call(
        functools.partial(_mid_kernel, router is not None),
        name="mid",
        grid=(t // tm,),
        in_specs=in_specs,
        out_specs=out_specs,
        out_shape=out_shape,
        compiler_params=_cparams("arbitrary"),
    )(*args)
    return res if router is not None else res[0]


def _ffn_kernel(x_ref, g_ref, wg_ref, wu_ref, wd_ref, o_ref, h_ref):
    f = pl.program_id(1)

    @pl.when(f == 0)
    def _():
        x = x_ref[...]
        h_ref[...] = _rms_bf16(x, g_ref[...])
        o_ref[...] = x

    h = h_ref[...]
    gate = _dot(h, wg_ref[...])
    a = (gate * jax.nn.sigmoid(gate)) * _dot(h, wu_ref[...])
    o_ref[...] += _dot(a.astype(BF16), wd_ref[...])


def _ffn(x, g, wg, wu, wd, tm=1024, tf=256):
    t, d = x.shape
    ff = wg.shape[1]
    return pl.pallas_call(
        _ffn_kernel,
        name="ffn",
        grid=(t // tm, ff // tf),
        in_specs=[pl.BlockSpec((tm, d), lambda i, f: (i, 0)),
                  pl.BlockSpec((1, d), lambda i, f: (0, 0)),
                  pl.BlockSpec((d, tf), lambda i, f: (0, f)),
                  pl.BlockSpec((d, tf), lambda i, f: (0, f)),
                  pl.BlockSpec((tf, d), lambda i, f: (f, 0))],
        out_specs=pl.BlockSpec((tm, d), lambda i, f: (i, 0)),
        out_shape=jax.ShapeDtypeStruct((t, d), F32),
        scratch_shapes=[pltpu.VMEM((tm, d), BF16)],
        compiler_params=_cparams("parallel", "arbitrary"),
    )(x, g.reshape(1, d), wg, wu, wd)


ROUTE_I1, ROUTE_I2, ROUTE_W1, ROUTE_W2, ROUTE_R1, ROUTE_R2 = range(6)
MOE_TILE = 1024


def _lane_pick(arr, lane, k):
    return jnp.sum(jnp.where(lane == k, arr, 0.0), axis=1, keepdims=True)


def _route_rows(h, rhi, rlo, cnt_ref):
    tm = h.shape[0]
    lane = lax.broadcasted_iota(jnp.int32, (tm, LANES), 1)
    lane_f = lane.astype(F32)
    logits = _dot(h, rhi) + _dot(h, rlo)
    logits = jnp.where(lane < N_EXPERTS, logits, NEG_INF)
    m1 = jnp.max(logits, axis=1, keepdims=True)
    i1 = jnp.min(jnp.where(logits == m1, lane_f, float(LANES)), axis=1, keepdims=True)
    rest = jnp.where(lane_f == i1, NEG_INF, logits)
    m2 = jnp.max(rest, axis=1, keepdims=True)
    i2 = jnp.min(jnp.where(rest == m2, lane_f, float(LANES)), axis=1, keepdims=True)
    e2 = jnp.exp(m2 - m1)
    w1 = 1.0 / (1.0 + e2)
    w2 = e2 / (1.0 + e2)

    chosen = jnp.where((lane_f == i1) | (lane_f == i2), 1.0, 0.0)
    r = lax.broadcasted_iota(jnp.int32, (tm, tm), 0)
    c = lax.broadcasted_iota(jnp.int32, (tm, tm), 1)
    earlier = jnp.where(c < r, 1.0, 0.0).astype(BF16)
    prefix = _dot(earlier, chosen.astype(BF16)) + cnt_ref[0:1, :]
    rank1 = jnp.sum(jnp.where(lane_f == i1, prefix, 0.0), axis=1, keepdims=True)
    rank2 = jnp.sum(jnp.where(lane_f == i2, prefix, 0.0), axis=1, keepdims=True)
    cnt_ref[0:1, :] = cnt_ref[0:1, :] + jnp.sum(chosen, axis=0, keepdims=True)

    cols = {ROUTE_I1: i1, ROUTE_I2: i2, ROUTE_W1: w1, ROUTE_W2: w2, ROUTE_R1: rank1, ROUTE_R2: rank2}
    route = jnp.zeros((tm, LANES), F32)
    for k, val in cols.items():
        route = jnp.where(lane == k, val, route)
    return route


def _dispatch_kernel(pos1_ref, pos2_ref, x_ref, xs_in, xs_hbm, sem):
    del xs_in
    n = pos1_ref.shape[0]

    def row_copy(r, p):
        return pltpu.make_async_copy(x_ref.at[pl.ds(r, 1)], xs_hbm.at[pl.ds(p, 1)], sem)

    def issue(r, carry):
        row_copy(r, pos1_ref[r]).start()
        row_copy(r, pos2_ref[r]).start()
        return carry

    lax.fori_loop(0, n, issue, 0, unroll=8)
    for _ in range(2):
        pltpu.make_async_copy(x_ref, xs_hbm.at[pl.ds(0, n)], sem).wait()


def _expert_kernel(te_ref, nu_ref, xs_ref, g_ref, wg_ref, wu_ref, wd_ref, y_ref, h_ref):
    del te_ref
    i = pl.program_id(0)
    f = pl.program_id(1)
    used = i < nu_ref[0]

    @pl.when(f == 0)
    def _():
        h_ref[...] = _rms_bf16(xs_ref[...], g_ref[...])
        y_ref[...] = jnp.zeros(y_ref.shape, y_ref.dtype)

    @pl.when(used)
    def _():
        h = h_ref[...]
        gate = _dot(h, wg_ref[...].astype(BF16))
        a = (gate * jax.nn.sigmoid(gate)) * _dot(h, wu_ref[...].astype(BF16))
        y_ref[...] += _dot(a.astype(BF16), wd_ref[...].astype(BF16))


def _combine_kernel(with_norm, pos1_ref, pos2_ref, x_ref, route_ref, g_ref, y_hbm, o_ref, buf1, buf2, sem):
    n = pos1_ref.shape[0]

    def row_copy(p, buf, r):
        return pltpu.make_async_copy(y_hbm.at[pl.ds(p, 1)], buf.at[pl.ds(r, 1)], sem)

    def issue(r, carry):
        row_copy(pos1_ref[r], buf1, r).start()
        row_copy(pos2_ref[r], buf2, r).start()
        return carry

    lax.fori_loop(0, n, issue, 0, unroll=8)
    for buf in (buf1, buf2):
        pltpu.make_async_copy(y_hbm.at[pl.ds(0, n)], buf, sem).wait()
    lane = lax.broadcasted_iota(jnp.int32, route_ref.shape, 1)
    route = route_ref[...]
    w1 = _lane_pick(route, lane, ROUTE_W1)
    w2 = _lane_pick(route, lane, ROUTE_W2)
    out = x_ref[...] + w1 * buf1[...] + w2 * buf2[...]
    if with_norm:
        ms = jnp.mean(out * out, axis=-1, keepdims=True)
        out = out * lax.rsqrt(ms + NORM_EPS) * g_ref[...]
    o_ref[...] = out


def _moe(x, g, route, cnt, wg, wu, wd, li, final_g=None, tf=512, t_disp=512, t_comb=256):
    t, d = x.shape
    _, ne, _, ff = wg.shape
    rows = MOE_TILE
    n_rows = 2 * t + ne * rows
    n_tiles = n_rows // rows
    nf = ff // tf
    g2 = g.reshape(1, d)

    counts = cnt[0, :ne].astype(jnp.int32)
    gsz = (counts + rows - 1) // rows * rows
    ends = jnp.cumsum(gsz)
    offs = ends - gsz
    i1 = route[:, ROUTE_I1].astype(jnp.int32)
    i2 = route[:, ROUTE_I2].astype(jnp.int32)
    pos1 = offs[i1] + route[:, ROUTE_R1].astype(jnp.int32)
    pos2 = offs[i2] + route[:, ROUTE_R2].astype(jnp.int32)
    tile_start = jnp.arange(n_tiles, dtype=jnp.int32) * rows
    tile_expert = jnp.minimum(jnp.sum(tile_start[:, None] >= ends[None, :], axis=1), ne - 1).astype(jnp.int32)
    n_used = (ends[-1] // rows).astype(jnp.int32).reshape(1)

    smem_idx = lambda n: pl.BlockSpec((n,), lambda i: (i,), memory_space=pltpu.SMEM)
    hbm = pl.BlockSpec(memory_space=pl.ANY)
    xs = pl.pallas_call(
        _dispatch_kernel,
        name="moe_dispatch",
        grid=(t // t_disp,),
        in_specs=[smem_idx(t_disp), smem_idx(t_disp), pl.BlockSpec((t_disp, d), lambda i: (i, 0)), hbm],
        out_specs=hbm,
        out_shape=jax.ShapeDtypeStruct((n_rows, d), F32),
        scratch_shapes=[pltpu.SemaphoreType.DMA(())],
        input_output_aliases={3: 0},
        compiler_params=_cparams("arbitrary"),
    )(pos1, pos2, x, jnp.zeros((n_rows, d), F32))

    def w_in_map(i, f, te, nu):
        return li, te[i], 0, jnp.where(i < nu[0], f, nf - 1)

    def w_out_map(i, f, te, nu):
        return li, te[i], jnp.where(i < nu[0], f, nf - 1), 0

    y = pl.pallas_call(
        _expert_kernel,
        name="moe_experts",
        grid_spec=pltpu.PrefetchScalarGridSpec(
            num_scalar_prefetch=2,
            grid=(n_tiles, nf),
            in_specs=[pl.BlockSpec((rows, d), lambda i, f, te, nu: (i, 0)),
                      pl.BlockSpec((1, d), lambda i, f, te, nu: (0, 0)),
                      pl.BlockSpec((None, None, d, tf), w_in_map),
                      pl.BlockSpec((None, None, d, tf), w_in_map),
                      pl.BlockSpec((None, None, tf, d), w_out_map)],
            out_specs=pl.BlockSpec((rows, d), lambda i, f, te, nu: (i, 0)),
            scratch_shapes=[pltpu.VMEM((rows, d), BF16)]),
        out_shape=jax.ShapeDtypeStruct((n_rows, d), F32),
        compiler_params=_cparams("arbitrary", "arbitrary"),
    )(tile_expert, n_used, xs, g2, wg, wu, wd)

    gain = jnp.ones((1, d), F32) if final_g is None else final_g.reshape(1, d)
    return pl.pallas_call(
        functools.partial(_combine_kernel, final_g is not None),
        name="moe_combine",
        grid=(t // t_comb,),
        in_specs=[smem_idx(t_comb), smem_idx(t_comb),
                  pl.BlockSpec((t_comb, d), lambda i: (i, 0)),
                  pl.BlockSpec((t_comb, LANES), lambda i: (i, 0)),
                  pl.BlockSpec((1, d), lambda i: (0, 0)),
                  hbm],
        out_specs=pl.BlockSpec((t_comb, d), lambda i: (i, 0)),
        out_shape=jax.ShapeDtypeStruct((t, d), F32),
        scratch_shapes=[pltpu.VMEM((t_comb, d), F32), pltpu.VMEM((t_comb, d), F32),
                        pltpu.SemaphoreType.DMA(())],
        compiler_params=_cparams("arbitrary"),
    )(pos1, pos2, x, route, gain, y)


def _gelu_tanh(x):
    return 0.5 * x * (1.0 + jnp.tanh(math.sqrt(2.0 / math.pi) * (x + 0.044715 * (x * x * x))))


def _compress_kernel(kv_ref, pe_ref, w1_ref, w2a_ref, w2b_ref, ka_ref, kb_ref, xf_ref, sh_ref):
    seq = kv_ref.shape[0]
    n = seq // CMP_STRIDE
    hid2 = w1_ref.shape[2]
    xf_ref[...] = kv_ref[...].astype(F32)
    top = jnp.zeros((n, hid2), F32)
    bot = jnp.zeros((n, hid2), F32)
    for i in range(CMP_STRIDE):
        xi = xf_ref[pl.ds(i, n, stride=CMP_STRIDE), :]
        top = top + _dot((xi + pe_ref[i:i + 1, :]).astype(BF16), w1_ref[i])
        bot = bot + _dot((xi + pe_ref[CMP_STRIDE + i:CMP_STRIDE + i + 1, :]).astype(BF16), w1_ref[CMP_STRIDE + i])
    sh_ref[0:n, :] = bot
    sh_ref[n:n + 8, :] = jnp.zeros((8, hid2), F32)
    act = _gelu_tanh(top + sh_ref[1:n + 1, :]).astype(BF16)
    ka_ref[...] = _dot(act, w2a_ref[...]).astype(ka_ref.dtype)
    kb_ref[...] = _dot(act, w2b_ref[...]).astype(kb_ref.dtype)


def _compress(kv, pe_k, pe_v, k_w1, k_w2, v_w1, v_w2, batch, seq):
    assert CMP_LEN == 2 * CMP_STRIDE
    hid = k_w1.shape[1]
    n = seq // CMP_STRIDE
    w1 = jnp.zeros((CMP_LEN, LANES, 2 * hid), F32)
    w1 = w1.at[:, :HEAD_DIM, :hid].set(k_w1.reshape(CMP_LEN, HEAD_DIM, hid))
    w1 = w1.at[:, HEAD_DIM:, hid:].set(v_w1.reshape(CMP_LEN, HEAD_DIM, hid)).astype(BF16)
    zero = jnp.zeros((hid, HEAD_DIM), F32)
    w2a = jnp.block([[k_w2, zero], [zero, v_w2]]).astype(BF16)
    w2b = jnp.block([[zero, k_w2], [v_w2, zero]]).astype(BF16)
    pe = jnp.concatenate([pe_k, pe_v], axis=-1)
    g = NSA_GROUPS
    out = jax.ShapeDtypeStruct((batch * g * n, LANES), BF16)
    return pl.pallas_call(
        _compress_kernel,
        name="compress",
        grid=(batch, g),
        in_specs=[pl.BlockSpec((seq, LANES), lambda b, gi: (b, gi)),
                  pl.BlockSpec((CMP_LEN, LANES), lambda b, gi: (0, 0)),
                  pl.BlockSpec((CMP_LEN, LANES, 2 * hid), lambda b, gi: (0, 0, 0)),
                  pl.BlockSpec((2 * hid, LANES), lambda b, gi: (0, 0)),
                  pl.BlockSpec((2 * hid, LANES), lambda b, gi: (0, 0))],
        out_specs=[pl.BlockSpec((n, LANES), lambda b, gi: (b * g + gi, 0)),
                   pl.BlockSpec((n, LANES), lambda b, gi: (b * g + gi, 0))],
        out_shape=[out, out],
        scratch_shapes=[pltpu.VMEM((seq, LANES), F32), pltpu.VMEM((n + 8, 2 * hid), F32)],
        compiler_params=_cparams("parallel", "parallel"),
    )(kv, pe, w1, w2a, w2b)


def _nsa_cmp_kernel(n_slc, q_ref, ka_ref, kb_ref, ovt_ref, ocmp_ref, selm_ref):
    qi = pl.program_id(2)
    tq = q_ref.shape[0]
    n_pad = ka_ref.shape[0]
    t_q = qi * tq + lax.broadcasted_iota(jnp.int32, (tq, n_pad), 0)
    n_c = lax.broadcasted_iota(jnp.int32, (tq, n_pad), 1)
    mask_c = (n_c * CMP_STRIDE + (CMP_LEN - 1)) <= t_q
    halves = _head_halves(tq)
    ka = ka_ref[...]
    kb = kb_ref[...]
    psum = jnp.zeros((tq, n_pad), F32)
    for pr in range(NSA_HPG // 2):
        qf = q_ref[:, pr * LANES:(pr + 1) * LANES].astype(F32)
        outs = []
        for h in range(2):
            qz = jnp.where(halves[h], qf, 0.0).astype(BF16)
            kmat, vmat = (ka, kb) if h == 0 else (kb, ka)
            s = jnp.where(mask_c, _dot_nt(qz, kmat), NEG_INF)
            m = jnp.max(s, axis=1, keepdims=True)
            m = jnp.where(m == NEG_INF, 0.0, m)
            e = jnp.exp(s - m)
            p = e / jnp.maximum(jnp.sum(e, axis=1, keepdims=True), 1e-30)
            psum = psum + p
            outs.append(_dot(p.astype(BF16), vmat))
        ocmp_ref[:, pr * LANES:(pr + 1) * LANES] = jnp.where(halves[0], outs[0], outs[1]).astype(ocmp_ref.dtype)

    ps_hi, ps_lo = _split_bf16(psum)
    ovt = ovt_ref[...]
    p_slc = (_dot_nt(ovt, ps_hi) + _dot_nt(ovt, ps_lo))[0:n_slc]
    cand = lax.broadcasted_iota(jnp.int32, (n_slc, tq), 0)
    t_blk = jnp.right_shift(qi * tq + lax.broadcasted_iota(jnp.int32, (n_slc, tq), 1), int(math.log2(SLC_BLOCK)))
    valid = cand <= t_blk
    forced = (cand == 0) | (cand == t_blk) | (cand == t_blk - 1)
    score = jnp.where(valid, jnp.where(forced, FORCE_SCORE, p_slc), -1.0)
    keep = _rank_select(score, valid, n_slc, min(SLC_TOPN, n_slc))
    neg = jnp.where(keep, 0.0, MASK_NEG)
    zero = jnp.zeros((HEAD_DIM - n_slc, tq), F32)
    selm_ref[...] = jnp.concatenate([neg, zero, neg, zero], axis=0).T.astype(selm_ref.dtype)


def _nsa_sw_tiles(tiles, q_ref, selm_ref, gate_ref, ex_ref, ocmp_ref, o_ref, ksel_ref, vsel_ref, kwin_ref, vwin_ref):
    tq = ATT_TILE
    halves = _head_halves(tq)
    causal, beyond = _tile_masks(tq)
    ex = ex_ref[...]
    sel_s, win_s = {}, {}
    for c in tiles:
        rows = slice(c * tq, (c + 1) * tq)
        w = (c + 1) * tq
        lo = max(c - WINDOW // tq, 0) * tq
        qf = q_ref[rows, :].astype(F32)
        selm = selm_ref[rows, :].astype(F32)
        for h in range(2):
            q_aug = jnp.where(halves[h], qf, selm).astype(BF16)
            qz = jnp.where(halves[h], qf, 0.0).astype(BF16)
            sel_s[c, h] = _mask_tile(_dot_nt(q_aug, ksel_ref[h, 0:w, :]), c * tq, causal)
            s = _mask_tile(_dot_nt(qz, kwin_ref[h, lo:w, :]), c * tq - lo, causal)
            if c * tq - lo == WINDOW:
                s = _mask_tile(s, 0, beyond)
            win_s[c, h] = s
    for c in tiles:
        rows = slice(c * tq, (c + 1) * tq)
        w = (c + 1) * tq
        lo = max(c - WINDOW // tq, 0) * tq
        win_o = [_softmax_pv(win_s[c, h], vwin_ref[h, lo:w, :], h) for h in range(2)]
        sel_o = [_softmax_pv(sel_s[c, h], vsel_ref[h, 0:w, :], h) for h in range(2)]
        o_sel = jnp.where(halves[0], sel_o[0], sel_o[1])
        o_win = jnp.where(halves[0], win_o[0], win_o[1])
        g_hi, g_lo = _split_bf16(gate_ref[rows, :])
        gexp = _dot(g_hi, ex) + _dot(g_lo, ex)
        y = (gexp[:, 0:LANES] * ocmp_ref[rows, :].astype(F32) + gexp[:, LANES:2 * LANES] * o_sel
             + gexp[:, 2 * LANES:3 * LANES] * o_win)
        o_ref[rows, :] = y.astype(o_ref.dtype)


def _nsa_sw_kernel(q_ref, selm_ref, kvs_ref, kvw_ref, gate_ref, ex_ref, ocmp_ref, o_ref,
                   ksel_ref, vsel_ref, kwin_ref, vwin_ref):
    pr = pl.program_id(2)
    qi = pl.program_id(3)
    seq = kvs_ref.shape[0]

    @pl.when((pr == 0) & (qi == 0))
    def _():
        first, second = _head_halves(seq)
        onehot = _block_onehot(seq, int(math.log2(SLC_BLOCK)))
        kv = kvs_ref[...].astype(F32)
        swapped = pltpu.roll(kv, HEAD_DIM, axis=1)
        ksel_ref[0] = jnp.where(first, kv, onehot).astype(BF16)
        ksel_ref[1] = jnp.where(second, swapped, onehot).astype(BF16)
        vsel_ref[0] = _with_ones(swapped, first)
        vsel_ref[1] = _with_ones(kv, second)
        kv = kvw_ref[...].astype(F32)
        swapped = pltpu.roll(kv, HEAD_DIM, axis=1)
        kwin_ref[0] = kvw_ref[...]
        kwin_ref[1] = swapped.astype(BF16)
        vwin_ref[0] = _with_ones(swapped, first)
        vwin_ref[1] = _with_ones(kv, second)

    nq = seq // ATT_TILE
    for j in range(nq // TILES_PER_STEP):
        pl.when(qi == j)(functools.partial(_nsa_sw_tiles, _step_tiles(j, nq), q_ref, selm_ref, gate_ref, ex_ref,
                                           ocmp_ref, o_ref, ksel_ref, vsel_ref, kwin_ref, vwin_ref))


def _nsa_attention(q_raw, q_rot, kv, ka, kb, gates, batch, seq):
    nq = seq // ATT_TILE
    g = NSA_GROUPS
    gw = NSA_HPG * HEAD_DIM
    n_pairs = NSA_HEADS // 2
    n_cmp = (seq - CMP_LEN) // CMP_STRIDE + 1
    n_cmp_pad = seq // CMP_STRIDE
    n_slc = seq // SLC_BLOCK
    assert n_cmp_pad == LANES and n_slc <= HEAD_DIM // 2 and WINDOW % ATT_TILE == 0

    ex = np.zeros((n_pairs, LANES, 3 * LANES), np.float32)
    for hd in range(NSA_HEADS):
        for br in range(3):
            lane0 = br * LANES + (hd % 2) * HEAD_DIM
            ex[hd // 2, hd * 3 + br, lane0:lane0 + HEAD_DIM] = 1.0
    c_s = np.arange(n_cmp) * CMP_STRIDE
    s_s = np.arange(n_slc) * SLC_BLOCK
    ov = np.clip(np.minimum(c_s[:, None] + CMP_LEN, s_s[None, :] + SLC_BLOCK)
                 - np.maximum(c_s[:, None], s_s[None, :]), 0, None) / CMP_LEN
    ovt = np.zeros((LANES, n_cmp_pad), np.float32)
    ovt[:n_slc, :n_cmp] = ov.T

    n_ct = seq // CMP_TILE
    o_cmp, selm = pl.pallas_call(
        functools.partial(_nsa_cmp_kernel, n_slc),
        name="nsa_cmp",
        grid=(batch, g, n_ct),
        in_specs=[pl.BlockSpec((CMP_TILE, gw), lambda b, gi, i: (b * n_ct + i, gi)),
                  pl.BlockSpec((n_cmp_pad, LANES), lambda b, gi, i: (b * g + gi, 0)),
                  pl.BlockSpec((n_cmp_pad, LANES), lambda b, gi, i: (b * g + gi, 0)),
                  pl.BlockSpec((LANES, n_cmp_pad), lambda b, gi, i: (0, 0))],
        out_specs=[pl.BlockSpec((CMP_TILE, gw), lambda b, gi, i: (b * n_ct + i, gi)),
                   pl.BlockSpec((CMP_TILE, LANES), lambda b, gi, i: (b * n_ct + i, gi))],
        out_shape=[jax.ShapeDtypeStruct((batch * seq, NSA_HEADS * HEAD_DIM), BF16),
                   jax.ShapeDtypeStruct((batch * seq, g * LANES), BF16)],
        compiler_params=_cparams("parallel", "parallel", "parallel"),
    )(q_raw, ka, kb, jnp.asarray(ovt, BF16))

    ppg = NSA_HPG // 2
    return pl.pallas_call(
        _nsa_sw_kernel,
        name="nsa_sw",
        grid=(batch, g, ppg, nq // TILES_PER_STEP),
        in_specs=[pl.BlockSpec((seq, LANES), lambda b, gi, p, i: (b, gi * ppg + p)),
                  pl.BlockSpec((seq, LANES), lambda b, gi, p, i: (b, gi)),
                  pl.BlockSpec((seq, LANES), lambda b, gi, p, i: (b, g + gi)),
                  pl.BlockSpec((seq, LANES), lambda b, gi, p, i: (b, 2 * g + gi)),
                  pl.BlockSpec((seq, LANES), lambda b, gi, p, i: (b, 0)),
                  pl.BlockSpec((None, LANES, 3 * LANES), lambda b, gi, p, i: (gi * ppg + p, 0, 0)),
                  pl.BlockSpec((seq, LANES), lambda b, gi, p, i: (b, gi * ppg + p))],
        out_specs=pl.BlockSpec((seq, LANES), lambda b, gi, p, i: (b, gi * ppg + p)),
        out_shape=jax.ShapeDtypeStruct((batch * seq, NSA_HEADS * HEAD_DIM), BF16),
        scratch_shapes=[pltpu.VMEM((2, seq, LANES), BF16) for _ in range(4)],
        compiler_params=_cparams("parallel", "parallel", "arbitrary", "arbitrary"),
    )(q_rot, selm, kv, kv, gates, jnp.asarray(ex, BF16), o_cmp)


def _even_mixer(x, g, w_in, pool_w, pool_scale, w_out, rope_pair, batch, seq):
    chunk = 2 * LANES
    plan = [(0, POOL_DIM, [(0, 0, "none", 0)])]
    for c in range(3 * MOBA_DIM // chunk):
        op = "rope_scale" if c < MOBA_DIM // chunk else ("rope" if c < 2 * MOBA_DIM // chunk else "none")
        plan.append((POOL_DIM + c * chunk, chunk, [(1, c * chunk, op, 0)]))
    u_pool, qkv = _norm_project(x, g, w_in.astype(BF16), plan, [rope_pair],
                                [(POOL_DIM, F32), (3 * MOBA_DIM, BF16)], seq)
    w_bd = jax.scipy.linalg.block_diag(*[pool_w[i] for i in range(pool_w.shape[0])]).astype(BF16)
    y_a = _pool_mixer(u_pool, w_bd, pool_scale, batch, seq)
    y_b = _moba_attention(qkv, batch, seq)
    w_out = w_out.astype(BF16)
    return [y_a, y_b], [w_out[:POOL_DIM], w_out[POOL_DIM:]]


def _odd_mixer(x, g, w_in, pe_k, pe_v, k_w1, k_w2, v_w1, v_w2, w_out, rope_pair, rope_single, batch, seq):
    d = x.shape[1]
    qd = NSA_HEADS * HEAD_DIM
    kvd = NSA_KV_DIM
    wq = w_in[:, :qd]
    parts = [w_in[:, qd + i * kvd:qd + (i + 1) * kvd].reshape(d, NSA_GROUPS, HEAD_DIM) for i in range(6)]
    pairs = [jnp.concatenate([parts[2 * i], parts[2 * i + 1]], axis=-1).reshape(d, 2 * kvd) for i in range(3)]
    n_gate = 3 * NSA_HEADS
    w_gate = jnp.zeros((d, LANES), F32).at[:, :n_gate].set(w_in[:, qd + 6 * kvd:])
    w_all = jnp.concatenate([wq] + pairs + [w_gate], axis=1).astype(BF16)

    chunk = 2 * LANES
    plan = []
    for c in range(qd // chunk):
        plan.append((c * chunk, chunk, [(0, c * chunk, "scale", 0), (1, c * chunk, "rope_scale", 0)]))
    for c in range(6 * kvd // chunk):
        branch = c // (2 * kvd // chunk)
        op = "none" if branch == 0 else "rope"
        plan.append((qd + c * chunk, chunk, [(2, c * chunk, op, 1)]))
    plan.append((qd + 6 * kvd, LANES, [(3, 0, "sigmoid", 0)]))
    q_raw, q_rot, kv, gates = _norm_project(
        x, g, w_all, plan, [rope_pair, rope_single],
        [(qd, BF16), (qd, BF16), (6 * kvd, BF16), (LANES, F32)], seq)

    ka, kb = _compress(kv, pe_k, pe_v, k_w1, k_w2, v_w1, v_w2, batch, seq)
    y = _nsa_attention(q_raw, q_rot, kv, ka, kb, gates, batch, seq)
    return [y], [w_out.astype(BF16)]


def kernel(x, mem, positions, norm_g, mem_g, final_g, w_in_ab, pool_w, pool_scale, w_out_ab, ffn_w_gate, ffn_w_up, ffn_w_down, w_in_c, cmp_pe_k, cmp_pe_v, cmp_k_w1, cmp_k_w2, cmp_v_w1, cmp_v_w2, w_out_c, router_w, moe_w_gate, moe_w_up, moe_w_down, xa_wq, xa_wkv, xa_wo):
    batch, seq, d = x.shape
    mem_len = mem.shape[1]
    depth = norm_g.shape[0]
    rope_pair, rope_single = _rope_tables(positions)

    wkv_all = jnp.concatenate([xa_wkv[l] for l in range(depth)], axis=1).astype(BF16)
    kv_all = _norm_matmul(mem.reshape(batch * mem_len, d), mem_g, wkv_all)

    xf = x.reshape(batch * seq, d)
    for layer in range(depth):
        i = layer // 2
        last = layer == depth - 1
        if layer % 2 == 0:
            ys, ws = _even_mixer(xf, norm_g[layer, 0], w_in_ab[i], pool_w[i], pool_scale[i], w_out_ab[i],
                                 rope_pair, batch, seq)
        else:
            ys, ws = _odd_mixer(xf, norm_g[layer, 0], w_in_c[i], cmp_pe_k[i], cmp_pe_v[i], cmp_k_w1[i],
                                cmp_k_w2[i], cmp_v_w1[i], cmp_v_w2[i], w_out_c[i], rope_pair, rope_single,
                                batch, seq)
        xf = _out_proj(xf, ys, ws)
        mid_args = (xf, norm_g[layer, 1], xa_wq[layer].astype(BF16), kv_all, layer,
                    xa_wo[layer].astype(BF16), seq, mem_len)
        if layer % 2 == 0:
            xf = _mid_layer(*mid_args)
            xf = _ffn(xf, norm_g[layer, 2], ffn_w_gate[i].astype(BF16), ffn_w_up[i].astype(BF16),
                      ffn_w_down[i].astype(BF16))
            if last:
                xf = _final_norm(xf, final_g)
        else:
            xf, route, cnt = _mid_layer(*mid_args, router=(norm_g[layer, 2], router_w[i]))
            xf = _moe(xf, norm_g[layer, 2], route, cnt, moe_w_gate, moe_w_up, moe_w_down, i,
                      final_g=final_g if last else None)
    return xf.reshape(batch, seq, d)
```

```python
import functools
import math

import numpy as np
import jax
import jax.numpy as jnp
from jax import lax
from jax.experimental import pallas as pl
from jax.experimental.pallas import tpu as pltpu

F32 = jnp.float32
BF16 = jnp.bfloat16

HEAD_DIM = 64
ROPE_DIM = 16
ROPE_THETA = 500000.0
NORM_EPS = 1e-5
QK_SCALE = HEAD_DIM ** -0.5
QK_SCALE_LOG2E = QK_SCALE * math.log2(math.e)

POOL_WINDOWS = (2, 4, 8, 16)
POOL_DIM = 256
MOBA_HEADS = 12
MOBA_DIM = MOBA_HEADS * HEAD_DIM
MOBA_BLOCK = 256
MOBA_TOPK = 3
NSA_HEADS = 16
NSA_GROUPS = 4
NSA_HPG = 4
NSA_KV_DIM = NSA_GROUPS * HEAD_DIM
CMP_LEN = 32
CMP_STRIDE = 16
SLC_BLOCK = 64
SLC_TOPN = 16
WINDOW = 512
FORCE_SCORE = 1e4
XA_HEADS = 4
XA_DIM = XA_HEADS * HEAD_DIM
N_EXPERTS = 8

LANES = 128
ATT_TILE = 256
CMP_TILE = 2048
TILES_PER_STEP = 4
VMEM_LIMIT = 56 * 1024 * 1024
NEG_INF = float("-inf")
MASK_NEG = -1e30


def _cparams(*sem):
    return pltpu.CompilerParams(dimension_semantics=sem, vmem_limit_bytes=VMEM_LIMIT)


def _dot(a, b):
    return jnp.dot(a, b, preferred_element_type=F32)


def _dot_nt(a, b):
    return lax.dot_general(a, b, (((1,), (1,)), ((), ())), preferred_element_type=F32)


def _split_bf16(a):
    hi = a.astype(BF16)
    lo = (a - hi.astype(F32)).astype(BF16)
    return hi, lo


def _rms_bf16(x, g):
    ms = jnp.mean(x * x, axis=-1, keepdims=True)
    return (x * lax.rsqrt(ms + NORM_EPS) * g).astype(BF16)


def _apply_rope(acc, c, a, b):
    half = ROPE_DIM // 2
    return acc * c + pltpu.roll(acc, LANES - half, axis=1) * a + pltpu.roll(acc, half, axis=1) * b


def _proj_kernel(plan, n_tab, x_ref, g_ref, w_ref, *rest):
    tabs = rest[:3 * n_tab]
    outs = rest[3 * n_tab:]
    h = _rms_bf16(x_ref[...], g_ref[...])
    for wc0, width, sinks in plan:
        acc = _dot(h, w_ref[:, wc0:wc0 + width])
        for oi, oc0, op, tab in sinks:
            for s in range(width // LANES):
                val = acc[:, s * LANES:(s + 1) * LANES]
                if op in ("rope", "rope_scale"):
                    c, a, b = (tabs[3 * tab + i][...] for i in range(3))
                    val = _apply_rope(val, c, a, b)
                if op == "scale":
                    val = val * QK_SCALE
                if op == "rope_scale":
                    val = val * QK_SCALE_LOG2E
                if op == "sigmoid":
                    val = jax.nn.sigmoid(val)
                o = outs[oi]
                o[:, oc0 + s * LANES:oc0 + (s + 1) * LANES] = val.astype(o.dtype)


def _norm_project(x, g, w, plan, tables, out_defs, seq, tm=512):
    t, d = x.shape
    n = w.shape[1]
    n_seq_tiles = seq // tm
    in_specs = [
        pl.BlockSpec((tm, d), lambda i: (i, 0)),
        pl.BlockSpec((1, d), lambda i: (0, 0)),
        pl.BlockSpec((d, n), lambda i: (0, 0)),
    ]
    flat_tabs = []
    for tset in tables:
        for tb in tset:
            flat_tabs.append(tb)
            in_specs.append(pl.BlockSpec((tm, LANES), lambda i: (i % n_seq_tiles, 0)))
    out_shape = [jax.ShapeDtypeStruct((t, wd), dt) for wd, dt in out_defs]
    out_specs = [pl.BlockSpec((tm, wd), lambda i: (i, 0)) for wd, _ in out_defs]
    return pl.pallas_call(
        functools.partial(_proj_kernel, plan, len(tables)),
        name="proj",
        grid=(t // tm,),
        in_specs=in_specs,
        out_specs=out_specs,
        out_shape=out_shape,
        compiler_params=_cparams("parallel"),
    )(x, g.reshape(1, d), w, *flat_tabs)


def _rope_tables(positions):
    half = ROPE_DIM // 2
    inv = ROPE_THETA ** (-jnp.arange(0, ROPE_DIM, 2, dtype=F32) / ROPE_DIM)
    ang = positions.astype(F32)[:, None] * inv[None, :]
    cos, sin = jnp.cos(ang), jnp.sin(ang)
    s = positions.shape[0]
    z8 = jnp.zeros((s, half), F32)
    rest0 = jnp.zeros((s, HEAD_DIM - ROPE_DIM), F32)
    rest1 = jnp.ones((s, HEAD_DIM - ROPE_DIM), F32)
    c64 = jnp.concatenate([cos, cos, rest1], -1)
    a64 = jnp.concatenate([-sin, z8, rest0], -1)
    b64 = jnp.concatenate([z8, sin, rest0], -1)
    one64 = jnp.ones((s, HEAD_DIM), F32)
    zero64 = jnp.zeros((s, HEAD_DIM), F32)
    pair = tuple(jnp.concatenate([m, m], -1) for m in (c64, a64, b64))
    single = (jnp.concatenate([c64, one64], -1), jnp.concatenate([a64, zero64], -1),
              jnp.concatenate([b64, zero64], -1))
    return pair, single


def _norm_matmul_kernel(x_ref, g_ref, w_ref, o_ref):
    h = _rms_bf16(x_ref[...], g_ref[...])
    o_ref[...] = _dot(h, w_ref[...]).astype(o_ref.dtype)


def _norm_matmul(x, g, w, tm=512):
    t, d = x.shape
    n = w.shape[1]
    return pl.pallas_call(
        _norm_matmul_kernel,
        name="mem_kv",
        grid=(t // tm,),
        in_specs=[pl.BlockSpec((tm, d), lambda i: (i, 0)),
                  pl.BlockSpec((1, d), lambda i: (0, 0)),
                  pl.BlockSpec((d, n), lambda i: (0, 0))],
        out_specs=pl.BlockSpec((tm, n), lambda i: (i, 0)),
        out_shape=jax.ShapeDtypeStruct((t, n), BF16),
        compiler_params=_cparams("parallel"),
    )(x, g.reshape(1, d), w)


def _final_norm_kernel(x_ref, g_ref, o_ref):
    x = x_ref[...]
    ms = jnp.mean(x * x, axis=-1, keepdims=True)
    o_ref[...] = x * lax.rsqrt(ms + NORM_EPS) * g_ref[...]


def _final_norm(x, g, tm=1024):
    t, d = x.shape
    return pl.pallas_call(
        _final_norm_kernel,
        name="final_norm",
        grid=(t // tm,),
        in_specs=[pl.BlockSpec((tm, d), lambda i: (i, 0)), pl.BlockSpec((1, d), lambda i: (0, 0))],
        out_specs=pl.BlockSpec((tm, d), lambda i: (i, 0)),
        out_shape=jax.ShapeDtypeStruct((t, d), F32),
        compiler_params=_cparams("parallel"),
    )(x, g.reshape(1, d))


def _pool_kernel(u_ref, w_ref, sc_ref, o_ref, pad_ref):
    s = u_ref.shape[0]
    maxw = POOL_WINDOWS[-1]
    u = u_ref[...]
    pad_ref[0:maxw, :] = jnp.zeros((maxw, POOL_DIM), F32)
    t1 = (lax.broadcasted_iota(jnp.int32, (s, POOL_DIM), 0) + 1).astype(F32)
    lane = lax.broadcasted_iota(jnp.int32, (s, POOL_DIM), 1)
    gdim = POOL_DIM // len(POOL_WINDOWS)
    cur = u
    pooled = jnp.zeros_like(u)
    shift = 1
    for gi, w in enumerate(POOL_WINDOWS):
        pad_ref[maxw:maxw + s, :] = cur
        cur = cur + pad_ref[maxw - shift:maxw - shift + s, :]
        shift *= 2
        assert shift == w
        mean = cur / jnp.minimum(t1, float(w))
        pooled = jnp.where((lane >= gi * gdim) & (lane < (gi + 1) * gdim), mean, pooled)
    pooled = (pooled - u).astype(BF16)
    o_ref[...] = (_dot(pooled, w_ref[...]) * sc_ref[...]).astype(o_ref.dtype)


def _pool_mixer(u_pool, w_bd, scale, batch, seq):
    return pl.pallas_call(
        _pool_kernel,
        name="pool",
        grid=(batch,),
        in_specs=[pl.BlockSpec((seq, POOL_DIM), lambda b: (b, 0)),
                  pl.BlockSpec((POOL_DIM, POOL_DIM), lambda b: (0, 0)),
                  pl.BlockSpec((1, POOL_DIM), lambda b: (0, 0))],
        out_specs=pl.BlockSpec((seq, POOL_DIM), lambda b: (b, 0)),
        out_shape=jax.ShapeDtypeStruct((batch * seq, POOL_DIM), BF16),
        scratch_shapes=[pltpu.VMEM((seq + POOL_WINDOWS[-1], POOL_DIM), F32)],
        compiler_params=_cparams("parallel"),
    )(u_pool, w_bd, scale.reshape(1, POOL_DIM))


def _rank_select(scoreT, valid, n_rows, topn):
    row = lax.broadcasted_iota(jnp.int32, scoreT.shape, 0)
    rank = jnp.zeros(scoreT.shape, F32)
    for j in range(n_rows):
        sj = scoreT[j:j + 1, :]
        beats = (scoreT > sj) | ((scoreT == sj) & (row < j))
        rj = jnp.sum(beats.astype(F32), axis=0, keepdims=True)
        rank = jnp.where(row == j, rj, rank)
    return (rank < float(topn)) & valid


def _head_halves(rows):
    lane = lax.broadcasted_iota(jnp.int32, (rows, LANES), 1)
    return lane < HEAD_DIM, lane >= HEAD_DIM


def _tile_masks(tq):
    qry = lax.broadcasted_iota(jnp.int32, (tq, tq), 0)
    key = lax.broadcasted_iota(jnp.int32, (tq, tq), 1)
    return key <= qry, key > qry


def _mask_tile(s, t0, mask):
    tq = mask.shape[0]
    parts = []
    if t0 > 0:
        parts.append(s[:, :t0])
    parts.append(jnp.where(mask, s[:, t0:t0 + tq], NEG_INF))
    if t0 + tq < s.shape[1]:
        parts.append(s[:, t0 + tq:])
    return parts[0] if len(parts) == 1 else jnp.concatenate(parts, axis=1)


def _softmax_pv(s, v, h):
    m = jnp.max(s, axis=1, keepdims=True)
    p = jnp.exp2((s - m).astype(BF16))
    o = _dot(p, v)
    denom_lane = (1 - h) * HEAD_DIM
    return o / o[:, denom_lane:denom_lane + 1]


def _with_ones(v, own):
    return jnp.where(own, v, 1.0).astype(BF16)


def _step_tiles(j, nq):
    half = TILES_PER_STEP // 2
    return tuple(t for k in range(half) for t in (nq - 1 - (j * half + k), j * half + k))


def _block_onehot(seq, shift):
    lane = lax.broadcasted_iota(jnp.int32, (seq, LANES), 1)
    blk = jnp.right_shift(lax.broadcasted_iota(jnp.int32, (seq, LANES), 0), shift)
    return jnp.where((lane & (HEAD_DIM - 1)) == blk, 1.0, 0.0)


def _moba_tiles(tiles, o_ref, qaug_ref, kaug_ref, vaug_ref):
    tq = ATT_TILE
    halves = _head_halves(tq)
    causal, _ = _tile_masks(tq)
    scores = {}
    for c in tiles:
        for h in range(2):
            s = _dot_nt(qaug_ref[h, c * tq:(c + 1) * tq, :], kaug_ref[h, 0:(c + 1) * tq, :])
            scores[c, h] = _mask_tile(s, c * tq, causal)
    for c in tiles:
        outs = [_softmax_pv(scores[c, h], vaug_ref[h, 0:(c + 1) * tq, :], h) for h in range(2)]
        o_ref[c * tq:(c + 1) * tq, :] = jnp.where(halves[0], outs[0], outs[1]).astype(o_ref.dtype)


def _moba_kernel(q_ref, k_ref, v_ref, o_ref, qaug_ref, kaug_ref, vaug_ref, kmean_ref):
    qi = pl.program_id(2)
    seq = k_ref.shape[0]
    nb = seq // MOBA_BLOCK
    blk_shift = int(math.log2(MOBA_BLOCK))

    @pl.when(qi == 0)
    def _():
        halves = _head_halves(seq)
        kf = k_ref[...].astype(F32)
        vf = v_ref[...].astype(F32)
        qp = q_ref[...]
        qf = qp.astype(F32)
        onehot = _block_onehot(seq, blk_shift)
        kmean_ref[...] = jnp.zeros(kmean_ref.shape, F32)
        for j in range(nb):
            kmean_ref[j:j + 1, :] = jnp.mean(kf[j * MOBA_BLOCK:(j + 1) * MOBA_BLOCK], axis=0, keepdims=True)
        cand = lax.broadcasted_iota(jnp.int32, (nb, seq), 0)
        own = jnp.right_shift(lax.broadcasted_iota(jnp.int32, (nb, seq), 1), blk_shift)
        valid = cand < own
        for h in range(2):
            kaug_ref[h] = jnp.where(halves[h], kf, onehot).astype(BF16)
            vaug_ref[h] = _with_ones(vf, halves[h])
            mine_k = _head_halves(kmean_ref.shape[0])[h]
            km_hi, km_lo = _split_bf16(jnp.where(mine_k, kmean_ref[...], 0.0))
            gate = (_dot_nt(km_hi, qp) + _dot_nt(km_lo, qp))[0:nb]
            gate = jnp.where(valid, gate, NEG_INF)
            keep = _rank_select(gate, valid, nb, MOBA_TOPK) | (cand == own)
            neg = jnp.where(keep, 0.0, MASK_NEG)
            lead = HEAD_DIM if h == 0 else 0
            pieces = [neg, jnp.zeros((LANES - lead - nb, seq), F32)]
            if lead:
                pieces.insert(0, jnp.zeros((lead, seq), F32))
            maskcols = jnp.concatenate(pieces, axis=0).T
            qaug_ref[h] = jnp.where(halves[h], qf, maskcols).astype(BF16)

    for j in range(nb // TILES_PER_STEP):
        pl.when(qi == j)(functools.partial(_moba_tiles, _step_tiles(j, nb), o_ref, qaug_ref, kaug_ref, vaug_ref))


def _moba_attention(qkv, batch, seq):
    nq = seq // ATT_TILE
    assert MOBA_BLOCK == ATT_TILE and nq <= 16 and nq % TILES_PER_STEP == 0
    hp = MOBA_DIM // LANES
    return pl.pallas_call(
        _moba_kernel,
        name="moba",
        grid=(batch, hp, nq // TILES_PER_STEP),
        in_specs=[pl.BlockSpec((seq, LANES), lambda b, p, i: (b, p)),
                  pl.BlockSpec((seq, LANES), lambda b, p, i: (b, hp + p)),
                  pl.BlockSpec((seq, LANES), lambda b, p, i: (b, 2 * hp + p))],
        out_specs=pl.BlockSpec((seq, LANES), lambda b, p, i: (b, p)),
        out_shape=jax.ShapeDtypeStruct((batch * seq, MOBA_DIM), BF16),
        scratch_shapes=[pltpu.VMEM((2, seq, LANES), BF16), pltpu.VMEM((2, seq, LANES), BF16),
                        pltpu.VMEM((2, seq, LANES), BF16), pltpu.VMEM((16, LANES), F32)],
        compiler_params=_cparams("parallel", "parallel", "arbitrary"),
    )(qkv, qkv, qkv)


def _out_proj_kernel(n_in, x_ref, *rest):
    ys = rest[:n_in]
    ws = rest[n_in:2 * n_in]
    o_ref = rest[2 * n_in]
    acc = x_ref[...]
    for y, w in zip(ys, ws):
        acc = acc + _dot(y[...], w[...])
    o_ref[...] = acc


def _out_proj(x, ys, ws, tm=512):
    t, d = x.shape
    in_specs = [pl.BlockSpec((tm, d), lambda i: (i, 0))]
    in_specs += [pl.BlockSpec((tm, y.shape[1]), lambda i: (i, 0)) for y in ys]
    in_specs += [pl.BlockSpec(w.shape, lambda i: (0, 0)) for w in ws]
    return pl.pallas_call(
        functools.partial(_out_proj_kernel, len(ys)),
        name="out_proj",
        grid=(t // tm,),
        in_specs=in_specs,
        out_specs=pl.BlockSpec((tm, d), lambda i: (i, 0)),
        out_shape=jax.ShapeDtypeStruct((t, d), F32),
        compiler_params=_cparams("parallel"),
    )(x, *ys, *ws)


def _mid_kernel(with_route, x_ref, g_ref, wq_ref, kv_ref, wo_ref, *rest):
    o_ref = rest[3] if with_route else rest[0]
    x = x_ref[...]
    h = _rms_bf16(x, g_ref[...])
    q = _dot(h, wq_ref[...]) * QK_SCALE_LOG2E
    halves = _head_halves(x.shape[0])
    mem_halves = _head_halves(kv_ref.shape[0])
    out = x
    for pr in range(XA_HEADS // 2):
        lanes = slice(pr * LANES, (pr + 1) * LANES)
        qf = q[:, lanes]
        kp = kv_ref[:, lanes]
        vf = kv_ref[:, XA_DIM + pr * LANES:XA_DIM + (pr + 1) * LANES].astype(F32)
        scores = [_dot_nt(jnp.where(halves[hh], qf, 0.0).astype(BF16), kp) for hh in range(2)]
        outs = [_softmax_pv(scores[hh], _with_ones(vf, mem_halves[hh]), hh) for hh in range(2)]
        o_pair = jnp.where(halves[0], outs[0], outs[1]).astype(BF16)
        out = out + _dot(o_pair, wo_ref[lanes, :])
    o_ref[...] = out

    if with_route:
        g3_ref, rhi_ref, rlo_ref, _, route_ref, cnt_ref = rest

        @pl.when(pl.program_id(0) == 0)
        def _():
            cnt_ref[...] = jnp.zeros(cnt_ref.shape, F32)

        route_ref[...] = _route_rows(_rms_bf16(out, g3_ref[...]), rhi_ref[...], rlo_ref[...], cnt_ref)


def _mid_layer(x, g, wq, kv_all, layer, wo, seq, mem_len, router=None, tm=512):
    t, d = x.shape
    tiles_per_seq = seq // tm
    const = lambda i: (0, 0)
    rows = lambda i: (i, 0)
    in_specs = [pl.BlockSpec((tm, d), rows),
                pl.BlockSpec((1, d), const),
                pl.BlockSpec((d, XA_DIM), const),
                pl.BlockSpec((mem_len, 2 * XA_DIM), lambda i: (i // tiles_per_seq, layer)),
                pl.BlockSpec((XA_DIM, d), const)]
    args = [x, g.reshape(1, d), wq, kv_all, wo]
    out_specs = [pl.BlockSpec((tm, d), rows)]
    out_shape = [jax.ShapeDtypeStruct((t, d), F32)]
    if router is not None:
        g3, router_w = router
        rw = jnp.zeros((d, LANES), F32).at[:, :router_w.shape[1]].set(router_w)
        rhi = rw.astype(BF16)
        rlo = (rw - rhi.astype(F32)).astype(BF16)
        in_specs += [pl.BlockSpec((1, d), const), pl.BlockSpec((d, LANES), const), pl.BlockSpec((d, LANES), const)]
        args += [g3.reshape(1, d), rhi, rlo]
        out_specs += [pl.BlockSpec((tm, LANES), rows), pl.BlockSpec((8, LANES), const)]
        out_shape += [jax.ShapeDtypeStruct((t, LANES), F32), jax.ShapeDtypeStruct((8, LANES), F32)]
    res = pl.pallas_call(
        functools.partial(_mid_kernel, router is not None),
        name="mid",
        grid=(t // tm,),
        in_specs=in_specs,
        out_specs=out_specs,
        out_shape=out_shape,
        compiler_params=_cparams("arbitrary"),
    )(*args)
    return res if router is not None else res[0]


def _ffn_kernel(x_ref, g_ref, wg_ref, wu_ref, wd_ref, o_ref, h_ref):
    f = pl.program_id(1)

    @pl.when(f == 0)
    def _():
        x = x_ref[...]
        h_ref[...] = _rms_bf16(x, g_ref[...])
        o_ref[...] = x

    h = h_ref[...]
    gate = _dot(h, wg_ref[...])
    a = (gate * jax.nn.sigmoid(gate)) * _dot(h, wu_ref[...])
    o_ref[...] += _dot(a.astype(BF16), wd_ref[...])


def _ffn(x, g, wg, wu, wd, tm=1024, tf=256):
    t, d = x.shape
    ff = wg.shape[1]
    return pl.pallas_call(
        _ffn_kernel,
        name="ffn",
        grid=(t // tm, ff // tf),
        in_specs=[pl.BlockSpec((tm, d), lambda i, f: (i, 0)),
                  pl.BlockSpec((1, d), lambda i, f: (0, 0)),
                  pl.BlockSpec((d, tf), lambda i, f: (0, f)),
                  pl.BlockSpec((d, tf), lambda i, f: (0, f)),
                  pl.BlockSpec((tf, d), lambda i, f: (f, 0))],
        out_specs=pl.BlockSpec((tm, d), lambda i, f: (i, 0)),
        out_shape=jax.ShapeDtypeStruct((t, d), F32),
        scratch_shapes=[pltpu.VMEM((tm, d), BF16)],
        compiler_params=_cparams("parallel", "arbitrary"),
    )(x, g.reshape(1, d), wg, wu, wd)


ROUTE_I1, ROUTE_I2, ROUTE_W1, ROUTE_W2, ROUTE_R1, ROUTE_R2 = range(6)
MOE_TILE = 1024


def _lane_pick(arr, lane, k):
    return jnp.sum(jnp.where(lane == k, arr, 0.0), axis=1, keepdims=True)


def _route_rows(h, rhi, rlo, cnt_ref):
    tm = h.shape[0]
    lane = lax.broadcasted_iota(jnp.int32, (tm, LANES), 1)
    lane_f = lane.astype(F32)
    logits = _dot(h, rhi) + _dot(h, rlo)
    logits = jnp.where(lane < N_EXPERTS, logits, NEG_INF)
    m1 = jnp.max(logits, axis=1, keepdims=True)
    i1 = jnp.min(jnp.where(logits == m1, lane_f, float(LANES)), axis=1, keepdims=True)
    rest = jnp.where(lane_f == i1, NEG_INF, logits)
    m2 = jnp.max(rest, axis=1, keepdims=True)
    i2 = jnp.min(jnp.where(rest == m2, lane_f, float(LANES)), axis=1, keepdims=True)
    e2 = jnp.exp(m2 - m1)
    w1 = 1.0 / (1.0 + e2)
    w2 = e2 / (1.0 + e2)

    chosen = jnp.where((lane_f == i1) | (lane_f == i2), 1.0, 0.0)
    r = lax.broadcasted_iota(jnp.int32, (tm, tm), 0)
    c = lax.broadcasted_iota(jnp.int32, (tm, tm), 1)
    earlier = jnp.where(c < r, 1.0, 0.0).astype(BF16)
    prefix = _dot(earlier, chosen.astype(BF16)) + cnt_ref[0:1, :]
    rank1 = jnp.sum(jnp.where(lane_f == i1, prefix, 0.0), axis=1, keepdims=True)
    rank2 = jnp.sum(jnp.where(lane_f == i2, prefix, 0.0), axis=1, keepdims=True)
    cnt_ref[0:1, :] = cnt_ref[0:1, :] + jnp.sum(chosen, axis=0, keepdims=True)

    cols = {ROUTE_I1: i1, ROUTE_I2: i2, ROUTE_W1: w1, ROUTE_W2: w2, ROUTE_R1: rank1, ROUTE_R2: rank2}
    route = jnp.zeros((tm, LANES), F32)
    for k, val in cols.items():
        route = jnp.where(lane == k, val, route)
    return route


def _dispatch_kernel(pos1_ref, pos2_ref, x_ref, xs_in, xs_hbm, sem):
    del xs_in
    n = pos1_ref.shape[0]

    def row_copy(r, p):
        return pltpu.make_async_copy(x_ref.at[pl.ds(r, 1)], xs_hbm.at[pl.ds(p, 1)], sem)

    def issue(r, carry):
        row_copy(r, pos1_ref[r]).start()
        row_copy(r, pos2_ref[r]).start()
        return carry

    lax.fori_loop(0, n, issue, 0, unroll=8)
    for _ in range(2):
        pltpu.make_async_copy(x_ref, xs_hbm.at[pl.ds(0, n)], sem).wait()


def _expert_kernel(te_ref, nu_ref, xs_ref, g_ref, wg_ref, wu_ref, wd_ref, y_ref, h_ref):
    del te_ref
    i = pl.program_id(0)
    f = pl.program_id(1)
    used = i < nu_ref[0]

    @pl.when(f == 0)
    def _():
        h_ref[...] = _rms_bf16(xs_ref[...], g_ref[...])
        y_ref[...] = jnp.zeros(y_ref.shape, y_ref.dtype)

    @pl.when(used)
    def _():
        h = h_ref[...]
        gate = _dot(h, wg_ref[...].astype(BF16))
        a = (gate * jax.nn.sigmoid(gate)) * _dot(h, wu_ref[...].astype(BF16))
        y_ref[...] += _dot(a.astype(BF16), wd_ref[...].astype(BF16))


def _combine_kernel(with_norm, pos1_ref, pos2_ref, x_ref, route_ref, g_ref, y_hbm, o_ref, buf1, buf2, sem):
    n = pos1_ref.shape[0]

    def row_copy(p, buf, r):
        return pltpu.make_async_copy(y_hbm.at[pl.ds(p, 1)], buf.at[pl.ds(r, 1)], sem)

    def issue(r, carry):
        row_copy(pos1_ref[r], buf1, r).start()
        row_copy(pos2_ref[r], buf2, r).start()
        return carry

    lax.fori_loop(0, n, issue, 0, unroll=8)
    for buf in (buf1, buf2):
        pltpu.make_async_copy(y_hbm.at[pl.ds(0, n)], buf, sem).wait()
    lane = lax.broadcasted_iota(jnp.int32, route_ref.shape, 1)
    route = route_ref[...]
    w1 = _lane_pick(route, lane, ROUTE_W1)
    w2 = _lane_pick(route, lane, ROUTE_W2)
    out = x_ref[...] + w1 * buf1[...] + w2 * buf2[...]
    if with_norm:
        ms = jnp.mean(out * out, axis=-1, keepdims=True)
        out = out * lax.rsqrt(ms + NORM_EPS) * g_ref[...]
    o_ref[...] = out


def _moe(x, g, route, cnt, wg, wu, wd, li, final_g=None, tf=512, t_disp=1024, t_comb=512):
    t, d = x.shape
    _, ne, _, ff = wg.shape
    rows = MOE_TILE
    n_rows = 2 * t + ne * rows
    n_tiles = n_rows // rows
    nf = ff // tf
    g2 = g.reshape(1, d)

    counts = cnt[0, :ne].astype(jnp.int32)
    gsz = (counts + rows - 1) // rows * rows
    ends = jnp.cumsum(gsz)
    offs = ends - gsz
    i1 = route[:, ROUTE_I1].astype(jnp.int32)
    i2 = route[:, ROUTE_I2].astype(jnp.int32)
    pos1 = offs[i1] + route[:, ROUTE_R1].astype(jnp.int32)
    pos2 = offs[i2] + route[:, ROUTE_R2].astype(jnp.int32)
    tile_start = jnp.arange(n_tiles, dtype=jnp.int32) * rows
    tile_expert = jnp.minimum(jnp.sum(tile_start[:, None] >= ends[None, :], axis=1), ne - 1).astype(jnp.int32)
    n_used = (ends[-1] // rows).astype(jnp.int32).reshape(1)

    smem_idx = lambda n: pl.BlockSpec((n,), lambda i: (i,), memory_space=pltpu.SMEM)
    hbm = pl.BlockSpec(memory_space=pl.ANY)
    xs = pl.pallas_call(
        _dispatch_kernel,
        name="moe_dispatch",
        grid=(t // t_disp,),
        in_specs=[smem_idx(t_disp), smem_idx(t_disp), pl.BlockSpec((t_disp, d), lambda i: (i, 0)), hbm],
        out_specs=hbm,
        out_shape=jax.ShapeDtypeStruct((n_rows, d), F32),
        scratch_shapes=[pltpu.SemaphoreType.DMA(())],
        input_output_aliases={3: 0},
        compiler_params=_cparams("arbitrary"),
    )(pos1, pos2, x, jnp.zeros((n_rows, d), F32))

    def w_in_map(i, f, te, nu):
        return li, te[i], 0, jnp.where(i < nu[0], f, nf - 1)

    def w_out_map(i, f, te, nu):
        return li, te[i], jnp.where(i < nu[0], f, nf - 1), 0

    y = pl.pallas_call(
        _expert_kernel,
        name="moe_experts",
        grid_spec=pltpu.PrefetchScalarGridSpec(
            num_scalar_prefetch=2,
            grid=(n_tiles, nf),
            in_specs=[pl.BlockSpec((rows, d), lambda i, f, te, nu: (i, 0)),
                      pl.BlockSpec((1, d), lambda i, f, te, nu: (0, 0)),
                      pl.BlockSpec((None, None, d, tf), w_in_map),
                      pl.BlockSpec((None, None, d, tf), w_in_map),
                      pl.BlockSpec((None, None, tf, d), w_out_map)],
            out_specs=pl.BlockSpec((rows, d), lambda i, f, te, nu: (i, 0)),
            scratch_shapes=[pltpu.VMEM((rows, d), BF16)]),
        out_shape=jax.ShapeDtypeStruct((n_rows, d), F32),
        compiler_params=_cparams("arbitrary", "arbitrary"),
    )(tile_expert, n_used, xs, g2, wg, wu, wd)

    gain = jnp.ones((1, d), F32) if final_g is None else final_g.reshape(1, d)
    return pl.pallas_call(
        functools.partial(_combine_kernel, final_g is not None),
        name="moe_combine",
        grid=(t // t_comb,),
        in_specs=[smem_idx(t_comb), smem_idx(t_comb),
                  pl.BlockSpec((t_comb, d), lambda i: (i, 0)),
                  pl.BlockSpec((t_comb, LANES), lambda i: (i, 0)),
                  pl.BlockSpec((1, d), lambda i: (0, 0)),
                  hbm],
        out_specs=pl.BlockSpec((t_comb, d), lambda i: (i, 0)),
        out_shape=jax.ShapeDtypeStruct((t, d), F32),
        scratch_shapes=[pltpu.VMEM((t_comb, d), F32), pltpu.VMEM((t_comb, d), F32),
                        pltpu.SemaphoreType.DMA(())],
        compiler_params=_cparams("arbitrary"),
    )(pos1, pos2, x, route, gain, y)


def _gelu_tanh(x):
    return 0.5 * x * (1.0 + jnp.tanh(math.sqrt(2.0 / math.pi) * (x + 0.044715 * (x * x * x))))


def _compress_kernel(kv_ref, pe_ref, w1_ref, w2a_ref, w2b_ref, ka_ref, kb_ref, xf_ref, sh_ref):
    seq = kv_ref.shape[0]
    n = seq // CMP_STRIDE
    hid2 = w1_ref.shape[2]
    xf_ref[...] = kv_ref[...].astype(F32)
    top = jnp.zeros((n, hid2), F32)
    bot = jnp.zeros((n, hid2), F32)
    for i in range(CMP_STRIDE):
        xi = xf_ref[pl.ds(i, n, stride=CMP_STRIDE), :]
        top = top + _dot((xi + pe_ref[i:i + 1, :]).astype(BF16), w1_ref[i])
        bot = bot + _dot((xi + pe_ref[CMP_STRIDE + i:CMP_STRIDE + i + 1, :]).astype(BF16), w1_ref[CMP_STRIDE + i])
    sh_ref[0:n, :] = bot
    sh_ref[n:n + 8, :] = jnp.zeros((8, hid2), F32)
    act = _gelu_tanh(top + sh_ref[1:n + 1, :]).astype(BF16)
    ka_ref[...] = _dot(act, w2a_ref[...]).astype(ka_ref.dtype)
    kb_ref[...] = _dot(act, w2b_ref[...]).astype(kb_ref.dtype)


def _compress(kv, pe_k, pe_v, k_w1, k_w2, v_w1, v_w2, batch, seq):
    assert CMP_LEN == 2 * CMP_STRIDE
    hid = k_w1.shape[1]
    n = seq // CMP_STRIDE
    w1 = jnp.zeros((CMP_LEN, LANES, 2 * hid), F32)
    w1 = w1.at[:, :HEAD_DIM, :hid].set(k_w1.reshape(CMP_LEN, HEAD_DIM, hid))
    w1 = w1.at[:, HEAD_DIM:, hid:].set(v_w1.reshape(CMP_LEN, HEAD_DIM, hid)).astype(BF16)
    zero = jnp.zeros((hid, HEAD_DIM), F32)
    w2a = jnp.block([[k_w2, zero], [zero, v_w2]]).astype(BF16)
    w2b = jnp.block([[zero, k_w2], [v_w2, zero]]).astype(BF16)
    pe = jnp.concatenate([pe_k, pe_v], axis=-1)
    g = NSA_GROUPS
    out = jax.ShapeDtypeStruct((batch * g * n, LANES), BF16)
    return pl.pallas_call(
        _compress_kernel,
        name="compress",
        grid=(batch, g),
        in_specs=[pl.BlockSpec((seq, LANES), lambda b, gi: (b, gi)),
                  pl.BlockSpec((CMP_LEN, LANES), lambda b, gi: (0, 0)),
                  pl.BlockSpec((CMP_LEN, LANES, 2 * hid), lambda b, gi: (0, 0, 0)),
                  pl.BlockSpec((2 * hid, LANES), lambda b, gi: (0, 0)),
                  pl.BlockSpec((2 * hid, LANES), lambda b, gi: (0, 0))],
        out_specs=[pl.BlockSpec((n, LANES), lambda b, gi: (b * g + gi, 0)),
                   pl.BlockSpec((n, LANES), lambda b, gi: (b * g + gi, 0))],
        out_shape=[out, out],
        scratch_shapes=[pltpu.VMEM((seq, LANES), F32), pltpu.VMEM((n + 8, 2 * hid), F32)],
        compiler_params=_cparams("parallel", "parallel"),
    )(kv, pe, w1, w2a, w2b)


def _nsa_cmp_kernel(n_slc, q_ref, ka_ref, kb_ref, ovt_ref, ocmp_ref, selm_ref):
    qi = pl.program_id(2)
    tq = q_ref.shape[0]
    n_pad = ka_ref.shape[0]
    t_q = qi * tq + lax.broadcasted_iota(jnp.int32, (tq, n_pad), 0)
    n_c = lax.broadcasted_iota(jnp.int32, (tq, n_pad), 1)
    mask_c = (n_c * CMP_STRIDE + (CMP_LEN - 1)) <= t_q
    halves = _head_halves(tq)
    ka = ka_ref[...]
    kb = kb_ref[...]
    psum = jnp.zeros((tq, n_pad), F32)
    for pr in range(NSA_HPG // 2):
        qf = q_ref[:, pr * LANES:(pr + 1) * LANES].astype(F32)
        outs = []
        for h in range(2):
            qz = jnp.where(halves[h], qf, 0.0).astype(BF16)
            kmat, vmat = (ka, kb) if h == 0 else (kb, ka)
            s = jnp.where(mask_c, _dot_nt(qz, kmat), NEG_INF)
            m = jnp.max(s, axis=1, keepdims=True)
            m = jnp.where(m == NEG_INF, 0.0, m)
            e = jnp.exp(s - m)
            p = e / jnp.maximum(jnp.sum(e, axis=1, keepdims=True), 1e-30)
            psum = psum + p
            outs.append(_dot(p.astype(BF16), vmat))
        ocmp_ref[:, pr * LANES:(pr + 1) * LANES] = jnp.where(halves[0], outs[0], outs[1]).astype(ocmp_ref.dtype)

    ps_hi, ps_lo = _split_bf16(psum)
    ovt = ovt_ref[...]
    p_slc = (_dot_nt(ovt, ps_hi) + _dot_nt(ovt, ps_lo))[0:n_slc]
    cand = lax.broadcasted_iota(jnp.int32, (n_slc, tq), 0)
    t_blk = jnp.right_shift(qi * tq + lax.broadcasted_iota(jnp.int32, (n_slc, tq), 1), int(math.log2(SLC_BLOCK)))
    valid = cand <= t_blk
    forced = (cand == 0) | (cand == t_blk) | (cand == t_blk - 1)
    score = jnp.where(valid, jnp.where(forced, FORCE_SCORE, p_slc), -1.0)
    keep = _rank_select(score, valid, n_slc, min(SLC_TOPN, n_slc))
    neg = jnp.where(keep, 0.0, MASK_NEG)
    zero = jnp.zeros((HEAD_DIM - n_slc, tq), F32)
    selm_ref[...] = jnp.concatenate([neg, zero, neg, zero], axis=0).T.astype(selm_ref.dtype)


def _nsa_sw_tiles(tiles, q_ref, selm_ref, gate_ref, ex_ref, ocmp_ref, o_ref, ksel_ref, vsel_ref, kwin_ref, vwin_ref):
    tq = ATT_TILE
    halves = _head_halves(tq)
    causal, beyond = _tile_masks(tq)
    ex = ex_ref[...]
    sel_s, win_s = {}, {}
    for c in tiles:
        rows = slice(c * tq, (c + 1) * tq)
        w = (c + 1) * tq
        lo = max(c - WINDOW // tq, 0) * tq
        qf = q_ref[rows, :].astype(F32)
        selm = selm_ref[rows, :].astype(F32)
        for h in range(2):
            q_aug = jnp.where(halves[h], qf, selm).astype(BF16)
            qz = jnp.where(halves[h], qf, 0.0).astype(BF16)
            sel_s[c, h] = _mask_tile(_dot_nt(q_aug, ksel_ref[h, 0:w, :]), c * tq, causal)
            s = _mask_tile(_dot_nt(qz, kwin_ref[h, lo:w, :]), c * tq - lo, causal)
            if c * tq - lo == WINDOW:
                s = _mask_tile(s, 0, beyond)
            win_s[c, h] = s
    for c in tiles:
        rows = slice(c * tq, (c + 1) * tq)
        w = (c + 1) * tq
        lo = max(c - WINDOW // tq, 0) * tq
        win_o = [_softmax_pv(win_s[c, h], vwin_ref[h, lo:w, :], h) for h in range(2)]
        sel_o = [_softmax_pv(sel_s[c, h], vsel_ref[h, 0:w, :], h) for h in range(2)]
        o_sel = jnp.where(halves[0], sel_o[0], sel_o[1])
        o_win = jnp.where(halves[0], win_o[0], win_o[1])
        g_hi, g_lo = _split_bf16(gate_ref[rows, :])
        gexp = _dot(g_hi, ex) + _dot(g_lo, ex)
        y = (gexp[:, 0:LANES] * ocmp_ref[rows, :].astype(F32) + gexp[:, LANES:2 * LANES] * o_sel
             + gexp[:, 2 * LANES:3 * LANES] * o_win)
        o_ref[rows, :] = y.astype(o_ref.dtype)


def _nsa_sw_kernel(q_ref, selm_ref, kvs_ref, kvw_ref, gate_ref, ex_ref, ocmp_ref, o_ref,
                   ksel_ref, vsel_ref, kwin_ref, vwin_ref):
    pr = pl.program_id(2)
    qi = pl.program_id(3)
    seq = kvs_ref.shape[0]

    @pl.when((pr == 0) & (qi == 0))
    def _():
        first, second = _head_halves(seq)
        onehot = _block_onehot(seq, int(math.log2(SLC_BLOCK)))
        kv = kvs_ref[...].astype(F32)
        swapped = pltpu.roll(kv, HEAD_DIM, axis=1)
        ksel_ref[0] = jnp.where(first, kv, onehot).astype(BF16)
        ksel_ref[1] = jnp.where(second, swapped, onehot).astype(BF16)
        vsel_ref[0] = _with_ones(swapped, first)
        vsel_ref[1] = _with_ones(kv, second)
        kv = kvw_ref[...].astype(F32)
        swapped = pltpu.roll(kv, HEAD_DIM, axis=1)
        kwin_ref[0] = kvw_ref[...]
        kwin_ref[1] = swapped.astype(BF16)
        vwin_ref[0] = _with_ones(swapped, first)
        vwin_ref[1] = _with_ones(kv, second)

    nq = seq // ATT_TILE
    for j in range(nq // TILES_PER_STEP):
        pl.when(qi == j)(functools.partial(_nsa_sw_tiles, _step_tiles(j, nq), q_ref, selm_ref, gate_ref, ex_ref,
                                           ocmp_ref, o_ref, ksel_ref, vsel_ref, kwin_ref, vwin_ref))


def _nsa_attention(q_raw, q_rot, kv, ka, kb, gates, batch, seq):
    nq = seq // ATT_TILE
    g = NSA_GROUPS
    gw = NSA_HPG * HEAD_DIM
    n_pairs = NSA_HEADS // 2
    n_cmp = (seq - CMP_LEN) // CMP_STRIDE + 1
    n_cmp_pad = seq // CMP_STRIDE
    n_slc = seq // SLC_BLOCK
    assert n_cmp_pad == LANES and n_slc <= HEAD_DIM // 2 and WINDOW % ATT_TILE == 0

    ex = np.zeros((n_pairs, LANES, 3 * LANES), np.float32)
    for hd in range(NSA_HEADS):
        for br in range(3):
            lane0 = br * LANES + (hd % 2) * HEAD_DIM
            ex[hd // 2, hd * 3 + br, lane0:lane0 + HEAD_DIM] = 1.0
    c_s = np.arange(n_cmp) * CMP_STRIDE
    s_s = np.arange(n_slc) * SLC_BLOCK
    ov = np.clip(np.minimum(c_s[:, None] + CMP_LEN, s_s[None, :] + SLC_BLOCK)
                 - np.maximum(c_s[:, None], s_s[None, :]), 0, None) / CMP_LEN
    ovt = np.zeros((LANES, n_cmp_pad), np.float32)
    ovt[:n_slc, :n_cmp] = ov.T

    n_ct = seq // CMP_TILE
    o_cmp, selm = pl.pallas_call(
        functools.partial(_nsa_cmp_kernel, n_slc),
        name="nsa_cmp",
        grid=(batch, g, n_ct),
        in_specs=[pl.BlockSpec((CMP_TILE, gw), lambda b, gi, i: (b * n_ct + i, gi)),
                  pl.BlockSpec((n_cmp_pad, LANES), lambda b, gi, i: (b * g + gi, 0)),
                  pl.BlockSpec((n_cmp_pad, LANES), lambda b, gi, i: (b * g + gi, 0)),
                  pl.BlockSpec((LANES, n_cmp_pad), lambda b, gi, i: (0, 0))],
        out_specs=[pl.BlockSpec((CMP_TILE, gw), lambda b, gi, i: (b * n_ct + i, gi)),
                   pl.BlockSpec((CMP_TILE, LANES), lambda b, gi, i: (b * n_ct + i, gi))],
        out_shape=[jax.ShapeDtypeStruct((batch * seq, NSA_HEADS * HEAD_DIM), BF16),
                   jax.ShapeDtypeStruct((batch * seq, g * LANES), BF16)],
        compiler_params=_cparams("parallel", "parallel", "parallel"),
    )(q_raw, ka, kb, jnp.asarray(ovt, BF16))

    ppg = NSA_HPG // 2
    return pl.pallas_call(
        _nsa_sw_kernel,
        name="nsa_sw",
        grid=(batch, g, ppg, nq // TILES_PER_STEP),
        in_specs=[pl.BlockSpec((seq, LANES), lambda b, gi, p, i: (b, gi * ppg + p)),
                  pl.BlockSpec((seq, LANES), lambda b, gi, p, i: (b, gi)),
                  pl.BlockSpec((seq, LANES), lambda b, gi, p, i: (b, g + gi)),
                  pl.BlockSpec((seq, LANES), lambda b, gi, p, i: (b, 2 * g + gi)),
                  pl.BlockSpec((seq, LANES), lambda b, gi, p, i: (b, 0)),
                  pl.BlockSpec((None, LANES, 3 * LANES), lambda b, gi, p, i: (gi * ppg + p, 0, 0)),
                  pl.BlockSpec((seq, LANES), lambda b, gi, p, i: (b, gi * ppg + p))],
        out_specs=pl.BlockSpec((seq, LANES), lambda b, gi, p, i: (b, gi * ppg + p)),
        out_shape=jax.ShapeDtypeStruct((batch * seq, NSA_HEADS * HEAD_DIM), BF16),
        scratch_shapes=[pltpu.VMEM((2, seq, LANES), BF16) for _ in range(4)],
        compiler_params=_cparams("parallel", "parallel", "arbitrary", "arbitrary"),
    )(q_rot, selm, kv, kv, gates, jnp.asarray(ex, BF16), o_cmp)


def _even_mixer(x, g, w_in, pool_w, pool_scale, w_out, rope_pair, batch, seq):
    chunk = 2 * LANES
    plan = [(0, POOL_DIM, [(0, 0, "none", 0)])]
    for c in range(3 * MOBA_DIM // chunk):
        op = "rope_scale" if c < MOBA_DIM // chunk else ("rope" if c < 2 * MOBA_DIM // chunk else "none")
        plan.append((POOL_DIM + c * chunk, chunk, [(1, c * chunk, op, 0)]))
    u_pool, qkv = _norm_project(x, g, w_in.astype(BF16), plan, [rope_pair],
                                [(POOL_DIM, F32), (3 * MOBA_DIM, BF16)], seq)
    w_bd = jax.scipy.linalg.block_diag(*[pool_w[i] for i in range(pool_w.shape[0])]).astype(BF16)
    y_a = _pool_mixer(u_pool, w_bd, pool_scale, batch, seq)
    y_b = _moba_attention(qkv, batch, seq)
    w_out = w_out.astype(BF16)
    return [y_a, y_b], [w_out[:POOL_DIM], w_out[POOL_DIM:]]


def _odd_mixer(x, g, w_in, pe_k, pe_v, k_w1, k_w2, v_w1, v_w2, w_out, rope_pair, rope_single, batch, seq):
    d = x.shape[1]
    qd = NSA_HEADS * HEAD_DIM
    kvd = NSA_KV_DIM
    wq = w_in[:, :qd]
    parts = [w_in[:, qd + i * kvd:qd + (i + 1) * kvd].reshape(d, NSA_GROUPS, HEAD_DIM) for i in range(6)]
    pairs = [jnp.concatenate([parts[2 * i], parts[2 * i + 1]], axis=-1).reshape(d, 2 * kvd) for i in range(3)]
    n_gate = 3 * NSA_HEADS
    w_gate = jnp.zeros((d, LANES), F32).at[:, :n_gate].set(w_in[:, qd + 6 * kvd:])
    w_all = jnp.concatenate([wq] + pairs + [w_gate], axis=1).astype(BF16)

    chunk = 2 * LANES
    plan = []
    for c in range(qd // chunk):
        plan.append((c * chunk, chunk, [(0, c * chunk, "scale", 0), (1, c * chunk, "rope_scale", 0)]))
    for c in range(6 * kvd // chunk):
        branch = c // (2 * kvd // chunk)
        op = "none" if branch == 0 else "rope"
        plan.append((qd + c * chunk, chunk, [(2, c * chunk, op, 1)]))
    plan.append((qd + 6 * kvd, LANES, [(3, 0, "sigmoid", 0)]))
    q_raw, q_rot, kv, gates = _norm_project(
        x, g, w_all, plan, [rope_pair, rope_single],
        [(qd, BF16), (qd, BF16), (6 * kvd, BF16), (LANES, F32)], seq)

    ka, kb = _compress(kv, pe_k, pe_v, k_w1, k_w2, v_w1, v_w2, batch, seq)
    y = _nsa_attention(q_raw, q_rot, kv, ka, kb, gates, batch, seq)
    return [y], [w_out.astype(BF16)]


def kernel(x, mem, positions, norm_g, mem_g, final_g, w_in_ab, pool_w, pool_scale, w_out_ab, ffn_w_gate, ffn_w_up, ffn_w_down, w_in_c, cmp_pe_k, cmp_pe_v, cmp_k_w1, cmp_k_w2, cmp_v_w1, cmp_v_w2, w_out_c, router_w, moe_w_gate, moe_w_up, moe_w_down, xa_wq, xa_wkv, xa_wo):
    batch, seq, d = x.shape
    mem_len = mem.shape[1]
    depth = norm_g.shape[0]
    rope_pair, rope_single = _rope_tables(positions)

    wkv_all = jnp.concatenate([xa_wkv[l] for l in range(depth)], axis=1).astype(BF16)
    kv_all = _norm_matmul(mem.reshape(batch * mem_len, d), mem_g, wkv_all)

    xf = x.reshape(batch * seq, d)
    for layer in range(depth):
        i = layer // 2
        last = layer == depth - 1
        if layer % 2 == 0:
            ys, ws = _even_mixer(xf, norm_g[layer, 0], w_in_ab[i], pool_w[i], pool_scale[i], w_out_ab[i],
                                 rope_pair, batch, seq)
        else:
            ys, ws = _odd_mixer(xf, norm_g[layer, 0], w_in_c[i], cmp_pe_k[i], cmp_pe_v[i], cmp_k_w1[i],
                                cmp_k_w2[i], cmp_v_w1[i], cmp_v_w2[i], w_out_c[i], rope_pair, rope_single,
                                batch, seq)
        xf = _out_proj(xf, ys, ws)
        mid_args = (xf, norm_g[layer, 1], xa_wq[layer].astype(BF16), kv_all, layer,
                    xa_wo[layer].astype(BF16), seq, mem_len)
        if layer % 2 == 0:
            xf = _mid_layer(*mid_args)
            xf = _ffn(xf, norm_g[layer, 2], ffn_w_gate[i].astype(BF16), ffn_w_up[i].astype(BF16),
                      ffn_w_down[i].astype(BF16))
            if last:
                xf = _final_norm(xf, final_g)
        else:
            xf, route, cnt = _mid_layer(*mid_args, router=(norm_g[layer, 2], router_w[i]))
            xf = _moe(xf, norm_g[layer, 2], route, cnt, moe_w_gate, moe_w_up, moe_w_down, i,
                      final_g=final_g if last else None)
    return xf.reshape(batch, seq, d)
```

```python
import functools
import math

import numpy as np
import jax
import jax.numpy as jnp
from jax import lax
from jax.experimental import pallas as pl
from jax.experimental.pallas import tpu as pltpu

F32 = jnp.float32
BF16 = jnp.bfloat16

HEAD_DIM = 64
ROPE_DIM = 16
ROPE_THETA = 500000.0
NORM_EPS = 1e-5
QK_SCALE = HEAD_DIM ** -0.5
QK_SCALE_LOG2E = QK_SCALE * math.log2(math.e)

POOL_WINDOWS = (2, 4, 8, 16)
POOL_DIM = 256
MOBA_HEADS = 12
MOBA_DIM = MOBA_HEADS * HEAD_DIM
MOBA_BLOCK = 256
MOBA_TOPK = 3
NSA_HEADS = 16
NSA_GROUPS = 4
NSA_HPG = 4
NSA_KV_DIM = NSA_GROUPS * HEAD_DIM
CMP_LEN = 32
CMP_STRIDE = 16
SLC_BLOCK = 64
SLC_TOPN = 16
WINDOW = 512
FORCE_SCORE = 1e4
XA_HEADS = 4
XA_DIM = XA_HEADS * HEAD_DIM
N_EXPERTS = 8

LANES = 128
ATT_TILE = 256
CMP_TILE = 2048
TILES_PER_STEP = 4
VMEM_LIMIT = 56 * 1024 * 1024
NEG_INF = float("-inf")
MASK_NEG = -1e30


def _cparams(*sem):
    return pltpu.CompilerParams(dimension_semantics=sem, vmem_limit_bytes=VMEM_LIMIT)


def _dot(a, b):
    return jnp.dot(a, b, preferred_element_type=F32)


def _dot_nt(a, b):
    return lax.dot_general(a, b, (((1,), (1,)), ((), ())), preferred_element_type=F32)


def _split_bf16(a):
    hi = a.astype(BF16)
    lo = (a - hi.astype(F32)).astype(BF16)
    return hi, lo


def _rms_bf16(x, g):
    ms = jnp.mean(x * x, axis=-1, keepdims=True)
    return (x * lax.rsqrt(ms + NORM_EPS) * g).astype(BF16)


def _apply_rope(acc, c, a, b):
    half = ROPE_DIM // 2
    return acc * c + pltpu.roll(acc, LANES - half, axis=1) * a + pltpu.roll(acc, half, axis=1) * b


def _proj_kernel(plan, n_tab, x_ref, g_ref, w_ref, *rest):
    tabs = rest[:3 * n_tab]
    outs = rest[3 * n_tab:]
    h = _rms_bf16(x_ref[...], g_ref[...])
    for wc0, width, sinks in plan:
        acc = _dot(h, w_ref[:, wc0:wc0 + width])
        for oi, oc0, op, tab in sinks:
            for s in range(width // LANES):
                val = acc[:, s * LANES:(s + 1) * LANES]
                if op in ("rope", "rope_scale"):
                    c, a, b = (tabs[3 * tab + i][...] for i in range(3))
                    val = _apply_rope(val, c, a, b)
                if op == "scale":
                    val = val * QK_SCALE
                if op == "rope_scale":
                    val = val * QK_SCALE_LOG2E
                if op == "sigmoid":
                    val = jax.nn.sigmoid(val)
                o = outs[oi]
                o[:, oc0 + s * LANES:oc0 + (s + 1) * LANES] = val.astype(o.dtype)


def _norm_project(x, g, w, plan, tables, out_defs, seq, tm=512):
    t, d = x.shape
    n = w.shape[1]
    n_seq_tiles = seq // tm
    in_specs = [
        pl.BlockSpec((tm, d), lambda i: (i, 0)),
        pl.BlockSpec((1, d), lambda i: (0, 0)),
        pl.BlockSpec((d, n), lambda i: (0, 0)),
    ]
    flat_tabs = []
    for tset in tables:
        for tb in tset:
            flat_tabs.append(tb)
            in_specs.append(pl.BlockSpec((tm, LANES), lambda i: (i % n_seq_tiles, 0)))
    out_shape = [jax.ShapeDtypeStruct((t, wd), dt) for wd, dt in out_defs]
    out_specs = [pl.BlockSpec((tm, wd), lambda i: (i, 0)) for wd, _ in out_defs]
    return pl.pallas_call(
        functools.partial(_proj_kernel, plan, len(tables)),
        name="proj",
        grid=(t // tm,),
        in_specs=in_specs,
        out_specs=out_specs,
        out_shape=out_shape,
        compiler_params=_cparams("parallel"),
    )(x, g.reshape(1, d), w, *flat_tabs)


def _rope_tables(positions):
    half = ROPE_DIM // 2
    inv = ROPE_THETA ** (-jnp.arange(0, ROPE_DIM, 2, dtype=F32) / ROPE_DIM)
    ang = positions.astype(F32)[:, None] * inv[None, :]
    cos, sin = jnp.cos(ang), jnp.sin(ang)
    s = positions.shape[0]
    z8 = jnp.zeros((s, half), F32)
    rest0 = jnp.zeros((s, HEAD_DIM - ROPE_DIM), F32)
    rest1 = jnp.ones((s, HEAD_DIM - ROPE_DIM), F32)
    c64 = jnp.concatenate([cos, cos, rest1], -1)
    a64 = jnp.concatenate([-sin, z8, rest0], -1)
    b64 = jnp.concatenate([z8, sin, rest0], -1)
    one64 = jnp.ones((s, HEAD_DIM), F32)
    zero64 = jnp.zeros((s, HEAD_DIM), F32)
    pair = tuple(jnp.concatenate([m, m], -1) for m in (c64, a64, b64))
    single = (jnp.concatenate([c64, one64], -1), jnp.concatenate([a64, zero64], -1),
              jnp.concatenate([b64, zero64], -1))
    return pair, single


def _norm_matmul_kernel(x_ref, g_ref, w_ref, o_ref):
    h = _rms_bf16(x_ref[...], g_ref[...])
    o_ref[...] = _dot(h, w_ref[...]).astype(o_ref.dtype)


def _norm_matmul(x, g, w, tm=512):
    t, d = x.shape
    n = w.shape[1]
    return pl.pallas_call(
        _norm_matmul_kernel,
        name="mem_kv",
        grid=(t // tm,),
        in_specs=[pl.BlockSpec((tm, d), lambda i: (i, 0)),
                  pl.BlockSpec((1, d), lambda i: (0, 0)),
                  pl.BlockSpec((d, n), lambda i: (0, 0))],
        out_specs=pl.BlockSpec((tm, n), lambda i: (i, 0)),
        out_shape=jax.ShapeDtypeStruct((t, n), BF16),
        compiler_params=_cparams("parallel"),
    )(x, g.reshape(1, d), w)


def _final_norm_kernel(x_ref, g_ref, o_ref):
    x = x_ref[...]
    ms = jnp.mean(x * x, axis=-1, keepdims=True)
    o_ref[...] = x * lax.rsqrt(ms + NORM_EPS) * g_ref[...]


def _final_norm(x, g, tm=1024):
    t, d = x.shape
    return pl.pallas_call(
        _final_norm_kernel,
        name="final_norm",
        grid=(t // tm,),
        in_specs=[pl.BlockSpec((tm, d), lambda i: (i, 0)), pl.BlockSpec((1, d), lambda i: (0, 0))],
        out_specs=pl.BlockSpec((tm, d), lambda i: (i, 0)),
        out_shape=jax.ShapeDtypeStruct((t, d), F32),
        compiler_params=_cparams("parallel"),
    )(x, g.reshape(1, d))


def _pool_kernel(u_ref, w_ref, sc_ref, o_ref, pad_ref):
    s = u_ref.shape[0]
    maxw = POOL_WINDOWS[-1]
    u = u_ref[...]
    pad_ref[0:maxw, :] = jnp.zeros((maxw, POOL_DIM), F32)
    t1 = (lax.broadcasted_iota(jnp.int32, (s, POOL_DIM), 0) + 1).astype(F32)
    lane = lax.broadcasted_iota(jnp.int32, (s, POOL_DIM), 1)
    gdim = POOL_DIM // len(POOL_WINDOWS)
    cur = u
    pooled = jnp.zeros_like(u)
    shift = 1
    for gi, w in enumerate(POOL_WINDOWS):
        pad_ref[maxw:maxw + s, :] = cur
        cur = cur + pad_ref[maxw - shift:maxw - shift + s, :]
        shift *= 2
        assert shift == w
        mean = cur / jnp.minimum(t1, float(w))
        pooled = jnp.where((lane >= gi * gdim) & (lane < (gi + 1) * gdim), mean, pooled)
    pooled = (pooled - u).astype(BF16)
    o_ref[...] = (_dot(pooled, w_ref[...]) * sc_ref[...]).astype(o_ref.dtype)


def _pool_mixer(u_pool, w_bd, scale, batch, seq):
    return pl.pallas_call(
        _pool_kernel,
        name="pool",
        grid=(batch,),
        in_specs=[pl.BlockSpec((seq, POOL_DIM), lambda b: (b, 0)),
                  pl.BlockSpec((POOL_DIM, POOL_DIM), lambda b: (0, 0)),
                  pl.BlockSpec((1, POOL_DIM), lambda b: (0, 0))],
        out_specs=pl.BlockSpec((seq, POOL_DIM), lambda b: (b, 0)),
        out_shape=jax.ShapeDtypeStruct((batch * seq, POOL_DIM), BF16),
        scratch_shapes=[pltpu.VMEM((seq + POOL_WINDOWS[-1], POOL_DIM), F32)],
        compiler_params=_cparams("parallel"),
    )(u_pool, w_bd, scale.reshape(1, POOL_DIM))


def _rank_select(scoreT, valid, n_rows, topn):
    row = lax.broadcasted_iota(jnp.int32, scoreT.shape, 0)
    rank = jnp.zeros(scoreT.shape, F32)
    for j in range(n_rows):
        sj = scoreT[j:j + 1, :]
        beats = (scoreT > sj) | ((scoreT == sj) & (row < j))
        rj = jnp.sum(beats.astype(F32), axis=0, keepdims=True)
        rank = jnp.where(row == j, rj, rank)
    return (rank < float(topn)) & valid


def _head_halves(rows):
    lane = lax.broadcasted_iota(jnp.int32, (rows, LANES), 1)
    return lane < HEAD_DIM, lane >= HEAD_DIM


def _tile_masks(tq):
    qry = lax.broadcasted_iota(jnp.int32, (tq, tq), 0)
    key = lax.broadcasted_iota(jnp.int32, (tq, tq), 1)
    return key <= qry, key > qry


def _mask_tile(s, t0, mask):
    tq = mask.shape[0]
    parts = []
    if t0 > 0:
        parts.append(s[:, :t0])
    parts.append(jnp.where(mask, s[:, t0:t0 + tq], NEG_INF))
    if t0 + tq < s.shape[1]:
        parts.append(s[:, t0 + tq:])
    return parts[0] if len(parts) == 1 else jnp.concatenate(parts, axis=1)


def _softmax_pv(s, v, h):
    m = jnp.max(s, axis=1, keepdims=True)
    p = jnp.exp2((s - m).astype(BF16))
    o = _dot(p, v)
    denom_lane = (1 - h) * HEAD_DIM
    return o / o[:, denom_lane:denom_lane + 1]


def _with_ones(v, own):
    return jnp.where(own, v, 1.0).astype(BF16)


def _step_tiles(j, nq):
    half = TILES_PER_STEP // 2
    return tuple(t for k in range(half) for t in (nq - 1 - (j * half + k), j * half + k))


def _block_onehot(seq, shift):
    lane = lax.broadcasted_iota(jnp.int32, (seq, LANES), 1)
    blk = jnp.right_shift(lax.broadcasted_iota(jnp.int32, (seq, LANES), 0), shift)
    return jnp.where((lane & (HEAD_DIM - 1)) == blk, 1.0, 0.0)


def _moba_tiles(tiles, o_ref, qaug_ref, kaug_ref, vaug_ref):
    tq = ATT_TILE
    halves = _head_halves(tq)
    causal, _ = _tile_masks(tq)
    scores = {}
    for c in tiles:
        for h in range(2):
            s = _dot_nt(qaug_ref[h, c * tq:(c + 1) * tq, :], kaug_ref[h, 0:(c + 1) * tq, :])
            scores[c, h] = _mask_tile(s, c * tq, causal)
    for c in tiles:
        outs = [_softmax_pv(scores[c, h], vaug_ref[h, 0:(c + 1) * tq, :], h) for h in range(2)]
        o_ref[c * tq:(c + 1) * tq, :] = jnp.where(halves[0], outs[0], outs[1]).astype(o_ref.dtype)


def _moba_kernel(q_ref, k_ref, v_ref, o_ref, qaug_ref, kaug_ref, vaug_ref, kmean_ref):
    qi = pl.program_id(2)
    seq = k_ref.shape[0]
    nb = seq // MOBA_BLOCK
    blk_shift = int(math.log2(MOBA_BLOCK))

    @pl.when(qi == 0)
    def _():
        halves = _head_halves(seq)
        kf = k_ref[...].astype(F32)
        vf = v_ref[...].astype(F32)
        qp = q_ref[...]
        qf = qp.astype(F32)
        onehot = _block_onehot(seq, blk_shift)
        kmean_ref[...] = jnp.zeros(kmean_ref.shape, F32)
        for j in range(nb):
            kmean_ref[j:j + 1, :] = jnp.mean(kf[j * MOBA_BLOCK:(j + 1) * MOBA_BLOCK], axis=0, keepdims=True)
        cand = lax.broadcasted_iota(jnp.int32, (nb, seq), 0)
        own = jnp.right_shift(lax.broadcasted_iota(jnp.int32, (nb, seq), 1), blk_shift)
        valid = cand < own
        for h in range(2):
            kaug_ref[h] = jnp.where(halves[h], kf, onehot).astype(BF16)
            vaug_ref[h] = _with_ones(vf, halves[h])
            mine_k = _head_halves(kmean_ref.shape[0])[h]
            km_hi, km_lo = _split_bf16(jnp.where(mine_k, kmean_ref[...], 0.0))
            gate = (_dot_nt(km_hi, qp) + _dot_nt(km_lo, qp))[0:nb]
            gate = jnp.where(valid, gate, NEG_INF)
            keep = _rank_select(gate, valid, nb, MOBA_TOPK) | (cand == own)
            neg = jnp.where(keep, 0.0, MASK_NEG)
            lead = HEAD_DIM if h == 0 else 0
            pieces = [neg, jnp.zeros((LANES - lead - nb, seq), F32)]
            if lead:
                pieces.insert(0, jnp.zeros((lead, seq), F32))
            maskcols = jnp.concatenate(pieces, axis=0).T
            qaug_ref[h] = jnp.where(halves[h], qf, maskcols).astype(BF16)

    for j in range(nb // TILES_PER_STEP):
        pl.when(qi == j)(functools.partial(_moba_tiles, _step_tiles(j, nb), o_ref, qaug_ref, kaug_ref, vaug_ref))


def _moba_attention(qkv, batch, seq):
    nq = seq // ATT_TILE
    assert MOBA_BLOCK == ATT_TILE and nq <= 16 and nq % TILES_PER_STEP == 0
    hp = MOBA_DIM // LANES
    return pl.pallas_call(
        _moba_kernel,
        name="moba",
        grid=(batch, hp, nq // TILES_PER_STEP),
        in_specs=[pl.BlockSpec((seq, LANES), lambda b, p, i: (b, p)),
                  pl.BlockSpec((seq, LANES), lambda b, p, i: (b, hp + p)),
                  pl.BlockSpec((seq, LANES), lambda b, p, i: (b, 2 * hp + p))],
        out_specs=pl.BlockSpec((seq, LANES), lambda b, p, i: (b, p)),
        out_shape=jax.ShapeDtypeStruct((batch * seq, MOBA_DIM), BF16),
        scratch_shapes=[pltpu.VMEM((2, seq, LANES), BF16), pltpu.VMEM((2, seq, LANES), BF16),
                        pltpu.VMEM((2, seq, LANES), BF16), pltpu.VMEM((16, LANES), F32)],
        compiler_params=_cparams("parallel", "parallel", "arbitrary"),
    )(qkv, qkv, qkv)


def _out_proj_kernel(n_in, x_ref, *rest):
    ys = rest[:n_in]
    ws = rest[n_in:2 * n_in]
    o_ref = rest[2 * n_in]
    acc = x_ref[...]
    for y, w in zip(ys, ws):
        acc = acc + _dot(y[...], w[...])
    o_ref[...] = acc


def _out_proj(x, ys, ws, tm=512):
    t, d = x.shape
    in_specs = [pl.BlockSpec((tm, d), lambda i: (i, 0))]
    in_specs += [pl.BlockSpec((tm, y.shape[1]), lambda i: (i, 0)) for y in ys]
    in_specs += [pl.BlockSpec(w.shape, lambda i: (0, 0)) for w in ws]
    return pl.pallas_call(
        functools.partial(_out_proj_kernel, len(ys)),
        name="out_proj",
        grid=(t // tm,),
        in_specs=in_specs,
        out_specs=pl.BlockSpec((tm, d), lambda i: (i, 0)),
        out_shape=jax.ShapeDtypeStruct((t, d), F32),
        compiler_params=_cparams("parallel"),
    )(x, *ys, *ws)


def _mid_kernel(with_route, x_ref, g_ref, wq_ref, kv_ref, wo_ref, *rest):
    o_ref = rest[3] if with_route else rest[0]
    x = x_ref[...]
    h = _rms_bf16(x, g_ref[...])
    q = _dot(h, wq_ref[...]) * QK_SCALE_LOG2E
    halves = _head_halves(x.shape[0])
    mem_halves = _head_halves(kv_ref.shape[0])
    out = x
    for pr in range(XA_HEADS // 2):
        lanes = slice(pr * LANES, (pr + 1) * LANES)
        qf = q[:, lanes]
        kp = kv_ref[:, lanes]
        vf = kv_ref[:, XA_DIM + pr * LANES:XA_DIM + (pr + 1) * LANES].astype(F32)
        scores = [_dot_nt(jnp.where(halves[hh], qf, 0.0).astype(BF16), kp) for hh in range(2)]
        outs = [_softmax_pv(scores[hh], _with_ones(vf, mem_halves[hh]), hh) for hh in range(2)]
        o_pair = jnp.where(halves[0], outs[0], outs[1]).astype(BF16)
        out = out + _dot(o_pair, wo_ref[lanes, :])
    o_ref[...] = out

    if with_route:
        g3_ref, rhi_ref, rlo_ref, _, route_ref, cnt_ref = rest

        @pl.when(pl.program_id(0) == 0)
        def _():
            cnt_ref[...] = jnp.zeros(cnt_ref.shape, F32)

        route_ref[...] = _route_rows(_rms_bf16(out, g3_ref[...]), rhi_ref[...], rlo_ref[...], cnt_ref)


def _mid_layer(x, g, wq, kv_all, layer, wo, seq, mem_len, router=None, tm=512):
    t, d = x.shape
    tiles_per_seq = seq // tm
    const = lambda i: (0, 0)
    rows = lambda i: (i, 0)
    in_specs = [pl.BlockSpec((tm, d), rows),
                pl.BlockSpec((1, d), const),
                pl.BlockSpec((d, XA_DIM), const),
                pl.BlockSpec((mem_len, 2 * XA_DIM), lambda i: (i // tiles_per_seq, layer)),
                pl.BlockSpec((XA_DIM, d), const)]
    args = [x, g.reshape(1, d), wq, kv_all, wo]
    out_specs = [pl.BlockSpec((tm, d), rows)]
    out_shape = [jax.ShapeDtypeStruct((t, d), F32)]
    if router is not None:
        g3, router_w = router
        rw = jnp.zeros((d, LANES), F32).at[:, :router_w.shape[1]].set(router_w)
        rhi = rw.astype(BF16)
        rlo = (rw - rhi.astype(F32)).astype(BF16)
        in_specs += [pl.BlockSpec((1, d), const), pl.BlockSpec((d, LANES), const), pl.BlockSpec((d, LANES), const)]
        args += [g3.reshape(1, d), rhi, rlo]
        out_specs += [pl.BlockSpec((tm, LANES), rows), pl.BlockSpec((8, LANES), const)]
        out_shape += [jax.ShapeDtypeStruct((t, LANES), F32), jax.ShapeDtypeStruct((8, LANES), F32)]
    res = pl.pallas_call(
        functools.partial(_mid_kernel, router is not None),
        name="mid",
        grid=(t // tm,),
        in_specs=in_specs,
        out_specs=out_specs,
        out_shape=out_shape,
        compiler_params=_cparams("arbitrary"),
    )(*args)
    return res if router is not None else res[0]


def _ffn_kernel(x_ref, g_ref, wg_ref, wu_ref, wd_ref, o_ref, h_ref):
    f = pl.program_id(1)

    @pl.when(f == 0)
    def _():
        x = x_ref[...]
        h_ref[...] = _rms_bf16(x, g_ref[...])
        o_ref[...] = x

    h = h_ref[...]
    gate = _dot(h, wg_ref[...])
    a = (gate * jax.nn.sigmoid(gate)) * _dot(h, wu_ref[...])
    o_ref[...] += _dot(a.astype(BF16), wd_ref[...])


def _ffn(x, g, wg, wu, wd, tm=1024, tf=256):
    t, d = x.shape
    ff = wg.shape[1]
    return pl.pallas_call(
        _ffn_kernel,
        name="ffn",
        grid=(t // tm, ff // tf),
        in_specs=[pl.BlockSpec((tm, d), lambda i, f: (i, 0)),
                  pl.BlockSpec((1, d), lambda i, f: (0, 0)),
                  pl.BlockSpec((d, tf), lambda i, f: (0, f)),
                  pl.BlockSpec((d, tf), lambda i, f: (0, f)),
                  pl.BlockSpec((tf, d), lambda i, f: (f, 0))],
        out_specs=pl.BlockSpec((tm, d), lambda i, f: (i, 0)),
        out_shape=jax.ShapeDtypeStruct((t, d), F32),
        scratch_shapes=[pltpu.VMEM((tm, d), BF16)],
        compiler_params=_cparams("parallel", "arbitrary"),
    )(x, g.reshape(1, d), wg, wu, wd)


ROUTE_I1, ROUTE_I2, ROUTE_W1, ROUTE_W2, ROUTE_R1, ROUTE_R2 = range(6)
MOE_TILE = 1024
MOE_SUB = 256


def _lane_pick(arr, lane, k):
    return jnp.sum(jnp.where(lane == k, arr, 0.0), axis=1, keepdims=True)


def _route_rows(h, rhi, rlo, cnt_ref):
    tm = h.shape[0]
    lane = lax.broadcasted_iota(jnp.int32, (tm, LANES), 1)
    lane_f = lane.astype(F32)
    logits = _dot(h, rhi) + _dot(h, rlo)
    logits = jnp.where(lane < N_EXPERTS, logits, NEG_INF)
    m1 = jnp.max(logits, axis=1, keepdims=True)
    i1 = jnp.min(jnp.where(logits == m1, lane_f, float(LANES)), axis=1, keepdims=True)
    rest = jnp.where(lane_f == i1, NEG_INF, logits)
    m2 = jnp.max(rest, axis=1, keepdims=True)
    i2 = jnp.min(jnp.where(rest == m2, lane_f, float(LANES)), axis=1, keepdims=True)
    e2 = jnp.exp(m2 - m1)
    w1 = 1.0 / (1.0 + e2)
    w2 = e2 / (1.0 + e2)

    chosen = jnp.where((lane_f == i1) | (lane_f == i2), 1.0, 0.0)
    r = lax.broadcasted_iota(jnp.int32, (tm, tm), 0)
    c = lax.broadcasted_iota(jnp.int32, (tm, tm), 1)
    earlier = jnp.where(c < r, 1.0, 0.0).astype(BF16)
    prefix = _dot(earlier, chosen.astype(BF16)) + cnt_ref[0:1, :]
    rank1 = jnp.sum(jnp.where(lane_f == i1, prefix, 0.0), axis=1, keepdims=True)
    rank2 = jnp.sum(jnp.where(lane_f == i2, prefix, 0.0), axis=1, keepdims=True)
    cnt_ref[0:1, :] = cnt_ref[0:1, :] + jnp.sum(chosen, axis=0, keepdims=True)

    cols = {ROUTE_I1: i1, ROUTE_I2: i2, ROUTE_W1: w1, ROUTE_W2: w2, ROUTE_R1: rank1, ROUTE_R2: rank2}
    route = jnp.zeros((tm, LANES), F32)
    for k, val in cols.items():
        route = jnp.where(lane == k, val, route)
    return route


def _dispatch_kernel(pos1_ref, pos2_ref, x_ref, xs_in, xs_hbm, sem):
    del xs_in
    n = pos1_ref.shape[0]

    def row_copy(r, p):
        return pltpu.make_async_copy(x_ref.at[pl.ds(r, 1)], xs_hbm.at[pl.ds(p, 1)], sem)

    def issue(r, carry):
        row_copy(r, pos1_ref[r]).start()
        row_copy(r, pos2_ref[r]).start()
        return carry

    lax.fori_loop(0, n, issue, 0, unroll=8)
    for _ in range(2):
        pltpu.make_async_copy(x_ref, xs_hbm.at[pl.ds(0, n)], sem).wait()


def _expert_kernel(te_ref, nu_ref, xs_ref, g_ref, wg_ref, wu_ref, wd_ref, y_ref, h_ref):
    del te_ref
    i = pl.program_id(0)
    f = pl.program_id(1)
    rows = xs_ref.shape[0]
    valid = nu_ref[1 + i]

    @pl.when(f == 0)
    def _():
        h_ref[...] = _rms_bf16(xs_ref[...], g_ref[...])
        y_ref[...] = jnp.zeros(y_ref.shape, y_ref.dtype)

    def swiglu(r0, nr):
        h = h_ref[r0:r0 + nr, :]
        gate = _dot(h, wg_ref[...].astype(BF16))
        a = (gate * jax.nn.sigmoid(gate)) * _dot(h, wu_ref[...].astype(BF16))
        y_ref[r0:r0 + nr, :] += _dot(a.astype(BF16), wd_ref[...].astype(BF16))

    pl.when(valid == rows)(functools.partial(swiglu, 0, rows))

    @pl.when((valid > 0) & (valid < rows))
    def _():
        for r0 in range(0, rows, MOE_SUB):
            pl.when(r0 < valid)(functools.partial(swiglu, r0, MOE_SUB))


def _combine_kernel(with_norm, pos1_ref, pos2_ref, x_ref, route_ref, g_ref, y_hbm, o_ref, buf1, buf2, sem):
    n = pos1_ref.shape[0]

    def row_copy(p, buf, r):
        return pltpu.make_async_copy(y_hbm.at[pl.ds(p, 1)], buf.at[pl.ds(r, 1)], sem)

    def issue(r, carry):
        row_copy(pos1_ref[r], buf1, r).start()
        row_copy(pos2_ref[r], buf2, r).start()
        return carry

    lax.fori_loop(0, n, issue, 0, unroll=8)
    for buf in (buf1, buf2):
        pltpu.make_async_copy(y_hbm.at[pl.ds(0, n)], buf, sem).wait()
    lane = lax.broadcasted_iota(jnp.int32, route_ref.shape, 1)
    route = route_ref[...]
    w1 = _lane_pick(route, lane, ROUTE_W1)
    w2 = _lane_pick(route, lane, ROUTE_W2)
    out = x_ref[...] + w1 * buf1[...] + w2 * buf2[...]
    if with_norm:
        ms = jnp.mean(out * out, axis=-1, keepdims=True)
        out = out * lax.rsqrt(ms + NORM_EPS) * g_ref[...]
    o_ref[...] = out


def _moe(x, g, route, cnt, wg, wu, wd, li, final_g=None, tf=512, t_disp=1024, t_comb=512):
    t, d = x.shape
    _, ne, _, ff = wg.shape
    rows = MOE_TILE
    n_rows = 2 * t + ne * rows
    n_tiles = n_rows // rows
    nf = ff // tf
    g2 = g.reshape(1, d)

    counts = cnt[0, :ne].astype(jnp.int32)
    gsz = (counts + rows - 1) // rows * rows
    ends = jnp.cumsum(gsz)
    offs = ends - gsz
    i1 = route[:, ROUTE_I1].astype(jnp.int32)
    i2 = route[:, ROUTE_I2].astype(jnp.int32)
    pos1 = offs[i1] + route[:, ROUTE_R1].astype(jnp.int32)
    pos2 = offs[i2] + route[:, ROUTE_R2].astype(jnp.int32)
    tile_start = jnp.arange(n_tiles, dtype=jnp.int32) * rows
    tile_expert = jnp.minimum(jnp.sum(tile_start[:, None] >= ends[None, :], axis=1), ne - 1).astype(jnp.int32)
    tile_valid = jnp.clip((offs + counts)[tile_expert] - tile_start, 0, rows)
    n_used = jnp.concatenate([(ends[-1] // rows).reshape(1), tile_valid]).astype(jnp.int32)

    smem_idx = lambda n: pl.BlockSpec((n,), lambda i: (i,), memory_space=pltpu.SMEM)
    hbm = pl.BlockSpec(memory_space=pl.ANY)
    xs = pl.pallas_call(
        _dispatch_kernel,
        name="moe_dispatch",
        grid=(t // t_disp,),
        in_specs=[smem_idx(t_disp), smem_idx(t_disp), pl.BlockSpec((t_disp, d), lambda i: (i, 0)), hbm],
        out_specs=hbm,
        out_shape=jax.ShapeDtypeStruct((n_rows, d), F32),
        scratch_shapes=[pltpu.SemaphoreType.DMA(())],
        input_output_aliases={3: 0},
        compiler_params=_cparams("arbitrary"),
    )(pos1, pos2, x, jnp.zeros((n_rows, d), F32))

    def w_in_map(i, f, te, nu):
        return li, te[i], 0, jnp.where(i < nu[0], f, nf - 1)

    def w_out_map(i, f, te, nu):
        return li, te[i], jnp.where(i < nu[0], f, nf - 1), 0

    y = pl.pallas_call(
        _expert_kernel,
        name="moe_experts",
        grid_spec=pltpu.PrefetchScalarGridSpec(
            num_scalar_prefetch=2,
            grid=(n_tiles, nf),
            in_specs=[pl.BlockSpec((rows, d), lambda i, f, te, nu: (i, 0)),
                      pl.BlockSpec((1, d), lambda i, f, te, nu: (0, 0)),
                      pl.BlockSpec((None, None, d, tf), w_in_map),
                      pl.BlockSpec((None, None, d, tf), w_in_map),
                      pl.BlockSpec((None, None, tf, d), w_out_map)],
            out_specs=pl.BlockSpec((rows, d), lambda i, f, te, nu: (i, 0)),
            scratch_shapes=[pltpu.VMEM((rows, d), BF16)]),
        out_shape=jax.ShapeDtypeStruct((n_rows, d), F32),
        compiler_params=_cparams("arbitrary", "arbitrary"),
    )(tile_expert, n_used, xs, g2, wg, wu, wd)

    gain = jnp.ones((1, d), F32) if final_g is None else final_g.reshape(1, d)
    return pl.pallas_call(
        functools.partial(_combine_kernel, final_g is not None),
        name="moe_combine",
        grid=(t // t_comb,),
        in_specs=[smem_idx(t_comb), smem_idx(t_comb),
                  pl.BlockSpec((t_comb, d), lambda i: (i, 0)),
                  pl.BlockSpec((t_comb, LANES), lambda i: (i, 0)),
                  pl.BlockSpec((1, d), lambda i: (0, 0)),
                  hbm],
        out_specs=pl.BlockSpec((t_comb, d), lambda i: (i, 0)),
        out_shape=jax.ShapeDtypeStruct((t, d), F32),
        scratch_shapes=[pltpu.VMEM((t_comb, d), F32), pltpu.VMEM((t_comb, d), F32),
                        pltpu.SemaphoreType.DMA(())],
        compiler_params=_cparams("arbitrary"),
    )(pos1, pos2, x, route, gain, y)


def _gelu_tanh(x):
    return 0.5 * x * (1.0 + jnp.tanh(math.sqrt(2.0 / math.pi) * (x + 0.044715 * (x * x * x))))


def _compress_kernel(kv_ref, pe_ref, w1_ref, w2a_ref, w2b_ref, ka_ref, kb_ref, xf_ref, sh_ref):
    seq = kv_ref.shape[0]
    n = seq // CMP_STRIDE
    hid2 = w1_ref.shape[2]
    xf_ref[...] = kv_ref[...].astype(F32)
    top = jnp.zeros((n, hid2), F32)
    bot = jnp.zeros((n, hid2), F32)
    for i in range(CMP_STRIDE):
        xi = xf_ref[pl.ds(i, n, stride=CMP_STRIDE), :]
        top = top + _dot((xi + pe_ref[i:i + 1, :]).astype(BF16), w1_ref[i])
        bot = bot + _dot((xi + pe_ref[CMP_STRIDE + i:CMP_STRIDE + i + 1, :]).astype(BF16), w1_ref[CMP_STRIDE + i])
    sh_ref[0:n, :] = bot
    sh_ref[n:n + 8, :] = jnp.zeros((8, hid2), F32)
    act = _gelu_tanh(top + sh_ref[1:n + 1, :]).astype(BF16)
    ka_ref[...] = _dot(act, w2a_ref[...]).astype(ka_ref.dtype)
    kb_ref[...] = _dot(act, w2b_ref[...]).astype(kb_ref.dtype)


def _compress(kv, pe_k, pe_v, k_w1, k_w2, v_w1, v_w2, batch, seq):
    assert CMP_LEN == 2 * CMP_STRIDE
    hid = k_w1.shape[1]
    n = seq // CMP_STRIDE
    w1 = jnp.zeros((CMP_LEN, LANES, 2 * hid), F32)
    w1 = w1.at[:, :HEAD_DIM, :hid].set(k_w1.reshape(CMP_LEN, HEAD_DIM, hid))
    w1 = w1.at[:, HEAD_DIM:, hid:].set(v_w1.reshape(CMP_LEN, HEAD_DIM, hid)).astype(BF16)
    zero = jnp.zeros((hid, HEAD_DIM), F32)
    w2a = jnp.block([[k_w2, zero], [zero, v_w2]]).astype(BF16)
    w2b = jnp.block([[zero, k_w2], [v_w2, zero]]).astype(BF16)
    pe = jnp.concatenate([pe_k, pe_v], axis=-1)
    g = NSA_GROUPS
    out = jax.ShapeDtypeStruct((batch * g * n, LANES), BF16)
    return pl.pallas_call(
        _compress_kernel,
        name="compress",
        grid=(batch, g),
        in_specs=[pl.BlockSpec((seq, LANES), lambda b, gi: (b, gi)),
                  pl.BlockSpec((CMP_LEN, LANES), lambda b, gi: (0, 0)),
                  pl.BlockSpec((CMP_LEN, LANES, 2 * hid), lambda b, gi: (0, 0, 0)),
                  pl.BlockSpec((2 * hid, LANES), lambda b, gi: (0, 0)),
                  pl.BlockSpec((2 * hid, LANES), lambda b, gi: (0, 0))],
        out_specs=[pl.BlockSpec((n, LANES), lambda b, gi: (b * g + gi, 0)),
                   pl.BlockSpec((n, LANES), lambda b, gi: (b * g + gi, 0))],
        out_shape=[out, out],
        scratch_shapes=[pltpu.VMEM((seq, LANES), F32), pltpu.VMEM((n + 8, 2 * hid), F32)],
        compiler_params=_cparams("parallel", "parallel"),
    )(kv, pe, w1, w2a, w2b)


def _nsa_cmp_kernel(n_slc, q_ref, ka_ref, kb_ref, ovt_ref, ocmp_ref, selm_ref):
    qi = pl.program_id(2)
    tq = q_ref.shape[0]
    n_pad = ka_ref.shape[0]
    t_q = qi * tq + lax.broadcasted_iota(jnp.int32, (tq, n_pad), 0)
    n_c = lax.broadcasted_iota(jnp.int32, (tq, n_pad), 1)
    mask_c = (n_c * CMP_STRIDE + (CMP_LEN - 1)) <= t_q
    halves = _head_halves(tq)
    ka = ka_ref[...]
    kb = kb_ref[...]
    psum = jnp.zeros((tq, n_pad), F32)
    for pr in range(NSA_HPG // 2):
        qf = q_ref[:, pr * LANES:(pr + 1) * LANES].astype(F32)
        outs = []
        for h in range(2):
            qz = jnp.where(halves[h], qf, 0.0).astype(BF16)
            kmat, vmat = (ka, kb) if h == 0 else (kb, ka)
            s = jnp.where(mask_c, _dot_nt(qz, kmat), NEG_INF)
            m = jnp.max(s, axis=1, keepdims=True)
            m = jnp.where(m == NEG_INF, 0.0, m)
            e = jnp.exp(s - m)
            p = e / jnp.maximum(jnp.sum(e, axis=1, keepdims=True), 1e-30)
            psum = psum + p
            outs.append(_dot(p.astype(BF16), vmat))
        ocmp_ref[:, pr * LANES:(pr + 1) * LANES] = jnp.where(halves[0], outs[0], outs[1]).astype(ocmp_ref.dtype)

    ps_hi, ps_lo = _split_bf16(psum)
    ovt = ovt_ref[...]
    p_slc = (_dot_nt(ovt, ps_hi) + _dot_nt(ovt, ps_lo))[0:n_slc]
    cand = lax.broadcasted_iota(jnp.int32, (n_slc, tq), 0)
    t_blk = jnp.right_shift(qi * tq + lax.broadcasted_iota(jnp.int32, (n_slc, tq), 1), int(math.log2(SLC_BLOCK)))
    valid = cand <= t_blk
    forced = (cand == 0) | (cand == t_blk) | (cand == t_blk - 1)
    score = jnp.where(valid, jnp.where(forced, FORCE_SCORE, p_slc), -1.0)
    keep = _rank_select(score, valid, n_slc, min(SLC_TOPN, n_slc))
    neg = jnp.where(keep, 0.0, MASK_NEG)
    zero = jnp.zeros((HEAD_DIM - n_slc, tq), F32)
    selm_ref[...] = jnp.concatenate([neg, zero, neg, zero], axis=0).T.astype(selm_ref.dtype)


def _nsa_sw_tiles(tiles, q_ref, selm_ref, gate_ref, ex_ref, ocmp_ref, o_ref, ksel_ref, vsel_ref, kwin_ref, vwin_ref):
    tq = ATT_TILE
    halves = _head_halves(tq)
    causal, beyond = _tile_masks(tq)
    ex = ex_ref[...]
    sel_s, win_s = {}, {}
    for c in tiles:
        rows = slice(c * tq, (c + 1) * tq)
        w = (c + 1) * tq
        lo = max(c - WINDOW // tq, 0) * tq
        qf = q_ref[rows, :].astype(F32)
        selm = selm_ref[rows, :].astype(F32)
        for h in range(2):
            q_aug = jnp.where(halves[h], qf, selm).astype(BF16)
            qz = jnp.where(halves[h], qf, 0.0).astype(BF16)
            sel_s[c, h] = _mask_tile(_dot_nt(q_aug, ksel_ref[h, 0:w, :]), c * tq, causal)
            s = _mask_tile(_dot_nt(qz, kwin_ref[h, lo:w, :]), c * tq - lo, causal)
            if c * tq - lo == WINDOW:
                s = _mask_tile(s, 0, beyond)
            win_s[c, h] = s
    for c in tiles:
        rows = slice(c * tq, (c + 1) * tq)
        w = (c + 1) * tq
        lo = max(c - WINDOW // tq, 0) * tq
        win_o = [_softmax_pv(win_s[c, h], vwin_ref[h, lo:w, :], h) for h in range(2)]
        sel_o = [_softmax_pv(sel_s[c, h], vsel_ref[h, 0:w, :], h) for h in range(2)]
        o_sel = jnp.where(halves[0], sel_o[0], sel_o[1])
        o_win = jnp.where(halves[0], win_o[0], win_o[1])
        g_hi, g_lo = _split_bf16(gate_ref[rows, :])
        gexp = _dot(g_hi, ex) + _dot(g_lo, ex)
        y = (gexp[:, 0:LANES] * ocmp_ref[rows, :].astype(F32) + gexp[:, LANES:2 * LANES] * o_sel
             + gexp[:, 2 * LANES:3 * LANES] * o_win)
        o_ref[rows, :] = y.astype(o_ref.dtype)


def _nsa_sw_kernel(q_ref, selm_ref, kvs_ref, kvw_ref, gate_ref, ex_ref, ocmp_ref, o_ref,
                   ksel_ref, vsel_ref, kwin_ref, vwin_ref):
    pr = pl.program_id(2)
    qi = pl.program_id(3)
    seq = kvs_ref.shape[0]

    @pl.when((pr == 0) & (qi == 0))
    def _():
        first, second = _head_halves(seq)
        onehot = _block_onehot(seq, int(math.log2(SLC_BLOCK)))
        kv = kvs_ref[...].astype(F32)
        swapped = pltpu.roll(kv, HEAD_DIM, axis=1)
        ksel_ref[0] = jnp.where(first, kv, onehot).astype(BF16)
        ksel_ref[1] = jnp.where(second, swapped, onehot).astype(BF16)
        vsel_ref[0] = _with_ones(swapped, first)
        vsel_ref[1] = _with_ones(kv, second)
        kv = kvw_ref[...].astype(F32)
        swapped = pltpu.roll(kv, HEAD_DIM, axis=1)
        kwin_ref[0] = kvw_ref[...]
        kwin_ref[1] = swapped.astype(BF16)
        vwin_ref[0] = _with_ones(swapped, first)
        vwin_ref[1] = _with_ones(kv, second)

    nq = seq // ATT_TILE
    for j in range(nq // TILES_PER_STEP):
        pl.when(qi == j)(functools.partial(_nsa_sw_tiles, _step_tiles(j, nq), q_ref, selm_ref, gate_ref, ex_ref,
                                           ocmp_ref, o_ref, ksel_ref, vsel_ref, kwin_ref, vwin_ref))


def _nsa_attention(q_raw, q_rot, kv, ka, kb, gates, batch, seq):
    nq = seq // ATT_TILE
    g = NSA_GROUPS
    gw = NSA_HPG * HEAD_DIM
    n_pairs = NSA_HEADS // 2
    n_cmp = (seq - CMP_LEN) // CMP_STRIDE + 1
    n_cmp_pad = seq // CMP_STRIDE
    n_slc = seq // SLC_BLOCK
    assert n_cmp_pad == LANES and n_slc <= HEAD_DIM // 2 and WINDOW % ATT_TILE == 0

    ex = np.zeros((n_pairs, LANES, 3 * LANES), np.float32)
    for hd in range(NSA_HEADS):
        for br in range(3):
            lane0 = br * LANES + (hd % 2) * HEAD_DIM
            ex[hd // 2, hd * 3 + br, lane0:lane0 + HEAD_DIM] = 1.0
    c_s = np.arange(n_cmp) * CMP_STRIDE
    s_s = np.arange(n_slc) * SLC_BLOCK
    ov = np.clip(np.minimum(c_s[:, None] + CMP_LEN, s_s[None, :] + SLC_BLOCK)
                 - np.maximum(c_s[:, None], s_s[None, :]), 0, None) / CMP_LEN
    ovt = np.zeros((LANES, n_cmp_pad), np.float32)
    ovt[:n_slc, :n_cmp] = ov.T

    n_ct = seq // CMP_TILE
    o_cmp, selm = pl.pallas_call(
        functools.partial(_nsa_cmp_kernel, n_slc),
        name="nsa_cmp",
        grid=(batch, g, n_ct),
        in_specs=[pl.BlockSpec((CMP_TILE, gw), lambda b, gi, i: (b * n_ct + i, gi)),
                  pl.BlockSpec((n_cmp_pad, LANES), lambda b, gi, i: (b * g + gi, 0)),
                  pl.BlockSpec((n_cmp_pad, LANES), lambda b, gi, i: (b * g + gi, 0)),
                  pl.BlockSpec((LANES, n_cmp_pad), lambda b, gi, i: (0, 0))],
        out_specs=[pl.BlockSpec((CMP_TILE, gw), lambda b, gi, i: (b * n_ct + i, gi)),
                   pl.BlockSpec((CMP_TILE, LANES), lambda b, gi, i: (b * n_ct + i, gi))],
        out_shape=[jax.ShapeDtypeStruct((batch * seq, NSA_HEADS * HEAD_DIM), BF16),
                   jax.ShapeDtypeStruct((batch * seq, g * LANES), BF16)],
        compiler_params=_cparams("parallel", "parallel", "parallel"),
    )(q_raw, ka, kb, jnp.asarray(ovt, BF16))

    ppg = NSA_HPG // 2
    return pl.pallas_call(
        _nsa_sw_kernel,
        name="nsa_sw",
        grid=(batch, g, ppg, nq // TILES_PER_STEP),
        in_specs=[pl.BlockSpec((seq, LANES), lambda b, gi, p, i: (b, gi * ppg + p)),
                  pl.BlockSpec((seq, LANES), lambda b, gi, p, i: (b, gi)),
                  pl.BlockSpec((seq, LANES), lambda b, gi, p, i: (b, g + gi)),
                  pl.BlockSpec((seq, LANES), lambda b, gi, p, i: (b, 2 * g + gi)),
                  pl.BlockSpec((seq, LANES), lambda b, gi, p, i: (b, 0)),
                  pl.BlockSpec((None, LANES, 3 * LANES), lambda b, gi, p, i: (gi * ppg + p, 0, 0)),
                  pl.BlockSpec((seq, LANES), lambda b, gi, p, i: (b, gi * ppg + p))],
        out_specs=pl.BlockSpec((seq, LANES), lambda b, gi, p, i: (b, gi * ppg + p)),
        out_shape=jax.ShapeDtypeStruct((batch * seq, NSA_HEADS * HEAD_DIM), BF16),
        scratch_shapes=[pltpu.VMEM((2, seq, LANES), BF16) for _ in range(4)],
        compiler_params=_cparams("parallel", "parallel", "arbitrary", "arbitrary"),
    )(q_rot, selm, kv, kv, gates, jnp.asarray(ex, BF16), o_cmp)


def _even_mixer(x, g, w_in, pool_w, pool_scale, w_out, rope_pair, batch, seq):
    chunk = 2 * LANES
    plan = [(0, POOL_DIM, [(0, 0, "none", 0)])]
    for c in range(3 * MOBA_DIM // chunk):
        op = "rope_scale" if c < MOBA_DIM // chunk else ("rope" if c < 2 * MOBA_DIM // chunk else "none")
        plan.append((POOL_DIM + c * chunk, chunk, [(1, c * chunk, op, 0)]))
    u_pool, qkv = _norm_project(x, g, w_in.astype(BF16), plan, [rope_pair],
                                [(POOL_DIM, F32), (3 * MOBA_DIM, BF16)], seq)
    w_bd = jax.scipy.linalg.block_diag(*[pool_w[i] for i in range(pool_w.shape[0])]).astype(BF16)
    y_a = _pool_mixer(u_pool, w_bd, pool_scale, batch, seq)
    y_b = _moba_attention(qkv, batch, seq)
    w_out = w_out.astype(BF16)
    return [y_a, y_b], [w_out[:POOL_DIM], w_out[POOL_DIM:]]


def _odd_mixer(x, g, w_in, pe_k, pe_v, k_w1, k_w2, v_w1, v_w2, w_out, rope_pair, rope_single, batch, seq):
    d = x.shape[1]
    qd = NSA_HEADS * HEAD_DIM
    kvd = NSA_KV_DIM
    wq = w_in[:, :qd]
    parts = [w_in[:, qd + i * kvd:qd + (i + 1) * kvd].reshape(d, NSA_GROUPS, HEAD_DIM) for i in range(6)]
    pairs = [jnp.concatenate([parts[2 * i], parts[2 * i + 1]], axis=-1).reshape(d, 2 * kvd) for i in range(3)]
    n_gate = 3 * NSA_HEADS
    w_gate = jnp.zeros((d, LANES), F32).at[:, :n_gate].set(w_in[:, qd + 6 * kvd:])
    w_all = jnp.concatenate([wq] + pairs + [w_gate], axis=1).astype(BF16)

    chunk = 2 * LANES
    plan = []
    for c in range(qd // chunk):
        plan.append((c * chunk, chunk, [(0, c * chunk, "scale", 0), (1, c * chunk, "rope_scale", 0)]))
    for c in range(6 * kvd // chunk):
        branch = c // (2 * kvd // chunk)
        op = "none" if branch == 0 else "rope"
        plan.append((qd + c * chunk, chunk, [(2, c * chunk, op, 1)]))
    plan.append((qd + 6 * kvd, LANES, [(3, 0, "sigmoid", 0)]))
    q_raw, q_rot, kv, gates = _norm_project(
        x, g, w_all, plan, [rope_pair, rope_single],
        [(qd, BF16), (qd, BF16), (6 * kvd, BF16), (LANES, F32)], seq)

    ka, kb = _compress(kv, pe_k, pe_v, k_w1, k_w2, v_w1, v_w2, batch, seq)
    y = _nsa_attention(q_raw, q_rot, kv, ka, kb, gates, batch, seq)
    return [y], [w_out.astype(BF16)]


def kernel(x, mem, positions, norm_g, mem_g, final_g, w_in_ab, pool_w, pool_scale, w_out_ab, ffn_w_gate, ffn_w_up, ffn_w_down, w_in_c, cmp_pe_k, cmp_pe_v, cmp_k_w1, cmp_k_w2, cmp_v_w1, cmp_v_w2, w_out_c, router_w, moe_w_gate, moe_w_up, moe_w_down, xa_wq, xa_wkv, xa_wo):
    batch, seq, d = x.shape
    mem_len = mem.shape[1]
    depth = norm_g.shape[0]
    rope_pair, rope_single = _rope_tables(positions)

    wkv_all = jnp.concatenate([xa_wkv[l] for l in range(depth)], axis=1).astype(BF16)
    kv_all = _norm_matmul(mem.reshape(batch * mem_len, d), mem_g, wkv_all)

    xf = x.reshape(batch * seq, d)
    for layer in range(depth):
        i = layer // 2
        last = layer == depth - 1
        if layer % 2 == 0:
            ys, ws = _even_mixer(xf, norm_g[layer, 0], w_in_ab[i], pool_w[i], pool_scale[i], w_out_ab[i],
                                 rope_pair, batch, seq)
        else:
            ys, ws = _odd_mixer(xf, norm_g[layer, 0], w_in_c[i], cmp_pe_k[i], cmp_pe_v[i], cmp_k_w1[i],
                                cmp_k_w2[i], cmp_v_w1[i], cmp_v_w2[i], w_out_c[i], rope_pair, rope_single,
                                batch, seq)
        xf = _out_proj(xf, ys, ws)
        mid_args = (xf, norm_g[layer, 1], xa_wq[layer].astype(BF16), kv_all, layer,
                    xa_wo[layer].astype(BF16), seq, mem_len)
        if layer % 2 == 0:
            xf = _mid_layer(*mid_args)
            xf = _ffn(xf, norm_g[layer, 2], ffn_w_gate[i].astype(BF16), ffn_w_up[i].astype(BF16),
                      ffn_w_down[i].astype(BF16))
            if last:
                xf = _final_norm(xf, final_g)
        else:
            xf, route, cnt = _mid_layer(*mid_args, router=(norm_g[layer, 2], router_w[i]))
            xf = _moe(xf, norm_g[layer, 2], route, cnt, moe_w_gate, moe_w_up, moe_w_down, i,
                      final_g=final_g if last else None)
    return xf.reshape(batch, seq, d)
```

```python
import functools
import math

import numpy as np
import jax
import jax.numpy as jnp
from jax import lax
from jax.experimental import pallas as pl
from jax.experimental.pallas import tpu as pltpu

F32 = jnp.float32
BF16 = jnp.bfloat16

HEAD_DIM = 64
ROPE_DIM = 16
ROPE_THETA = 500000.0
NORM_EPS = 1e-5
QK_SCALE = HEAD_DIM ** -0.5
QK_SCALE_LOG2E = QK_SCALE * math.log2(math.e)

POOL_WINDOWS = (2, 4, 8, 16)
POOL_DIM = 256
MOBA_HEADS = 12
MOBA_DIM = MOBA_HEADS * HEAD_DIM
MOBA_BLOCK = 256
MOBA_TOPK = 3
NSA_HEADS = 16
NSA_GROUPS = 4
NSA_HPG = 4
NSA_KV_DIM = NSA_GROUPS * HEAD_DIM
CMP_LEN = 32
CMP_STRIDE = 16
SLC_BLOCK = 64
SLC_TOPN = 16
WINDOW = 512
FORCE_SCORE = 1e4
XA_HEADS = 4
XA_DIM = XA_HEADS * HEAD_DIM
N_EXPERTS = 8

LANES = 128
ATT_TILE = 256
CMP_TILE = 2048
TILES_PER_STEP = 4
VMEM_LIMIT = 56 * 1024 * 1024
NEG_INF = float("-inf")
MASK_NEG = -1e30


def _cparams(*sem):
    return pltpu.CompilerParams(dimension_semantics=sem, vmem_limit_bytes=VMEM_LIMIT)


def _dot(a, b):
    return jnp.dot(a, b, preferred_element_type=F32)


def _dot_nt(a, b):
    return lax.dot_general(a, b, (((1,), (1,)), ((), ())), preferred_element_type=F32)


def _split_bf16(a):
    hi = a.astype(BF16)
    lo = (a - hi.astype(F32)).astype(BF16)
    return hi, lo


def _rms_bf16(x, g):
    ms = jnp.mean(x * x, axis=-1, keepdims=True)
    return (x * lax.rsqrt(ms + NORM_EPS) * g).astype(BF16)


def _apply_rope(acc, c, a, b):
    half = ROPE_DIM // 2
    return acc * c + pltpu.roll(acc, LANES - half, axis=1) * a + pltpu.roll(acc, half, axis=1) * b


def _proj_kernel(plan, n_tab, x_ref, g_ref, w_ref, *rest):
    tabs = rest[:3 * n_tab]
    outs = rest[3 * n_tab:]
    h = _rms_bf16(x_ref[...], g_ref[...])
    for wc0, width, sinks in plan:
        acc = _dot(h, w_ref[:, wc0:wc0 + width])
        for oi, oc0, op, tab in sinks:
            for s in range(width // LANES):
                val = acc[:, s * LANES:(s + 1) * LANES]
                if op in ("rope", "rope_scale"):
                    c, a, b = (tabs[3 * tab + i][...] for i in range(3))
                    val = _apply_rope(val, c, a, b)
                if op == "scale":
                    val = val * QK_SCALE
                if op == "rope_scale":
                    val = val * QK_SCALE_LOG2E
                if op == "sigmoid":
                    val = jax.nn.sigmoid(val)
                o = outs[oi]
                o[:, oc0 + s * LANES:oc0 + (s + 1) * LANES] = val.astype(o.dtype)


def _norm_project(x, g, w, plan, tables, out_defs, seq, tm=512):
    t, d = x.shape
    n = w.shape[1]
    n_seq_tiles = seq // tm
    in_specs = [
        pl.BlockSpec((tm, d), lambda i: (i, 0)),
        pl.BlockSpec((1, d), lambda i: (0, 0)),
        pl.BlockSpec((d, n), lambda i: (0, 0)),
    ]
    flat_tabs = []
    for tset in tables:
        for tb in tset:
            flat_tabs.append(tb)
            in_specs.append(pl.BlockSpec((tm, LANES), lambda i: (i % n_seq_tiles, 0)))
    out_shape = [jax.ShapeDtypeStruct((t, wd), dt) for wd, dt in out_defs]
    out_specs = [pl.BlockSpec((tm, wd), lambda i: (i, 0)) for wd, _ in out_defs]
    return pl.pallas_call(
        functools.partial(_proj_kernel, plan, len(tables)),
        name="proj",
        grid=(t // tm,),
        in_specs=in_specs,
        out_specs=out_specs,
        out_shape=out_shape,
        compiler_params=_cparams("parallel"),
    )(x, g.reshape(1, d), w, *flat_tabs)


def _rope_tables(positions):
    half = ROPE_DIM // 2
    inv = ROPE_THETA ** (-jnp.arange(0, ROPE_DIM, 2, dtype=F32) / ROPE_DIM)
    ang = positions.astype(F32)[:, None] * inv[None, :]
    cos, sin = jnp.cos(ang), jnp.sin(ang)
    s = positions.shape[0]
    z8 = jnp.zeros((s, half), F32)
    rest0 = jnp.zeros((s, HEAD_DIM - ROPE_DIM), F32)
    rest1 = jnp.ones((s, HEAD_DIM - ROPE_DIM), F32)
    c64 = jnp.concatenate([cos, cos, rest1], -1)
    a64 = jnp.concatenate([-sin, z8, rest0], -1)
    b64 = jnp.concatenate([z8, sin, rest0], -1)
    one64 = jnp.ones((s, HEAD_DIM), F32)
    zero64 = jnp.zeros((s, HEAD_DIM), F32)
    pair = tuple(jnp.concatenate([m, m], -1) for m in (c64, a64, b64))
    single = (jnp.concatenate([c64, one64], -1), jnp.concatenate([a64, zero64], -1),
              jnp.concatenate([b64, zero64], -1))
    return pair, single


def _norm_matmul_kernel(x_ref, g_ref, w_ref, o_ref):
    h = _rms_bf16(x_ref[...], g_ref[...])
    o_ref[...] = _dot(h, w_ref[...]).astype(o_ref.dtype)


def _norm_matmul(x, g, w, tm=512):
    t, d = x.shape
    n = w.shape[1]
    return pl.pallas_call(
        _norm_matmul_kernel,
        name="mem_kv",
        grid=(t // tm,),
        in_specs=[pl.BlockSpec((tm, d), lambda i: (i, 0)),
                  pl.BlockSpec((1, d), lambda i: (0, 0)),
                  pl.BlockSpec((d, n), lambda i: (0, 0))],
        out_specs=pl.BlockSpec((tm, n), lambda i: (i, 0)),
        out_shape=jax.ShapeDtypeStruct((t, n), BF16),
        compiler_params=_cparams("parallel"),
    )(x, g.reshape(1, d), w)


def _final_norm_kernel(x_ref, g_ref, o_ref):
    x = x_ref[...]
    ms = jnp.mean(x * x, axis=-1, keepdims=True)
    o_ref[...] = x * lax.rsqrt(ms + NORM_EPS) * g_ref[...]


def _final_norm(x, g, tm=1024):
    t, d = x.shape
    return pl.pallas_call(
        _final_norm_kernel,
        name="final_norm",
        grid=(t // tm,),
        in_specs=[pl.BlockSpec((tm, d), lambda i: (i, 0)), pl.BlockSpec((1, d), lambda i: (0, 0))],
        out_specs=pl.BlockSpec((tm, d), lambda i: (i, 0)),
        out_shape=jax.ShapeDtypeStruct((t, d), F32),
        compiler_params=_cparams("parallel"),
    )(x, g.reshape(1, d))


def _pool_kernel(u_ref, w_ref, sc_ref, o_ref, pad_ref):
    s = u_ref.shape[0]
    maxw = POOL_WINDOWS[-1]
    u = u_ref[...]
    pad_ref[0:maxw, :] = jnp.zeros((maxw, POOL_DIM), F32)
    t1 = (lax.broadcasted_iota(jnp.int32, (s, POOL_DIM), 0) + 1).astype(F32)
    lane = lax.broadcasted_iota(jnp.int32, (s, POOL_DIM), 1)
    gdim = POOL_DIM // len(POOL_WINDOWS)
    cur = u
    pooled = jnp.zeros_like(u)
    shift = 1
    for gi, w in enumerate(POOL_WINDOWS):
        pad_ref[maxw:maxw + s, :] = cur
        cur = cur + pad_ref[maxw - shift:maxw - shift + s, :]
        shift *= 2
        assert shift == w
        mean = cur / jnp.minimum(t1, float(w))
        pooled = jnp.where((lane >= gi * gdim) & (lane < (gi + 1) * gdim), mean, pooled)
    pooled = (pooled - u).astype(BF16)
    o_ref[...] = (_dot(pooled, w_ref[...]) * sc_ref[...]).astype(o_ref.dtype)


def _pool_mixer(u_pool, w_bd, scale, batch, seq):
    return pl.pallas_call(
        _pool_kernel,
        name="pool",
        grid=(batch,),
        in_specs=[pl.BlockSpec((seq, POOL_DIM), lambda b: (b, 0)),
                  pl.BlockSpec((POOL_DIM, POOL_DIM), lambda b: (0, 0)),
                  pl.BlockSpec((1, POOL_DIM), lambda b: (0, 0))],
        out_specs=pl.BlockSpec((seq, POOL_DIM), lambda b: (b, 0)),
        out_shape=jax.ShapeDtypeStruct((batch * seq, POOL_DIM), BF16),
        scratch_shapes=[pltpu.VMEM((seq + POOL_WINDOWS[-1], POOL_DIM), F32)],
        compiler_params=_cparams("parallel"),
    )(u_pool, w_bd, scale.reshape(1, POOL_DIM))


def _rank_select(scoreT, valid, n_rows, topn):
    row = lax.broadcasted_iota(jnp.int32, scoreT.shape, 0)
    rank = jnp.zeros(scoreT.shape, F32)
    for j in range(n_rows):
        sj = scoreT[j:j + 1, :]
        beats = (scoreT > sj) | ((scoreT == sj) & (row < j))
        rj = jnp.sum(beats.astype(F32), axis=0, keepdims=True)
        rank = jnp.where(row == j, rj, rank)
    return (rank < float(topn)) & valid


def _head_halves(rows):
    lane = lax.broadcasted_iota(jnp.int32, (rows, LANES), 1)
    return lane < HEAD_DIM, lane >= HEAD_DIM


def _tile_masks(tq):
    qry = lax.broadcasted_iota(jnp.int32, (tq, tq), 0)
    key = lax.broadcasted_iota(jnp.int32, (tq, tq), 1)
    return key <= qry, key > qry


def _mask_tile(s, t0, mask):
    tq = mask.shape[0]
    parts = []
    if t0 > 0:
        parts.append(s[:, :t0])
    parts.append(jnp.where(mask, s[:, t0:t0 + tq], NEG_INF))
    if t0 + tq < s.shape[1]:
        parts.append(s[:, t0 + tq:])
    return parts[0] if len(parts) == 1 else jnp.concatenate(parts, axis=1)


def _softmax_pv(s, v, h):
    m = jnp.max(s, axis=1, keepdims=True)
    p = jnp.exp2((s - m).astype(BF16))
    o = _dot(p, v)
    denom_lane = (1 - h) * HEAD_DIM
    return o / o[:, denom_lane:denom_lane + 1]


def _with_ones(v, own):
    return jnp.where(own, v, 1.0).astype(BF16)


def _step_tiles(j, nq):
    half = TILES_PER_STEP // 2
    return tuple(t for k in range(half) for t in (nq - 1 - (j * half + k), j * half + k))


def _block_onehot(seq, shift):
    lane = lax.broadcasted_iota(jnp.int32, (seq, LANES), 1)
    blk = jnp.right_shift(lax.broadcasted_iota(jnp.int32, (seq, LANES), 0), shift)
    return jnp.where((lane & (HEAD_DIM - 1)) == blk, 1.0, 0.0)


def _moba_tiles(tiles, o_ref, qaug_ref, kaug_ref, vaug_ref):
    tq = ATT_TILE
    halves = _head_halves(tq)
    causal, _ = _tile_masks(tq)
    scores = {}
    for c in tiles:
        for h in range(2):
            s = _dot_nt(qaug_ref[h, c * tq:(c + 1) * tq, :], kaug_ref[h, 0:(c + 1) * tq, :])
            scores[c, h] = _mask_tile(s, c * tq, causal)
    for c in tiles:
        outs = [_softmax_pv(scores[c, h], vaug_ref[h, 0:(c + 1) * tq, :], h) for h in range(2)]
        o_ref[c * tq:(c + 1) * tq, :] = jnp.where(halves[0], outs[0], outs[1]).astype(o_ref.dtype)


def _moba_kernel(q_ref, k_ref, v_ref, o_ref, qaug_ref, kaug_ref, vaug_ref, kmean_ref):
    qi = pl.program_id(2)
    seq = k_ref.shape[0]
    nb = seq // MOBA_BLOCK
    blk_shift = int(math.log2(MOBA_BLOCK))

    @pl.when(qi == 0)
    def _():
        halves = _head_halves(seq)
        kf = k_ref[...].astype(F32)
        vf = v_ref[...].astype(F32)
        qp = q_ref[...]
        qf = qp.astype(F32)
        onehot = _block_onehot(seq, blk_shift)
        kmean_ref[...] = jnp.zeros(kmean_ref.shape, F32)
        for j in range(nb):
            kmean_ref[j:j + 1, :] = jnp.mean(kf[j * MOBA_BLOCK:(j + 1) * MOBA_BLOCK], axis=0, keepdims=True)
        cand = lax.broadcasted_iota(jnp.int32, (nb, seq), 0)
        own = jnp.right_shift(lax.broadcasted_iota(jnp.int32, (nb, seq), 1), blk_shift)
        valid = cand < own
        for h in range(2):
            kaug_ref[h] = jnp.where(halves[h], kf, onehot).astype(BF16)
            vaug_ref[h] = _with_ones(vf, halves[h])
            mine_k = _head_halves(kmean_ref.shape[0])[h]
            km_hi, km_lo = _split_bf16(jnp.where(mine_k, kmean_ref[...], 0.0))
            gate = (_dot_nt(km_hi, qp) + _dot_nt(km_lo, qp))[0:nb]
            gate = jnp.where(valid, gate, NEG_INF)
            keep = _rank_select(gate, valid, nb, MOBA_TOPK) | (cand == own)
            neg = jnp.where(keep, 0.0, MASK_NEG)
            lead = HEAD_DIM if h == 0 else 0
            pieces = [neg, jnp.zeros((LANES - lead - nb, seq), F32)]
            if lead:
                pieces.insert(0, jnp.zeros((lead, seq), F32))
            maskcols = jnp.concatenate(pieces, axis=0).T
            qaug_ref[h] = jnp.where(halves[h], qf, maskcols).astype(BF16)

    for j in range(nb // TILES_PER_STEP):
        pl.when(qi == j)(functools.partial(_moba_tiles, _step_tiles(j, nb), o_ref, qaug_ref, kaug_ref, vaug_ref))


def _moba_attention(qkv, batch, seq):
    nq = seq // ATT_TILE
    assert MOBA_BLOCK == ATT_TILE and nq <= 16 and nq % TILES_PER_STEP == 0
    hp = MOBA_DIM // LANES
    return pl.pallas_call(
        _moba_kernel,
        name="moba",
        grid=(batch, hp, nq // TILES_PER_STEP),
        in_specs=[pl.BlockSpec((seq, LANES), lambda b, p, i: (b, p)),
                  pl.BlockSpec((seq, LANES), lambda b, p, i: (b, hp + p)),
                  pl.BlockSpec((seq, LANES), lambda b, p, i: (b, 2 * hp + p))],
        out_specs=pl.BlockSpec((seq, LANES), lambda b, p, i: (b, p)),
        out_shape=jax.ShapeDtypeStruct((batch * seq, MOBA_DIM), BF16),
        scratch_shapes=[pltpu.VMEM((2, seq, LANES), BF16), pltpu.VMEM((2, seq, LANES), BF16),
                        pltpu.VMEM((2, seq, LANES), BF16), pltpu.VMEM((16, LANES), F32)],
        compiler_params=_cparams("parallel", "parallel", "arbitrary"),
    )(qkv, qkv, qkv)


def _out_proj_kernel(n_in, x_ref, *rest):
    ys = rest[:n_in]
    ws = rest[n_in:2 * n_in]
    o_ref = rest[2 * n_in]
    acc = x_ref[...]
    for y, w in zip(ys, ws):
        acc = acc + _dot(y[...], w[...])
    o_ref[...] = acc


def _out_proj(x, ys, ws, tm=512):
    t, d = x.shape
    in_specs = [pl.BlockSpec((tm, d), lambda i: (i, 0))]
    in_specs += [pl.BlockSpec((tm, y.shape[1]), lambda i: (i, 0)) for y in ys]
    in_specs += [pl.BlockSpec(w.shape, lambda i: (0, 0)) for w in ws]
    return pl.pallas_call(
        functools.partial(_out_proj_kernel, len(ys)),
        name="out_proj",
        grid=(t // tm,),
        in_specs=in_specs,
        out_specs=pl.BlockSpec((tm, d), lambda i: (i, 0)),
        out_shape=jax.ShapeDtypeStruct((t, d), F32),
        compiler_params=_cparams("parallel"),
    )(x, *ys, *ws)


def _mid_kernel(with_route, x_ref, g_ref, wq_ref, kv_ref, wo_ref, *rest):
    o_ref = rest[3] if with_route else rest[0]
    x = x_ref[...]
    h = _rms_bf16(x, g_ref[...])
    q = _dot(h, wq_ref[...]) * QK_SCALE_LOG2E
    halves = _head_halves(x.shape[0])
    mem_halves = _head_halves(kv_ref.shape[0])
    out = x
    for pr in range(XA_HEADS // 2):
        lanes = slice(pr * LANES, (pr + 1) * LANES)
        qf = q[:, lanes]
        kp = kv_ref[:, lanes]
        vf = kv_ref[:, XA_DIM + pr * LANES:XA_DIM + (pr + 1) * LANES].astype(F32)
        scores = [_dot_nt(jnp.where(halves[hh], qf, 0.0).astype(BF16), kp) for hh in range(2)]
        outs = [_softmax_pv(scores[hh], _with_ones(vf, mem_halves[hh]), hh) for hh in range(2)]
        o_pair = jnp.where(halves[0], outs[0], outs[1]).astype(BF16)
        out = out + _dot(o_pair, wo_ref[lanes, :])
    o_ref[...] = out

    if with_route:
        g3_ref, rhi_ref, rlo_ref, _, route_ref, cnt_ref = rest

        @pl.when(pl.program_id(0) == 0)
        def _():
            cnt_ref[...] = jnp.zeros(cnt_ref.shape, F32)

        route_ref[...] = _route_rows(_rms_bf16(out, g3_ref[...]), rhi_ref[...], rlo_ref[...], cnt_ref)


def _mid_layer(x, g, wq, kv_all, layer, wo, seq, mem_len, router=None, tm=512):
    t, d = x.shape
    tiles_per_seq = seq // tm
    const = lambda i: (0, 0)
    rows = lambda i: (i, 0)
    in_specs = [pl.BlockSpec((tm, d), rows),
                pl.BlockSpec((1, d), const),
                pl.BlockSpec((d, XA_DIM), const),
                pl.BlockSpec((mem_len, 2 * XA_DIM), lambda i: (i // tiles_per_seq, layer)),
                pl.BlockSpec((XA_DIM, d), const)]
    args = [x, g.reshape(1, d), wq, kv_all, wo]
    out_specs = [pl.BlockSpec((tm, d), rows)]
    out_shape = [jax.ShapeDtypeStruct((t, d), F32)]
    if router is not None:
        g3, router_w = router
        rw = jnp.zeros((d, LANES), F32).at[:, :router_w.shape[1]].set(router_w)
        rhi = rw.astype(BF16)
        rlo = (rw - rhi.astype(F32)).astype(BF16)
        in_specs += [pl.BlockSpec((1, d), const), pl.BlockSpec((d, LANES), const), pl.BlockSpec((d, LANES), const)]
        args += [g3.reshape(1, d), rhi, rlo]
        out_specs += [pl.BlockSpec((tm, LANES), rows), pl.BlockSpec((8, LANES), const)]
        out_shape += [jax.ShapeDtypeStruct((t, LANES), F32), jax.ShapeDtypeStruct((8, LANES), F32)]
    res = pl.pallas_call(
        functools.partial(_mid_kernel, router is not None),
        name="mid",
        grid=(t // tm,),
        in_specs=in_specs,
        out_specs=out_specs,
        out_shape=out_shape,
        compiler_params=_cparams("arbitrary"),
    )(*args)
    return res if router is not None else res[0]


def _ffn_kernel(x_ref, g_ref, wg_ref, wu_ref, wd_ref, o_ref, h_ref):
    f = pl.program_id(1)

    @pl.when(f == 0)
    def _():
        x = x_ref[...]
        h_ref[...] = _rms_bf16(x, g_ref[...])
        o_ref[...] = x

    h = h_ref[...]
    gate = _dot(h, wg_ref[...])
    a = (gate * jax.nn.sigmoid(gate)) * _dot(h, wu_ref[...])
    o_ref[...] += _dot(a.astype(BF16), wd_ref[...])


def _ffn(x, g, wg, wu, wd, tm=1024, tf=256):
    t, d = x.shape
    ff = wg.shape[1]
    return pl.pallas_call(
        _ffn_kernel,
        name="ffn",
        grid=(t // tm, ff // tf),
        in_specs=[pl.BlockSpec((tm, d), lambda i, f: (i, 0)),
                  pl.BlockSpec((1, d), lambda i, f: (0, 0)),
                  pl.BlockSpec((d, tf), lambda i, f: (0, f)),
                  pl.BlockSpec((d, tf), lambda i, f: (0, f)),
                  pl.BlockSpec((tf, d), lambda i, f: (f, 0))],
        out_specs=pl.BlockSpec((tm, d), lambda i, f: (i, 0)),
        out_shape=jax.ShapeDtypeStruct((t, d), F32),
        scratch_shapes=[pltpu.VMEM((tm, d), BF16)],
        compiler_params=_cparams("parallel", "arbitrary"),
    )(x, g.reshape(1, d), wg, wu, wd)


ROUTE_I1, ROUTE_I2, ROUTE_W1, ROUTE_W2, ROUTE_R1, ROUTE_R2 = range(6)
MOE_TILE = 1024
MOE_SUB = 256


def _lane_pick(arr, lane, k):
    return jnp.sum(jnp.where(lane == k, arr, 0.0), axis=1, keepdims=True)


def _route_rows(h, rhi, rlo, cnt_ref):
    tm = h.shape[0]
    lane = lax.broadcasted_iota(jnp.int32, (tm, LANES), 1)
    lane_f = lane.astype(F32)
    logits = _dot(h, rhi) + _dot(h, rlo)
    logits = jnp.where(lane < N_EXPERTS, logits, NEG_INF)
    m1 = jnp.max(logits, axis=1, keepdims=True)
    i1 = jnp.min(jnp.where(logits == m1, lane_f, float(LANES)), axis=1, keepdims=True)
    rest = jnp.where(lane_f == i1, NEG_INF, logits)
    m2 = jnp.max(rest, axis=1, keepdims=True)
    i2 = jnp.min(jnp.where(rest == m2, lane_f, float(LANES)), axis=1, keepdims=True)
    e2 = jnp.exp(m2 - m1)
    w1 = 1.0 / (1.0 + e2)
    w2 = e2 / (1.0 + e2)

    chosen = jnp.where((lane_f == i1) | (lane_f == i2), 1.0, 0.0)
    r = lax.broadcasted_iota(jnp.int32, (tm, tm), 0)
    c = lax.broadcasted_iota(jnp.int32, (tm, tm), 1)
    earlier = jnp.where(c < r, 1.0, 0.0).astype(BF16)
    prefix = _dot(earlier, chosen.astype(BF16)) + cnt_ref[0:1, :]
    rank1 = jnp.sum(jnp.where(lane_f == i1, prefix, 0.0), axis=1, keepdims=True)
    rank2 = jnp.sum(jnp.where(lane_f == i2, prefix, 0.0), axis=1, keepdims=True)
    cnt_ref[0:1, :] = cnt_ref[0:1, :] + jnp.sum(chosen, axis=0, keepdims=True)

    cols = {ROUTE_I1: i1, ROUTE_I2: i2, ROUTE_W1: w1, ROUTE_W2: w2, ROUTE_R1: rank1, ROUTE_R2: rank2}
    route = jnp.zeros((tm, LANES), F32)
    for k, val in cols.items():
        route = jnp.where(lane == k, val, route)
    return route


def _dispatch_kernel(pos1_ref, pos2_ref, x_ref, xs_in, xs_hbm, sem):
    del xs_in
    n = pos1_ref.shape[0]

    def row_copy(r, p):
        return pltpu.make_async_copy(x_ref.at[pl.ds(r, 1)], xs_hbm.at[pl.ds(p, 1)], sem)

    def issue(r, carry):
        row_copy(r, pos1_ref[r]).start()
        row_copy(r, pos2_ref[r]).start()
        return carry

    lax.fori_loop(0, n, issue, 0, unroll=8)
    for _ in range(2):
        pltpu.make_async_copy(x_ref, xs_hbm.at[pl.ds(0, n)], sem).wait()


def _expert_kernel(te_ref, nu_ref, xs_ref, g_ref, wg_ref, wu_ref, wd_ref, y_ref, h_ref):
    del te_ref
    i = pl.program_id(0)
    f = pl.program_id(1)
    rows = xs_ref.shape[0]
    valid = nu_ref[1 + i]

    @pl.when(f == 0)
    def _():
        h_ref[...] = _rms_bf16(xs_ref[...], g_ref[...])
        y_ref[...] = jnp.zeros(y_ref.shape, y_ref.dtype)

    def swiglu(r0, nr):
        h = h_ref[r0:r0 + nr, :]
        gate = _dot(h, wg_ref[...].astype(BF16))
        a = (gate * jax.nn.sigmoid(gate)) * _dot(h, wu_ref[...].astype(BF16))
        y_ref[r0:r0 + nr, :] += _dot(a.astype(BF16), wd_ref[...].astype(BF16))

    pl.when(valid == rows)(functools.partial(swiglu, 0, rows))

    @pl.when((valid > 0) & (valid < rows))
    def _():
        for r0 in range(0, rows, MOE_SUB):
            pl.when(r0 < valid)(functools.partial(swiglu, r0, MOE_SUB))


def _combine_kernel(with_norm, pos1_ref, pos2_ref, x_ref, route_ref, g_ref, y_hbm, o_ref, buf1, buf2, sem):
    n = pos1_ref.shape[0]

    def row_copy(p, buf, r):
        return pltpu.make_async_copy(y_hbm.at[pl.ds(p, 1)], buf.at[pl.ds(r, 1)], sem)

    def issue(r, carry):
        row_copy(pos1_ref[r], buf1, r).start()
        row_copy(pos2_ref[r], buf2, r).start()
        return carry

    lax.fori_loop(0, n, issue, 0, unroll=8)
    for buf in (buf1, buf2):
        pltpu.make_async_copy(y_hbm.at[pl.ds(0, n)], buf, sem).wait()
    lane = lax.broadcasted_iota(jnp.int32, route_ref.shape, 1)
    route = route_ref[...]
    w1 = _lane_pick(route, lane, ROUTE_W1)
    w2 = _lane_pick(route, lane, ROUTE_W2)
    out = x_ref[...] + w1 * buf1[...] + w2 * buf2[...]
    if with_norm:
        ms = jnp.mean(out * out, axis=-1, keepdims=True)
        out = out * lax.rsqrt(ms + NORM_EPS) * g_ref[...]
    o_ref[...] = out


def _moe(x, g, route, cnt, wg, wu, wd, li, final_g=None, tf=512, t_disp=2048, t_comb=1024):
    t, d = x.shape
    _, ne, _, ff = wg.shape
    rows = MOE_TILE
    n_rows = 2 * t + ne * rows
    n_tiles = n_rows // rows
    nf = ff // tf
    g2 = g.reshape(1, d)

    counts = cnt[0, :ne].astype(jnp.int32)
    gsz = (counts + rows - 1) // rows * rows
    ends = jnp.cumsum(gsz)
    offs = ends - gsz
    i1 = route[:, ROUTE_I1].astype(jnp.int32)
    i2 = route[:, ROUTE_I2].astype(jnp.int32)
    pos1 = offs[i1] + route[:, ROUTE_R1].astype(jnp.int32)
    pos2 = offs[i2] + route[:, ROUTE_R2].astype(jnp.int32)
    tile_start = jnp.arange(n_tiles, dtype=jnp.int32) * rows
    tile_expert = jnp.minimum(jnp.sum(tile_start[:, None] >= ends[None, :], axis=1), ne - 1).astype(jnp.int32)
    tile_valid = jnp.clip((offs + counts)[tile_expert] - tile_start, 0, rows)
    n_used = jnp.concatenate([(ends[-1] // rows).reshape(1), tile_valid]).astype(jnp.int32)

    smem_idx = lambda n: pl.BlockSpec((n,), lambda i: (i,), memory_space=pltpu.SMEM)
    hbm = pl.BlockSpec(memory_space=pl.ANY)
    xs = pl.pallas_call(
        _dispatch_kernel,
        name="moe_dispatch",
        grid=(t // t_disp,),
        in_specs=[smem_idx(t_disp), smem_idx(t_disp), pl.BlockSpec((t_disp, d), lambda i: (i, 0)), hbm],
        out_specs=hbm,
        out_shape=jax.ShapeDtypeStruct((n_rows, d), F32),
        scratch_shapes=[pltpu.SemaphoreType.DMA(())],
        input_output_aliases={3: 0},
        compiler_params=_cparams("arbitrary"),
    )(pos1, pos2, x, jnp.zeros((n_rows, d), F32))

    def w_in_map(i, f, te, nu):
        return li, te[i], 0, jnp.where(i < nu[0], f, nf - 1)

    def w_out_map(i, f, te, nu):
        return li, te[i], jnp.where(i < nu[0], f, nf - 1), 0

    y = pl.pallas_call(
        _expert_kernel,
        name="moe_experts",
        grid_spec=pltpu.PrefetchScalarGridSpec(
            num_scalar_prefetch=2,
            grid=(n_tiles, nf),
            in_specs=[pl.BlockSpec((rows, d), lambda i, f, te, nu: (i, 0)),
                      pl.BlockSpec((1, d), lambda i, f, te, nu: (0, 0)),
                      pl.BlockSpec((None, None, d, tf), w_in_map),
                      pl.BlockSpec((None, None, d, tf), w_in_map),
                      pl.BlockSpec((None, None, tf, d), w_out_map)],
            out_specs=pl.BlockSpec((rows, d), lambda i, f, te, nu: (i, 0)),
            scratch_shapes=[pltpu.VMEM((rows, d), BF16)]),
        out_shape=jax.ShapeDtypeStruct((n_rows, d), F32),
        compiler_params=_cparams("arbitrary", "arbitrary"),
    )(tile_expert, n_used, xs, g2, wg, wu, wd)

    gain = jnp.ones((1, d), F32) if final_g is None else final_g.reshape(1, d)
    return pl.pallas_call(
        functools.partial(_combine_kernel, final_g is not None),
        name="moe_combine",
        grid=(t // t_comb,),
        in_specs=[smem_idx(t_comb), smem_idx(t_comb),
                  pl.BlockSpec((t_comb, d), lambda i: (i, 0)),
                  pl.BlockSpec((t_comb, LANES), lambda i: (i, 0)),
                  pl.BlockSpec((1, d), lambda i: (0, 0)),
                  hbm],
        out_specs=pl.BlockSpec((t_comb, d), lambda i: (i, 0)),
        out_shape=jax.ShapeDtypeStruct((t, d), F32),
        scratch_shapes=[pltpu.VMEM((t_comb, d), F32), pltpu.VMEM((t_comb, d), F32),
                        pltpu.SemaphoreType.DMA(())],
        compiler_params=_cparams("arbitrary"),
    )(pos1, pos2, x, route, gain, y)


def _gelu_tanh(x):
    return 0.5 * x * (1.0 + jnp.tanh(math.sqrt(2.0 / math.pi) * (x + 0.044715 * (x * x * x))))


def _compress_kernel(kv_ref, pe_ref, w1_ref, w2a_ref, w2b_ref, ka_ref, kb_ref, xf_ref, sh_ref):
    seq = kv_ref.shape[0]
    n = seq // CMP_STRIDE
    hid2 = w1_ref.shape[2]
    xf_ref[...] = kv_ref[...].astype(F32)
    top = jnp.zeros((n, hid2), F32)
    bot = jnp.zeros((n, hid2), F32)
    for i in range(CMP_STRIDE):
        xi = xf_ref[pl.ds(i, n, stride=CMP_STRIDE), :]
        top = top + _dot((xi + pe_ref[i:i + 1, :]).astype(BF16), w1_ref[i])
        bot = bot + _dot((xi + pe_ref[CMP_STRIDE + i:CMP_STRIDE + i + 1, :]).astype(BF16), w1_ref[CMP_STRIDE + i])
    sh_ref[0:n, :] = bot
    sh_ref[n:n + 8, :] = jnp.zeros((8, hid2), F32)
    act = _gelu_tanh(top + sh_ref[1:n + 1, :]).astype(BF16)
    ka_ref[...] = _dot(act, w2a_ref[...]).astype(ka_ref.dtype)
    kb_ref[...] = _dot(act, w2b_ref[...]).astype(kb_ref.dtype)


def _compress(kv, pe_k, pe_v, k_w1, k_w2, v_w1, v_w2, batch, seq):
    assert CMP_LEN == 2 * CMP_STRIDE
    hid = k_w1.shape[1]
    n = seq // CMP_STRIDE
    w1 = jnp.zeros((CMP_LEN, LANES, 2 * hid), F32)
    w1 = w1.at[:, :HEAD_DIM, :hid].set(k_w1.reshape(CMP_LEN, HEAD_DIM, hid))
    w1 = w1.at[:, HEAD_DIM:, hid:].set(v_w1.reshape(CMP_LEN, HEAD_DIM, hid)).astype(BF16)
    zero = jnp.zeros((hid, HEAD_DIM), F32)
    w2a = jnp.block([[k_w2, zero], [zero, v_w2]]).astype(BF16)
    w2b = jnp.block([[zero, k_w2], [v_w2, zero]]).astype(BF16)
    pe = jnp.concatenate([pe_k, pe_v], axis=-1)
    g = NSA_GROUPS
    out = jax.ShapeDtypeStruct((batch * g * n, LANES), BF16)
    return pl.pallas_call(
        _compress_kernel,
        name="compress",
        grid=(batch, g),
        in_specs=[pl.BlockSpec((seq, LANES), lambda b, gi: (b, gi)),
                  pl.BlockSpec((CMP_LEN, LANES), lambda b, gi: (0, 0)),
                  pl.BlockSpec((CMP_LEN, LANES, 2 * hid), lambda b, gi: (0, 0, 0)),
                  pl.BlockSpec((2 * hid, LANES), lambda b, gi: (0, 0)),
                  pl.BlockSpec((2 * hid, LANES), lambda b, gi: (0, 0))],
        out_specs=[pl.BlockSpec((n, LANES), lambda b, gi: (b * g + gi, 0)),
                   pl.BlockSpec((n, LANES), lambda b, gi: (b * g + gi, 0))],
        out_shape=[out, out],
        scratch_shapes=[pltpu.VMEM((seq, LANES), F32), pltpu.VMEM((n + 8, 2 * hid), F32)],
        compiler_params=_cparams("parallel", "parallel"),
    )(kv, pe, w1, w2a, w2b)


def _nsa_cmp_kernel(n_slc, q_ref, ka_ref, kb_ref, ovt_ref, ocmp_ref, selm_ref):
    qi = pl.program_id(2)
    tq = q_ref.shape[0]
    n_pad = ka_ref.shape[0]
    t_q = qi * tq + lax.broadcasted_iota(jnp.int32, (tq, n_pad), 0)
    n_c = lax.broadcasted_iota(jnp.int32, (tq, n_pad), 1)
    mask_c = (n_c * CMP_STRIDE + (CMP_LEN - 1)) <= t_q
    halves = _head_halves(tq)
    ka = ka_ref[...]
    kb = kb_ref[...]
    psum = jnp.zeros((tq, n_pad), F32)
    for pr in range(NSA_HPG // 2):
        qf = q_ref[:, pr * LANES:(pr + 1) * LANES].astype(F32)
        outs = []
        for h in range(2):
            qz = jnp.where(halves[h], qf, 0.0).astype(BF16)
            kmat, vmat = (ka, kb) if h == 0 else (kb, ka)
            s = jnp.where(mask_c, _dot_nt(qz, kmat), NEG_INF)
            m = jnp.max(s, axis=1, keepdims=True)
            m = jnp.where(m == NEG_INF, 0.0, m)
            e = jnp.exp(s - m)
            p = e / jnp.maximum(jnp.sum(e, axis=1, keepdims=True), 1e-30)
            psum = psum + p
            outs.append(_dot(p.astype(BF16), vmat))
        ocmp_ref[:, pr * LANES:(pr + 1) * LANES] = jnp.where(halves[0], outs[0], outs[1]).astype(ocmp_ref.dtype)

    ps_hi, ps_lo = _split_bf16(psum)
    ovt = ovt_ref[...]
    p_slc = (_dot_nt(ovt, ps_hi) + _dot_nt(ovt, ps_lo))[0:n_slc]
    cand = lax.broadcasted_iota(jnp.int32, (n_slc, tq), 0)
    t_blk = jnp.right_shift(qi * tq + lax.broadcasted_iota(jnp.int32, (n_slc, tq), 1), int(math.log2(SLC_BLOCK)))
    valid = cand <= t_blk
    forced = (cand == 0) | (cand == t_blk) | (cand == t_blk - 1)
    score = jnp.where(valid, jnp.where(forced, FORCE_SCORE, p_slc), -1.0)
    keep = _rank_select(score, valid, n_slc, min(SLC_TOPN, n_slc))
    neg = jnp.where(keep, 0.0, MASK_NEG)
    zero = jnp.zeros((HEAD_DIM - n_slc, tq), F32)
    selm_ref[...] = jnp.concatenate([neg, zero, neg, zero], axis=0).T.astype(selm_ref.dtype)


def _nsa_sw_tiles(tiles, q_ref, selm_ref, gate_ref, ex_ref, ocmp_ref, o_ref, ksel_ref, vsel_ref, kwin_ref, vwin_ref):
    tq = ATT_TILE
    halves = _head_halves(tq)
    causal, beyond = _tile_masks(tq)
    ex = ex_ref[...]
    sel_s, win_s = {}, {}
    for c in tiles:
        rows = slice(c * tq, (c + 1) * tq)
        w = (c + 1) * tq
        lo = max(c - WINDOW // tq, 0) * tq
        qf = q_ref[rows, :].astype(F32)
        selm = selm_ref[rows, :].astype(F32)
        for h in range(2):
            q_aug = jnp.where(halves[h], qf, selm).astype(BF16)
            qz = jnp.where(halves[h], qf, 0.0).astype(BF16)
            sel_s[c, h] = _mask_tile(_dot_nt(q_aug, ksel_ref[h, 0:w, :]), c * tq, causal)
            s = _mask_tile(_dot_nt(qz, kwin_ref[h, lo:w, :]), c * tq - lo, causal)
            if c * tq - lo == WINDOW:
                s = _mask_tile(s, 0, beyond)
            win_s[c, h] = s
    for c in tiles:
        rows = slice(c * tq, (c + 1) * tq)
        w = (c + 1) * tq
        lo = max(c - WINDOW // tq, 0) * tq
        win_o = [_softmax_pv(win_s[c, h], vwin_ref[h, lo:w, :], h) for h in range(2)]
        sel_o = [_softmax_pv(sel_s[c, h], vsel_ref[h, 0:w, :], h) for h in range(2)]
        o_sel = jnp.where(halves[0], sel_o[0], sel_o[1])
        o_win = jnp.where(halves[0], win_o[0], win_o[1])
        g_hi, g_lo = _split_bf16(gate_ref[rows, :])
        gexp = _dot(g_hi, ex) + _dot(g_lo, ex)
        y = (gexp[:, 0:LANES] * ocmp_ref[rows, :].astype(F32) + gexp[:, LANES:2 * LANES] * o_sel
             + gexp[:, 2 * LANES:3 * LANES] * o_win)
        o_ref[rows, :] = y.astype(o_ref.dtype)


def _nsa_sw_kernel(q_ref, selm_ref, kvs_ref, kvw_ref, gate_ref, ex_ref, ocmp_ref, o_ref,
                   ksel_ref, vsel_ref, kwin_ref, vwin_ref):
    pr = pl.program_id(2)
    qi = pl.program_id(3)
    seq = kvs_ref.shape[0]

    @pl.when((pr == 0) & (qi == 0))
    def _():
        first, second = _head_halves(seq)
        onehot = _block_onehot(seq, int(math.log2(SLC_BLOCK)))
        kv = kvs_ref[...].astype(F32)
        swapped = pltpu.roll(kv, HEAD_DIM, axis=1)
        ksel_ref[0] = jnp.where(first, kv, onehot).astype(BF16)
        ksel_ref[1] = jnp.where(second, swapped, onehot).astype(BF16)
        vsel_ref[0] = _with_ones(swapped, first)
        vsel_ref[1] = _with_ones(kv, second)
        kv = kvw_ref[...].astype(F32)
        swapped = pltpu.roll(kv, HEAD_DIM, axis=1)
        kwin_ref[0] = kvw_ref[...]
        kwin_ref[1] = swapped.astype(BF16)
        vwin_ref[0] = _with_ones(swapped, first)
        vwin_ref[1] = _with_ones(kv, second)

    nq = seq // ATT_TILE
    for j in range(nq // TILES_PER_STEP):
        pl.when(qi == j)(functools.partial(_nsa_sw_tiles, _step_tiles(j, nq), q_ref, selm_ref, gate_ref, ex_ref,
                                           ocmp_ref, o_ref, ksel_ref, vsel_ref, kwin_ref, vwin_ref))


def _nsa_attention(q_raw, q_rot, kv, ka, kb, gates, batch, seq):
    nq = seq // ATT_TILE
    g = NSA_GROUPS
    gw = NSA_HPG * HEAD_DIM
    n_pairs = NSA_HEADS // 2
    n_cmp = (seq - CMP_LEN) // CMP_STRIDE + 1
    n_cmp_pad = seq // CMP_STRIDE
    n_slc = seq // SLC_BLOCK
    assert n_cmp_pad == LANES and n_slc <= HEAD_DIM // 2 and WINDOW % ATT_TILE == 0

    ex = np.zeros((n_pairs, LANES, 3 * LANES), np.float32)
    for hd in range(NSA_HEADS):
        for br in range(3):
            lane0 = br * LANES + (hd % 2) * HEAD_DIM
            ex[hd // 2, hd * 3 + br, lane0:lane0 + HEAD_DIM] = 1.0
    c_s = np.arange(n_cmp) * CMP_STRIDE
    s_s = np.arange(n_slc) * SLC_BLOCK
    ov = np.clip(np.minimum(c_s[:, None] + CMP_LEN, s_s[None, :] + SLC_BLOCK)
                 - np.maximum(c_s[:, None], s_s[None, :]), 0, None) / CMP_LEN
    ovt = np.zeros((LANES, n_cmp_pad), np.float32)
    ovt[:n_slc, :n_cmp] = ov.T

    n_ct = seq // CMP_TILE
    o_cmp, selm = pl.pallas_call(
        functools.partial(_nsa_cmp_kernel, n_slc),
        name="nsa_cmp",
        grid=(batch, g, n_ct),
        in_specs=[pl.BlockSpec((CMP_TILE, gw), lambda b, gi, i: (b * n_ct + i, gi)),
                  pl.BlockSpec((n_cmp_pad, LANES), lambda b, gi, i: (b * g + gi, 0)),
                  pl.BlockSpec((n_cmp_pad, LANES), lambda b, gi, i: (b * g + gi, 0)),
                  pl.BlockSpec((LANES, n_cmp_pad), lambda b, gi, i: (0, 0))],
        out_specs=[pl.BlockSpec((CMP_TILE, gw), lambda b, gi, i: (b * n_ct + i, gi)),
                   pl.BlockSpec((CMP_TILE, LANES), lambda b, gi, i: (b * n_ct + i, gi))],
        out_shape=[jax.ShapeDtypeStruct((batch * seq, NSA_HEADS * HEAD_DIM), BF16),
                   jax.ShapeDtypeStruct((batch * seq, g * LANES), BF16)],
        compiler_params=_cparams("parallel", "parallel", "parallel"),
    )(q_raw, ka, kb, jnp.asarray(ovt, BF16))

    ppg = NSA_HPG // 2
    return pl.pallas_call(
        _nsa_sw_kernel,
        name="nsa_sw",
        grid=(batch, g, ppg, nq // TILES_PER_STEP),
        in_specs=[pl.BlockSpec((seq, LANES), lambda b, gi, p, i: (b, gi * ppg + p)),
                  pl.BlockSpec((seq, LANES), lambda b, gi, p, i: (b, gi)),
                  pl.BlockSpec((seq, LANES), lambda b, gi, p, i: (b, g + gi)),
                  pl.BlockSpec((seq, LANES), lambda b, gi, p, i: (b, 2 * g + gi)),
                  pl.BlockSpec((seq, LANES), lambda b, gi, p, i: (b, 0)),
                  pl.BlockSpec((None, LANES, 3 * LANES), lambda b, gi, p, i: (gi * ppg + p, 0, 0)),
                  pl.BlockSpec((seq, LANES), lambda b, gi, p, i: (b, gi * ppg + p))],
        out_specs=pl.BlockSpec((seq, LANES), lambda b, gi, p, i: (b, gi * ppg + p)),
        out_shape=jax.ShapeDtypeStruct((batch * seq, NSA_HEADS * HEAD_DIM), BF16),
        scratch_shapes=[pltpu.VMEM((2, seq, LANES), BF16) for _ in range(4)],
        compiler_params=_cparams("parallel", "parallel", "arbitrary", "arbitrary"),
    )(q_rot, selm, kv, kv, gates, jnp.asarray(ex, BF16), o_cmp)


def _even_mixer(x, g, w_in, pool_w, pool_scale, w_out, rope_pair, batch, seq):
    chunk = 2 * LANES
    plan = [(0, POOL_DIM, [(0, 0, "none", 0)])]
    for c in range(3 * MOBA_DIM // chunk):
        op = "rope_scale" if c < MOBA_DIM // chunk else ("rope" if c < 2 * MOBA_DIM // chunk else "none")
        plan.append((POOL_DIM + c * chunk, chunk, [(1, c * chunk, op, 0)]))
    u_pool, qkv = _norm_project(x, g, w_in.astype(BF16), plan, [rope_pair],
                                [(POOL_DIM, F32), (3 * MOBA_DIM, BF16)], seq)
    w_bd = jax.scipy.linalg.block_diag(*[pool_w[i] for i in range(pool_w.shape[0])]).astype(BF16)
    y_a = _pool_mixer(u_pool, w_bd, pool_scale, batch, seq)
    y_b = _moba_attention(qkv, batch, seq)
    w_out = w_out.astype(BF16)
    return [y_a, y_b], [w_out[:POOL_DIM], w_out[POOL_DIM:]]


def _odd_mixer(x, g, w_in, pe_k, pe_v, k_w1, k_w2, v_w1, v_w2, w_out, rope_pair, rope_single, batch, seq):
    d = x.shape[1]
    qd = NSA_HEADS * HEAD_DIM
    kvd = NSA_KV_DIM
    wq = w_in[:, :qd]
    parts = [w_in[:, qd + i * kvd:qd + (i + 1) * kvd].reshape(d, NSA_GROUPS, HEAD_DIM) for i in range(6)]
    pairs = [jnp.concatenate([parts[2 * i], parts[2 * i + 1]], axis=-1).reshape(d, 2 * kvd) for i in range(3)]
    n_gate = 3 * NSA_HEADS
    w_gate = jnp.zeros((d, LANES), F32).at[:, :n_gate].set(w_in[:, qd + 6 * kvd:])
    w_all = jnp.concatenate([wq] + pairs + [w_gate], axis=1).astype(BF16)

    chunk = 2 * LANES
    plan = []
    for c in range(qd // chunk):
        plan.append((c * chunk, chunk, [(0, c * chunk, "scale", 0), (1, c * chunk, "rope_scale", 0)]))
    for c in range(6 * kvd // chunk):
        branch = c // (2 * kvd // chunk)
        op = "none" if branch == 0 else "rope"
        plan.append((qd + c * chunk, chunk, [(2, c * chunk, op, 1)]))
    plan.append((qd + 6 * kvd, LANES, [(3, 0, "sigmoid", 0)]))
    q_raw, q_rot, kv, gates = _norm_project(
        x, g, w_all, plan, [rope_pair, rope_single],
        [(qd, BF16), (qd, BF16), (6 * kvd, BF16), (LANES, F32)], seq)

    ka, kb = _compress(kv, pe_k, pe_v, k_w1, k_w2, v_w1, v_w2, batch, seq)
    y = _nsa_attention(q_raw, q_rot, kv, ka, kb, gates, batch, seq)
    return [y], [w_out.astype(BF16)]


def kernel(x, mem, positions, norm_g, mem_g, final_g, w_in_ab, pool_w, pool_scale, w_out_ab, ffn_w_gate, ffn_w_up, ffn_w_down, w_in_c, cmp_pe_k, cmp_pe_v, cmp_k_w1, cmp_k_w2, cmp_v_w1, cmp_v_w2, w_out_c, router_w, moe_w_gate, moe_w_up, moe_w_down, xa_wq, xa_wkv, xa_wo):
    batch, seq, d = x.shape
    mem_len = mem.shape[1]
    depth = norm_g.shape[0]
    rope_pair, rope_single = _rope_tables(positions)

    wkv_all = jnp.concatenate([xa_wkv[l] for l in range(depth)], axis=1).astype(BF16)
    kv_all = _norm_matmul(mem.reshape(batch * mem_len, d), mem_g, wkv_all)

    xf = x.reshape(batch * seq, d)
    for layer in range(depth):
        i = layer // 2
        last = layer == depth - 1
        if layer % 2 == 0:
            ys, ws = _even_mixer(xf, norm_g[layer, 0], w_in_ab[i], pool_w[i], pool_scale[i], w_out_ab[i],
                                 rope_pair, batch, seq)
        else:
            ys, ws = _odd_mixer(xf, norm_g[layer, 0], w_in_c[i], cmp_pe_k[i], cmp_pe_v[i], cmp_k_w1[i],
                                cmp_k_w2[i], cmp_v_w1[i], cmp_v_w2[i], w_out_c[i], rope_pair, rope_single,
                                batch, seq)
        xf = _out_proj(xf, ys, ws)
        mid_args = (xf, norm_g[layer, 1], xa_wq[layer].astype(BF16), kv_all, layer,
                    xa_wo[layer].astype(BF16), seq, mem_len)
        if layer % 2 == 0:
            xf = _mid_layer(*mid_args)
            xf = _ffn(xf, norm_g[layer, 2], ffn_w_gate[i].astype(BF16), ffn_w_up[i].astype(BF16),
                      ffn_w_down[i].astype(BF16))
            if last:
                xf = _final_norm(xf, final_g)
        else:
            xf, route, cnt = _mid_layer(*mid_args, router=(norm_g[layer, 2], router_w[i]))
            xf = _moe(xf, norm_g[layer, 2], route, cnt, moe_w_gate, moe_w_up, moe_w_down, i,
                      final_g=final_g if last else None)
    return xf.reshape(batch, seq, d)
```

```python
import functools
import math

import numpy as np
import jax
import jax.numpy as jnp
from jax import lax
from jax.experimental import pallas as pl
from jax.experimental.pallas import tpu as pltpu

F32 = jnp.float32
BF16 = jnp.bfloat16

HEAD_DIM = 64
ROPE_DIM = 16
ROPE_THETA = 500000.0
NORM_EPS = 1e-5
QK_SCALE = HEAD_DIM ** -0.5
QK_SCALE_LOG2E = QK_SCALE * math.log2(math.e)

POOL_WINDOWS = (2, 4, 8, 16)
POOL_DIM = 256
MOBA_HEADS = 12
MOBA_DIM = MOBA_HEADS * HEAD_DIM
MOBA_BLOCK = 256
MOBA_TOPK = 3
NSA_HEADS = 16
NSA_GROUPS = 4
NSA_HPG = 4
NSA_KV_DIM = NSA_GROUPS * HEAD_DIM
CMP_LEN = 32
CMP_STRIDE = 16
SLC_BLOCK = 64
SLC_TOPN = 16
WINDOW = 512
FORCE_SCORE = 1e4
XA_HEADS = 4
XA_DIM = XA_HEADS * HEAD_DIM
N_EXPERTS = 8

LANES = 128
ATT_TILE = 256
CMP_TILE = 2048
TILES_PER_STEP = 4
VMEM_LIMIT = 56 * 1024 * 1024
NEG_INF = float("-inf")
MASK_NEG = -1e30


def _cparams(*sem):
    return pltpu.CompilerParams(dimension_semantics=sem, vmem_limit_bytes=VMEM_LIMIT)


def _dot(a, b):
    return jnp.dot(a, b, preferred_element_type=F32)


def _dot_nt(a, b):
    return lax.dot_general(a, b, (((1,), (1,)), ((), ())), preferred_element_type=F32)


def _split_bf16(a):
    hi = a.astype(BF16)
    lo = (a - hi.astype(F32)).astype(BF16)
    return hi, lo


def _rms_bf16(x, g):
    ms = jnp.mean(x * x, axis=-1, keepdims=True)
    return (x * lax.rsqrt(ms + NORM_EPS) * g).astype(BF16)


def _apply_rope(acc, c, a, b):
    half = ROPE_DIM // 2
    return acc * c + pltpu.roll(acc, LANES - half, axis=1) * a + pltpu.roll(acc, half, axis=1) * b


def _proj_kernel(plan, n_tab, x_ref, g_ref, w_ref, *rest):
    tabs = rest[:3 * n_tab]
    outs = rest[3 * n_tab:]
    h = _rms_bf16(x_ref[...], g_ref[...])
    for wc0, width, sinks in plan:
        acc = _dot(h, w_ref[:, wc0:wc0 + width])
        for oi, oc0, op, tab in sinks:
            for s in range(width // LANES):
                val = acc[:, s * LANES:(s + 1) * LANES]
                if op in ("rope", "rope_scale"):
                    c, a, b = (tabs[3 * tab + i][...] for i in range(3))
                    val = _apply_rope(val, c, a, b)
                if op == "scale":
                    val = val * QK_SCALE
                if op == "rope_scale":
                    val = val * QK_SCALE_LOG2E
                if op == "sigmoid":
                    val = jax.nn.sigmoid(val)
                o = outs[oi]
                o[:, oc0 + s * LANES:oc0 + (s + 1) * LANES] = val.astype(o.dtype)


def _norm_project(x, g, w, plan, tables, out_defs, seq, tm=512):
    t, d = x.shape
    n = w.shape[1]
    n_seq_tiles = seq // tm
    in_specs = [
        pl.BlockSpec((tm, d), lambda i: (i, 0)),
        pl.BlockSpec((1, d), lambda i: (0, 0)),
        pl.BlockSpec((d, n), lambda i: (0, 0)),
    ]
    flat_tabs = []
    for tset in tables:
        for tb in tset:
            flat_tabs.append(tb)
            in_specs.append(pl.BlockSpec((tm, LANES), lambda i: (i % n_seq_tiles, 0)))
    out_shape = [jax.ShapeDtypeStruct((t, wd), dt) for wd, dt in out_defs]
    out_specs = [pl.BlockSpec((tm, wd), lambda i: (i, 0)) for wd, _ in out_defs]
    return pl.pallas_call(
        functools.partial(_proj_kernel, plan, len(tables)),
        name="proj",
        grid=(t // tm,),
        in_specs=in_specs,
        out_specs=out_specs,
        out_shape=out_shape,
        compiler_params=_cparams("parallel"),
    )(x, g.reshape(1, d), w, *flat_tabs)


def _rope_tables(positions):
    half = ROPE_DIM // 2
    inv = ROPE_THETA ** (-jnp.arange(0, ROPE_DIM, 2, dtype=F32) / ROPE_DIM)
    ang = positions.astype(F32)[:, None] * inv[None, :]
    cos, sin = jnp.cos(ang), jnp.sin(ang)
    s = positions.shape[0]
    z8 = jnp.zeros((s, half), F32)
    rest0 = jnp.zeros((s, HEAD_DIM - ROPE_DIM), F32)
    rest1 = jnp.ones((s, HEAD_DIM - ROPE_DIM), F32)
    c64 = jnp.concatenate([cos, cos, rest1], -1)
    a64 = jnp.concatenate([-sin, z8, rest0], -1)
    b64 = jnp.concatenate([z8, sin, rest0], -1)
    one64 = jnp.ones((s, HEAD_DIM), F32)
    zero64 = jnp.zeros((s, HEAD_DIM), F32)
    pair = tuple(jnp.concatenate([m, m], -1) for m in (c64, a64, b64))
    single = (jnp.concatenate([c64, one64], -1), jnp.concatenate([a64, zero64], -1),
              jnp.concatenate([b64, zero64], -1))
    return pair, single


def _norm_matmul_kernel(x_ref, g_ref, w_ref, o_ref):
    h = _rms_bf16(x_ref[...], g_ref[...])
    o_ref[...] = _dot(h, w_ref[...]).astype(o_ref.dtype)


def _norm_matmul(x, g, w, tm=512):
    t, d = x.shape
    n = w.shape[1]
    return pl.pallas_call(
        _norm_matmul_kernel,
        name="mem_kv",
        grid=(t // tm,),
        in_specs=[pl.BlockSpec((tm, d), lambda i: (i, 0)),
                  pl.BlockSpec((1, d), lambda i: (0, 0)),
                  pl.BlockSpec((d, n), lambda i: (0, 0))],
        out_specs=pl.BlockSpec((tm, n), lambda i: (i, 0)),
        out_shape=jax.ShapeDtypeStruct((t, n), BF16),
        compiler_params=_cparams("parallel"),
    )(x, g.reshape(1, d), w)


def _final_norm_kernel(x_ref, g_ref, o_ref):
    x = x_ref[...]
    ms = jnp.mean(x * x, axis=-1, keepdims=True)
    o_ref[...] = x * lax.rsqrt(ms + NORM_EPS) * g_ref[...]


def _final_norm(x, g, tm=1024):
    t, d = x.shape
    return pl.pallas_call(
        _final_norm_kernel,
        name="final_norm",
        grid=(t // tm,),
        in_specs=[pl.BlockSpec((tm, d), lambda i: (i, 0)), pl.BlockSpec((1, d), lambda i: (0, 0))],
        out_specs=pl.BlockSpec((tm, d), lambda i: (i, 0)),
        out_shape=jax.ShapeDtypeStruct((t, d), F32),
        compiler_params=_cparams("parallel"),
    )(x, g.reshape(1, d))


def _pool_kernel(u_ref, w_ref, sc_ref, o_ref, pad_ref):
    s = u_ref.shape[0]
    maxw = POOL_WINDOWS[-1]
    u = u_ref[...]
    pad_ref[0:maxw, :] = jnp.zeros((maxw, POOL_DIM), F32)
    t1 = (lax.broadcasted_iota(jnp.int32, (s, POOL_DIM), 0) + 1).astype(F32)
    lane = lax.broadcasted_iota(jnp.int32, (s, POOL_DIM), 1)
    gdim = POOL_DIM // len(POOL_WINDOWS)
    cur = u
    pooled = jnp.zeros_like(u)
    shift = 1
    for gi, w in enumerate(POOL_WINDOWS):
        pad_ref[maxw:maxw + s, :] = cur
        cur = cur + pad_ref[maxw - shift:maxw - shift + s, :]
        shift *= 2
        assert shift == w
        mean = cur / jnp.minimum(t1, float(w))
        pooled = jnp.where((lane >= gi * gdim) & (lane < (gi + 1) * gdim), mean, pooled)
    pooled = (pooled - u).astype(BF16)
    o_ref[...] = (_dot(pooled, w_ref[...]) * sc_ref[...]).astype(o_ref.dtype)


def _pool_mixer(u_pool, w_bd, scale, batch, seq):
    return pl.pallas_call(
        _pool_kernel,
        name="pool",
        grid=(batch,),
        in_specs=[pl.BlockSpec((seq, POOL_DIM), lambda b: (b, 0)),
                  pl.BlockSpec((POOL_DIM, POOL_DIM), lambda b: (0, 0)),
                  pl.BlockSpec((1, POOL_DIM), lambda b: (0, 0))],
        out_specs=pl.BlockSpec((seq, POOL_DIM), lambda b: (b, 0)),
        out_shape=jax.ShapeDtypeStruct((batch * seq, POOL_DIM), BF16),
        scratch_shapes=[pltpu.VMEM((seq + POOL_WINDOWS[-1], POOL_DIM), F32)],
        compiler_params=_cparams("parallel"),
    )(u_pool, w_bd, scale.reshape(1, POOL_DIM))


def _rank_select(scoreT, valid, n_rows, topn):
    row = lax.broadcasted_iota(jnp.int32, scoreT.shape, 0)
    rank = jnp.zeros(scoreT.shape, F32)
    for j in range(n_rows):
        sj = scoreT[j:j + 1, :]
        beats = (scoreT > sj) | ((scoreT == sj) & (row < j))
        rj = jnp.sum(beats.astype(F32), axis=0, keepdims=True)
        rank = jnp.where(row == j, rj, rank)
    return (rank < float(topn)) & valid


def _head_halves(rows):
    lane = lax.broadcasted_iota(jnp.int32, (rows, LANES), 1)
    return lane < HEAD_DIM, lane >= HEAD_DIM


def _tile_masks(tq):
    qry = lax.broadcasted_iota(jnp.int32, (tq, tq), 0)
    key = lax.broadcasted_iota(jnp.int32, (tq, tq), 1)
    return key <= qry, key > qry


def _mask_tile(s, t0, mask):
    tq = mask.shape[0]
    parts = []
    if t0 > 0:
        parts.append(s[:, :t0])
    parts.append(jnp.where(mask, s[:, t0:t0 + tq], NEG_INF))
    if t0 + tq < s.shape[1]:
        parts.append(s[:, t0 + tq:])
    return parts[0] if len(parts) == 1 else jnp.concatenate(parts, axis=1)


def _softmax_pv(s, v, h):
    m = jnp.max(s, axis=1, keepdims=True)
    p = jnp.exp2((s - m).astype(BF16))
    o = _dot(p, v)
    denom_lane = (1 - h) * HEAD_DIM
    return o / o[:, denom_lane:denom_lane + 1]


def _with_ones(v, own):
    return jnp.where(own, v, 1.0).astype(BF16)


def _step_tiles(j, nq):
    half = TILES_PER_STEP // 2
    return tuple(t for k in range(half) for t in (nq - 1 - (j * half + k), j * half + k))


def _block_onehot(seq, shift):
    lane = lax.broadcasted_iota(jnp.int32, (seq, LANES), 1)
    blk = jnp.right_shift(lax.broadcasted_iota(jnp.int32, (seq, LANES), 0), shift)
    return jnp.where((lane & (HEAD_DIM - 1)) == blk, 1.0, 0.0)


def _moba_tiles(tiles, o_ref, qaug_ref, kaug_ref, vaug_ref):
    tq = ATT_TILE
    halves = _head_halves(tq)
    causal, _ = _tile_masks(tq)
    scores = {}
    for c in tiles:
        for h in range(2):
            s = _dot_nt(qaug_ref[h, c * tq:(c + 1) * tq, :], kaug_ref[h, 0:(c + 1) * tq, :])
            scores[c, h] = _mask_tile(s, c * tq, causal)
    for c in tiles:
        outs = [_softmax_pv(scores[c, h], vaug_ref[h, 0:(c + 1) * tq, :], h) for h in range(2)]
        o_ref[c * tq:(c + 1) * tq, :] = jnp.where(halves[0], outs[0], outs[1]).astype(o_ref.dtype)


def _moba_kernel(q_ref, k_ref, v_ref, o_ref, qaug_ref, kaug_ref, vaug_ref, kmean_ref):
    qi = pl.program_id(2)
    seq = k_ref.shape[0]
    nb = seq // MOBA_BLOCK
    blk_shift = int(math.log2(MOBA_BLOCK))

    @pl.when(qi == 0)
    def _():
        halves = _head_halves(seq)
        kf = k_ref[...].astype(F32)
        vf = v_ref[...].astype(F32)
        qp = q_ref[...]
        qf = qp.astype(F32)
        onehot = _block_onehot(seq, blk_shift)
        kmean_ref[...] = jnp.zeros(kmean_ref.shape, F32)
        for j in range(nb):
            kmean_ref[j:j + 1, :] = jnp.mean(kf[j * MOBA_BLOCK:(j + 1) * MOBA_BLOCK], axis=0, keepdims=True)
        cand = lax.broadcasted_iota(jnp.int32, (nb, seq), 0)
        own = jnp.right_shift(lax.broadcasted_iota(jnp.int32, (nb, seq), 1), blk_shift)
        valid = cand < own
        for h in range(2):
            kaug_ref[h] = jnp.where(halves[h], kf, onehot).astype(BF16)
            vaug_ref[h] = _with_ones(vf, halves[h])
            mine_k = _head_halves(kmean_ref.shape[0])[h]
            km_hi, km_lo = _split_bf16(jnp.where(mine_k, kmean_ref[...], 0.0))
            gate = (_dot_nt(km_hi, qp) + _dot_nt(km_lo, qp))[0:nb]
            gate = jnp.where(valid, gate, NEG_INF)
            keep = _rank_select(gate, valid, nb, MOBA_TOPK) | (cand == own)
            neg = jnp.where(keep, 0.0, MASK_NEG)
            lead = HEAD_DIM if h == 0 else 0
            pieces = [neg, jnp.zeros((LANES - lead - nb, seq), F32)]
            if lead:
                pieces.insert(0, jnp.zeros((lead, seq), F32))
            maskcols = jnp.concatenate(pieces, axis=0).T
            qaug_ref[h] = jnp.where(halves[h], qf, maskcols).astype(BF16)

    for j in range(nb // TILES_PER_STEP):
        pl.when(qi == j)(functools.partial(_moba_tiles, _step_tiles(j, nb), o_ref, qaug_ref, kaug_ref, vaug_ref))


def _moba_attention(qkv, batch, seq):
    nq = seq // ATT_TILE
    assert MOBA_BLOCK == ATT_TILE and nq <= 16 and nq % TILES_PER_STEP == 0
    hp = MOBA_DIM // LANES
    return pl.pallas_call(
        _moba_kernel,
        name="moba",
        grid=(batch, hp, nq // TILES_PER_STEP),
        in_specs=[pl.BlockSpec((seq, LANES), lambda b, p, i: (b, p)),
                  pl.BlockSpec((seq, LANES), lambda b, p, i: (b, hp + p)),
                  pl.BlockSpec((seq, LANES), lambda b, p, i: (b, 2 * hp + p))],
        out_specs=pl.BlockSpec((seq, LANES), lambda b, p, i: (b, p)),
        out_shape=jax.ShapeDtypeStruct((batch * seq, MOBA_DIM), BF16),
        scratch_shapes=[pltpu.VMEM((2, seq, LANES), BF16), pltpu.VMEM((2, seq, LANES), BF16),
                        pltpu.VMEM((2, seq, LANES), BF16), pltpu.VMEM((16, LANES), F32)],
        compiler_params=_cparams("parallel", "parallel", "arbitrary"),
    )(qkv, qkv, qkv)


def _out_proj_kernel(n_in, x_ref, *rest):
    ys = rest[:n_in]
    ws = rest[n_in:2 * n_in]
    o_ref = rest[2 * n_in]
    acc = x_ref[...]
    for y, w in zip(ys, ws):
        acc = acc + _dot(y[...], w[...])
    o_ref[...] = acc


def _out_proj(x, ys, ws, tm=512):
    t, d = x.shape
    in_specs = [pl.BlockSpec((tm, d), lambda i: (i, 0))]
    in_specs += [pl.BlockSpec((tm, y.shape[1]), lambda i: (i, 0)) for y in ys]
    in_specs += [pl.BlockSpec(w.shape, lambda i: (0, 0)) for w in ws]
    return pl.pallas_call(
        functools.partial(_out_proj_kernel, len(ys)),
        name="out_proj",
        grid=(t // tm,),
        in_specs=in_specs,
        out_specs=pl.BlockSpec((tm, d), lambda i: (i, 0)),
        out_shape=jax.ShapeDtypeStruct((t, d), F32),
        compiler_params=_cparams("parallel"),
    )(x, *ys, *ws)


def _mid_kernel(with_route, x_ref, g_ref, wq_ref, kv_ref, wo_ref, *rest):
    o_ref = rest[3] if with_route else rest[0]
    x = x_ref[...]
    h = _rms_bf16(x, g_ref[...])
    q = _dot(h, wq_ref[...]) * QK_SCALE_LOG2E
    halves = _head_halves(x.shape[0])
    mem_halves = _head_halves(kv_ref.shape[0])
    out = x
    for pr in range(XA_HEADS // 2):
        lanes = slice(pr * LANES, (pr + 1) * LANES)
        qf = q[:, lanes]
        kp = kv_ref[:, lanes]
        vf = kv_ref[:, XA_DIM + pr * LANES:XA_DIM + (pr + 1) * LANES].astype(F32)
        scores = [_dot_nt(jnp.where(halves[hh], qf, 0.0).astype(BF16), kp) for hh in range(2)]
        outs = [_softmax_pv(scores[hh], _with_ones(vf, mem_halves[hh]), hh) for hh in range(2)]
        o_pair = jnp.where(halves[0], outs[0], outs[1]).astype(BF16)
        out = out + _dot(o_pair, wo_ref[lanes, :])
    o_ref[...] = out

    if with_route:
        g3_ref, rhi_ref, rlo_ref, _, route_ref, cnt_ref = rest

        @pl.when(pl.program_id(0) == 0)
        def _():
            cnt_ref[...] = jnp.zeros(cnt_ref.shape, F32)

        route_ref[...] = _route_rows(_rms_bf16(out, g3_ref[...]), rhi_ref[...], rlo_ref[...], cnt_ref)


def _mid_layer(x, g, wq, kv_all, layer, wo, seq, mem_len, router=None, tm=512):
    t, d = x.shape
    tiles_per_seq = seq // tm
    const = lambda i: (0, 0)
    rows = lambda i: (i, 0)
    in_specs = [pl.BlockSpec((tm, d), rows),
                pl.BlockSpec((1, d), const),
                pl.BlockSpec((d, XA_DIM), const),
                pl.BlockSpec((mem_len, 2 * XA_DIM), lambda i: (i // tiles_per_seq, layer)),
                pl.BlockSpec((XA_DIM, d), const)]
    args = [x, g.reshape(1, d), wq, kv_all, wo]
    out_specs = [pl.BlockSpec((tm, d), rows)]
    out_shape = [jax.ShapeDtypeStruct((t, d), F32)]
    if router is not None:
        g3, router_w = router
        rw = jnp.zeros((d, LANES), F32).at[:, :router_w.shape[1]].set(router_w)
        rhi = rw.astype(BF16)
        rlo = (rw - rhi.astype(F32)).astype(BF16)
        in_specs += [pl.BlockSpec((1, d), const), pl.BlockSpec((d, LANES), const), pl.BlockSpec((d, LANES), const)]
        args += [g3.reshape(1, d), rhi, rlo]
        out_specs += [pl.BlockSpec((tm, LANES), rows), pl.BlockSpec((8, LANES), const)]
        out_shape += [jax.ShapeDtypeStruct((t, LANES), F32), jax.ShapeDtypeStruct((8, LANES), F32)]
    res = pl.pallas_call(
        functools.partial(_mid_kernel, router is not None),
        name="mid",
        grid=(t // tm,),
        in_specs=in_specs,
        out_specs=out_specs,
        out_shape=out_shape,
        compiler_params=_cparams("arbitrary"),
    )(*args)
    return res if router is not None else res[0]


def _ffn_kernel(x_ref, g_ref, wg_ref, wu_ref, wd_ref, o_ref, h_ref):
    f = pl.program_id(1)

    @pl.when(f == 0)
    def _():
        x = x_ref[...]
        h_ref[...] = _rms_bf16(x, g_ref[...])
        o_ref[...] = x

    h = h_ref[...]
    gate = _dot(h, wg_ref[...])
    a = (gate * jax.nn.sigmoid(gate)) * _dot(h, wu_ref[...])
    o_ref[...] += _dot(a.astype(BF16), wd_ref[...])


def _ffn(x, g, wg, wu, wd, tm=2048, tf=256):
    t, d = x.shape
    ff = wg.shape[1]
    return pl.pallas_call(
        _ffn_kernel,
        name="ffn",
        grid=(t // tm, ff // tf),
        in_specs=[pl.BlockSpec((tm, d), lambda i, f: (i, 0)),
                  pl.BlockSpec((1, d), lambda i, f: (0, 0)),
                  pl.BlockSpec((d, tf), lambda i, f: (0, f)),
                  pl.BlockSpec((d, tf), lambda i, f: (0, f)),
                  pl.BlockSpec((tf, d), lambda i, f: (f, 0))],
        out_specs=pl.BlockSpec((tm, d), lambda i, f: (i, 0)),
        out_shape=jax.ShapeDtypeStruct((t, d), F32),
        scratch_shapes=[pltpu.VMEM((tm, d), BF16)],
        compiler_params=_cparams("parallel", "arbitrary"),
    )(x, g.reshape(1, d), wg, wu, wd)


ROUTE_I1, ROUTE_I2, ROUTE_W1, ROUTE_W2, ROUTE_R1, ROUTE_R2 = range(6)
MOE_TILE = 1024
MOE_SUB = 256


def _lane_pick(arr, lane, k):
    return jnp.sum(jnp.where(lane == k, arr, 0.0), axis=1, keepdims=True)


def _route_rows(h, rhi, rlo, cnt_ref):
    tm = h.shape[0]
    lane = lax.broadcasted_iota(jnp.int32, (tm, LANES), 1)
    lane_f = lane.astype(F32)
    logits = _dot(h, rhi) + _dot(h, rlo)
    logits = jnp.where(lane < N_EXPERTS, logits, NEG_INF)
    m1 = jnp.max(logits, axis=1, keepdims=True)
    i1 = jnp.min(jnp.where(logits == m1, lane_f, float(LANES)), axis=1, keepdims=True)
    rest = jnp.where(lane_f == i1, NEG_INF, logits)
    m2 = jnp.max(rest, axis=1, keepdims=True)
    i2 = jnp.min(jnp.where(rest == m2, lane_f, float(LANES)), axis=1, keepdims=True)
    e2 = jnp.exp(m2 - m1)
    w1 = 1.0 / (1.0 + e2)
    w2 = e2 / (1.0 + e2)

    chosen = jnp.where((lane_f == i1) | (lane_f == i2), 1.0, 0.0)
    r = lax.broadcasted_iota(jnp.int32, (tm, tm), 0)
    c = lax.broadcasted_iota(jnp.int32, (tm, tm), 1)
    earlier = jnp.where(c < r, 1.0, 0.0).astype(BF16)
    prefix = _dot(earlier, chosen.astype(BF16)) + cnt_ref[0:1, :]
    rank1 = jnp.sum(jnp.where(lane_f == i1, prefix, 0.0), axis=1, keepdims=True)
    rank2 = jnp.sum(jnp.where(lane_f == i2, prefix, 0.0), axis=1, keepdims=True)
    cnt_ref[0:1, :] = cnt_ref[0:1, :] + jnp.sum(chosen, axis=0, keepdims=True)

    cols = {ROUTE_I1: i1, ROUTE_I2: i2, ROUTE_W1: w1, ROUTE_W2: w2, ROUTE_R1: rank1, ROUTE_R2: rank2}
    route = jnp.zeros((tm, LANES), F32)
    for k, val in cols.items():
        route = jnp.where(lane == k, val, route)
    return route


def _dispatch_kernel(pos1_ref, pos2_ref, x_ref, xs_in, xs_hbm, sem):
    del xs_in
    n = pos1_ref.shape[0]

    def row_copy(r, p):
        return pltpu.make_async_copy(x_ref.at[pl.ds(r, 1)], xs_hbm.at[pl.ds(p, 1)], sem)

    def issue(r, carry):
        row_copy(r, pos1_ref[r]).start()
        row_copy(r, pos2_ref[r]).start()
        return carry

    lax.fori_loop(0, n, issue, 0, unroll=8)
    for _ in range(2):
        pltpu.make_async_copy(x_ref, xs_hbm.at[pl.ds(0, n)], sem).wait()


def _expert_kernel(te_ref, nu_ref, xs_ref, g_ref, wg_ref, wu_ref, wd_ref, y_ref, h_ref):
    del te_ref
    i = pl.program_id(0)
    f = pl.program_id(1)
    rows = xs_ref.shape[0]
    valid = nu_ref[1 + i]

    @pl.when(f == 0)
    def _():
        h_ref[...] = _rms_bf16(xs_ref[...], g_ref[...])
        y_ref[...] = jnp.zeros(y_ref.shape, y_ref.dtype)

    def swiglu(r0, nr):
        h = h_ref[r0:r0 + nr, :]
        gate = _dot(h, wg_ref[...].astype(BF16))
        a = (gate * jax.nn.sigmoid(gate)) * _dot(h, wu_ref[...].astype(BF16))
        y_ref[r0:r0 + nr, :] += _dot(a.astype(BF16), wd_ref[...].astype(BF16))

    pl.when(valid == rows)(functools.partial(swiglu, 0, rows))

    @pl.when((valid > 0) & (valid < rows))
    def _():
        for r0 in range(0, rows, MOE_SUB):
            pl.when(r0 < valid)(functools.partial(swiglu, r0, MOE_SUB))


def _combine_kernel(with_norm, pos1_ref, pos2_ref, x_ref, route_ref, g_ref, y_hbm, o_ref, buf1, buf2, sem):
    n = pos1_ref.shape[0]

    def row_copy(p, buf, r):
        return pltpu.make_async_copy(y_hbm.at[pl.ds(p, 1)], buf.at[pl.ds(r, 1)], sem)

    def issue(r, carry):
        row_copy(pos1_ref[r], buf1, r).start()
        row_copy(pos2_ref[r], buf2, r).start()
        return carry

    lax.fori_loop(0, n, issue, 0, unroll=8)
    for buf in (buf1, buf2):
        pltpu.make_async_copy(y_hbm.at[pl.ds(0, n)], buf, sem).wait()
    lane = lax.broadcasted_iota(jnp.int32, route_ref.shape, 1)
    route = route_ref[...]
    w1 = _lane_pick(route, lane, ROUTE_W1)
    w2 = _lane_pick(route, lane, ROUTE_W2)
    out = x_ref[...] + w1 * buf1[...] + w2 * buf2[...]
    if with_norm:
        ms = jnp.mean(out * out, axis=-1, keepdims=True)
        out = out * lax.rsqrt(ms + NORM_EPS) * g_ref[...]
    o_ref[...] = out


def _moe(x, g, route, cnt, wg, wu, wd, li, final_g=None, tf=512, t_disp=2048, t_comb=1024):
    t, d = x.shape
    _, ne, _, ff = wg.shape
    rows = MOE_TILE
    n_rows = 2 * t + ne * rows
    n_tiles = n_rows // rows
    nf = ff // tf
    g2 = g.reshape(1, d)

    counts = cnt[0, :ne].astype(jnp.int32)
    gsz = (counts + rows - 1) // rows * rows
    ends = jnp.cumsum(gsz)
    offs = ends - gsz
    i1 = route[:, ROUTE_I1].astype(jnp.int32)
    i2 = route[:, ROUTE_I2].astype(jnp.int32)
    pos1 = offs[i1] + route[:, ROUTE_R1].astype(jnp.int32)
    pos2 = offs[i2] + route[:, ROUTE_R2].astype(jnp.int32)
    tile_start = jnp.arange(n_tiles, dtype=jnp.int32) * rows
    tile_expert = jnp.minimum(jnp.sum(tile_start[:, None] >= ends[None, :], axis=1), ne - 1).astype(jnp.int32)
    tile_valid = jnp.clip((offs + counts)[tile_expert] - tile_start, 0, rows)
    n_used = jnp.concatenate([(ends[-1] // rows).reshape(1), tile_valid]).astype(jnp.int32)

    smem_idx = lambda n: pl.BlockSpec((n,), lambda i: (i,), memory_space=pltpu.SMEM)
    hbm = pl.BlockSpec(memory_space=pl.ANY)
    xs = pl.pallas_call(
        _dispatch_kernel,
        name="moe_dispatch",
        grid=(t // t_disp,),
        in_specs=[smem_idx(t_disp), smem_idx(t_disp), pl.BlockSpec((t_disp, d), lambda i: (i, 0)), hbm],
        out_specs=hbm,
        out_shape=jax.ShapeDtypeStruct((n_rows, d), F32),
        scratch_shapes=[pltpu.SemaphoreType.DMA(())],
        input_output_aliases={3: 0},
        compiler_params=_cparams("arbitrary"),
    )(pos1, pos2, x, jnp.zeros((n_rows, d), F32))

    def w_in_map(i, f, te, nu):
        return li, te[i], 0, jnp.where(i < nu[0], f, nf - 1)

    def w_out_map(i, f, te, nu):
        return li, te[i], jnp.where(i < nu[0], f, nf - 1), 0

    y = pl.pallas_call(
        _expert_kernel,
        name="moe_experts",
        grid_spec=pltpu.PrefetchScalarGridSpec(
            num_scalar_prefetch=2,
            grid=(n_tiles, nf),
            in_specs=[pl.BlockSpec((rows, d), lambda i, f, te, nu: (i, 0)),
                      pl.BlockSpec((1, d), lambda i, f, te, nu: (0, 0)),
                      pl.BlockSpec((None, None, d, tf), w_in_map),
                      pl.BlockSpec((None, None, d, tf), w_in_map),
                      pl.BlockSpec((None, None, tf, d), w_out_map)],
            out_specs=pl.BlockSpec((rows, d), lambda i, f, te, nu: (i, 0)),
            scratch_shapes=[pltpu.VMEM((rows, d), BF16)]),
        out_shape=jax.ShapeDtypeStruct((n_rows, d), F32),
        compiler_params=_cparams("arbitrary", "arbitrary"),
    )(tile_expert, n_used, xs, g2, wg, wu, wd)

    gain = jnp.ones((1, d), F32) if final_g is None else final_g.reshape(1, d)
    return pl.pallas_call(
        functools.partial(_combine_kernel, final_g is not None),
        name="moe_combine",
        grid=(t // t_comb,),
        in_specs=[smem_idx(t_comb), smem_idx(t_comb),
                  pl.BlockSpec((t_comb, d), lambda i: (i, 0)),
                  pl.BlockSpec((t_comb, LANES), lambda i: (i, 0)),
                  pl.BlockSpec((1, d), lambda i: (0, 0)),
                  hbm],
        out_specs=pl.BlockSpec((t_comb, d), lambda i: (i, 0)),
        out_shape=jax.ShapeDtypeStruct((t, d), F32),
        scratch_shapes=[pltpu.VMEM((t_comb, d), F32), pltpu.VMEM((t_comb, d), F32),
                        pltpu.SemaphoreType.DMA(())],
        compiler_params=_cparams("arbitrary"),
    )(pos1, pos2, x, route, gain, y)


def _gelu_tanh(x):
    return 0.5 * x * (1.0 + jnp.tanh(math.sqrt(2.0 / math.pi) * (x + 0.044715 * (x * x * x))))


def _compress_kernel(kv_ref, pe_ref, w1_ref, w2a_ref, w2b_ref, ka_ref, kb_ref, xf_ref, sh_ref):
    seq = kv_ref.shape[0]
    n = seq // CMP_STRIDE
    hid2 = w1_ref.shape[2]
    xf_ref[...] = kv_ref[...].astype(F32)
    top = jnp.zeros((n, hid2), F32)
    bot = jnp.zeros((n, hid2), F32)
    for i in range(CMP_STRIDE):
        xi = xf_ref[pl.ds(i, n, stride=CMP_STRIDE), :]
        top = top + _dot((xi + pe_ref[i:i + 1, :]).astype(BF16), w1_ref[i])
        bot = bot + _dot((xi + pe_ref[CMP_STRIDE + i:CMP_STRIDE + i + 1, :]).astype(BF16), w1_ref[CMP_STRIDE + i])
    sh_ref[0:n, :] = bot
    sh_ref[n:n + 8, :] = jnp.zeros((8, hid2), F32)
    act = _gelu_tanh(top + sh_ref[1:n + 1, :]).astype(BF16)
    ka_ref[...] = _dot(act, w2a_ref[...]).astype(ka_ref.dtype)
    kb_ref[...] = _dot(act, w2b_ref[...]).astype(kb_ref.dtype)


def _compress(kv, pe_k, pe_v, k_w1, k_w2, v_w1, v_w2, batch, seq):
    assert CMP_LEN == 2 * CMP_STRIDE
    hid = k_w1.shape[1]
    n = seq // CMP_STRIDE
    w1 = jnp.zeros((CMP_LEN, LANES, 2 * hid), F32)
    w1 = w1.at[:, :HEAD_DIM, :hid].set(k_w1.reshape(CMP_LEN, HEAD_DIM, hid))
    w1 = w1.at[:, HEAD_DIM:, hid:].set(v_w1.reshape(CMP_LEN, HEAD_DIM, hid)).astype(BF16)
    zero = jnp.zeros((hid, HEAD_DIM), F32)
    w2a = jnp.block([[k_w2, zero], [zero, v_w2]]).astype(BF16)
    w2b = jnp.block([[zero, k_w2], [v_w2, zero]]).astype(BF16)
    pe = jnp.concatenate([pe_k, pe_v], axis=-1)
    g = NSA_GROUPS
    out = jax.ShapeDtypeStruct((batch * g * n, LANES), BF16)
    return pl.pallas_call(
        _compress_kernel,
        name="compress",
        grid=(batch, g),
        in_specs=[pl.BlockSpec((seq, LANES), lambda b, gi: (b, gi)),
                  pl.BlockSpec((CMP_LEN, LANES), lambda b, gi: (0, 0)),
                  pl.BlockSpec((CMP_LEN, LANES, 2 * hid), lambda b, gi: (0, 0, 0)),
                  pl.BlockSpec((2 * hid, LANES), lambda b, gi: (0, 0)),
                  pl.BlockSpec((2 * hid, LANES), lambda b, gi: (0, 0))],
        out_specs=[pl.BlockSpec((n, LANES), lambda b, gi: (b * g + gi, 0)),
                   pl.BlockSpec((n, LANES), lambda b, gi: (b * g + gi, 0))],
        out_shape=[out, out],
        scratch_shapes=[pltpu.VMEM((seq, LANES), F32), pltpu.VMEM((n + 8, 2 * hid), F32)],
        compiler_params=_cparams("parallel", "parallel"),
    )(kv, pe, w1, w2a, w2b)


def _nsa_cmp_kernel(n_slc, q_ref, ka_ref, kb_ref, ovt_ref, ocmp_ref, selm_ref):
    qi = pl.program_id(2)
    tq = q_ref.shape[0]
    n_pad = ka_ref.shape[0]
    t_q = qi * tq + lax.broadcasted_iota(jnp.int32, (tq, n_pad), 0)
    n_c = lax.broadcasted_iota(jnp.int32, (tq, n_pad), 1)
    mask_c = (n_c * CMP_STRIDE + (CMP_LEN - 1)) <= t_q
    halves = _head_halves(tq)
    ka = ka_ref[...]
    kb = kb_ref[...]
    psum = jnp.zeros((tq, n_pad), F32)
    for pr in range(NSA_HPG // 2):
        qf = q_ref[:, pr * LANES:(pr + 1) * LANES].astype(F32)
        outs = []
        for h in range(2):
            qz = jnp.where(halves[h], qf, 0.0).astype(BF16)
            kmat, vmat = (ka, kb) if h == 0 else (kb, ka)
            s = jnp.where(mask_c, _dot_nt(qz, kmat), NEG_INF)
            m = jnp.max(s, axis=1, keepdims=True)
            m = jnp.where(m == NEG_INF, 0.0, m)
            e = jnp.exp(s - m)
            p = e / jnp.maximum(jnp.sum(e, axis=1, keepdims=True), 1e-30)
            psum = psum + p
            outs.append(_dot(p.astype(BF16), vmat))
        ocmp_ref[:, pr * LANES:(pr + 1) * LANES] = jnp.where(halves[0], outs[0], outs[1]).astype(ocmp_ref.dtype)

    ps_hi, ps_lo = _split_bf16(psum)
    ovt = ovt_ref[...]
    p_slc = (_dot_nt(ovt, ps_hi) + _dot_nt(ovt, ps_lo))[0:n_slc]
    cand = lax.broadcasted_iota(jnp.int32, (n_slc, tq), 0)
    t_blk = jnp.right_shift(qi * tq + lax.broadcasted_iota(jnp.int32, (n_slc, tq), 1), int(math.log2(SLC_BLOCK)))
    valid = cand <= t_blk
    forced = (cand == 0) | (cand == t_blk) | (cand == t_blk - 1)
    score = jnp.where(valid, jnp.where(forced, FORCE_SCORE, p_slc), -1.0)
    keep = _rank_select(score, valid, n_slc, min(SLC_TOPN, n_slc))
    neg = jnp.where(keep, 0.0, MASK_NEG)
    zero = jnp.zeros((HEAD_DIM - n_slc, tq), F32)
    selm_ref[...] = jnp.concatenate([neg, zero, neg, zero], axis=0).T.astype(selm_ref.dtype)


def _nsa_sw_tiles(tiles, q_ref, selm_ref, gate_ref, ex_ref, ocmp_ref, o_ref, ksel_ref, vsel_ref, kwin_ref, vwin_ref):
    tq = ATT_TILE
    halves = _head_halves(tq)
    causal, beyond = _tile_masks(tq)
    ex = ex_ref[...]
    sel_s, win_s = {}, {}
    for c in tiles:
        rows = slice(c * tq, (c + 1) * tq)
        w = (c + 1) * tq
        lo = max(c - WINDOW // tq, 0) * tq
        qf = q_ref[rows, :].astype(F32)
        selm = selm_ref[rows, :].astype(F32)
        for h in range(2):
            q_aug = jnp.where(halves[h], qf, selm).astype(BF16)
            qz = jnp.where(halves[h], qf, 0.0).astype(BF16)
            sel_s[c, h] = _mask_tile(_dot_nt(q_aug, ksel_ref[h, 0:w, :]), c * tq, causal)
            s = _mask_tile(_dot_nt(qz, kwin_ref[h, lo:w, :]), c * tq - lo, causal)
            if c * tq - lo == WINDOW:
                s = _mask_tile(s, 0, beyond)
            win_s[c, h] = s
    for c in tiles:
        rows = slice(c * tq, (c + 1) * tq)
        w = (c + 1) * tq
        lo = max(c - WINDOW // tq, 0) * tq
        win_o = [_softmax_pv(win_s[c, h], vwin_ref[h, lo:w, :], h) for h in range(2)]
        sel_o = [_softmax_pv(sel_s[c, h], vsel_ref[h, 0:w, :], h) for h in range(2)]
        o_sel = jnp.where(halves[0], sel_o[0], sel_o[1])
        o_win = jnp.where(halves[0], win_o[0], win_o[1])
        g_hi, g_lo = _split_bf16(gate_ref[rows, :])
        gexp = _dot(g_hi, ex) + _dot(g_lo, ex)
        y = (gexp[:, 0:LANES] * ocmp_ref[rows, :].astype(F32) + gexp[:, LANES:2 * LANES] * o_sel
             + gexp[:, 2 * LANES:3 * LANES] * o_win)
        o_ref[rows, :] = y.astype(o_ref.dtype)


def _nsa_sw_kernel(q_ref, selm_ref, kvs_ref, kvw_ref, gate_ref, ex_ref, ocmp_ref, o_ref,
                   ksel_ref, vsel_ref, kwin_ref, vwin_ref):
    pr = pl.program_id(2)
    qi = pl.program_id(3)
    seq = kvs_ref.shape[0]

    @pl.when((pr == 0) & (qi == 0))
    def _():
        first, second = _head_halves(seq)
        onehot = _block_onehot(seq, int(math.log2(SLC_BLOCK)))
        kv = kvs_ref[...].astype(F32)
        swapped = pltpu.roll(kv, HEAD_DIM, axis=1)
        ksel_ref[0] = jnp.where(first, kv, onehot).astype(BF16)
        ksel_ref[1] = jnp.where(second, swapped, onehot).astype(BF16)
        vsel_ref[0] = _with_ones(swapped, first)
        vsel_ref[1] = _with_ones(kv, second)
        kv = kvw_ref[...].astype(F32)
        swapped = pltpu.roll(kv, HEAD_DIM, axis=1)
        kwin_ref[0] = kvw_ref[...]
        kwin_ref[1] = swapped.astype(BF16)
        vwin_ref[0] = _with_ones(swapped, first)
        vwin_ref[1] = _with_ones(kv, second)

    nq = seq // ATT_TILE
    for j in range(nq // TILES_PER_STEP):
        pl.when(qi == j)(functools.partial(_nsa_sw_tiles, _step_tiles(j, nq), q_ref, selm_ref, gate_ref, ex_ref,
                                           ocmp_ref, o_ref, ksel_ref, vsel_ref, kwin_ref, vwin_ref))


def _nsa_attention(q_raw, q_rot, kv, ka, kb, gates, batch, seq):
    nq = seq // ATT_TILE
    g = NSA_GROUPS
    gw = NSA_HPG * HEAD_DIM
    n_pairs = NSA_HEADS // 2
    n_cmp = (seq - CMP_LEN) // CMP_STRIDE + 1
    n_cmp_pad = seq // CMP_STRIDE
    n_slc = seq // SLC_BLOCK
    assert n_cmp_pad == LANES and n_slc <= HEAD_DIM // 2 and WINDOW % ATT_TILE == 0

    ex = np.zeros((n_pairs, LANES, 3 * LANES), np.float32)
    for hd in range(NSA_HEADS):
        for br in range(3):
            lane0 = br * LANES + (hd % 2) * HEAD_DIM
            ex[hd // 2, hd * 3 + br, lane0:lane0 + HEAD_DIM] = 1.0
    c_s = np.arange(n_cmp) * CMP_STRIDE
    s_s = np.arange(n_slc) * SLC_BLOCK
    ov = np.clip(np.minimum(c_s[:, None] + CMP_LEN, s_s[None, :] + SLC_BLOCK)
                 - np.maximum(c_s[:, None], s_s[None, :]), 0, None) / CMP_LEN
    ovt = np.zeros((LANES, n_cmp_pad), np.float32)
    ovt[:n_slc, :n_cmp] = ov.T

    n_ct = seq // CMP_TILE
    o_cmp, selm = pl.pallas_call(
        functools.partial(_nsa_cmp_kernel, n_slc),
        name="nsa_cmp",
        grid=(batch, g, n_ct),
        in_specs=[pl.BlockSpec((CMP_TILE, gw), lambda b, gi, i: (b * n_ct + i, gi)),
                  pl.BlockSpec((n_cmp_pad, LANES), lambda b, gi, i: (b * g + gi, 0)),
                  pl.BlockSpec((n_cmp_pad, LANES), lambda b, gi, i: (b * g + gi, 0)),
                  pl.BlockSpec((LANES, n_cmp_pad), lambda b, gi, i: (0, 0))],
        out_specs=[pl.BlockSpec((CMP_TILE, gw), lambda b, gi, i: (b * n_ct + i, gi)),
                   pl.BlockSpec((CMP_TILE, LANES), lambda b, gi, i: (b * n_ct + i, gi))],
        out_shape=[jax.ShapeDtypeStruct((batch * seq, NSA_HEADS * HEAD_DIM), BF16),
                   jax.ShapeDtypeStruct((batch * seq, g * LANES), BF16)],
        compiler_params=_cparams("parallel", "parallel", "parallel"),
    )(q_raw, ka, kb, jnp.asarray(ovt, BF16))

    ppg = NSA_HPG // 2
    return pl.pallas_call(
        _nsa_sw_kernel,
        name="nsa_sw",
        grid=(batch, g, ppg, nq // TILES_PER_STEP),
        in_specs=[pl.BlockSpec((seq, LANES), lambda b, gi, p, i: (b, gi * ppg + p)),
                  pl.BlockSpec((seq, LANES), lambda b, gi, p, i: (b, gi)),
                  pl.BlockSpec((seq, LANES), lambda b, gi, p, i: (b, g + gi)),
                  pl.BlockSpec((seq, LANES), lambda b, gi, p, i: (b, 2 * g + gi)),
                  pl.BlockSpec((seq, LANES), lambda b, gi, p, i: (b, 0)),
                  pl.BlockSpec((None, LANES, 3 * LANES), lambda b, gi, p, i: (gi * ppg + p, 0, 0)),
                  pl.BlockSpec((seq, LANES), lambda b, gi, p, i: (b, gi * ppg + p))],
        out_specs=pl.BlockSpec((seq, LANES), lambda b, gi, p, i: (b, gi * ppg + p)),
        out_shape=jax.ShapeDtypeStruct((batch * seq, NSA_HEADS * HEAD_DIM), BF16),
        scratch_shapes=[pltpu.VMEM((2, seq, LANES), BF16) for _ in range(4)],
        compiler_params=_cparams("parallel", "parallel", "arbitrary", "arbitrary"),
    )(q_rot, selm, kv, kv, gates, jnp.asarray(ex, BF16), o_cmp)


def _even_mixer(x, g, w_in, pool_w, pool_scale, w_out, rope_pair, batch, seq):
    chunk = 2 * LANES
    plan = [(0, POOL_DIM, [(0, 0, "none", 0)])]
    for c in range(3 * MOBA_DIM // chunk):
        op = "rope_scale" if c < MOBA_DIM // chunk else ("rope" if c < 2 * MOBA_DIM // chunk else "none")
        plan.append((POOL_DIM + c * chunk, chunk, [(1, c * chunk, op, 0)]))
    u_pool, qkv = _norm_project(x, g, w_in.astype(BF16), plan, [rope_pair],
                                [(POOL_DIM, F32), (3 * MOBA_DIM, BF16)], seq)
    w_bd = jax.scipy.linalg.block_diag(*[pool_w[i] for i in range(pool_w.shape[0])]).astype(BF16)
    y_a = _pool_mixer(u_pool, w_bd, pool_scale, batch, seq)
    y_b = _moba_attention(qkv, batch, seq)
    w_out = w_out.astype(BF16)
    return [y_a, y_b], [w_out[:POOL_DIM], w_out[POOL_DIM:]]


def _odd_mixer(x, g, w_in, pe_k, pe_v, k_w1, k_w2, v_w1, v_w2, w_out, rope_pair, rope_single, batch, seq):
    d = x.shape[1]
    qd = NSA_HEADS * HEAD_DIM
    kvd = NSA_KV_DIM
    wq = w_in[:, :qd]
    parts = [w_in[:, qd + i * kvd:qd + (i + 1) * kvd].reshape(d, NSA_GROUPS, HEAD_DIM) for i in range(6)]
    pairs = [jnp.concatenate([parts[2 * i], parts[2 * i + 1]], axis=-1).reshape(d, 2 * kvd) for i in range(3)]
    n_gate = 3 * NSA_HEADS
    w_gate = jnp.zeros((d, LANES), F32).at[:, :n_gate].set(w_in[:, qd + 6 * kvd:])
    w_all = jnp.concatenate([wq] + pairs + [w_gate], axis=1).astype(BF16)

    chunk = 2 * LANES
    plan = []
    for c in range(qd // chunk):
        plan.append((c * chunk, chunk, [(0, c * chunk, "scale", 0), (1, c * chunk, "rope_scale", 0)]))
    for c in range(6 * kvd // chunk):
        branch = c // (2 * kvd // chunk)
        op = "none" if branch == 0 else "rope"
        plan.append((qd + c * chunk, chunk, [(2, c * chunk, op, 1)]))
    plan.append((qd + 6 * kvd, LANES, [(3, 0, "sigmoid", 0)]))
    q_raw, q_rot, kv, gates = _norm_project(
        x, g, w_all, plan, [rope_pair, rope_single],
        [(qd, BF16), (qd, BF16), (6 * kvd, BF16), (LANES, F32)], seq)

    ka, kb = _compress(kv, pe_k, pe_v, k_w1, k_w2, v_w1, v_w2, batch, seq)
    y = _nsa_attention(q_raw, q_rot, kv, ka, kb, gates, batch, seq)
    return [y], [w_out.astype(BF16)]


def kernel(x, mem, positions, norm_g, mem_g, final_g, w_in_ab, pool_w, pool_scale, w_out_ab, ffn_w_gate, ffn_w_up, ffn_w_down, w_in_c, cmp_pe_k, cmp_pe_v, cmp_k_w1, cmp_k_w2, cmp_v_w1, cmp_v_w2, w_out_c, router_w, moe_w_gate, moe_w_up, moe_w_down, xa_wq, xa_wkv, xa_wo):
    batch, seq, d = x.shape
    mem_len = mem.shape[1]
    depth = norm_g.shape[0]
    rope_pair, rope_single = _rope_tables(positions)

    wkv_all = jnp.concatenate([xa_wkv[l] for l in range(depth)], axis=1).astype(BF16)
    kv_all = _norm_matmul(mem.reshape(batch * mem_len, d), mem_g, wkv_all)

    xf = x.reshape(batch * seq, d)
    for layer in range(depth):
        i = layer // 2
        last = layer == depth - 1
        if layer % 2 == 0:
            ys, ws = _even_mixer(xf, norm_g[layer, 0], w_in_ab[i], pool_w[i], pool_scale[i], w_out_ab[i],
                                 rope_pair, batch, seq)
        else:
            ys, ws = _odd_mixer(xf, norm_g[layer, 0], w_in_c[i], cmp_pe_k[i], cmp_pe_v[i], cmp_k_w1[i],
                                cmp_k_w2[i], cmp_v_w1[i], cmp_v_w2[i], w_out_c[i], rope_pair, rope_single,
                                batch, seq)
        xf = _out_proj(xf, ys, ws)
        mid_args = (xf, norm_g[layer, 1], xa_wq[layer].astype(BF16), kv_all, layer,
                    xa_wo[layer].astype(BF16), seq, mem_len)
        if layer % 2 == 0:
            xf = _mid_layer(*mid_args)
            xf = _ffn(xf, norm_g[layer, 2], ffn_w_gate[i].astype(BF16), ffn_w_up[i].astype(BF16),
                      ffn_w_down[i].astype(BF16))
            if last:
                xf = _final_norm(xf, final_g)
        else:
            xf, route, cnt = _mid_layer(*mid_args, router=(norm_g[layer, 2], router_w[i]))
            xf = _moe(xf, norm_g[layer, 2], route, cnt, moe_w_gate, moe_w_up, moe_w_down, i,
                      final_g=final_g if last else None)
    return xf.reshape(batch, seq, d)
```
